```python
import math
import jax
import jax.numpy as jnp
from jax import lax
import numpy as np

D_MODEL = 2048
BATCH = 4
SEQ = 2048
DEPTH = 1
DEC_BATCH = 32
DEC_SEQ = 4
PAST_LEN = 8192
PAGE_SIZE = 128

HEAD_DIM = 128
HEADS_PER_GROUP = 4
ATTN_GROUPS = ((128, 1), (512, 4), (2048, 16))
N_GROUPS = len(ATTN_GROUPS)
N_ATTN_HEADS = N_GROUPS * HEADS_PER_GROUP
ATTN_WIDTH = N_ATTN_HEADS * HEAD_DIM
ATTN_OUT_WIDTH = HEADS_PER_GROUP * HEAD_DIM
ATTN_SCALE = HEAD_DIM ** -0.5
N_REL_BUCKETS = 32
REL_MAX_DIST = 2048
POOL_WINDOWS = (2, 4, 8, 16)
POOL_GROUPS = len(POOL_WINDOWS)
POOL_WIDTH = D_MODEL // 4
PGW = POOL_WIDTH // POOL_GROUPS
POOL_BUF = max(POOL_WINDOWS) - 1
IN_WIDTH = 3 * ATTN_WIDTH + POOL_WIDTH + 2 * D_MODEL
N_EXPERTS = 32
TOP_K = 4
D_FF = D_MODEL
SWIGLU_LIMIT = 7.0
SWIGLU_ALPHA = 1.702
MOE_BLK = 128
NORM_EPS = 1e-5
NEG_INF = -1e30
F32 = jnp.float32

kernel_name = 'hybrid_dilated_attn_pool_moe_step'


def rmsnorm(x, g):
    xf = x.astype(F32)
    y = xf * lax.rsqrt(jnp.mean(xf * xf, axis=-1, keepdims=True) + NORM_EPS)
    return (y * g.astype(F32)).astype(x.dtype)


def t5_bucket(dist):
    max_exact = N_REL_BUCKETS // 2
    df = jnp.maximum(dist, 1).astype(F32)
    large = max_exact + (jnp.log(df / max_exact) / math.log(REL_MAX_DIST / max_exact)
                         * (N_REL_BUCKETS - max_exact)).astype(jnp.int32)
    large = jnp.minimum(large, N_REL_BUCKETS - 1)
    return jnp.where(dist < max_exact, dist, large)


def group_step_bias(rel_bias, g):
    window, dil = ATTN_GROUPS[g]
    n_back = window // dil
    buckets = t5_bucket(jnp.arange(n_back + 1, dtype=jnp.int32) * dil)
    tab = rel_bias[buckets].astype(F32)
    return tab[:, g * HEADS_PER_GROUP:(g + 1) * HEADS_PER_GROUP].T


def softmax_lse(s):
    m = jnp.max(s, axis=-1, keepdims=True)
    p = jnp.exp(s - m)
    den = jnp.sum(p, axis=-1, keepdims=True)
    return p / den, (m + jnp.log(den))[..., 0]


def dilated_attn_prompt(q, k, v, bias_j, dil, n_back):
    b, s, h, dh = q.shape
    length = s // dil
    nb = -(-length // n_back)
    lp = nb * n_back
    z = b * dil

    def to_res(a):
        a = a.reshape(b, length, dil, h, dh).transpose(0, 2, 1, 3, 4).reshape(z, length, h, dh)
        return jnp.pad(a, ((0, 0), (0, lp - length), (0, 0), (0, 0)))

    def band(a):
        a = jnp.pad(a, ((0, 0), (n_back, 0), (0, 0), (0, 0))).reshape(z, nb + 1, n_back, h, dh)
        return jnp.concatenate([a[:, :-1], a[:, 1:]], axis=2)

    qb = to_res(q).reshape(z, nb, n_back, h, dh)
    kb = band(to_res(k))
    vb = band(to_res(v))
    sc = jnp.einsum('znqhd,znkhd->znhqk', qb.astype(F32), kb.astype(F32)) * ATTN_SCALE
    qi = jnp.arange(n_back)[:, None]
    kj = jnp.arange(2 * n_back)[None, :]
    steps = qi + n_back - kj
    key_idx = jnp.arange(nb)[:, None, None] * n_back + kj[None] - n_back
    valid = (steps >= 0)[None] & (steps <= n_back)[None] & (key_idx >= 0)
    bias = bias_j[:, jnp.clip(steps, 0, n_back)]
    sc = jnp.where(valid[None, :, None], sc + bias[None, None], NEG_INF)
    p, lse = softmax_lse(sc)
    o = jnp.einsum('znhqk,znkhd->znqhd', p, vb.astype(F32))
    o = o.reshape(z, lp, h, dh)[:, :length].reshape(b, dil, length, h, dh)
    o = o.transpose(0, 2, 1, 3, 4).reshape(b, s, h, dh)
    lse = lse.transpose(0, 1, 3, 2).reshape(z, lp, h)[:, :length].reshape(b, dil, length, h)
    lse = lse.transpose(0, 2, 1, 3).reshape(b, s, h)
    return o, lse


def dilated_attn_sample(q, k_new, v_new, kv_buf, bias_j, dil, n_back):
    t = q.shape[1]
    lb = kv_buf.shape[1]
    k_all = jnp.concatenate([kv_buf[:, :, 0].astype(k_new.dtype), k_new], axis=1)
    v_all = jnp.concatenate([kv_buf[:, :, 1].astype(v_new.dtype), v_new], axis=1)
    steps = jnp.arange(n_back + 1, dtype=jnp.int32)
    idx = lb + jnp.arange(t, dtype=jnp.int32)[:, None] - steps[None, :] * dil
    valid = idx >= 0
    idx = jnp.maximum(idx, 0)
    kg = k_all[:, idx]
    vg = v_all[:, idx]
    sc = jnp.einsum('bthd,btjhd->bhtj', q.astype(F32), kg.astype(F32)) * ATTN_SCALE
    sc = jnp.where(valid[None, None], sc + bias_j[None, :, None, :], NEG_INF)
    p, lse = softmax_lse(sc)
    o = jnp.einsum('bhtj,btjhd->bthd', p, vg.astype(F32))
    new_buf = jnp.stack([k_all[:, t:], v_all[:, t:]], axis=2)
    return o, lse.transpose(0, 2, 1), new_buf


def combine_groups(outs, lses):
    alpha = jax.nn.softmax(jnp.stack(lses, axis=0), axis=0)
    o = jnp.sum(alpha[..., None] * jnp.stack(outs, axis=0), axis=0)
    b, t = o.shape[:2]
    return o.reshape(b, t, ATTN_OUT_WIDTH)


def pool_mix(z_ext, pos, w_pool, pool_scale):
    b, _, c = z_ext.shape
    t = pos.shape[0]
    zf = z_ext.astype(F32)
    cs = jnp.concatenate([jnp.zeros((b, 1, c), F32), jnp.cumsum(zf, axis=1)], axis=1)
    cur = zf[:, POOL_BUF:]
    hi = cs[:, POOL_BUF + 1:POOL_BUF + 1 + t]
    parts = []
    for g, w in enumerate(POOL_WINDOWS):
        sl = slice(g * PGW, (g + 1) * PGW)
        lo = cs[:, POOL_BUF + 1 - w:POOL_BUF + 1 - w + t, sl]
        cnt = jnp.minimum(w, pos + 1).astype(F32)[None, :, None]
        parts.append((hi[..., sl] - lo) / cnt - cur[..., sl])
    m = jnp.stack(parts, axis=2)
    y = jnp.einsum('btgc,gcd->btgd', m, w_pool.astype(F32))
    y = y * pool_scale.astype(F32).reshape(POOL_GROUPS, PGW)
    return y.reshape(b, t, c).astype(z_ext.dtype)


def moe_ffn(x, router_w, router_b, w_gate_up, b_gate_up, w_down, b_down):
    shp = x.shape
    xt = x.reshape(-1, shp[-1])
    m = xt.shape[0]
    logits = xt.astype(F32) @ router_w.astype(F32) + router_b.astype(F32)
    top_v, top_e = lax.top_k(logits, TOP_K)
    gate = jax.nn.softmax(top_v, axis=-1)
    n_assign = m * TOP_K
    flat_e = top_e.reshape(-1)
    order = jnp.argsort(flat_e)
    se = flat_e[order]
    tok = (order // TOP_K).astype(jnp.int32)
    gw = gate.reshape(-1)[order]
    counts = jnp.zeros((N_EXPERTS,), jnp.int32).at[flat_e].add(1)
    padded = (counts + MOE_BLK - 1) // MOE_BLK * MOE_BLK
    pad_end = jnp.cumsum(padded)
    pad_start = pad_end - padded
    start = jnp.cumsum(counts) - counts
    dest = pad_start[se] + jnp.arange(n_assign, dtype=jnp.int32) - start[se]
    n_blk = -(-(n_assign + N_EXPERTS * (MOE_BLK - 1)) // MOE_BLK)
    rows = n_blk * MOE_BLK
    row_tok = jnp.full((rows,), m, jnp.int32).at[dest].set(tok)
    row_w = jnp.zeros((rows,), F32).at[dest].set(gw)
    blk_e = jnp.minimum(jnp.searchsorted(pad_end, jnp.arange(n_blk, dtype=jnp.int32) * MOE_BLK, side='right'),
                        N_EXPERTS - 1)
    x_pad = jnp.concatenate([xt, jnp.zeros((1, xt.shape[1]), xt.dtype)], axis=0)
    xb = x_pad[row_tok].reshape(n_blk, MOE_BLK, xt.shape[1])

    def expert_block(args):
        xe, e = args
        gu = (xe @ w_gate_up[e] + b_gate_up[e]).astype(F32)
        g_, u_ = jnp.split(gu, 2, axis=-1)
        g_ = jnp.minimum(g_, SWIGLU_LIMIT)
        u_ = jnp.clip(u_, -SWIGLU_LIMIT, SWIGLU_LIMIT)
        hdn = ((u_ + 1.0) * (g_ * jax.nn.sigmoid(SWIGLU_ALPHA * g_))).astype(xe.dtype)
        return hdn @ w_down[e] + b_down[e]

    yb = lax.map(expert_block, (xb, blk_e)).reshape(rows, -1)
    out = jnp.zeros((m + 1, yb.shape[1]), F32).at[row_tok].add(yb.astype(F32) * row_w[:, None])[:m]
    return out.astype(x.dtype).reshape(shp)


def in_projection(hn, w_in):
    p = hn @ w_in
    cuts = [ATTN_WIDTH, 2 * ATTN_WIDTH, 3 * ATTN_WIDTH, 3 * ATTN_WIDTH + POOL_WIDTH,
            3 * ATTN_WIDTH + POOL_WIDTH + D_MODEL]
    q, k, v, z, ga, gb = jnp.split(p, cuts, axis=-1)
    b, t = hn.shape[:2]
    heads = lambda a: a.reshape(b, t, N_ATTN_HEADS, HEAD_DIM)
    return heads(q), heads(k), heads(v), z, ga, gb


def merge_branches(attn, pool, ga, gb, w_branch_a, w_branch_b, w_out):
    ya = attn.astype(ga.dtype) @ w_branch_a
    yb = pool.astype(ga.dtype) @ w_branch_b
    return (jax.nn.sigmoid(ga) * ya + jax.nn.sigmoid(gb) * yb) @ w_out


def setup_inputs(seed: int = 0) -> dict:
    key = jax.random.key(seed)
    ks = jax.random.split(key, 24)
    nrm = lambda k, shape, scale: jax.random.normal(k, shape, F32) * scale

    def kv_cache(k, window):
        return nrm(k, (DEPTH, DEC_BATCH, min(window, PAST_LEN), 2, HEADS_PER_GROUP, HEAD_DIM), 1.0)

    return {
        'x_prompt': nrm(ks[0], (BATCH, SEQ, D_MODEL), 1.0),
        'x_sample': nrm(ks[1], (DEC_BATCH, DEC_SEQ, D_MODEL), 1.0),
        'cache_kv_w128': kv_cache(ks[2], ATTN_GROUPS[0][0]),
        'cache_kv_w512': kv_cache(ks[3], ATTN_GROUPS[1][0]),
        'cache_kv_w2048': kv_cache(ks[4], ATTN_GROUPS[2][0]),
        'state_pool': nrm(ks[5], (DEPTH, DEC_BATCH, POOL_BUF, POOL_WIDTH), 1.0),
        'rel_bias': nrm(ks[6], (N_REL_BUCKETS, N_ATTN_HEADS), 0.5),
        'norm1': 1.0 + nrm(ks[7], (DEPTH, D_MODEL), 0.05),
        'w_in': nrm(ks[8], (DEPTH, D_MODEL, IN_WIDTH), D_MODEL ** -0.5),
        'w_branch_a': nrm(ks[9], (DEPTH, ATTN_OUT_WIDTH, D_MODEL), ATTN_OUT_WIDTH ** -0.5),
        'w_branch_b': nrm(ks[10], (DEPTH, POOL_WIDTH, D_MODEL), POOL_WIDTH ** -0.5),
        'w_out': nrm(ks[11], (DEPTH, D_MODEL, D_MODEL), D_MODEL ** -0.5),
        'w_pool': nrm(ks[12], (DEPTH, POOL_GROUPS, PGW, PGW), PGW ** -0.5),
        'pool_scale': 1.0 + nrm(ks[13], (DEPTH, POOL_WIDTH), 0.1),
        'norm2': 1.0 + nrm(ks[14], (DEPTH, D_MODEL), 0.05),
        'router_w': nrm(ks[15], (DEPTH, D_MODEL, N_EXPERTS), D_MODEL ** -0.5),
        'router_b': nrm(ks[16], (DEPTH, N_EXPERTS), 0.01),
        'w_gate_up': nrm(ks[17], (DEPTH, N_EXPERTS, D_MODEL, 2 * D_FF), D_MODEL ** -0.5),
        'b_gate_up': nrm(ks[18], (DEPTH, N_EXPERTS, 2 * D_FF), 0.01),
        'w_down': nrm(ks[19], (DEPTH, N_EXPERTS, D_FF, D_MODEL), D_FF ** -0.5),
        'b_down': nrm(ks[20], (DEPTH, N_EXPERTS, D_MODEL), 0.01),
        'norm_f': 1.0 + nrm(ks[21], (D_MODEL,), 0.05),
    }


def reference(x_prompt, x_sample, cache_kv_w128, cache_kv_w512, cache_kv_w2048, state_pool,
              rel_bias, norm1, w_in, w_branch_a, w_branch_b, w_out, w_pool, pool_scale,
              norm2, router_w, router_b, w_gate_up, b_gate_up, w_down, b_down, norm_f):
    caches = (cache_kv_w128, cache_kv_w512, cache_kv_w2048)
    step_bias = [group_step_bias(rel_bias, g) for g in range(N_GROUPS)]
    b, s, _ = x_prompt.shape
    t = x_sample.shape[1]
    pos_p = jnp.arange(s, dtype=jnp.int32)
    pos_s = PAST_LEN + jnp.arange(t, dtype=jnp.int32)
    hp, hs = x_prompt, x_sample
    kv_p = [[] for _ in range(N_GROUPS)]
    kv_s = [[] for _ in range(N_GROUPS)]
    pool_p, pool_s = [], []
    for l in range(DEPTH):
        wgu, bgu, wdn, bdn = w_gate_up[l], b_gate_up[l], w_down[l], b_down[l]
        q, k, v, z, ga, gb = in_projection(rmsnorm(hp, norm1[l]), w_in[l])
        outs, lses = [], []
        for g, (window, dil) in enumerate(ATTN_GROUPS):
            hsl = slice(g * HEADS_PER_GROUP, (g + 1) * HEADS_PER_GROUP)
            o, lse = dilated_attn_prompt(q[:, :, hsl], k[:, :, hsl], v[:, :, hsl], step_bias[g], dil, window // dil)
            outs.append(o)
            lses.append(lse)
            keep = min(window, s)
            kv_p[g].append(jnp.stack([k[:, s - keep:, hsl], v[:, s - keep:, hsl]], axis=2))
        attn = combine_groups(outs, lses)
        z_ext = jnp.concatenate([jnp.zeros((b, POOL_BUF, POOL_WIDTH), z.dtype), z], axis=1)
        pool = pool_mix(z_ext, pos_p, w_pool[l], pool_scale[l])
        pool_p.append(z[:, s - POOL_BUF:])
        hp = hp + merge_branches(attn, pool, ga, gb, w_branch_a[l], w_branch_b[l], w_out[l])
        hp = hp + moe_ffn(rmsnorm(hp, norm2[l]), router_w[l], router_b[l], wgu, bgu, wdn, bdn)
        q, k, v, z, ga, gb = in_projection(rmsnorm(hs, norm1[l]), w_in[l])
        outs, lses = [], []
        for g, (window, dil) in enumerate(ATTN_GROUPS):
            hsl = slice(g * HEADS_PER_GROUP, (g + 1) * HEADS_PER_GROUP)
            o, lse, new_buf = dilated_attn_sample(q[:, :, hsl], k[:, :, hsl], v[:, :, hsl], caches[g][l],
                                                  step_bias[g], dil, window // dil)
            outs.append(o)
            lses.append(lse)
            kv_s[g].append(new_buf)
        attn = combine_groups(outs, lses)
        z_ext = jnp.concatenate([state_pool[l].astype(z.dtype), z], axis=1)
        pool = pool_mix(z_ext, pos_s, w_pool[l], pool_scale[l])
        pool_s.append(z_ext[:, t:])
        hs = hs + merge_branches(attn, pool, ga, gb, w_branch_a[l], w_branch_b[l], w_out[l])
        hs = hs + moe_ffn(rmsnorm(hs, norm2[l]), router_w[l], router_b[l], wgu, bgu, wdn, bdn)
    y_prompt = rmsnorm(hp, norm_f)
    y_sample = rmsnorm(hs, norm_f)
    return (y_prompt, y_sample,
            jnp.stack(kv_p[0]), jnp.stack(kv_p[1]), jnp.stack(kv_p[2]), jnp.stack(pool_p),
            jnp.stack(kv_s[0]), jnp.stack(kv_s[1]), jnp.stack(kv_s[2]), jnp.stack(pool_s))
```

```python
import functools
import math

import jax
import jax.numpy as jnp
from jax import lax
from jax.experimental import pallas as pl
from jax.experimental.pallas import tpu as pltpu

F32 = jnp.float32
BF16 = jnp.bfloat16

HEAD_DIM = 128
HEADS_PER_GROUP = 4
ATTN_GROUPS = ((128, 1), (512, 4), (2048, 16))
N_GROUPS = len(ATTN_GROUPS)
N_ATTN_HEADS = N_GROUPS * HEADS_PER_GROUP
ATTN_WIDTH = N_ATTN_HEADS * HEAD_DIM
ATTN_OUT_WIDTH = HEADS_PER_GROUP * HEAD_DIM
ATTN_SCALE = HEAD_DIM ** -0.5
N_BACK = 128
N_REL_BUCKETS = 32
REL_MAX_DIST = 2048
POOL_WINDOWS = (2, 4, 8, 16)
POOL_GROUPS = len(POOL_WINDOWS)
PGW = 128
POOL_WIDTH = POOL_GROUPS * PGW
POOL_BUF = max(POOL_WINDOWS) - 1
POOL_HIST = 16
TOP_K = 4
SWIGLU_LIMIT = 7.0
SWIGLU_ALPHA = 1.702
NORM_EPS = 1e-5
NEG_INF = -1e30
PAST_LEN = 8192

VMEM_LIMIT_BYTES = 56 * 1024 * 1024
MOE_ROW_TILE = 256


def _cparams(*sem):
    return pltpu.CompilerParams(dimension_semantics=sem, vmem_limit_bytes=VMEM_LIMIT_BYTES)


def _rmsnorm_f32(x, g):
    return x * lax.rsqrt(jnp.mean(x * x, axis=-1, keepdims=True) + NORM_EPS) * g


def _inproj_kernel(x_ref, g_ref, w_ref, o_ref):
    xn = _rmsnorm_f32(x_ref[...], g_ref[...]).astype(BF16)
    o_ref[...] = jnp.dot(xn, w_ref[...], preferred_element_type=F32)


def _inproj(x, g, w, tm, tn):
    m, d = x.shape
    n = w.shape[1]
    return pl.pallas_call(
        _inproj_kernel,
        grid=(n // tn, m // tm),
        in_specs=[
            pl.BlockSpec((tm, d), lambda j, i: (i, 0)),
            pl.BlockSpec((1, d), lambda j, i: (0, 0)),
            pl.BlockSpec((d, tn), lambda j, i: (0, j)),
        ],
        out_specs=pl.BlockSpec((tm, tn), lambda j, i: (i, j)),
        out_shape=jax.ShapeDtypeStruct((m, n), F32),
        compiler_params=_cparams("arbitrary", "arbitrary"),
        name="inproj",
    )(x, g, w)


def _t5_bucket(dist):
    max_exact = N_REL_BUCKETS // 2
    df = jnp.maximum(dist, 1).astype(F32)
    large = max_exact + (jnp.log(df / max_exact) / math.log(REL_MAX_DIST / max_exact)
                         * (N_REL_BUCKETS - max_exact)).astype(jnp.int32)
    large = jnp.minimum(large, N_REL_BUCKETS - 1)
    return jnp.where(dist < max_exact, dist, large)


def _step_bias(rel_bias, g):
    _, dil = ATTN_GROUPS[g]
    buckets = _t5_bucket(jnp.arange(N_BACK + 1, dtype=jnp.int32) * dil)
    tab = rel_bias[buckets].astype(F32)
    return tab[:, g * HEADS_PER_GROUP:(g + 1) * HEADS_PER_GROUP].T


def _prompt_bias_tiles(rel_bias):
    qi = jnp.arange(N_BACK)[:, None]
    kj = jnp.arange(2 * N_BACK)[None, :]
    steps = qi + N_BACK - kj
    valid = (steps >= 0) & (steps <= N_BACK)
    tiles = []
    for g in range(N_GROUPS):
        bj = _step_bias(rel_bias, g)
        t = jnp.where(valid[None], bj[:, jnp.clip(steps, 0, N_BACK)], NEG_INF)
        tiles.append(t)
    return jnp.stack(tiles, axis=1)


def _sample_bias_tiles(rel_bias, t_new):
    tq = jnp.arange(t_new)[:, None]
    outs = []
    new = []
    for g, (window, dil) in enumerate(ATTN_GROUPS):
        bj = _step_bias(rel_bias, g)
        if g < 2:
            p = jnp.arange(window)[None, :]
            dist = window + tq - p
            ok = (dist % dil == 0) & (dist // dil <= N_BACK)
            outs.append(jnp.where(ok[None], bj[:, jnp.clip(dist // dil, 0, N_BACK)], NEG_INF))
        else:
            col = jnp.arange(t_new * N_BACK)[None, :]
            blk, r = col // N_BACK, col % N_BACK
            ok = blk == tq
            outs.append(jnp.where(ok[None], bj[:, jnp.broadcast_to(N_BACK - r, (t_new, t_new * N_BACK))], NEG_INF))
        tk = jnp.arange(t_new)[None, :]
        d = tq - tk
        okn = (d >= 0) & (d % dil == 0) & (d // dil <= N_BACK)
        new.append(jnp.where(okn[None], bj[:, jnp.clip(d // dil, 0, N_BACK)], NEG_INF))
    return outs, jnp.stack(new, axis=0)


def _softmax_block(s, v):
    m = jnp.max(s, axis=-1, keepdims=True)
    p = jnp.exp(s - m)
    den = jnp.sum(p, axis=-1, keepdims=True)
    o = jnp.dot(p.astype(BF16), v, preferred_element_type=F32) / den
    return o, m + jnp.log(den)


def _prompt_attn_kernel(q0, k0, v0, q1, k1, v1, q2, k2, v2, bias_ref, o_ref, o_sc, l_sc, *, seq):
    qkv = ((q0, k0, v0), (q1, k1, v1), (q2, k2, v2))
    nb = N_BACK

    def rows(ref, start, size, dil):
        if dil == 1:
            return ref[pl.ds(start, size), :]
        return ref[pl.ds(start, size, stride=dil), :]

    def put(g, start, dil, o, lse):
        idx = pl.ds(start, nb) if dil == 1 else pl.ds(start, nb, stride=dil)
        o_sc[g, idx, :] = o
        l_sc[g, idx, :] = jnp.broadcast_to(lse, (nb, HEAD_DIM))

    for g, (_, dil) in enumerate(ATTN_GROUPS):
        q_ref, k_ref, v_ref = qkv[g]
        n_blk = seq // (dil * nb)
        span = dil * nb
        for r in range(dil):
            q = rows(q_ref, r, nb, dil).astype(BF16)
            k = rows(k_ref, r, nb, dil).astype(BF16)
            v = rows(v_ref, r, nb, dil).astype(BF16)
            s = lax.dot_general(q, k, (((1,), (1,)), ((), ())), preferred_element_type=F32) * ATTN_SCALE
            s = s + bias_ref[0, g, :, nb:]
            o, lse = _softmax_block(s, v)
            put(g, r, dil, o, lse)

            if n_blk > 1:
                def body(b, carry, q_ref=q_ref, k_ref=k_ref, v_ref=v_ref, g=g, dil=dil, r=r, span=span):
                    q = rows(q_ref, b * span + r, nb, dil).astype(BF16)
                    k = rows(k_ref, (b - 1) * span + r, 2 * nb, dil).astype(BF16)
                    v = rows(v_ref, (b - 1) * span + r, 2 * nb, dil).astype(BF16)
                    s = lax.dot_general(q, k, (((1,), (1,)), ((), ())), preferred_element_type=F32) * ATTN_SCALE
                    s = s + bias_ref[0, g]
                    o, lse = _softmax_block(s, v)
                    put(g, b * span + r, dil, o, lse)
                    return carry

                lax.fori_loop(1, n_blk, body, 0)

    chunk = 256

    def comb(c, carry):
        sl = pl.ds(pl.multiple_of(c * chunk, chunk), chunk)
        l0, l1, l2 = l_sc[0, sl, :], l_sc[1, sl, :], l_sc[2, sl, :]
        mx = jnp.maximum(jnp.maximum(l0, l1), l2)
        e0, e1, e2 = jnp.exp(l0 - mx), jnp.exp(l1 - mx), jnp.exp(l2 - mx)
        num = e0 * o_sc[0, sl, :] + e1 * o_sc[1, sl, :] + e2 * o_sc[2, sl, :]
        o_ref[sl, :] = (num / (e0 + e1 + e2)).astype(o_ref.dtype)
        return carry

    lax.fori_loop(0, seq // chunk, comb, 0)


def _prompt_attn(p, bias, batch, seq, col0):
    def spec(kind, g):
        base = col0 + kind * N_ATTN_HEADS + g * HEADS_PER_GROUP
        return pl.BlockSpec((seq, HEAD_DIM), lambda b, i, base=base: (b, base + i))

    in_specs = []
    for g in range(N_GROUPS):
        in_specs += [spec(0, g), spec(1, g), spec(2, g)]
    in_specs.append(pl.BlockSpec((1, N_GROUPS, N_BACK, 2 * N_BACK), lambda b, i: (i, 0, 0, 0)))
    return pl.pallas_call(
        functools.partial(_prompt_attn_kernel, seq=seq),
        grid=(batch, HEADS_PER_GROUP),
        in_specs=in_specs,
        out_specs=pl.BlockSpec((seq, HEAD_DIM), lambda b, i: (b, i)),
        out_shape=jax.ShapeDtypeStruct((batch * seq, ATTN_OUT_WIDTH), BF16),
        scratch_shapes=[pltpu.VMEM((N_GROUPS, seq, HEAD_DIM), F32), pltpu.VMEM((N_GROUPS, seq, HEAD_DIM), F32)],
        compiler_params=_cparams("arbitrary", "arbitrary"),
        name="prompt_attn",
    )(*([p] * 9), bias)


def _sample_attn_kernel(qkv_ref, c0_ref, c1_ref, c2_ref, b0_ref, b1_ref, b2_ref, bn_ref, o_ref, *, t_new):
    caches = (c0_ref, c1_ref, c2_ref)
    biases = (b0_ref, b1_ref, b2_ref)
    kv_w = HEADS_PER_GROUP * HEAD_DIM
    row_w = 2 * kv_w
    dn = (((1,), (1,)), ((), ()))
    for i in range(HEADS_PER_GROUP):
        outs, lses = [], []
        for g in range(N_GROUPS):
            h = g * HEADS_PER_GROUP + i
            q = qkv_ref[0, :, h * HEAD_DIM:(h + 1) * HEAD_DIM].astype(BF16)
            kn = qkv_ref[0, :, ATTN_WIDTH + h * HEAD_DIM:ATTN_WIDTH + (h + 1) * HEAD_DIM].astype(BF16)
            vn = qkv_ref[0, :, 2 * ATTN_WIDTH + h * HEAD_DIM:2 * ATTN_WIDTH + (h + 1) * HEAD_DIM].astype(BF16)
            c_ref = caches[g]
            if g < 2:
                kc = c_ref[0, :, i * HEAD_DIM:(i + 1) * HEAD_DIM].astype(BF16)
                vc = c_ref[0, :, kv_w + i * HEAD_DIM:kv_w + (i + 1) * HEAD_DIM].astype(BF16)
            else:
                kc = jnp.concatenate(
                    [c_ref[0, :, t * row_w + i * HEAD_DIM:t * row_w + (i + 1) * HEAD_DIM] for t in range(t_new)],
                    axis=0).astype(BF16)
                vc = jnp.concatenate(
                    [c_ref[0, :, t * row_w + kv_w + i * HEAD_DIM:t * row_w + kv_w + (i + 1) * HEAD_DIM]
                     for t in range(t_new)], axis=0).astype(BF16)
            sc = lax.dot_general(q, kc, dn, preferred_element_type=F32) * ATTN_SCALE + biases[g][i]
            sn = lax.dot_general(q, kn, dn, preferred_element_type=F32) * ATTN_SCALE + bn_ref[g, i]
            m = jnp.maximum(jnp.max(sc, axis=-1, keepdims=True), jnp.max(sn, axis=-1, keepdims=True))
            pc = jnp.exp(sc - m)
            pn = jnp.exp(sn - m)
            den = jnp.sum(pc, axis=-1, keepdims=True) + jnp.sum(pn, axis=-1, keepdims=True)
            o = (jnp.dot(pc.astype(BF16), vc, preferred_element_type=F32)
                 + jnp.dot(pn.astype(BF16), vn, preferred_element_type=F32)) / den
            outs.append(o)
            lses.append(m + jnp.log(den))
        mx = jnp.maximum(jnp.maximum(lses[0], lses[1]), lses[2])
        es = [jnp.exp(l - mx) for l in lses]
        num = es[0] * outs[0] + es[1] * outs[1] + es[2] * outs[2]
        o_ref[0, :, i * HEAD_DIM:(i + 1) * HEAD_DIM] = (num / (es[0] + es[1] + es[2])).astype(o_ref.dtype)


def _sample_attn(qkv_s, c0, c1, c2v, bias_c, bias_n):
    db, t_new, _ = qkv_s.shape
    row_w = c0.shape[-1]

    def full(a):
        return pl.BlockSpec(a.shape, lambda b, nd=a.ndim: (0,) * nd)

    return pl.pallas_call(
        functools.partial(_sample_attn_kernel, t_new=t_new),
        grid=(db,),
        in_specs=[
            pl.BlockSpec((1, t_new, qkv_s.shape[-1]), lambda b: (b, 0, 0)),
            pl.BlockSpec((1, c0.shape[1], row_w), lambda b: (b, 0, 0)),
            pl.BlockSpec((1, c1.shape[1], row_w), lambda b: (b, 0, 0)),
            pl.BlockSpec((1, c2v.shape[1], t_new * row_w), lambda b: (b, 0, 0)),
            full(bias_c[0]), full(bias_c[1]), full(bias_c[2]), full(bias_n),
        ],
        out_specs=pl.BlockSpec((1, t_new, ATTN_OUT_WIDTH), lambda b: (b, 0, 0)),
        out_shape=jax.ShapeDtypeStruct((db, t_new, ATTN_OUT_WIDTH), BF16),
        compiler_params=_cparams("arbitrary"),
        name="sample_attn",
    )(qkv_s, c0, c1, c2v, bias_c[0], bias_c[1], bias_c[2], bias_n)


def _pool_kernel(z_ref, w_ref, s_ref, o_ref, *, t_len, pos0, chunk):
    for c in range(t_len // chunk):
        base = POOL_HIST + c * chunk
        for g, w in enumerate(POOL_WINDOWS):
            sl = slice(g * PGW, (g + 1) * PGW)
            cur = z_ref[0, base:base + chunk, sl]
            tot = cur
            for i in range(1, w):
                tot = tot + z_ref[0, base - i:base - i + chunk, sl]
            pos = pos0 + c * chunk + lax.broadcasted_iota(jnp.int32, (chunk, PGW), 0)
            cnt = jnp.minimum(w, pos + 1).astype(F32)
            mixed = tot / cnt - cur
            y = jnp.dot(mixed.astype(BF16), w_ref[g], preferred_element_type=F32) * s_ref[:, sl]
            o_ref[0, c * chunk:(c + 1) * chunk, sl] = y.astype(o_ref.dtype)


def _pool_mix(z_ext, w_pool, pool_scale, pos0):
    b, t_ext, c = z_ext.shape
    t_len = t_ext - POOL_HIST
    chunk = min(t_len, 256)
    return pl.pallas_call(
        functools.partial(_pool_kernel, t_len=t_len, pos0=pos0, chunk=chunk),
        grid=(b,),
        in_specs=[
            pl.BlockSpec((1, t_ext, c), lambda i: (i, 0, 0)),
            pl.BlockSpec(w_pool.shape, lambda i: (0, 0, 0)),
            pl.BlockSpec((1, c), lambda i: (0, 0)),
        ],
        out_specs=pl.BlockSpec((1, t_len, c), lambda i: (i, 0, 0)),
        out_shape=jax.ShapeDtypeStruct((b, t_len, c), BF16),
        compiler_params=_cparams("arbitrary"),
        name="pool_mix",
    )(z_ext, w_pool, pool_scale)


def _merge_kernel(h_ref, a_ref, p_ref, ga_ref, gb_ref, wa_ref, wb_ref, wo_ref, g2_ref, h1_ref, xn_ref):
    ya = jnp.dot(a_ref[...], wa_ref[...], preferred_element_type=F32)
    yb = jnp.dot(p_ref[...], wb_ref[...], preferred_element_type=F32)
    u = jax.nn.sigmoid(ga_ref[...]) * ya + jax.nn.sigmoid(gb_ref[...]) * yb
    h1 = h_ref[...] + jnp.dot(u.astype(BF16), wo_ref[...], preferred_element_type=F32)
    h1_ref[...] = h1
    xn_ref[...] = _rmsnorm_f32(h1, g2_ref[...])


def _merge(h, attn, pool, proj, wa, wb, wo, g2, tm):
    m, d = h.shape

    def const(a):
        return pl.BlockSpec(a.shape, lambda i, nd=a.ndim: (0,) * nd, pipeline_mode=pl.Buffered(1))

    return pl.pallas_call(
        _merge_kernel,
        grid=(m // tm,),
        in_specs=[
            pl.BlockSpec((tm, d), lambda i: (i, 0)),
            pl.BlockSpec((tm, attn.shape[1]), lambda i: (i, 0)),
            pl.BlockSpec((tm, pool.shape[1]), lambda i: (i, 0)),
            pl.BlockSpec((tm, d), lambda i: (i, 0)),
            pl.BlockSpec((tm, d), lambda i: (i, 1)),
            const(wa), const(wb), const(wo), const(g2),
        ],
        out_specs=[pl.BlockSpec((tm, d), lambda i: (i, 0)), pl.BlockSpec((tm, d), lambda i: (i, 0))],
        out_shape=[jax.ShapeDtypeStruct((m, d), F32), jax.ShapeDtypeStruct((m, d), F32)],
        compiler_params=_cparams("arbitrary"),
        name="merge",
    )(h, attn, pool, proj, proj, wa, wb, wo, g2)


def _router_kernel(x_ref, wh_ref, wl_ref, b_ref, e_ref, gate_ref):
    x = x_ref[...]
    xh = x.astype(BF16)
    xl = (x - xh.astype(F32)).astype(BF16)
    logits = (jnp.dot(xh, wh_ref[...], preferred_element_type=F32)
              + jnp.dot(xh, wl_ref[...], preferred_element_type=F32)
              + jnp.dot(xl, wh_ref[...], preferred_element_type=F32)) + b_ref[...]
    n_e = logits.shape[-1]
    lane = lax.broadcasted_iota(jnp.int32, logits.shape, 1)
    vals = logits
    tops, idxs = [], []
    for _ in range(TOP_K):
        m = jnp.max(vals, axis=-1, keepdims=True)
        idx = jnp.min(jnp.where(vals == m, lane, n_e), axis=-1, keepdims=True)
        tops.append(m)
        idxs.append(idx)
        vals = jnp.where(lane == idx, -jnp.inf, vals)
    ex = [jnp.exp(t - tops[0]) for t in tops]
    den = ex[0] + ex[1] + ex[2] + ex[3]
    for k in range(TOP_K):
        e_ref[:, k:k + 1] = idxs[k]
        gate_ref[:, k:k + 1] = ex[k] / den


def _router(xn, wh, wl, b, tm):
    m, d = xn.shape
    n_e = wh.shape[1]
    return pl.pallas_call(
        _router_kernel,
        grid=(m // tm,),
        in_specs=[
            pl.BlockSpec((tm, d), lambda i: (i, 0)),
            pl.BlockSpec((d, n_e), lambda i: (0, 0)),
            pl.BlockSpec((d, n_e), lambda i: (0, 0)),
            pl.BlockSpec((1, n_e), lambda i: (0, 0)),
        ],
        out_specs=[pl.BlockSpec((tm, TOP_K), lambda i: (i, 0)), pl.BlockSpec((tm, TOP_K), lambda i: (i, 0))],
        out_shape=[jax.ShapeDtypeStruct((m, TOP_K), jnp.int32), jax.ShapeDtypeStruct((m, TOP_K), F32)],
        compiler_params=_cparams("arbitrary"),
        name="router",
    )(xn, wh, wl, b)


def _row_gather_kernel(nused_ref, tok_ref, src_hbm, o_ref, buf, sem, *, rows):
    t = pl.program_id(0)
    n_used = nused_ref[0]

    def issue(tile, slot):
        def body(i, carry):
            pltpu.make_async_copy(src_hbm.at[pl.ds(tok_ref[tile * rows + i], 1)],
                                  buf.at[slot, pl.ds(i, 1)], sem.at[slot]).start()
            return carry
        lax.fori_loop(0, rows, body, 0)

    @pl.when(jnp.logical_and(t == 0, n_used > 0))
    def _():
        issue(0, 0)

    @pl.when(t + 1 < n_used)
    def _():
        issue(t + 1, (t + 1) % 2)

    @pl.when(t < n_used)
    def _():
        slot = t % 2

        def wbody(i, carry):
            pltpu.make_async_copy(src_hbm.at[pl.ds(0, 1)], buf.at[slot, pl.ds(i, 1)], sem.at[slot]).wait()
            return carry
        lax.fori_loop(0, rows, wbody, 0)
        o_ref[...] = buf[slot].astype(o_ref.dtype)

    @pl.when(t >= n_used)
    def _():
        o_ref[...] = jnp.zeros_like(o_ref)


def _row_gather(n_used, row_idx, src, n_tiles, rows, out_dtype):
    d = src.shape[1]
    grid_spec = pltpu.PrefetchScalarGridSpec(
        num_scalar_prefetch=2,
        grid=(n_tiles,),
        in_specs=[pl.BlockSpec(memory_space=pl.ANY)],
        out_specs=pl.BlockSpec((rows, d), lambda t, *_: (t, 0)),
        scratch_shapes=[pltpu.VMEM((2, rows, d), src.dtype), pltpu.SemaphoreType.DMA((2,))],
    )
    return pl.pallas_call(
        functools.partial(_row_gather_kernel, rows=rows),
        grid_spec=grid_spec,
        out_shape=jax.ShapeDtypeStruct((n_tiles * rows, d), out_dtype),
        compiler_params=_cparams("arbitrary"),
        name="moe_row_gather",
    )(n_used, row_idx, src)


def _expert_changed(te_ref, t):
    prev = te_ref[jnp.maximum(t - 1, 0)]
    return jnp.logical_or(t == 0, te_ref[t] != prev)


def _moe_up_kernel(te_ref, nused_ref, x_ref, wg_ref, wu_ref, bg_ref, bu_ref, o_ref, wg_sc, wu_sc):
    t = pl.program_id(1)

    @pl.when(t < nused_ref[0])
    def _():
        @pl.when(_expert_changed(te_ref, t))
        def _():
            wg_sc[...] = wg_ref[0].astype(BF16)
            wu_sc[...] = wu_ref[0].astype(BF16)

        x = x_ref[...]
        g = jnp.dot(x, wg_sc[...], preferred_element_type=F32) + bg_ref[0]
        u = jnp.dot(x, wu_sc[...], preferred_element_type=F32) + bu_ref[0]
        g = jnp.minimum(g, SWIGLU_LIMIT)
        u = jnp.clip(u, -SWIGLU_LIMIT, SWIGLU_LIMIT)
        o_ref[...] = ((u + 1.0) * (g * jax.nn.sigmoid(SWIGLU_ALPHA * g))).astype(o_ref.dtype)

    @pl.when(t >= nused_ref[0])
    def _():
        o_ref[...] = jnp.zeros_like(o_ref)


def _moe_up(tile_e, n_used, xs, w_gu, b_gu, rows, nc):
    r_pad, d = xs.shape
    n_tiles = r_pad // rows
    d_ff = w_gu.shape[2] // 2
    n_chunks = d_ff // nc
    grid_spec = pltpu.PrefetchScalarGridSpec(
        num_scalar_prefetch=2,
        grid=(n_chunks, n_tiles),
        in_specs=[
            pl.BlockSpec((rows, d), lambda n, t, te, nu: (t, 0)),
            pl.BlockSpec((1, d, nc), lambda n, t, te, nu: (te[t], 0, n)),
            pl.BlockSpec((1, d, nc), lambda n, t, te, nu: (te[t], 0, n_chunks + n)),
            pl.BlockSpec((1, 1, nc), lambda n, t, te, nu: (te[t], 0, n)),
            pl.BlockSpec((1, 1, nc), lambda n, t, te, nu: (te[t], 0, n_chunks + n)),
        ],
        out_specs=pl.BlockSpec((rows, nc), lambda n, t, te, nu: (t, n)),
        scratch_shapes=[pltpu.VMEM((d, nc), BF16), pltpu.VMEM((d, nc), BF16)],
    )
    return pl.pallas_call(
        _moe_up_kernel,
        grid_spec=grid_spec,
        out_shape=jax.ShapeDtypeStruct((r_pad, d_ff), BF16),
        compiler_params=_cparams("arbitrary", "arbitrary"),
        name="moe_up",
    )(tile_e, n_used, xs, w_gu, w_gu, b_gu, b_gu)


def _moe_down_kernel(te_ref, nused_ref, x_ref, w_ref, b_ref, o_ref, w_sc):
    t = pl.program_id(1)

    @pl.when(t < nused_ref[0])
    def _():
        @pl.when(_expert_changed(te_ref, t))
        def _():
            w_sc[...] = w_ref[0].astype(BF16)

        o_ref[...] = jnp.dot(x_ref[...], w_sc[...], preferred_element_type=F32) + b_ref[0]

    @pl.when(t >= nused_ref[0])
    def _():
        o_ref[...] = jnp.zeros_like(o_ref)


def _moe_down(tile_e, n_used, hid, w_dn, b_dn, rows, nc):
    r_pad, d_ff = hid.shape
    n_tiles = r_pad // rows
    d = w_dn.shape[2]
    n_chunks = d // nc
    grid_spec = pltpu.PrefetchScalarGridSpec(
        num_scalar_prefetch=2,
        grid=(n_chunks, n_tiles),
        in_specs=[
            pl.BlockSpec((rows, d_ff), lambda n, t, te, nu: (t, 0)),
            pl.BlockSpec((1, d_ff, nc), lambda n, t, te, nu: (te[t], 0, n)),
            pl.BlockSpec((1, 1, nc), lambda n, t, te, nu: (te[t], 0, n)),
        ],
        out_specs=pl.BlockSpec((rows, nc), lambda n, t, te, nu: (t, n)),
        scratch_shapes=[pltpu.VMEM((d_ff, nc), BF16)],
    )
    return pl.pallas_call(
        _moe_down_kernel,
        grid_spec=grid_spec,
        out_shape=jax.ShapeDtypeStruct((r_pad, d), F32),
        compiler_params=_cparams("arbitrary", "arbitrary"),
        name="moe_down",
    )(tile_e, n_used, hid, w_dn, b_dn)


def _combine_kernel(dest_ref, h_ref, gate_ref, gf_ref, y_hbm, o_ref, buf, sem, *, rows, final_norm):
    t = pl.program_id(0)
    n_t = pl.num_programs(0)
    n_dma = rows * TOP_K

    def issue(tile, slot):
        def body(i, carry):
            pltpu.make_async_copy(y_hbm.at[pl.ds(dest_ref[tile * n_dma + i], 1)],
                                  buf.at[slot, pl.ds(i, 1)], sem.at[slot]).start()
            return carry
        lax.fori_loop(0, n_dma, body, 0)

    @pl.when(t == 0)
    def _():
        issue(0, 0)

    @pl.when(t + 1 < n_t)
    def _():
        issue(t + 1, (t + 1) % 2)

    slot = t % 2

    def wbody(i, carry):
        pltpu.make_async_copy(y_hbm.at[pl.ds(0, 1)], buf.at[slot, pl.ds(i, 1)], sem.at[slot]).wait()
        return carry
    lax.fori_loop(0, n_dma, wbody, 0)

    acc = h_ref[...]
    for k in range(TOP_K):
        acc = acc + gate_ref[:, k:k + 1] * buf[slot, k * rows:(k + 1) * rows, :]
    o_ref[...] = _rmsnorm_f32(acc, gf_ref[...]) if final_norm else acc


def _combine(dest_km, h1, gate, gf, y, rows, final_norm):
    m, d = h1.shape
    grid_spec = pltpu.PrefetchScalarGridSpec(
        num_scalar_prefetch=1,
        grid=(m // rows,),
        in_specs=[
            pl.BlockSpec((rows, d), lambda t, *_: (t, 0)),
            pl.BlockSpec((rows, TOP_K), lambda t, *_: (t, 0)),
            pl.BlockSpec((1, d), lambda t, *_: (0, 0)),
            pl.BlockSpec(memory_space=pl.ANY),
        ],
        out_specs=pl.BlockSpec((rows, d), lambda t, *_: (t, 0)),
        scratch_shapes=[pltpu.VMEM((2, rows * TOP_K, d), F32), pltpu.SemaphoreType.DMA((2,))],
    )
    return pl.pallas_call(
        functools.partial(_combine_kernel, rows=rows, final_norm=final_norm),
        grid_spec=grid_spec,
        out_shape=jax.ShapeDtypeStruct((m, d), F32),
        compiler_params=_cparams("arbitrary"),
        name="moe_combine",
    )(dest_km, h1, gate, gf, y)


def _routing_tables(top_e, n_experts, rows, n_tiles):
    m = top_e.shape[0]
    flat_e = top_e.reshape(-1)
    onehot = (flat_e[:, None] == jnp.arange(n_experts, dtype=jnp.int32)[None, :]).astype(jnp.int32)
    csum = jnp.cumsum(onehot, axis=0)
    rank = jnp.sum((csum - onehot) * onehot, axis=1)
    counts = csum[-1]
    tiles_per_e = (counts + rows - 1) // rows
    tile_end = jnp.cumsum(tiles_per_e)
    pad_start = (tile_end - tiles_per_e) * rows
    dest = pad_start[flat_e] + rank
    n_used = tile_end[-1:]
    tile_e = jnp.minimum(jnp.searchsorted(tile_end, jnp.arange(n_tiles, dtype=jnp.int32), side='right'),
                         n_experts - 1).astype(jnp.int32)
    tok = jnp.arange(m * TOP_K, dtype=jnp.int32) // TOP_K
    row_tok = jnp.zeros((n_tiles * rows,), jnp.int32).at[dest].set(tok)
    return dest.astype(jnp.int32), row_tok, tile_e, n_used.astype(jnp.int32)


def _largest_divisor(n, cap, mult):
    best = mult
    for c in range(mult, cap + 1, mult):
        if n % c == 0:
            best = c
    return best


def kernel(x_prompt, x_sample, cache_kv_w128, cache_kv_w512, cache_kv_w2048, state_pool, rel_bias, norm1, w_in,
           w_branch_a, w_branch_b, w_out, w_pool, pool_scale, norm2, router_w, router_b, w_gate_up, b_gate_up,
           w_down, b_down, norm_f):
    batch, seq, d = x_prompt.shape
    db, t_new, _ = x_sample.shape
    depth = norm1.shape[0]
    n_experts = router_w.shape[-1]
    caches = (cache_kv_w128, cache_kv_w512, cache_kv_w2048)
    m_p, m_s = batch * seq, db * t_new
    m = m_p + m_s
    tm = _largest_divisor(m, 640, 8)

    h = jnp.concatenate([x_prompt.reshape(m_p, d), x_sample.reshape(m_s, d)], axis=0)
    bias_p = _prompt_bias_tiles(rel_bias)
    bias_sc, bias_sn = _sample_bias_tiles(rel_bias, t_new)

    cuts = [ATTN_WIDTH, 2 * ATTN_WIDTH, 3 * ATTN_WIDTH, 3 * ATTN_WIDTH + POOL_WIDTH, 3 * ATTN_WIDTH + POOL_WIDTH + d]
    qkv0 = 2 * d
    z0 = qkv0 + 3 * ATTN_WIDTH

    kv_p = [[] for _ in range(N_GROUPS)]
    kv_s = [[] for _ in range(N_GROUPS)]
    pool_p, pool_s = [], []
    for l in range(depth):
        wq, wk, wv, wz, wga, wgb = jnp.split(w_in[l], cuts, axis=-1)
        w_perm = jnp.concatenate([wga, wgb, wq, wk, wv, wz], axis=-1).astype(BF16)
        proj = _inproj(h, norm1[l][None, :], w_perm, tm, _largest_divisor(w_perm.shape[1], 1536, 128))

        k_all = proj[:, qkv0 + ATTN_WIDTH:qkv0 + 2 * ATTN_WIDTH]
        v_all = proj[:, qkv0 + 2 * ATTN_WIDTH:qkv0 + 3 * ATTN_WIDTH]
        z_all = proj[:, z0:z0 + POOL_WIDTH]
        k_p = k_all[:m_p].reshape(batch, seq, N_ATTN_HEADS, HEAD_DIM)
        v_p = v_all[:m_p].reshape(batch, seq, N_ATTN_HEADS, HEAD_DIM)
        k_s = k_all[m_p:].reshape(db, t_new, N_ATTN_HEADS, HEAD_DIM)
        v_s = v_all[m_p:].reshape(db, t_new, N_ATTN_HEADS, HEAD_DIM)
        z_p = z_all[:m_p].reshape(batch, seq, POOL_WIDTH)
        z_s = z_all[m_p:].reshape(db, t_new, POOL_WIDTH)

        attn_p = _prompt_attn(proj, bias_p, batch, seq, qkv0 // HEAD_DIM)
        qkv_s = proj[m_p:, qkv0:qkv0 + 3 * ATTN_WIDTH].reshape(db, t_new, 3 * ATTN_WIDTH)
        row_w = 2 * HEADS_PER_GROUP * HEAD_DIM
        c0 = caches[0][l].reshape(db, -1, row_w)
        c1 = caches[1][l].reshape(db, -1, row_w)
        dil2 = ATTN_GROUPS[2][1]
        c2v = caches[2][l].reshape(db, -1, dil2 * row_w)
        attn_s = _sample_attn(qkv_s, c0, c1, c2v, bias_sc, bias_sn)
        attn = jnp.concatenate([attn_p, attn_s.reshape(m_s, ATTN_OUT_WIDTH)], axis=0)

        for g, (window, _) in enumerate(ATTN_GROUPS):
            hsl = slice(g * HEADS_PER_GROUP, (g + 1) * HEADS_PER_GROUP)
            keep = min(window, seq)
            kv_p[g].append(jnp.stack([k_p[:, seq - keep:, hsl], v_p[:, seq - keep:, hsl]], axis=2))
            new_rows = jnp.stack([k_s[:, :, hsl], v_s[:, :, hsl]], axis=2)
            kv_s[g].append(jnp.concatenate([caches[g][l][:, t_new:], new_rows], axis=1))

        w_pool_b = w_pool[l].astype(BF16)
        scale = pool_scale[l][None, :]
        zp_ext = jnp.concatenate([jnp.zeros((batch, POOL_HIST, POOL_WIDTH), F32), z_p], axis=1)
        hist = jnp.concatenate([jnp.zeros((db, POOL_HIST - POOL_BUF, POOL_WIDTH), F32), state_pool[l]], axis=1)
        zs_ext = jnp.concatenate([hist, z_s], axis=1)
        pool = jnp.concatenate([
            _pool_mix(zp_ext, w_pool_b, scale, 0).reshape(m_p, POOL_WIDTH),
            _pool_mix(zs_ext, w_pool_b, scale, PAST_LEN).reshape(m_s, POOL_WIDTH),
        ], axis=0)
        pool_p.append(z_p[:, seq - POOL_BUF:])
        pool_s.append(zs_ext[:, POOL_HIST - POOL_BUF + t_new:])

        h1, xn2 = _merge(h, attn, pool, proj, w_branch_a[l].astype(BF16), w_branch_b[l].astype(BF16),
                         w_out[l].astype(BF16), norm2[l][None, :], _largest_divisor(m, 320, 8))

        rw = router_w[l]
        rw_hi = rw.astype(BF16)
        rw_lo = (rw - rw_hi.astype(F32)).astype(BF16)
        top_e, gate = _router(xn2, rw_hi, rw_lo, router_b[l][None, :], tm)
        rows = MOE_ROW_TILE
        n_tiles = (m * TOP_K + n_experts * (rows - 1)) // rows
        dest, row_tok, tile_e, n_used = _routing_tables(top_e, n_experts, rows, n_tiles)
        xs = _row_gather(n_used, row_tok, xn2, n_tiles, rows, BF16)
        hid = _moe_up(tile_e, n_used, xs, w_gate_up[l], b_gate_up[l][:, None, :], rows, 512)
        y = _moe_down(tile_e, n_used, hid, w_down[l], b_down[l][:, None, :], rows, 1024)
        crow = _largest_divisor(m, 128, 8)
        dest_km = dest.reshape(m // crow, crow, TOP_K).transpose(0, 2, 1).reshape(-1)
        h = _combine(dest_km, h1, gate, norm_f[None, :], y, crow, final_norm=(l == depth - 1))

    y_prompt = h[:m_p].reshape(batch, seq, d)
    y_sample = h[m_p:].reshape(db, t_new, d)
    return (y_prompt, y_sample,
            jnp.stack(kv_p[0]), jnp.stack(kv_p[1]), jnp.stack(kv_p[2]), jnp.stack(pool_p),
            jnp.stack(kv_s[0]), jnp.stack(kv_s[1]), jnp.stack(kv_s[2]), jnp.stack(pool_s))
```

```python
import functools
import math

import jax
import jax.numpy as jnp
from jax import lax
from jax.experimental import pallas as pl
from jax.experimental.pallas import tpu as pltpu

F32 = jnp.float32
BF16 = jnp.bfloat16

HEAD_DIM = 128
HEADS_PER_GROUP = 4
ATTN_GROUPS = ((128, 1), (512, 4), (2048, 16))
N_GROUPS = len(ATTN_GROUPS)
N_ATTN_HEADS = N_GROUPS * HEADS_PER_GROUP
ATTN_WIDTH = N_ATTN_HEADS * HEAD_DIM
ATTN_OUT_WIDTH = HEADS_PER_GROUP * HEAD_DIM
ATTN_SCALE = HEAD_DIM ** -0.5
N_BACK = 128
N_REL_BUCKETS = 32
REL_MAX_DIST = 2048
POOL_WINDOWS = (2, 4, 8, 16)
POOL_GROUPS = len(POOL_WINDOWS)
PGW = 128
POOL_WIDTH = POOL_GROUPS * PGW
POOL_BUF = max(POOL_WINDOWS) - 1
POOL_HIST = 16
TOP_K = 4
SWIGLU_LIMIT = 7.0
SWIGLU_ALPHA = 1.702
NORM_EPS = 1e-5
NEG_INF = -1e30
PAST_LEN = 8192
CACHE_ROW_TILE = 2 * HEADS_PER_GROUP

VMEM_LIMIT_BYTES = 56 * 1024 * 1024
MOE_ROW_TILE = 256
DMA_ISSUE_UNROLL = 8


def _cparams(*sem):
    return pltpu.CompilerParams(dimension_semantics=sem, vmem_limit_bytes=VMEM_LIMIT_BYTES)


def _rmsnorm_f32(x, g):
    return x * lax.rsqrt(jnp.mean(x * x, axis=-1, keepdims=True) + NORM_EPS) * g


def _inproj_kernel(x_ref, g_ref, w_ref, o_ref):
    xn = _rmsnorm_f32(x_ref[...], g_ref[...]).astype(BF16)
    o_ref[...] = jnp.dot(xn, w_ref[...], preferred_element_type=F32)


def _inproj(x, g, w, tm, tn):
    m, d = x.shape
    n = w.shape[1]
    return pl.pallas_call(
        _inproj_kernel,
        grid=(n // tn, m // tm),
        in_specs=[
            pl.BlockSpec((tm, d), lambda j, i: (i, 0)),
            pl.BlockSpec((1, d), lambda j, i: (0, 0)),
            pl.BlockSpec((d, tn), lambda j, i: (0, j)),
        ],
        out_specs=pl.BlockSpec((tm, tn), lambda j, i: (i, j)),
        out_shape=jax.ShapeDtypeStruct((m, n), F32),
        compiler_params=_cparams("arbitrary", "arbitrary"),
        name="inproj",
    )(x, g, w)


def _t5_bucket(dist):
    max_exact = N_REL_BUCKETS // 2
    df = jnp.maximum(dist, 1).astype(F32)
    large = max_exact + (jnp.log(df / max_exact) / math.log(REL_MAX_DIST / max_exact)
                         * (N_REL_BUCKETS - max_exact)).astype(jnp.int32)
    large = jnp.minimum(large, N_REL_BUCKETS - 1)
    return jnp.where(dist < max_exact, dist, large)


def _step_bias(rel_bias, g):
    _, dil = ATTN_GROUPS[g]
    buckets = _t5_bucket(jnp.arange(N_BACK + 1, dtype=jnp.int32) * dil)
    tab = rel_bias[buckets].astype(F32)
    return tab[:, g * HEADS_PER_GROUP:(g + 1) * HEADS_PER_GROUP].T


def _prompt_bias_tiles(rel_bias):
    nb = N_BACK
    period = 3 * nb - 1
    tiles = []
    for g in range(N_GROUPS):
        bj = _step_bias(rel_bias, g)
        h = bj.shape[0]
        pad = jnp.full((h, nb - 1), NEG_INF, F32)
        v = jnp.concatenate([pad, bj[:, ::-1], pad], axis=1)
        skew = jnp.tile(v, (1, nb + 1))[:, :nb * (period + 1)].reshape(h, nb, period + 1)
        tiles.append(skew[:, ::-1, :2 * nb])
    return jnp.stack(tiles, axis=1)


def _sample_bias_tiles(rel_bias, t_new):
    tq = jnp.arange(t_new)[:, None]
    outs = []
    new = []
    for g, (window, dil) in enumerate(ATTN_GROUPS):
        bj = _step_bias(rel_bias, g)
        if g < 2:
            p = jnp.arange(window)[None, :]
            dist = window + tq - p
            ok = (dist % dil == 0) & (dist // dil <= N_BACK)
            outs.append(jnp.where(ok[None], bj[:, jnp.clip(dist // dil, 0, N_BACK)], NEG_INF))
        else:
            col = jnp.arange(t_new * N_BACK)[None, :]
            blk, r = col // N_BACK, col % N_BACK
            ok = blk == tq
            outs.append(jnp.where(ok[None], bj[:, jnp.broadcast_to(N_BACK - r, (t_new, t_new * N_BACK))], NEG_INF))
        tk = jnp.arange(t_new)[None, :]
        d = tq - tk
        okn = (d >= 0) & (d % dil == 0) & (d // dil <= N_BACK)
        new.append(jnp.where(okn[None], bj[:, jnp.clip(d // dil, 0, N_BACK)], NEG_INF))
    return outs, jnp.stack(new, axis=0)


def _softmax_block(s, v):
    m = jnp.max(s, axis=-1, keepdims=True)
    p = jnp.exp(s - m)
    den = jnp.sum(p, axis=-1, keepdims=True)
    o = jnp.dot(p.astype(BF16), v, preferred_element_type=F32) / den
    return o, m + jnp.log(den)


def _prompt_attn_kernel(q0, k0, v0, q1, k1, v1, q2, k2, v2, bias_ref, o_ref, o_sc, l_sc, *, seq):
    qkv = ((q0, k0, v0), (q1, k1, v1), (q2, k2, v2))
    nb = N_BACK

    def rows(ref, start, size, dil):
        if dil == 1:
            return ref[pl.ds(start, size), :]
        return ref[pl.ds(start, size, stride=dil), :]

    def put(g, start, dil, o, lse):
        idx = pl.ds(start, nb) if dil == 1 else pl.ds(start, nb, stride=dil)
        o_sc[g, idx, :] = o
        l_sc[g, idx, :] = jnp.broadcast_to(lse, (nb, HEAD_DIM))

    for g, (_, dil) in enumerate(ATTN_GROUPS):
        q_ref, k_ref, v_ref = qkv[g]
        n_blk = seq // (dil * nb)
        span = dil * nb
        for r in range(dil):
            q = rows(q_ref, r, nb, dil).astype(BF16)
            k = rows(k_ref, r, nb, dil).astype(BF16)
            v = rows(v_ref, r, nb, dil).astype(BF16)
            s = lax.dot_general(q, k, (((1,), (1,)), ((), ())), preferred_element_type=F32) * ATTN_SCALE
            s = s + bias_ref[0, g, :, nb:]
            o, lse = _softmax_block(s, v)
            put(g, r, dil, o, lse)

            if n_blk > 1:
                def body(b, carry, q_ref=q_ref, k_ref=k_ref, v_ref=v_ref, g=g, dil=dil, r=r, span=span):
                    q = rows(q_ref, b * span + r, nb, dil).astype(BF16)
                    k = rows(k_ref, (b - 1) * span + r, 2 * nb, dil).astype(BF16)
                    v = rows(v_ref, (b - 1) * span + r, 2 * nb, dil).astype(BF16)
                    s = lax.dot_general(q, k, (((1,), (1,)), ((), ())), preferred_element_type=F32) * ATTN_SCALE
                    s = s + bias_ref[0, g]
                    o, lse = _softmax_block(s, v)
                    put(g, b * span + r, dil, o, lse)
                    return carry

                lax.fori_loop(1, n_blk, body, 0)

    chunk = 256

    def comb(c, carry):
        sl = pl.ds(pl.multiple_of(c * chunk, chunk), chunk)
        l0, l1, l2 = l_sc[0, sl, :], l_sc[1, sl, :], l_sc[2, sl, :]
        mx = jnp.maximum(jnp.maximum(l0, l1), l2)
        e0, e1, e2 = jnp.exp(l0 - mx), jnp.exp(l1 - mx), jnp.exp(l2 - mx)
        num = e0 * o_sc[0, sl, :] + e1 * o_sc[1, sl, :] + e2 * o_sc[2, sl, :]
        o_ref[sl, :] = (num / (e0 + e1 + e2)).astype(o_ref.dtype)
        return carry

    lax.fori_loop(0, seq // chunk, comb, 0)


def _prompt_attn(p, bias, batch, seq, col0):
    def spec(kind, g):
        base = col0 + kind * N_ATTN_HEADS + g * HEADS_PER_GROUP
        return pl.BlockSpec((seq, HEAD_DIM), lambda b, i, base=base: (b, base + i))

    in_specs = []
    for g in range(N_GROUPS):
        in_specs += [spec(0, g), spec(1, g), spec(2, g)]
    in_specs.append(pl.BlockSpec((1, N_GROUPS, N_BACK, 2 * N_BACK), lambda b, i: (i, 0, 0, 0)))
    return pl.pallas_call(
        functools.partial(_prompt_attn_kernel, seq=seq),
        grid=(batch, HEADS_PER_GROUP),
        in_specs=in_specs,
        out_specs=pl.BlockSpec((seq, HEAD_DIM), lambda b, i: (b, i)),
        out_shape=jax.ShapeDtypeStruct((batch * seq, ATTN_OUT_WIDTH), BF16),
        scratch_shapes=[pltpu.VMEM((N_GROUPS, seq, HEAD_DIM), F32), pltpu.VMEM((N_GROUPS, seq, HEAD_DIM), F32)],
        compiler_params=_cparams("arbitrary", "arbitrary"),
        name="prompt_attn",
    )(*([p] * 9), bias)


def _sample_attn_kernel(qkv_ref, c0_ref, c1_ref, c2_ref, b0_ref, b1_ref, b2_ref, bn_ref, o_ref, *, t_new):
    caches = (c0_ref, c1_ref, c2_ref)
    biases = (b0_ref, b1_ref, b2_ref)
    dn = (((1,), (1,)), ((), ()))
    for i in range(HEADS_PER_GROUP):
        outs, lses = [], []
        for g in range(N_GROUPS):
            h = g * HEADS_PER_GROUP + i
            q = qkv_ref[0, :, h * HEAD_DIM:(h + 1) * HEAD_DIM].astype(BF16)
            kn = qkv_ref[0, :, ATTN_WIDTH + h * HEAD_DIM:ATTN_WIDTH + (h + 1) * HEAD_DIM].astype(BF16)
            vn = qkv_ref[0, :, 2 * ATTN_WIDTH + h * HEAD_DIM:2 * ATTN_WIDTH + (h + 1) * HEAD_DIM].astype(BF16)
            c_ref = caches[g]
            if g < 2:
                n_rows = c_ref.shape[1] // CACHE_ROW_TILE
                kc = c_ref[0, pl.ds(i, n_rows, stride=CACHE_ROW_TILE), :].astype(BF16)
                vc = c_ref[0, pl.ds(HEADS_PER_GROUP + i, n_rows, stride=CACHE_ROW_TILE), :].astype(BF16)
            else:
                kc = jnp.concatenate(
                    [c_ref[0, :, t * CACHE_ROW_TILE + i, :] for t in range(t_new)], axis=0).astype(BF16)
                vc = jnp.concatenate(
                    [c_ref[0, :, t * CACHE_ROW_TILE + HEADS_PER_GROUP + i, :] for t in range(t_new)],
                    axis=0).astype(BF16)
            sc = lax.dot_general(q, kc, dn, preferred_element_type=F32) * ATTN_SCALE + biases[g][i]
            sn = lax.dot_general(q, kn, dn, preferred_element_type=F32) * ATTN_SCALE + bn_ref[g, i]
            m = jnp.maximum(jnp.max(sc, axis=-1, keepdims=True), jnp.max(sn, axis=-1, keepdims=True))
            pc = jnp.exp(sc - m)
            pn = jnp.exp(sn - m)
            den = jnp.sum(pc, axis=-1, keepdims=True) + jnp.sum(pn, axis=-1, keepdims=True)
            o = (jnp.dot(pc.astype(BF16), vc, preferred_element_type=F32)
                 + jnp.dot(pn.astype(BF16), vn, preferred_element_type=F32)) / den
            outs.append(o)
            lses.append(m + jnp.log(den))
        mx = jnp.maximum(jnp.maximum(lses[0], lses[1]), lses[2])
        es = [jnp.exp(l - mx) for l in lses]
        num = es[0] * outs[0] + es[1] * outs[1] + es[2] * outs[2]
        o_ref[0, :, i * HEAD_DIM:(i + 1) * HEAD_DIM] = (num / (es[0] + es[1] + es[2])).astype(o_ref.dtype)


def _sample_attn(qkv_s, c0, c1, c2v, bias_c, bias_n):
    db, t_new, _ = qkv_s.shape

    def full(a):
        return pl.BlockSpec(a.shape, lambda b, nd=a.ndim: (0,) * nd)

    return pl.pallas_call(
        functools.partial(_sample_attn_kernel, t_new=t_new),
        grid=(db,),
        in_specs=[
            pl.BlockSpec((1, t_new, qkv_s.shape[-1]), lambda b: (b, 0, 0)),
            pl.BlockSpec((1, c0.shape[1], HEAD_DIM), lambda b: (b, 0, 0)),
            pl.BlockSpec((1, c1.shape[1], HEAD_DIM), lambda b: (b, 0, 0)),
            pl.BlockSpec((1, c2v.shape[1], t_new * CACHE_ROW_TILE, HEAD_DIM), lambda b: (b, 0, 0, 0)),
            full(bias_c[0]), full(bias_c[1]), full(bias_c[2]), full(bias_n),
        ],
        out_specs=pl.BlockSpec((1, t_new, ATTN_OUT_WIDTH), lambda b: (b, 0, 0)),
        out_shape=jax.ShapeDtypeStruct((db, t_new, ATTN_OUT_WIDTH), BF16),
        compiler_params=_cparams("arbitrary"),
        name="sample_attn",
    )(qkv_s, c0, c1, c2v, bias_c[0], bias_c[1], bias_c[2], bias_n)


CACHE_COPY_SPLIT = 8


def _cache_roll_kernel(*refs, n_caches, t_new):
    caches = refs[:n_caches]
    news = refs[n_caches:2 * n_caches]
    outs = refs[2 * n_caches:3 * n_caches]
    sem = refs[3 * n_caches]
    copies = []
    for g in range(n_caches):
        db, rows = caches[g].shape[0], caches[g].shape[1]
        per = db // CACHE_COPY_SPLIT
        for c in range(CACHE_COPY_SPLIT):
            copies.append(pltpu.make_async_copy(
                caches[g].at[pl.ds(c * per, per), pl.ds(t_new, rows - t_new)],
                outs[g].at[pl.ds(c * per, per), pl.ds(0, rows - t_new)],
                sem.at[len(copies)]))
        copies.append(pltpu.make_async_copy(news[g], outs[g].at[:, pl.ds(rows - t_new, t_new)], sem.at[len(copies)]))
    for c in copies:
        c.start()
    for c in copies:
        c.wait()


def _cache_roll(caches, news):
    n = len(caches)
    t_new = news[0].shape[1]
    assert all(c.shape[0] % CACHE_COPY_SPLIT == 0 for c in caches)
    any_spec = pl.BlockSpec(memory_space=pl.ANY)
    return pl.pallas_call(
        functools.partial(_cache_roll_kernel, n_caches=n, t_new=t_new),
        in_specs=[any_spec] * n + [pl.BlockSpec(memory_space=pltpu.VMEM)] * n,
        out_specs=[any_spec] * n,
        out_shape=[jax.ShapeDtypeStruct(c.shape, c.dtype) for c in caches],
        scratch_shapes=[pltpu.SemaphoreType.DMA((n * (CACHE_COPY_SPLIT + 1),))],
        compiler_params=pltpu.CompilerParams(vmem_limit_bytes=VMEM_LIMIT_BYTES),
        name="cache_roll",
    )(*caches, *news)


def _pool_kernel(z_ref, w_ref, s_ref, o_ref, *, t_len, pos0, chunk):
    for c in range(t_len // chunk):
        base = POOL_HIST + c * chunk
        for g, w in enumerate(POOL_WINDOWS):
            sl = slice(g * PGW, (g + 1) * PGW)
            cur = z_ref[0, base:base + chunk, sl]
            tot = cur
            for i in range(1, w):
                tot = tot + z_ref[0, base - i:base - i + chunk, sl]
            pos = pos0 + c * chunk + lax.broadcasted_iota(jnp.int32, (chunk, PGW), 0)
            cnt = jnp.minimum(w, pos + 1).astype(F32)
            mixed = tot / cnt - cur
            y = jnp.dot(mixed.astype(BF16), w_ref[g], preferred_element_type=F32) * s_ref[:, sl]
            o_ref[0, c * chunk:(c + 1) * chunk, sl] = y.astype(o_ref.dtype)


def _pool_mix(z_ext, w_pool, pool_scale, pos0):
    b, t_ext, c = z_ext.shape
    t_len = t_ext - POOL_HIST
    chunk = min(t_len, 256)
    return pl.pallas_call(
        functools.partial(_pool_kernel, t_len=t_len, pos0=pos0, chunk=chunk),
        grid=(b,),
        in_specs=[
            pl.BlockSpec((1, t_ext, c), lambda i: (i, 0, 0)),
            pl.BlockSpec(w_pool.shape, lambda i: (0, 0, 0)),
            pl.BlockSpec((1, c), lambda i: (0, 0)),
        ],
        out_specs=pl.BlockSpec((1, t_len, c), lambda i: (i, 0, 0)),
        out_shape=jax.ShapeDtypeStruct((b, t_len, c), BF16),
        compiler_params=_cparams("arbitrary"),
        name="pool_mix",
    )(z_ext, w_pool, pool_scale)


def _merge_kernel(h_ref, a_ref, p_ref, ga_ref, gb_ref, wa_ref, wb_ref, wo_ref, g2_ref, h1_ref, xn_ref):
    ya = jnp.dot(a_ref[...], wa_ref[...], preferred_element_type=F32)
    yb = jnp.dot(p_ref[...], wb_ref[...], preferred_element_type=F32)
    u = jax.nn.sigmoid(ga_ref[...]) * ya + jax.nn.sigmoid(gb_ref[...]) * yb
    h1 = h_ref[...] + jnp.dot(u.astype(BF16), wo_ref[...], preferred_element_type=F32)
    h1_ref[...] = h1
    xn_ref[...] = _rmsnorm_f32(h1, g2_ref[...])


def _merge(h, attn, pool, proj, wa, wb, wo, g2, tm):
    m, d = h.shape

    def const(a):
        return pl.BlockSpec(a.shape, lambda i, nd=a.ndim: (0,) * nd, pipeline_mode=pl.Buffered(1))

    return pl.pallas_call(
        _merge_kernel,
        grid=(m // tm,),
        in_specs=[
            pl.BlockSpec((tm, d), lambda i: (i, 0)),
            pl.BlockSpec((tm, attn.shape[1]), lambda i: (i, 0)),
            pl.BlockSpec((tm, pool.shape[1]), lambda i: (i, 0)),
            pl.BlockSpec((tm, d), lambda i: (i, 0)),
            pl.BlockSpec((tm, d), lambda i: (i, 1)),
            const(wa), const(wb), const(wo), const(g2),
        ],
        out_specs=[pl.BlockSpec((tm, d), lambda i: (i, 0)), pl.BlockSpec((tm, d), lambda i: (i, 0))],
        out_shape=[jax.ShapeDtypeStruct((m, d), F32), jax.ShapeDtypeStruct((m, d), F32)],
        compiler_params=_cparams("arbitrary"),
        name="merge",
    )(h, attn, pool, proj, proj, wa, wb, wo, g2)


def _router_kernel(x_ref, wh_ref, wl_ref, b_ref, e_ref, gate_ref, rank_ref, cnt_ref, carry):
    @pl.when(pl.program_id(0) == 0)
    def _():
        carry[...] = jnp.zeros_like(carry)

    x = x_ref[...]
    xh = x.astype(BF16)
    xl = (x - xh.astype(F32)).astype(BF16)
    logits = (jnp.dot(xh, wh_ref[...], preferred_element_type=F32)
              + jnp.dot(xh, wl_ref[...], preferred_element_type=F32)
              + jnp.dot(xl, wh_ref[...], preferred_element_type=F32)) + b_ref[...]
    n_e = logits.shape[-1]
    lane = lax.broadcasted_iota(jnp.int32, logits.shape, 1)
    vals = logits
    tops, idxs = [], []
    for _ in range(TOP_K):
        m = jnp.max(vals, axis=-1, keepdims=True)
        idx = jnp.min(jnp.where(vals == m, lane, n_e), axis=-1, keepdims=True)
        tops.append(m)
        idxs.append(idx)
        vals = jnp.where(lane == idx, -jnp.inf, vals)
    ex = [jnp.exp(t - tops[0]) for t in tops]
    den = ex[0] + ex[1] + ex[2] + ex[3]
    for k in range(TOP_K):
        e_ref[:, k:k + 1] = idxs[k]
        gate_ref[:, k:k + 1] = ex[k] / den

    onehots = [(lane == idx).astype(F32) for idx in idxs]
    cnt = onehots[0] + onehots[1] + onehots[2] + onehots[3]
    tm = cnt.shape[0]
    lower = (lax.broadcasted_iota(jnp.int32, (tm, tm), 0) > lax.broadcasted_iota(jnp.int32, (tm, tm), 1))
    prefix = jnp.dot(lower.astype(BF16), cnt.astype(BF16), preferred_element_type=F32)
    base = carry[...] + prefix
    for k in range(TOP_K):
        rank_ref[:, k:k + 1] = jnp.sum(onehots[k] * base, axis=-1, keepdims=True).astype(jnp.int32)
    carry[...] = carry[...] + jnp.sum(cnt, axis=0, keepdims=True)
    cnt_ref[...] = carry[...].astype(jnp.int32)


def _router(xn, wh, wl, b, tm):
    m, d = xn.shape
    n_e = wh.shape[1]
    tok_spec = pl.BlockSpec((tm, TOP_K), lambda i: (i, 0))
    return pl.pallas_call(
        _router_kernel,
        grid=(m // tm,),
        in_specs=[
            pl.BlockSpec((tm, d), lambda i: (i, 0)),
            pl.BlockSpec((d, n_e), lambda i: (0, 0)),
            pl.BlockSpec((d, n_e), lambda i: (0, 0)),
            pl.BlockSpec((1, n_e), lambda i: (0, 0)),
        ],
        out_specs=[tok_spec, tok_spec, tok_spec, pl.BlockSpec((1, n_e), lambda i: (0, 0))],
        out_shape=[jax.ShapeDtypeStruct((m, TOP_K), jnp.int32), jax.ShapeDtypeStruct((m, TOP_K), F32),
                   jax.ShapeDtypeStruct((m, TOP_K), jnp.int32), jax.ShapeDtypeStruct((1, n_e), jnp.int32)],
        scratch_shapes=[pltpu.VMEM((1, n_e), F32)],
        compiler_params=_cparams("arbitrary"),
        name="router",
    )(xn, wh, wl, b)


def _row_gather_kernel(nused_ref, tok_ref, src_hbm, o_ref, buf, sem, *, rows):
    t = pl.program_id(0)
    n_used = nused_ref[0]

    def issue(tile, slot):
        def body(i, carry):
            pltpu.make_async_copy(src_hbm.at[pl.ds(tok_ref[tile * rows + i], 1)],
                                  buf.at[slot, pl.ds(i, 1)], sem.at[slot]).start()
            return carry
        lax.fori_loop(0, rows, body, 0, unroll=DMA_ISSUE_UNROLL)

    @pl.when(jnp.logical_and(t == 0, n_used > 0))
    def _():
        issue(0, 0)

    @pl.when(t + 1 < n_used)
    def _():
        issue(t + 1, (t + 1) % 2)

    @pl.when(t < n_used)
    def _():
        slot = t % 2
        pltpu.make_async_copy(src_hbm.at[pl.ds(0, rows)], buf.at[slot], sem.at[slot]).wait()
        o_ref[...] = buf[slot].astype(o_ref.dtype)

    @pl.when(t >= n_used)
    def _():
        o_ref[...] = jnp.zeros_like(o_ref)


def _row_gather(n_used, row_idx, src, n_tiles, rows, out_dtype):
    d = src.shape[1]
    grid_spec = pltpu.PrefetchScalarGridSpec(
        num_scalar_prefetch=2,
        grid=(n_tiles,),
        in_specs=[pl.BlockSpec(memory_space=pl.ANY)],
        out_specs=pl.BlockSpec((rows, d), lambda t, *_: (t, 0)),
        scratch_shapes=[pltpu.VMEM((2, rows, d), src.dtype), pltpu.SemaphoreType.DMA((2,))],
    )
    return pl.pallas_call(
        functools.partial(_row_gather_kernel, rows=rows),
        grid_spec=grid_spec,
        out_shape=jax.ShapeDtypeStruct((n_tiles * rows, d), out_dtype),
        compiler_params=_cparams("arbitrary"),
        name="moe_row_gather",
    )(n_used, row_idx, src)


def _expert_changed(te_ref, t):
    prev = te_ref[jnp.maximum(t - 1, 0)]
    return jnp.logical_or(t == 0, te_ref[t] != prev)


def _moe_up_kernel(te_ref, nused_ref, x_ref, wg_ref, wu_ref, bg_ref, bu_ref, o_ref, wg_sc, wu_sc):
    t = pl.program_id(1)

    @pl.when(t < nused_ref[0])
    def _():
        @pl.when(_expert_changed(te_ref, t))
        def _():
            wg_sc[...] = wg_ref[0].astype(BF16)
            wu_sc[...] = wu_ref[0].astype(BF16)

        x = x_ref[...]
        g = jnp.dot(x, wg_sc[...], preferred_element_type=F32) + bg_ref[0]
        u = jnp.dot(x, wu_sc[...], preferred_element_type=F32) + bu_ref[0]
        g = jnp.minimum(g, SWIGLU_LIMIT)
        u = jnp.clip(u, -SWIGLU_LIMIT, SWIGLU_LIMIT)
        o_ref[...] = ((u + 1.0) * (g * jax.nn.sigmoid(SWIGLU_ALPHA * g))).astype(o_ref.dtype)

    @pl.when(t >= nused_ref[0])
    def _():
        o_ref[...] = jnp.zeros_like(o_ref)


def _moe_up(tile_e, n_used, xs, w_gu, b_gu, rows, nc):
    r_pad, d = xs.shape
    n_tiles = r_pad // rows
    d_ff = w_gu.shape[2] // 2
    n_chunks = d_ff // nc
    grid_spec = pltpu.PrefetchScalarGridSpec(
        num_scalar_prefetch=2,
        grid=(n_chunks, n_tiles),
        in_specs=[
            pl.BlockSpec((rows, d), lambda n, t, te, nu: (t, 0)),
            pl.BlockSpec((1, d, nc), lambda n, t, te, nu: (te[t], 0, n)),
            pl.BlockSpec((1, d, nc), lambda n, t, te, nu: (te[t], 0, n_chunks + n)),
            pl.BlockSpec((1, 1, nc), lambda n, t, te, nu: (te[t], 0, n)),
            pl.BlockSpec((1, 1, nc), lambda n, t, te, nu: (te[t], 0, n_chunks + n)),
        ],
        out_specs=pl.BlockSpec((rows, nc), lambda n, t, te, nu: (t, n)),
        scratch_shapes=[pltpu.VMEM((d, nc), BF16), pltpu.VMEM((d, nc), BF16)],
    )
    return pl.pallas_call(
        _moe_up_kernel,
        grid_spec=grid_spec,
        out_shape=jax.ShapeDtypeStruct((r_pad, d_ff), BF16),
        compiler_params=_cparams("arbitrary", "arbitrary"),
        name="moe_up",
    )(tile_e, n_used, xs, w_gu, w_gu, b_gu, b_gu)


def _moe_down_kernel(te_ref, nused_ref, x_ref, w_ref, b_ref, o_ref, w_sc):
    t = pl.program_id(1)

    @pl.when(t < nused_ref[0])
    def _():
        @pl.when(_expert_changed(te_ref, t))
        def _():
            w_sc[...] = w_ref[0].astype(BF16)

        o_ref[...] = jnp.dot(x_ref[...], w_sc[...], preferred_element_type=F32) + b_ref[0]

    @pl.when(t >= nused_ref[0])
    def _():
        o_ref[...] = jnp.zeros_like(o_ref)


def _moe_down(tile_e, n_used, hid, w_dn, b_dn, rows, nc):
    r_pad, d_ff = hid.shape
    n_tiles = r_pad // rows
    d = w_dn.shape[2]
    n_chunks = d // nc
    grid_spec = pltpu.PrefetchScalarGridSpec(
        num_scalar_prefetch=2,
        grid=(n_chunks, n_tiles),
        in_specs=[
            pl.BlockSpec((rows, d_ff), lambda n, t, te, nu: (t, 0)),
            pl.BlockSpec((1, d_ff, nc), lambda n, t, te, nu: (te[t], 0, n)),
            pl.BlockSpec((1, 1, nc), lambda n, t, te, nu: (te[t], 0, n)),
        ],
        out_specs=pl.BlockSpec((rows, nc), lambda n, t, te, nu: (t, n)),
        scratch_shapes=[pltpu.VMEM((d_ff, nc), BF16)],
    )
    return pl.pallas_call(
        _moe_down_kernel,
        grid_spec=grid_spec,
        out_shape=jax.ShapeDtypeStruct((r_pad, d), F32),
        compiler_params=_cparams("arbitrary", "arbitrary"),
        name="moe_down",
    )(tile_e, n_used, hid, w_dn, b_dn)


def _combine_kernel(dest_ref, h_ref, gate_ref, gf_ref, y_hbm, o_ref, buf, sem, *, rows, final_norm):
    t = pl.program_id(0)
    n_t = pl.num_programs(0)
    n_dma = rows * TOP_K

    def issue(tile, slot):
        def body(i, carry):
            pltpu.make_async_copy(y_hbm.at[pl.ds(dest_ref[tile * n_dma + i], 1)],
                                  buf.at[slot, pl.ds(i, 1)], sem.at[slot]).start()
            return carry
        lax.fori_loop(0, n_dma, body, 0, unroll=DMA_ISSUE_UNROLL)

    @pl.when(t == 0)
    def _():
        issue(0, 0)

    @pl.when(t + 1 < n_t)
    def _():
        issue(t + 1, (t + 1) % 2)

    slot = t % 2
    pltpu.make_async_copy(y_hbm.at[pl.ds(0, n_dma)], buf.at[slot], sem.at[slot]).wait()

    acc = h_ref[...]
    for k in range(TOP_K):
        acc = acc + gate_ref[:, k:k + 1] * buf[slot, k * rows:(k + 1) * rows, :]
    o_ref[...] = _rmsnorm_f32(acc, gf_ref[...]) if final_norm else acc


def _combine(dest_km, h1, gate, gf, y, rows, final_norm):
    m, d = h1.shape
    grid_spec = pltpu.PrefetchScalarGridSpec(
        num_scalar_prefetch=1,
        grid=(m // rows,),
        in_specs=[
            pl.BlockSpec((rows, d), lambda t, *_: (t, 0)),
            pl.BlockSpec((rows, TOP_K), lambda t, *_: (t, 0)),
            pl.BlockSpec((1, d), lambda t, *_: (0, 0)),
            pl.BlockSpec(memory_space=pl.ANY),
        ],
        out_specs=pl.BlockSpec((rows, d), lambda t, *_: (t, 0)),
        scratch_shapes=[pltpu.VMEM((2, rows * TOP_K, d), F32), pltpu.SemaphoreType.DMA((2,))],
    )
    return pl.pallas_call(
        functools.partial(_combine_kernel, rows=rows, final_norm=final_norm),
        grid_spec=grid_spec,
        out_shape=jax.ShapeDtypeStruct((m, d), F32),
        compiler_params=_cparams("arbitrary"),
        name="moe_combine",
    )(dest_km, h1, gate, gf, y)


def _routing_tables(top_e, rank, counts, rows, n_tiles):
    m = top_e.shape[0]
    n_experts = counts.shape[0]
    tiles_per_e = (counts + rows - 1) // rows
    tile_end = jnp.cumsum(tiles_per_e)
    pad_start = (tile_end - tiles_per_e) * rows
    experts = jnp.arange(n_experts, dtype=jnp.int32)
    dest = rank + jnp.sum(jnp.where(top_e[..., None] == experts, pad_start, 0), axis=-1)
    dest = dest.astype(jnp.int32).reshape(-1)
    n_used = tile_end[-1:].astype(jnp.int32)
    tile_ids = jnp.arange(n_tiles, dtype=jnp.int32)
    tile_e = jnp.minimum(jnp.sum(tile_ids[:, None] >= tile_end[None, :], axis=1), n_experts - 1).astype(jnp.int32)
    tok = jnp.arange(m * TOP_K, dtype=jnp.int32) // TOP_K
    row_tok = jnp.zeros((n_tiles * rows,), jnp.int32).at[dest].set(tok)
    return dest, row_tok, tile_e, n_used


def _largest_divisor(n, cap, mult):
    best = mult
    for c in range(mult, cap + 1, mult):
        if n % c == 0:
            best = c
    return best


def kernel(x_prompt, x_sample, cache_kv_w128, cache_kv_w512, cache_kv_w2048, state_pool, rel_bias, norm1, w_in,
           w_branch_a, w_branch_b, w_out, w_pool, pool_scale, norm2, router_w, router_b, w_gate_up, b_gate_up,
           w_down, b_down, norm_f):
    batch, seq, d = x_prompt.shape
    db, t_new, _ = x_sample.shape
    depth = norm1.shape[0]
    n_experts = router_w.shape[-1]
    caches = (cache_kv_w128, cache_kv_w512, cache_kv_w2048)
    m_p, m_s = batch * seq, db * t_new
    m = m_p + m_s
    tm = _largest_divisor(m, 640, 8)

    h = jnp.concatenate([x_prompt.reshape(m_p, d), x_sample.reshape(m_s, d)], axis=0)
    bias_p = _prompt_bias_tiles(rel_bias)
    bias_sc, bias_sn = _sample_bias_tiles(rel_bias, t_new)

    cuts = [ATTN_WIDTH, 2 * ATTN_WIDTH, 3 * ATTN_WIDTH, 3 * ATTN_WIDTH + POOL_WIDTH, 3 * ATTN_WIDTH + POOL_WIDTH + d]
    qkv0 = 2 * d
    z0 = qkv0 + 3 * ATTN_WIDTH

    kv_p = [[] for _ in range(N_GROUPS)]
    kv_s = [[] for _ in range(N_GROUPS)]
    pool_p, pool_s = [], []
    for l in range(depth):
        wq, wk, wv, wz, wga, wgb = jnp.split(w_in[l], cuts, axis=-1)
        w_perm = jnp.concatenate([wga, wgb, wq, wk, wv, wz], axis=-1).astype(BF16)
        proj = _inproj(h, norm1[l][None, :], w_perm, tm, _largest_divisor(w_perm.shape[1], 1536, 128))

        k_all = proj[:, qkv0 + ATTN_WIDTH:qkv0 + 2 * ATTN_WIDTH]
        v_all = proj[:, qkv0 + 2 * ATTN_WIDTH:qkv0 + 3 * ATTN_WIDTH]
        z_all = proj[:, z0:z0 + POOL_WIDTH]
        k_p = k_all[:m_p].reshape(batch, seq, N_ATTN_HEADS, HEAD_DIM)
        v_p = v_all[:m_p].reshape(batch, seq, N_ATTN_HEADS, HEAD_DIM)
        k_s = k_all[m_p:].reshape(db, t_new, N_ATTN_HEADS, HEAD_DIM)
        v_s = v_all[m_p:].reshape(db, t_new, N_ATTN_HEADS, HEAD_DIM)
        z_p = z_all[:m_p].reshape(batch, seq, POOL_WIDTH)
        z_s = z_all[m_p:].reshape(db, t_new, POOL_WIDTH)

        attn_p = _prompt_attn(proj, bias_p, batch, seq, qkv0 // HEAD_DIM)
        qkv_s = proj[m_p:, qkv0:qkv0 + 3 * ATTN_WIDTH].reshape(db, t_new, 3 * ATTN_WIDTH)
        c_rows = [caches[g][l].reshape(db, -1, CACHE_ROW_TILE, HEAD_DIM) for g in range(N_GROUPS)]
        c0 = c_rows[0].reshape(db, -1, HEAD_DIM)
        c1 = c_rows[1].reshape(db, -1, HEAD_DIM)
        dil2 = ATTN_GROUPS[2][1]
        c2v = c_rows[2].reshape(db, -1, dil2 * CACHE_ROW_TILE, HEAD_DIM)
        attn_s = _sample_attn(qkv_s, c0, c1, c2v, bias_sc, bias_sn)
        attn = jnp.concatenate([attn_p, attn_s.reshape(m_s, ATTN_OUT_WIDTH)], axis=0)

        news = []
        for g, (window, _) in enumerate(ATTN_GROUPS):
            hsl = slice(g * HEADS_PER_GROUP, (g + 1) * HEADS_PER_GROUP)
            keep = min(window, seq)
            kv_p[g].append(jnp.stack([k_p[:, seq - keep:, hsl], v_p[:, seq - keep:, hsl]], axis=2))
            news.append(jnp.concatenate([k_s[:, :, hsl], v_s[:, :, hsl]], axis=2))
        rolled = _cache_roll(c_rows, news)
        for g in range(N_GROUPS):
            kv_s[g].append(rolled[g].reshape(caches[g][l].shape))

        w_pool_b = w_pool[l].astype(BF16)
        scale = pool_scale[l][None, :]
        zp_ext = jnp.concatenate([jnp.zeros((batch, POOL_HIST, POOL_WIDTH), F32), z_p], axis=1)
        hist = jnp.concatenate([jnp.zeros((db, POOL_HIST - POOL_BUF, POOL_WIDTH), F32), state_pool[l]], axis=1)
        zs_ext = jnp.concatenate([hist, z_s], axis=1)
        pool = jnp.concatenate([
            _pool_mix(zp_ext, w_pool_b, scale, 0).reshape(m_p, POOL_WIDTH),
            _pool_mix(zs_ext, w_pool_b, scale, PAST_LEN).reshape(m_s, POOL_WIDTH),
        ], axis=0)
        pool_p.append(z_p[:, seq - POOL_BUF:])
        pool_s.append(zs_ext[:, POOL_HIST - POOL_BUF + t_new:])

        h1, xn2 = _merge(h, attn, pool, proj, w_branch_a[l].astype(BF16), w_branch_b[l].astype(BF16),
                         w_out[l].astype(BF16), norm2[l][None, :], _largest_divisor(m, 320, 8))

        rw = router_w[l]
        rw_hi = rw.astype(BF16)
        rw_lo = (rw - rw_hi.astype(F32)).astype(BF16)
        top_e, gate, rank, counts = _router(xn2, rw_hi, rw_lo, router_b[l][None, :], tm)
        rows = MOE_ROW_TILE
        n_tiles = (m * TOP_K + n_experts * (rows - 1)) // rows
        dest, row_tok, tile_e, n_used = _routing_tables(top_e, rank, counts[0], rows, n_tiles)
        xs = _row_gather(n_used, row_tok, xn2, n_tiles, rows, BF16)
        hid = _moe_up(tile_e, n_used, xs, w_gate_up[l], b_gate_up[l][:, None, :], rows, 512)
        y = _moe_down(tile_e, n_used, hid, w_down[l], b_down[l][:, None, :], rows, 1024)
        crow = _largest_divisor(m, 128, 8)
        dest_km = dest.reshape(m // crow, crow, TOP_K).transpose(0, 2, 1).reshape(-1)
        h = _combine(dest_km, h1, gate, norm_f[None, :], y, crow, final_norm=(l == depth - 1))

    y_prompt = h[:m_p].reshape(batch, seq, d)
    y_sample = h[m_p:].reshape(db, t_new, d)
    return (y_prompt, y_sample,
            jnp.stack(kv_p[0]), jnp.stack(kv_p[1]), jnp.stack(kv_p[2]), jnp.stack(pool_p),
            jnp.stack(kv_s[0]), jnp.stack(kv_s[1]), jnp.stack(kv_s[2]), jnp.stack(pool_s))
```

```python
import functools
import math

import jax
import jax.numpy as jnp
from jax import lax
from jax.experimental import pallas as pl
from jax.experimental.pallas import tpu as pltpu

F32 = jnp.float32
BF16 = jnp.bfloat16

HEAD_DIM = 128
HEADS_PER_GROUP = 4
ATTN_GROUPS = ((128, 1), (512, 4), (2048, 16))
N_GROUPS = len(ATTN_GROUPS)
N_ATTN_HEADS = N_GROUPS * HEADS_PER_GROUP
ATTN_WIDTH = N_ATTN_HEADS * HEAD_DIM
ATTN_OUT_WIDTH = HEADS_PER_GROUP * HEAD_DIM
ATTN_SCALE = HEAD_DIM ** -0.5
N_BACK = 128
N_REL_BUCKETS = 32
REL_MAX_DIST = 2048
POOL_WINDOWS = (2, 4, 8, 16)
POOL_GROUPS = len(POOL_WINDOWS)
PGW = 128
POOL_WIDTH = POOL_GROUPS * PGW
POOL_BUF = max(POOL_WINDOWS) - 1
POOL_HIST = 16
TOP_K = 4
SWIGLU_LIMIT = 7.0
SWIGLU_ALPHA = 1.702
NORM_EPS = 1e-5
NEG_INF = -1e30
PAST_LEN = 8192
CACHE_ROW_TILE = 2 * HEADS_PER_GROUP

VMEM_LIMIT_BYTES = 56 * 1024 * 1024
MOE_ROW_TILE = 256
DMA_ISSUE_UNROLL = 8


def _cparams(*sem):
    return pltpu.CompilerParams(dimension_semantics=sem, vmem_limit_bytes=VMEM_LIMIT_BYTES)


def _rmsnorm_f32(x, g):
    return x * lax.rsqrt(jnp.mean(x * x, axis=-1, keepdims=True) + NORM_EPS) * g


def _inproj_kernel(x_ref, g_ref, w_ref, o_ref):
    xn = _rmsnorm_f32(x_ref[...], g_ref[...]).astype(BF16)
    o_ref[...] = jnp.dot(xn, w_ref[...], preferred_element_type=F32)


def _inproj(x, g, w, tm, tn):
    m, d = x.shape
    n = w.shape[1]
    return pl.pallas_call(
        _inproj_kernel,
        grid=(n // tn, m // tm),
        in_specs=[
            pl.BlockSpec((tm, d), lambda j, i: (i, 0)),
            pl.BlockSpec((1, d), lambda j, i: (0, 0)),
            pl.BlockSpec((d, tn), lambda j, i: (0, j)),
        ],
        out_specs=pl.BlockSpec((tm, tn), lambda j, i: (i, j)),
        out_shape=jax.ShapeDtypeStruct((m, n), F32),
        compiler_params=_cparams("arbitrary", "arbitrary"),
        name="inproj",
    )(x, g, w)


def _t5_bucket(dist):
    max_exact = N_REL_BUCKETS // 2
    df = jnp.maximum(dist, 1).astype(F32)
    large = max_exact + (jnp.log(df / max_exact) / math.log(REL_MAX_DIST / max_exact)
                         * (N_REL_BUCKETS - max_exact)).astype(jnp.int32)
    large = jnp.minimum(large, N_REL_BUCKETS - 1)
    return jnp.where(dist < max_exact, dist, large)


def _step_bias(rel_bias, g):
    _, dil = ATTN_GROUPS[g]
    buckets = _t5_bucket(jnp.arange(N_BACK + 1, dtype=jnp.int32) * dil)
    tab = rel_bias[buckets].astype(F32)
    return tab[:, g * HEADS_PER_GROUP:(g + 1) * HEADS_PER_GROUP].T


def _prompt_bias_tiles(rel_bias):
    nb = N_BACK
    period = 3 * nb - 1
    tiles = []
    for g in range(N_GROUPS):
        bj = _step_bias(rel_bias, g)
        h = bj.shape[0]
        pad = jnp.full((h, nb - 1), NEG_INF, F32)
        v = jnp.concatenate([pad, bj[:, ::-1], pad], axis=1)
        skew = jnp.tile(v, (1, nb + 1))[:, :nb * (period + 1)].reshape(h, nb, period + 1)
        tiles.append(skew[:, ::-1, :2 * nb])
    return jnp.stack(tiles, axis=1)


def _sample_bias_tiles(rel_bias, t_new):
    tq = jnp.arange(t_new)[:, None]
    outs = []
    new = []
    for g, (window, dil) in enumerate(ATTN_GROUPS):
        bj = _step_bias(rel_bias, g)
        if g < 2:
            p = jnp.arange(window)[None, :]
            dist = window + tq - p
            ok = (dist % dil == 0) & (dist // dil <= N_BACK)
            outs.append(jnp.where(ok[None], bj[:, jnp.clip(dist // dil, 0, N_BACK)], NEG_INF))
        else:
            col = jnp.arange(t_new * N_BACK)[None, :]
            blk, r = col // N_BACK, col % N_BACK
            ok = blk == tq
            outs.append(jnp.where(ok[None], bj[:, jnp.broadcast_to(N_BACK - r, (t_new, t_new * N_BACK))], NEG_INF))
        tk = jnp.arange(t_new)[None, :]
        d = tq - tk
        okn = (d >= 0) & (d % dil == 0) & (d // dil <= N_BACK)
        new.append(jnp.where(okn[None], bj[:, jnp.clip(d // dil, 0, N_BACK)], NEG_INF))
    return outs, jnp.stack(new, axis=0)


def _softmax_block(s, v):
    m = jnp.max(s, axis=-1, keepdims=True)
    p = jnp.exp(s - m)
    den = jnp.sum(p, axis=-1, keepdims=True)
    o = jnp.dot(p.astype(BF16), v, preferred_element_type=F32) / den
    return o, m + jnp.log(den)


def _prompt_attn_kernel(q0, k0, v0, q1, k1, v1, q2, k2, v2, bias_ref, o_ref, o_sc, l_sc, *, seq):
    qkv = ((q0, k0, v0), (q1, k1, v1), (q2, k2, v2))
    nb = N_BACK

    def rows(ref, start, size, dil):
        if dil == 1:
            return ref[pl.ds(start, size), :]
        return ref[pl.ds(start, size, stride=dil), :]

    def put(g, start, dil, o, lse):
        idx = pl.ds(start, nb) if dil == 1 else pl.ds(start, nb, stride=dil)
        o_sc[g, idx, :] = o
        l_sc[g, idx, :] = jnp.broadcast_to(lse, (nb, HEAD_DIM))

    for g, (_, dil) in enumerate(ATTN_GROUPS):
        q_ref, k_ref, v_ref = qkv[g]
        n_blk = seq // (dil * nb)
        span = dil * nb
        for r in range(dil):
            q = rows(q_ref, r, nb, dil).astype(BF16)
            k = rows(k_ref, r, nb, dil).astype(BF16)
            v = rows(v_ref, r, nb, dil).astype(BF16)
            s = lax.dot_general(q, k, (((1,), (1,)), ((), ())), preferred_element_type=F32) * ATTN_SCALE
            s = s + bias_ref[0, g, :, nb:]
            o, lse = _softmax_block(s, v)
            put(g, r, dil, o, lse)

            if n_blk > 1:
                def body(b, carry, q_ref=q_ref, k_ref=k_ref, v_ref=v_ref, g=g, dil=dil, r=r, span=span):
                    q = rows(q_ref, b * span + r, nb, dil).astype(BF16)
                    k = rows(k_ref, (b - 1) * span + r, 2 * nb, dil).astype(BF16)
                    v = rows(v_ref, (b - 1) * span + r, 2 * nb, dil).astype(BF16)
                    s = lax.dot_general(q, k, (((1,), (1,)), ((), ())), preferred_element_type=F32) * ATTN_SCALE
                    s = s + bias_ref[0, g]
                    o, lse = _softmax_block(s, v)
                    put(g, b * span + r, dil, o, lse)
                    return carry

                lax.fori_loop(1, n_blk, body, 0)

    chunk = 256

    def comb(c, carry):
        sl = pl.ds(pl.multiple_of(c * chunk, chunk), chunk)
        l0, l1, l2 = l_sc[0, sl, :], l_sc[1, sl, :], l_sc[2, sl, :]
        mx = jnp.maximum(jnp.maximum(l0, l1), l2)
        e0, e1, e2 = jnp.exp(l0 - mx), jnp.exp(l1 - mx), jnp.exp(l2 - mx)
        num = e0 * o_sc[0, sl, :] + e1 * o_sc[1, sl, :] + e2 * o_sc[2, sl, :]
        o_ref[sl, :] = (num / (e0 + e1 + e2)).astype(o_ref.dtype)
        return carry

    lax.fori_loop(0, seq // chunk, comb, 0)


def _prompt_attn(p, bias, batch, seq, col0):
    def spec(kind, g):
        base = col0 + kind * N_ATTN_HEADS + g * HEADS_PER_GROUP
        return pl.BlockSpec((seq, HEAD_DIM), lambda b, i, base=base: (b, base + i))

    in_specs = []
    for g in range(N_GROUPS):
        in_specs += [spec(0, g), spec(1, g), spec(2, g)]
    in_specs.append(pl.BlockSpec((1, N_GROUPS, N_BACK, 2 * N_BACK), lambda b, i: (i, 0, 0, 0)))
    return pl.pallas_call(
        functools.partial(_prompt_attn_kernel, seq=seq),
        grid=(batch, HEADS_PER_GROUP),
        in_specs=in_specs,
        out_specs=pl.BlockSpec((seq, HEAD_DIM), lambda b, i: (b, i)),
        out_shape=jax.ShapeDtypeStruct((batch * seq, ATTN_OUT_WIDTH), BF16),
        scratch_shapes=[pltpu.VMEM((N_GROUPS, seq, HEAD_DIM), F32), pltpu.VMEM((N_GROUPS, seq, HEAD_DIM), F32)],
        compiler_params=_cparams("arbitrary", "arbitrary"),
        name="prompt_attn",
    )(*([p] * 9), bias)


def _sample_attn_kernel(qkv_ref, c0_ref, c1_ref, c2_ref, b0_ref, b1_ref, b2_ref, bn_ref, o_ref, *, t_new):
    caches = (c0_ref, c1_ref, c2_ref)
    biases = (b0_ref, b1_ref, b2_ref)
    dn = (((1,), (1,)), ((), ()))
    for i in range(HEADS_PER_GROUP):
        outs, lses = [], []
        for g in range(N_GROUPS):
            h = g * HEADS_PER_GROUP + i
            q = qkv_ref[0, :, h * HEAD_DIM:(h + 1) * HEAD_DIM].astype(BF16)
            kn = qkv_ref[0, :, ATTN_WIDTH + h * HEAD_DIM:ATTN_WIDTH + (h + 1) * HEAD_DIM].astype(BF16)
            vn = qkv_ref[0, :, 2 * ATTN_WIDTH + h * HEAD_DIM:2 * ATTN_WIDTH + (h + 1) * HEAD_DIM].astype(BF16)
            c_ref = caches[g]
            if g < 2:
                n_rows = c_ref.shape[1] // CACHE_ROW_TILE
                kc = c_ref[0, pl.ds(i, n_rows, stride=CACHE_ROW_TILE), :].astype(BF16)
                vc = c_ref[0, pl.ds(HEADS_PER_GROUP + i, n_rows, stride=CACHE_ROW_TILE), :].astype(BF16)
            else:
                kc = jnp.concatenate(
                    [c_ref[0, :, t * CACHE_ROW_TILE + i, :] for t in range(t_new)], axis=0).astype(BF16)
                vc = jnp.concatenate(
                    [c_ref[0, :, t * CACHE_ROW_TILE + HEADS_PER_GROUP + i, :] for t in range(t_new)],
                    axis=0).astype(BF16)
            sc = lax.dot_general(q, kc, dn, preferred_element_type=F32) * ATTN_SCALE + biases[g][i]
            sn = lax.dot_general(q, kn, dn, preferred_element_type=F32) * ATTN_SCALE + bn_ref[g, i]
            m = jnp.maximum(jnp.max(sc, axis=-1, keepdims=True), jnp.max(sn, axis=-1, keepdims=True))
            pc = jnp.exp(sc - m)
            pn = jnp.exp(sn - m)
            den = jnp.sum(pc, axis=-1, keepdims=True) + jnp.sum(pn, axis=-1, keepdims=True)
            o = (jnp.dot(pc.astype(BF16), vc, preferred_element_type=F32)
                 + jnp.dot(pn.astype(BF16), vn, preferred_element_type=F32)) / den
            outs.append(o)
            lses.append(m + jnp.log(den))
        mx = jnp.maximum(jnp.maximum(lses[0], lses[1]), lses[2])
        es = [jnp.exp(l - mx) for l in lses]
        num = es[0] * outs[0] + es[1] * outs[1] + es[2] * outs[2]
        o_ref[0, :, i * HEAD_DIM:(i + 1) * HEAD_DIM] = (num / (es[0] + es[1] + es[2])).astype(o_ref.dtype)


def _sample_attn(qkv_s, c0, c1, c2v, bias_c, bias_n):
    db, t_new, _ = qkv_s.shape

    def full(a):
        return pl.BlockSpec(a.shape, lambda b, nd=a.ndim: (0,) * nd)

    return pl.pallas_call(
        functools.partial(_sample_attn_kernel, t_new=t_new),
        grid=(db,),
        in_specs=[
            pl.BlockSpec((1, t_new, qkv_s.shape[-1]), lambda b: (b, 0, 0)),
            pl.BlockSpec((1, c0.shape[1], HEAD_DIM), lambda b: (b, 0, 0)),
            pl.BlockSpec((1, c1.shape[1], HEAD_DIM), lambda b: (b, 0, 0)),
            pl.BlockSpec((1, c2v.shape[1], t_new * CACHE_ROW_TILE, HEAD_DIM), lambda b: (b, 0, 0, 0)),
            full(bias_c[0]), full(bias_c[1]), full(bias_c[2]), full(bias_n),
        ],
        out_specs=pl.BlockSpec((1, t_new, ATTN_OUT_WIDTH), lambda b: (b, 0, 0)),
        out_shape=jax.ShapeDtypeStruct((db, t_new, ATTN_OUT_WIDTH), BF16),
        compiler_params=_cparams("arbitrary"),
        name="sample_attn",
    )(qkv_s, c0, c1, c2v, bias_c[0], bias_c[1], bias_c[2], bias_n)


def _cache_roll_kernel(*refs, n_caches, t_new):
    caches = refs[:n_caches]
    news = refs[n_caches:2 * n_caches]
    outs = refs[2 * n_caches:3 * n_caches]
    for g in range(n_caches):
        rows = caches[g].shape[1]
        outs[g][0, :rows - t_new] = caches[g][0, t_new:]
        outs[g][0, rows - t_new:] = news[g][0]


def _cache_roll(caches, news):
    n = len(caches)
    db, t_new = news[0].shape[:2]

    def per_seq(a):
        return pl.BlockSpec((1,) + a.shape[1:], lambda b: (b, 0, 0, 0))

    return pl.pallas_call(
        functools.partial(_cache_roll_kernel, n_caches=n, t_new=t_new),
        grid=(db,),
        in_specs=[per_seq(c) for c in caches] + [per_seq(a) for a in news],
        out_specs=[per_seq(c) for c in caches],
        out_shape=[jax.ShapeDtypeStruct(c.shape, c.dtype) for c in caches],
        compiler_params=_cparams("arbitrary"),
        name="cache_roll",
    )(*caches, *news)


def _pool_kernel(hist_ref, z_ref, w_ref, s_ref, o_ref, zbuf, *, t_len, pos0, chunk):
    zbuf[0:POOL_HIST, :] = hist_ref[0]
    zbuf[POOL_HIST:, :] = z_ref[...].reshape(t_len, POOL_WIDTH)
    for c in range(t_len // chunk):
        base = POOL_HIST + c * chunk
        for g, w in enumerate(POOL_WINDOWS):
            sl = slice(g * PGW, (g + 1) * PGW)
            cur = zbuf[base:base + chunk, sl]
            tot = cur
            for i in range(1, w):
                tot = tot + zbuf[base - i:base - i + chunk, sl]
            pos = pos0 + c * chunk + lax.broadcasted_iota(jnp.int32, (chunk, PGW), 0)
            cnt = jnp.minimum(w, pos + 1).astype(F32)
            mixed = tot / cnt - cur
            y = jnp.dot(mixed.astype(BF16), w_ref[g], preferred_element_type=F32) * s_ref[:, sl]
            o_ref[0, c * chunk:(c + 1) * chunk, sl] = y.astype(o_ref.dtype)


def _pool_mix(hist, z, z_spec, n_seq, t_len, w_pool, pool_scale, pos0):
    chunk = min(t_len, 256)
    return pl.pallas_call(
        functools.partial(_pool_kernel, t_len=t_len, pos0=pos0, chunk=chunk),
        grid=(n_seq,),
        in_specs=[
            pl.BlockSpec((1, POOL_HIST, POOL_WIDTH), lambda i: (i, 0, 0)),
            z_spec,
            pl.BlockSpec(w_pool.shape, lambda i: (0, 0, 0)),
            pl.BlockSpec((1, POOL_WIDTH), lambda i: (0, 0)),
        ],
        out_specs=pl.BlockSpec((1, t_len, POOL_WIDTH), lambda i: (i, 0, 0)),
        out_shape=jax.ShapeDtypeStruct((n_seq, t_len, POOL_WIDTH), BF16),
        scratch_shapes=[pltpu.VMEM((POOL_HIST + t_len, POOL_WIDTH), F32)],
        compiler_params=_cparams("arbitrary"),
        name="pool_mix",
    )(hist, z, w_pool, pool_scale)


def _merge_kernel(h_ref, a_ref, p_ref, ga_ref, gb_ref, wa_ref, wb_ref, wo_ref, g2_ref, h1_ref, xn_ref):
    ya = jnp.dot(a_ref[...], wa_ref[...], preferred_element_type=F32)
    yb = jnp.dot(p_ref[...], wb_ref[...], preferred_element_type=F32)
    u = jax.nn.sigmoid(ga_ref[...]) * ya + jax.nn.sigmoid(gb_ref[...]) * yb
    h1 = h_ref[...] + jnp.dot(u.astype(BF16), wo_ref[...], preferred_element_type=F32)
    h1_ref[...] = h1
    xn_ref[...] = _rmsnorm_f32(h1, g2_ref[...])


def _merge(h, attn, pool, proj, wa, wb, wo, g2, tm):
    m, d = h.shape

    def const(a):
        return pl.BlockSpec(a.shape, lambda i, nd=a.ndim: (0,) * nd, pipeline_mode=pl.Buffered(1))

    return pl.pallas_call(
        _merge_kernel,
        grid=(m // tm,),
        in_specs=[
            pl.BlockSpec((tm, d), lambda i: (i, 0)),
            pl.BlockSpec((tm, attn.shape[1]), lambda i: (i, 0)),
            pl.BlockSpec((tm, pool.shape[1]), lambda i: (i, 0)),
            pl.BlockSpec((tm, d), lambda i: (i, 0)),
            pl.BlockSpec((tm, d), lambda i: (i, 1)),
            const(wa), const(wb), const(wo), const(g2),
        ],
        out_specs=[pl.BlockSpec((tm, d), lambda i: (i, 0)), pl.BlockSpec((tm, d), lambda i: (i, 0))],
        out_shape=[jax.ShapeDtypeStruct((m, d), F32), jax.ShapeDtypeStruct((m, d), F32)],
        compiler_params=_cparams("arbitrary"),
        name="merge",
    )(h, attn, pool, proj, proj, wa, wb, wo, g2)


def _router_kernel(x_ref, wh_ref, wl_ref, b_ref, e_ref, gate_ref, rank_ref, cnt_ref, carry):
    @pl.when(pl.program_id(0) == 0)
    def _():
        carry[...] = jnp.zeros_like(carry)

    x = x_ref[...]
    xh = x.astype(BF16)
    xl = (x - xh.astype(F32)).astype(BF16)
    logits = (jnp.dot(xh, wh_ref[...], preferred_element_type=F32)
              + jnp.dot(xh, wl_ref[...], preferred_element_type=F32)
              + jnp.dot(xl, wh_ref[...], preferred_element_type=F32)) + b_ref[...]
    n_e = logits.shape[-1]
    lane = lax.broadcasted_iota(jnp.int32, logits.shape, 1)
    vals = logits
    tops, idxs = [], []
    for _ in range(TOP_K):
        m = jnp.max(vals, axis=-1, keepdims=True)
        idx = jnp.min(jnp.where(vals == m, lane, n_e), axis=-1, keepdims=True)
        tops.append(m)
        idxs.append(idx)
        vals = jnp.where(lane == idx, -jnp.inf, vals)
    ex = [jnp.exp(t - tops[0]) for t in tops]
    den = ex[0] + ex[1] + ex[2] + ex[3]
    for k in range(TOP_K):
        e_ref[:, k:k + 1] = idxs[k]
        gate_ref[:, k:k + 1] = ex[k] / den

    onehots = [(lane == idx).astype(F32) for idx in idxs]
    cnt = onehots[0] + onehots[1] + onehots[2] + onehots[3]
    tm = cnt.shape[0]
    lower = (lax.broadcasted_iota(jnp.int32, (tm, tm), 0) > lax.broadcasted_iota(jnp.int32, (tm, tm), 1))
    prefix = jnp.dot(lower.astype(BF16), cnt.astype(BF16), preferred_element_type=F32)
    base = carry[...] + prefix
    for k in range(TOP_K):
        rank_ref[:, k:k + 1] = jnp.sum(onehots[k] * base, axis=-1, keepdims=True).astype(jnp.int32)
    carry[...] = carry[...] + jnp.sum(cnt, axis=0, keepdims=True)
    cnt_ref[...] = carry[...].astype(jnp.int32)


def _router(xn, wh, wl, b, tm):
    m, d = xn.shape
    n_e = wh.shape[1]
    tok_spec = pl.BlockSpec((tm, TOP_K), lambda i: (i, 0))
    return pl.pallas_call(
        _router_kernel,
        grid=(m // tm,),
        in_specs=[
            pl.BlockSpec((tm, d), lambda i: (i, 0)),
            pl.BlockSpec((d, n_e), lambda i: (0, 0)),
            pl.BlockSpec((d, n_e), lambda i: (0, 0)),
            pl.BlockSpec((1, n_e), lambda i: (0, 0)),
        ],
        out_specs=[tok_spec, tok_spec, tok_spec, pl.BlockSpec((1, n_e), lambda i: (0, 0))],
        out_shape=[jax.ShapeDtypeStruct((m, TOP_K), jnp.int32), jax.ShapeDtypeStruct((m, TOP_K), F32),
                   jax.ShapeDtypeStruct((m, TOP_K), jnp.int32), jax.ShapeDtypeStruct((1, n_e), jnp.int32)],
        scratch_shapes=[pltpu.VMEM((1, n_e), F32)],
        compiler_params=_cparams("arbitrary"),
        name="router",
    )(xn, wh, wl, b)


def _row_gather_kernel(nused_ref, tok_ref, src_hbm, o_ref, buf, sem, *, rows):
    t = pl.program_id(0)
    n_used = nused_ref[0]

    def issue(tile, slot):
        def body(i, carry):
            pltpu.make_async_copy(src_hbm.at[pl.ds(tok_ref[tile * rows + i], 1)],
                                  buf.at[slot, pl.ds(i, 1)], sem.at[slot]).start()
            return carry
        lax.fori_loop(0, rows, body, 0, unroll=DMA_ISSUE_UNROLL)

    @pl.when(jnp.logical_and(t == 0, n_used > 0))
    def _():
        issue(0, 0)

    @pl.when(t + 1 < n_used)
    def _():
        issue(t + 1, (t + 1) % 2)

    @pl.when(t < n_used)
    def _():
        slot = t % 2
        pltpu.make_async_copy(src_hbm.at[pl.ds(0, rows)], buf.at[slot], sem.at[slot]).wait()
        o_ref[...] = buf[slot].astype(o_ref.dtype)

    @pl.when(t >= n_used)
    def _():
        o_ref[...] = jnp.zeros_like(o_ref)


def _row_gather(n_used, row_idx, src, n_tiles, rows, out_dtype):
    d = src.shape[1]
    grid_spec = pltpu.PrefetchScalarGridSpec(
        num_scalar_prefetch=2,
        grid=(n_tiles,),
        in_specs=[pl.BlockSpec(memory_space=pl.ANY)],
        out_specs=pl.BlockSpec((rows, d), lambda t, *_: (t, 0)),
        scratch_shapes=[pltpu.VMEM((2, rows, d), src.dtype), pltpu.SemaphoreType.DMA((2,))],
    )
    return pl.pallas_call(
        functools.partial(_row_gather_kernel, rows=rows),
        grid_spec=grid_spec,
        out_shape=jax.ShapeDtypeStruct((n_tiles * rows, d), out_dtype),
        compiler_params=_cparams("arbitrary"),
        name="moe_row_gather",
    )(n_used, row_idx, src)


def _expert_chunk_loop(n_chunks, in_copy, out_copy, compute, prepare):
    @pl.when(n_chunks > 0)
    def _():
        in_copy(0, 0).start()
        prepare()

        def body(r, carry):
            slot = r % 2

            @pl.when(r + 1 < n_chunks)
            def _():
                in_copy(r + 1, 1 - slot).start()

            in_copy(r, slot).wait()

            @pl.when(r >= 2)
            def _():
                out_copy(r - 2, slot).wait()

            compute(slot)
            out_copy(r, slot).start()
            return carry

        lax.fori_loop(0, n_chunks, body, 0)

        @pl.when(n_chunks >= 2)
        def _():
            out_copy(n_chunks - 2, n_chunks % 2).wait()

        out_copy(n_chunks - 1, (n_chunks - 1) % 2).wait()


def _zero_fill_tiles(first, last, zbuf, dst, sem):
    zbuf[...] = jnp.zeros_like(zbuf)

    def copy(t):
        return pltpu.make_async_copy(zbuf, dst(t), sem)

    def start(t, carry):
        copy(t).start()
        return carry

    def wait(t, carry):
        copy(t).wait()
        return carry

    lax.fori_loop(first, last, start, 0)
    lax.fori_loop(first, last, wait, 0)


def _moe_up_kernel(tpe_ref, start_ref, xs_hbm, wg_ref, wu_ref, bg_ref, bu_ref, hid_hbm,
                   wg_sc, wu_sc, xbuf, obuf, in_sem, out_sem, *, rows):
    e = pl.program_id(0)
    n = pl.program_id(1)
    row0 = start_ref[e]

    def chunk_rows(r):
        return pl.ds(pl.multiple_of(row0 + r * rows, rows), rows)

    def in_copy(r, slot):
        return pltpu.make_async_copy(xs_hbm.at[chunk_rows(r)], xbuf.at[slot], in_sem.at[slot])

    def out_copy(r, slot):
        return pltpu.make_async_copy(obuf.at[slot], hid_hbm.at[n, chunk_rows(r)], out_sem.at[slot])

    def cast_weights():
        wg_sc[...] = wg_ref[0].astype(BF16)
        wu_sc[...] = wu_ref[0].astype(BF16)

    def compute(slot):
        x = xbuf[slot]
        g = jnp.dot(x, wg_sc[...], preferred_element_type=F32) + bg_ref[0]
        u = jnp.dot(x, wu_sc[...], preferred_element_type=F32) + bu_ref[0]
        g = jnp.minimum(g, SWIGLU_LIMIT)
        u = jnp.clip(u, -SWIGLU_LIMIT, SWIGLU_LIMIT)
        obuf[slot] = ((u + 1.0) * (g * jax.nn.sigmoid(SWIGLU_ALPHA * g))).astype(obuf.dtype)

    _expert_chunk_loop(tpe_ref[e], in_copy, out_copy, compute, cast_weights)

    @pl.when(e == pl.num_programs(0) - 1)
    def _():
        _zero_fill_tiles(row0 // rows + tpe_ref[e], hid_hbm.shape[1] // rows, obuf.at[0],
                         lambda t: hid_hbm.at[n, pl.ds(pl.multiple_of(t * rows, rows), rows)], out_sem.at[0])


def _moe_up(tiles_per_e, row_start, xs, w_gu, b_gu, rows, nc):
    r_pad, d = xs.shape
    n_e = w_gu.shape[0]
    d_ff = w_gu.shape[2] // 2
    n_chunks = d_ff // nc
    any_spec = pl.BlockSpec(memory_space=pl.ANY)
    grid_spec = pltpu.PrefetchScalarGridSpec(
        num_scalar_prefetch=2,
        grid=(n_e, n_chunks),
        in_specs=[
            any_spec,
            pl.BlockSpec((1, d, nc), lambda e, n, *_: (e, 0, n)),
            pl.BlockSpec((1, d, nc), lambda e, n, *_: (e, 0, n_chunks + n)),
            pl.BlockSpec((1, 1, nc), lambda e, n, *_: (e, 0, n)),
            pl.BlockSpec((1, 1, nc), lambda e, n, *_: (e, 0, n_chunks + n)),
        ],
        out_specs=any_spec,
        scratch_shapes=[pltpu.VMEM((d, nc), BF16), pltpu.VMEM((d, nc), BF16),
                        pltpu.VMEM((2, rows, d), xs.dtype), pltpu.VMEM((2, rows, nc), BF16),
                        pltpu.SemaphoreType.DMA((2,)), pltpu.SemaphoreType.DMA((2,))],
    )
    return pl.pallas_call(
        functools.partial(_moe_up_kernel, rows=rows),
        grid_spec=grid_spec,
        out_shape=jax.ShapeDtypeStruct((n_chunks, r_pad, nc), BF16),
        compiler_params=_cparams("arbitrary", "arbitrary"),
        name="moe_up",
    )(tiles_per_e, row_start, xs, w_gu, w_gu, b_gu, b_gu)


def _moe_down_kernel(tpe_ref, start_ref, hid_hbm, w_ref, b_ref, y_hbm, w_sc, xbuf, obuf, in_sem, out_sem, *, rows):
    e = pl.program_id(0)
    n = pl.program_id(1)
    row0 = start_ref[e]
    k_chunks, _, kc = xbuf.shape[1:]
    nc = obuf.shape[-1]

    def chunk_rows(r):
        return pl.ds(pl.multiple_of(row0 + r * rows, rows), rows)

    def in_copy(r, slot):
        return pltpu.make_async_copy(hid_hbm.at[:, chunk_rows(r)], xbuf.at[slot], in_sem.at[slot])

    def out_copy(r, slot):
        cols = pl.ds(pl.multiple_of(n * nc, nc), nc)
        return pltpu.make_async_copy(obuf.at[slot], y_hbm.at[chunk_rows(r), cols], out_sem.at[slot])

    def cast_weights():
        w_sc[...] = w_ref[0].astype(BF16)

    def compute(slot):
        acc = b_ref[0] + jnp.dot(xbuf[slot, 0], w_sc[0:kc, :], preferred_element_type=F32)
        for c in range(1, k_chunks):
            acc = acc + jnp.dot(xbuf[slot, c], w_sc[c * kc:(c + 1) * kc, :], preferred_element_type=F32)
        obuf[slot] = acc

    _expert_chunk_loop(tpe_ref[e], in_copy, out_copy, compute, cast_weights)

    @pl.when(e == pl.num_programs(0) - 1)
    def _():
        cols = pl.ds(pl.multiple_of(n * nc, nc), nc)
        _zero_fill_tiles(row0 // rows + tpe_ref[e], y_hbm.shape[0] // rows, obuf.at[0],
                         lambda t: y_hbm.at[pl.ds(pl.multiple_of(t * rows, rows), rows), cols], out_sem.at[0])


def _moe_down(tiles_per_e, row_start, hid, w_dn, b_dn, rows, nc):
    k_chunks, r_pad, kc = hid.shape
    n_e, d_ff, d = w_dn.shape
    n_chunks = d // nc
    any_spec = pl.BlockSpec(memory_space=pl.ANY)
    grid_spec = pltpu.PrefetchScalarGridSpec(
        num_scalar_prefetch=2,
        grid=(n_e, n_chunks),
        in_specs=[
            any_spec,
            pl.BlockSpec((1, d_ff, nc), lambda e, n, *_: (e, 0, n)),
            pl.BlockSpec((1, 1, nc), lambda e, n, *_: (e, 0, n)),
        ],
        out_specs=any_spec,
        scratch_shapes=[pltpu.VMEM((d_ff, nc), BF16),
                        pltpu.VMEM((2, k_chunks, rows, kc), hid.dtype), pltpu.VMEM((2, rows, nc), F32),
                        pltpu.SemaphoreType.DMA((2,)), pltpu.SemaphoreType.DMA((2,))],
    )
    return pl.pallas_call(
        functools.partial(_moe_down_kernel, rows=rows),
        grid_spec=grid_spec,
        out_shape=jax.ShapeDtypeStruct((r_pad, d), F32),
        compiler_params=_cparams("arbitrary", "arbitrary"),
        name="moe_down",
    )(tiles_per_e, row_start, hid, w_dn, b_dn)


def _combine_kernel(dest_ref, h_ref, gate_ref, gf_ref, y_hbm, o_ref, buf, sem, *, rows, final_norm):
    t = pl.program_id(0)
    n_t = pl.num_programs(0)
    n_dma = rows * TOP_K

    def issue(tile, slot):
        def body(i, carry):
            pltpu.make_async_copy(y_hbm.at[pl.ds(dest_ref[tile * n_dma + i], 1)],
                                  buf.at[slot, pl.ds(i, 1)], sem.at[slot]).start()
            return carry
        lax.fori_loop(0, n_dma, body, 0, unroll=DMA_ISSUE_UNROLL)

    @pl.when(t == 0)
    def _():
        issue(0, 0)

    @pl.when(t + 1 < n_t)
    def _():
        issue(t + 1, (t + 1) % 2)

    slot = t % 2
    pltpu.make_async_copy(y_hbm.at[pl.ds(0, n_dma)], buf.at[slot], sem.at[slot]).wait()

    acc = h_ref[...]
    for k in range(TOP_K):
        acc = acc + gate_ref[:, k:k + 1] * buf[slot, k * rows:(k + 1) * rows, :]
    o_ref[...] = _rmsnorm_f32(acc, gf_ref[...]) if final_norm else acc


def _combine(dest_km, h1, gate, gf, y, rows, final_norm):
    m, d = h1.shape
    grid_spec = pltpu.PrefetchScalarGridSpec(
        num_scalar_prefetch=1,
        grid=(m // rows,),
        in_specs=[
            pl.BlockSpec((rows, d), lambda t, *_: (t, 0)),
            pl.BlockSpec((rows, TOP_K), lambda t, *_: (t, 0)),
            pl.BlockSpec((1, d), lambda t, *_: (0, 0)),
            pl.BlockSpec(memory_space=pl.ANY),
        ],
        out_specs=pl.BlockSpec((rows, d), lambda t, *_: (t, 0)),
        scratch_shapes=[pltpu.VMEM((2, rows * TOP_K, d), F32), pltpu.SemaphoreType.DMA((2,))],
    )
    return pl.pallas_call(
        functools.partial(_combine_kernel, rows=rows, final_norm=final_norm),
        grid_spec=grid_spec,
        out_shape=jax.ShapeDtypeStruct((m, d), F32),
        compiler_params=_cparams("arbitrary"),
        name="moe_combine",
    )(dest_km, h1, gate, gf, y)


def _routing_tables(top_e, rank, counts, rows, n_tiles):
    m = top_e.shape[0]
    n_experts = counts.shape[0]
    tiles_per_e = (counts + rows - 1) // rows
    tile_end = jnp.cumsum(tiles_per_e)
    pad_start = (tile_end - tiles_per_e) * rows
    experts = jnp.arange(n_experts, dtype=jnp.int32)
    dest = rank + jnp.sum(jnp.where(top_e[..., None] == experts, pad_start, 0), axis=-1)
    dest = dest.astype(jnp.int32).reshape(-1)
    n_used = tile_end[-1:].astype(jnp.int32)
    tok = jnp.arange(m * TOP_K, dtype=jnp.int32) // TOP_K
    row_tok = jnp.zeros((n_tiles * rows,), jnp.int32).at[dest].set(tok)
    return dest, row_tok, tiles_per_e.astype(jnp.int32), pad_start.astype(jnp.int32), n_used


def _largest_divisor(n, cap, mult):
    best = mult
    for c in range(mult, cap + 1, mult):
        if n % c == 0:
            best = c
    return best


def kernel(x_prompt, x_sample, cache_kv_w128, cache_kv_w512, cache_kv_w2048, state_pool, rel_bias, norm1, w_in,
           w_branch_a, w_branch_b, w_out, w_pool, pool_scale, norm2, router_w, router_b, w_gate_up, b_gate_up,
           w_down, b_down, norm_f):
    batch, seq, d = x_prompt.shape
    db, t_new, _ = x_sample.shape
    depth = norm1.shape[0]
    n_experts = router_w.shape[-1]
    caches = (cache_kv_w128, cache_kv_w512, cache_kv_w2048)
    m_p, m_s = batch * seq, db * t_new
    m = m_p + m_s
    tm = _largest_divisor(m, 640, 8)

    h = jnp.concatenate([x_prompt.reshape(m_p, d), x_sample.reshape(m_s, d)], axis=0)
    bias_p = _prompt_bias_tiles(rel_bias)
    bias_sc, bias_sn = _sample_bias_tiles(rel_bias, t_new)

    cuts = [ATTN_WIDTH, 2 * ATTN_WIDTH, 3 * ATTN_WIDTH, 3 * ATTN_WIDTH + POOL_WIDTH, 3 * ATTN_WIDTH + POOL_WIDTH + d]
    qkv0 = 2 * d
    z0 = qkv0 + 3 * ATTN_WIDTH

    kv_p = [[] for _ in range(N_GROUPS)]
    kv_s = [[] for _ in range(N_GROUPS)]
    pool_p, pool_s = [], []
    for l in range(depth):
        wq, wk, wv, wz, wga, wgb = jnp.split(w_in[l], cuts, axis=-1)
        w_perm = jnp.concatenate([wga, wgb, wq, wk, wv, wz], axis=-1).astype(BF16)
        proj = _inproj(h, norm1[l][None, :], w_perm, tm, _largest_divisor(w_perm.shape[1], 1536, 128))

        k0, v0 = qkv0 + ATTN_WIDTH, qkv0 + 2 * ATTN_WIDTH
        kv_w = HEADS_PER_GROUP * HEAD_DIM
        proj_s = proj[m_p:]
        z_s = proj_s[:, z0:z0 + POOL_WIDTH].reshape(db, t_new, POOL_WIDTH)

        attn_p = _prompt_attn(proj, bias_p, batch, seq, qkv0 // HEAD_DIM)
        qkv_s = proj_s[:, qkv0:qkv0 + 3 * ATTN_WIDTH].reshape(db, t_new, 3 * ATTN_WIDTH)
        c_rows = [caches[g][l].reshape(db, -1, CACHE_ROW_TILE, HEAD_DIM) for g in range(N_GROUPS)]
        c0 = c_rows[0].reshape(db, -1, HEAD_DIM)
        c1 = c_rows[1].reshape(db, -1, HEAD_DIM)
        dil2 = ATTN_GROUPS[2][1]
        c2v = c_rows[2].reshape(db, -1, dil2 * CACHE_ROW_TILE, HEAD_DIM)
        attn_s = _sample_attn(qkv_s, c0, c1, c2v, bias_sc, bias_sn)
        attn = jnp.concatenate([attn_p, attn_s.reshape(m_s, ATTN_OUT_WIDTH)], axis=0)

        news = []
        for g, (window, _) in enumerate(ATTN_GROUPS):
            keep = min(window, seq)
            kc, vc = k0 + g * kv_w, v0 + g * kv_w

            def kv_rows(a, r0, r1, kc=kc, vc=vc):
                return jnp.stack([a[r0:r1, kc:kc + kv_w].reshape(-1, HEADS_PER_GROUP, HEAD_DIM),
                                  a[r0:r1, vc:vc + kv_w].reshape(-1, HEADS_PER_GROUP, HEAD_DIM)], axis=1)

            kv_p[g].append(jnp.stack([kv_rows(proj, (b + 1) * seq - keep, (b + 1) * seq) for b in range(batch)]))
            news.append(kv_rows(proj_s, 0, m_s).reshape(db, t_new, CACHE_ROW_TILE, HEAD_DIM))
        rolled = _cache_roll(c_rows, news)
        for g in range(N_GROUPS):
            kv_s[g].append(rolled[g].reshape(caches[g][l].shape))

        w_pool_b = w_pool[l].astype(BF16)
        scale = pool_scale[l][None, :]
        hist_p = jnp.zeros((batch, POOL_HIST, POOL_WIDTH), F32)
        hist_s = jnp.concatenate([jnp.zeros((db, POOL_HIST - POOL_BUF, POOL_WIDTH), F32), state_pool[l]], axis=1)
        pool = jnp.concatenate([
            _pool_mix(hist_p, proj, pl.BlockSpec((seq, POOL_WIDTH), lambda i: (i, z0 // POOL_WIDTH)),
                      batch, seq, w_pool_b, scale, 0).reshape(m_p, POOL_WIDTH),
            _pool_mix(hist_s, z_s, pl.BlockSpec((1, t_new, POOL_WIDTH), lambda i: (i, 0, 0)),
                      db, t_new, w_pool_b, scale, PAST_LEN).reshape(m_s, POOL_WIDTH),
        ], axis=0)
        pool_p.append(jnp.stack([proj[(b + 1) * seq - POOL_BUF:(b + 1) * seq, z0:z0 + POOL_WIDTH]
                                 for b in range(batch)]))
        pool_s.append(jnp.concatenate([state_pool[l], z_s], axis=1)[:, t_new:])

        h1, xn2 = _merge(h, attn, pool, proj, w_branch_a[l].astype(BF16), w_branch_b[l].astype(BF16),
                         w_out[l].astype(BF16), norm2[l][None, :], _largest_divisor(m, 320, 8))

        rw = router_w[l]
        rw_hi = rw.astype(BF16)
        rw_lo = (rw - rw_hi.astype(F32)).astype(BF16)
        top_e, gate, rank, counts = _router(xn2, rw_hi, rw_lo, router_b[l][None, :], tm)
        rows = MOE_ROW_TILE
        n_tiles = (m * TOP_K + n_experts * (rows - 1)) // rows
        dest, row_tok, tiles_per_e, row_start, n_used = _routing_tables(top_e, rank, counts[0], rows, n_tiles)
        xs = _row_gather(n_used, row_tok, xn2, n_tiles, rows, BF16)
        hid = _moe_up(tiles_per_e, row_start, xs, w_gate_up[l], b_gate_up[l][:, None, :], rows, 512)
        y = _moe_down(tiles_per_e, row_start, hid, w_down[l], b_down[l][:, None, :], rows, 1024)
        crow = _largest_divisor(m, 128, 8)
        dest_km = dest.reshape(m // crow, crow, TOP_K).transpose(0, 2, 1).reshape(-1)
        h = _combine(dest_km, h1, gate, norm_f[None, :], y, crow, final_norm=(l == depth - 1))

    y_prompt = h[:m_p].reshape(batch, seq, d)
    y_sample = h[m_p:].reshape(db, t_new, d)
    return (y_prompt, y_sample,
            jnp.stack(kv_p[0]), jnp.stack(kv_p[1]), jnp.stack(kv_p[2]), jnp.stack(pool_p),
            jnp.stack(kv_s[0]), jnp.stack(kv_s[1]), jnp.stack(kv_s[2]), jnp.stack(pool_s))
```

```python
import functools
import math

import jax
import jax.numpy as jnp
from jax import lax
from jax.experimental import pallas as pl
from jax.experimental.pallas import tpu as pltpu

F32 = jnp.float32
BF16 = jnp.bfloat16

HEAD_DIM = 128
HEADS_PER_GROUP = 4
ATTN_GROUPS = ((128, 1), (512, 4), (2048, 16))
N_GROUPS = len(ATTN_GROUPS)
N_ATTN_HEADS = N_GROUPS * HEADS_PER_GROUP
ATTN_WIDTH = N_ATTN_HEADS * HEAD_DIM
ATTN_OUT_WIDTH = HEADS_PER_GROUP * HEAD_DIM
ATTN_SCALE = HEAD_DIM ** -0.5
N_BACK = 128
N_REL_BUCKETS = 32
REL_MAX_DIST = 2048
POOL_WINDOWS = (2, 4, 8, 16)
POOL_GROUPS = len(POOL_WINDOWS)
PGW = 128
POOL_WIDTH = POOL_GROUPS * PGW
POOL_BUF = max(POOL_WINDOWS) - 1
POOL_HIST = 16
TOP_K = 4
SWIGLU_LIMIT = 7.0
SWIGLU_ALPHA = 1.702
NORM_EPS = 1e-5
NEG_INF = -1e30
PAST_LEN = 8192
CACHE_ROW_TILE = 2 * HEADS_PER_GROUP

VMEM_LIMIT_BYTES = 56 * 1024 * 1024
MOE_ROW_TILE = 256
DMA_ISSUE_UNROLL = 8
DMA_SPLIT = 8
CHUNK_DMA_SPLIT = 4


def _cparams(*sem):
    return pltpu.CompilerParams(dimension_semantics=sem, vmem_limit_bytes=VMEM_LIMIT_BYTES)


def _rmsnorm_f32(x, g):
    return x * lax.rsqrt(jnp.mean(x * x, axis=-1, keepdims=True) + NORM_EPS) * g


def _inproj_kernel(*refs):
    x_refs, (g_ref, w_ref, o_ref) = refs[:CHUNK_DMA_SPLIT], refs[CHUNK_DMA_SPLIT:]
    xs = [r[...] for r in x_refs]
    kc = xs[0].shape[1]
    sq = xs[0] * xs[0]
    for x in xs[1:]:
        sq = sq + x * x
    inv = lax.rsqrt(jnp.sum(sq, axis=-1, keepdims=True) / (kc * len(xs)) + NORM_EPS)
    acc = None
    for k, x in enumerate(xs):
        xn = (x * inv * g_ref[:, k * kc:(k + 1) * kc]).astype(BF16)
        part = jnp.dot(xn, w_ref[k * kc:(k + 1) * kc, :], preferred_element_type=F32)
        acc = part if acc is None else acc + part
    o_ref[...] = acc


def _inproj(x, g, w, tm, tn):
    m, d = x.shape
    n = w.shape[1]
    kc = d // CHUNK_DMA_SPLIT
    return pl.pallas_call(
        _inproj_kernel,
        grid=(n // tn, m // tm),
        in_specs=[pl.BlockSpec((tm, kc), lambda j, i, k=k: (i, k)) for k in range(CHUNK_DMA_SPLIT)] + [
            pl.BlockSpec((1, d), lambda j, i: (0, 0)),
            pl.BlockSpec((d, tn), lambda j, i: (0, j)),
        ],
        out_specs=pl.BlockSpec((tm, tn), lambda j, i: (i, j)),
        out_shape=jax.ShapeDtypeStruct((m, n), F32),
        compiler_params=_cparams("arbitrary", "arbitrary"),
        name="inproj",
    )(*([x] * CHUNK_DMA_SPLIT), g, w)


def _t5_bucket(dist):
    max_exact = N_REL_BUCKETS // 2
    df = jnp.maximum(dist, 1).astype(F32)
    large = max_exact + (jnp.log(df / max_exact) / math.log(REL_MAX_DIST / max_exact)
                         * (N_REL_BUCKETS - max_exact)).astype(jnp.int32)
    large = jnp.minimum(large, N_REL_BUCKETS - 1)
    return jnp.where(dist < max_exact, dist, large)


def _step_bias(rel_bias, g):
    _, dil = ATTN_GROUPS[g]
    buckets = _t5_bucket(jnp.arange(N_BACK + 1, dtype=jnp.int32) * dil)
    tab = rel_bias[buckets].astype(F32)
    return tab[:, g * HEADS_PER_GROUP:(g + 1) * HEADS_PER_GROUP].T


def _prompt_bias_tiles(rel_bias):
    nb = N_BACK
    period = 3 * nb - 1
    tiles = []
    for g in range(N_GROUPS):
        bj = _step_bias(rel_bias, g)
        h = bj.shape[0]
        pad = jnp.full((h, nb - 1), NEG_INF, F32)
        v = jnp.concatenate([pad, bj[:, ::-1], pad], axis=1)
        skew = jnp.tile(v, (1, nb + 1))[:, :nb * (period + 1)].reshape(h, nb, period + 1)
        tiles.append(skew[:, ::-1, :2 * nb])
    return jnp.stack(tiles, axis=1)


def _sample_bias_tiles(rel_bias, t_new):
    tq = jnp.arange(t_new)[:, None]
    outs = []
    new = []
    for g, (window, dil) in enumerate(ATTN_GROUPS):
        bj = _step_bias(rel_bias, g)
        if g < 2:
            p = jnp.arange(window)[None, :]
            dist = window + tq - p
            ok = (dist % dil == 0) & (dist // dil <= N_BACK)
            outs.append(jnp.where(ok[None], bj[:, jnp.clip(dist // dil, 0, N_BACK)], NEG_INF))
        else:
            col = jnp.arange(t_new * N_BACK)[None, :]
            blk, r = col // N_BACK, col % N_BACK
            ok = blk == tq
            outs.append(jnp.where(ok[None], bj[:, jnp.broadcast_to(N_BACK - r, (t_new, t_new * N_BACK))], NEG_INF))
        tk = jnp.arange(t_new)[None, :]
        d = tq - tk
        okn = (d >= 0) & (d % dil == 0) & (d // dil <= N_BACK)
        new.append(jnp.where(okn[None], bj[:, jnp.clip(d // dil, 0, N_BACK)], NEG_INF))
    return outs, jnp.stack(new, axis=0)


def _softmax_block(s, v):
    m = jnp.max(s, axis=-1, keepdims=True)
    p = jnp.exp(s - m)
    den = jnp.sum(p, axis=-1, keepdims=True)
    o = jnp.dot(p.astype(BF16), v, preferred_element_type=F32) / den
    return o, m + jnp.log(den)


def _prompt_attn_kernel(q0, k0, v0, q1, k1, v1, q2, k2, v2, bias_ref, o_ref, o_sc, l_sc, *, seq):
    qkv = ((q0, k0, v0), (q1, k1, v1), (q2, k2, v2))
    nb = N_BACK

    def rows(ref, start, size, dil):
        if dil == 1:
            return ref[pl.ds(start, size), :]
        return ref[pl.ds(start, size, stride=dil), :]

    def put(g, start, dil, o, lse):
        idx = pl.ds(start, nb) if dil == 1 else pl.ds(start, nb, stride=dil)
        o_sc[g, idx, :] = o
        l_sc[g, idx, :] = jnp.broadcast_to(lse, (nb, HEAD_DIM))

    for g, (_, dil) in enumerate(ATTN_GROUPS):
        q_ref, k_ref, v_ref = qkv[g]
        n_blk = seq // (dil * nb)
        span = dil * nb
        for r in range(dil):
            q = rows(q_ref, r, nb, dil).astype(BF16)
            k = rows(k_ref, r, nb, dil).astype(BF16)
            v = rows(v_ref, r, nb, dil).astype(BF16)
            s = lax.dot_general(q, k, (((1,), (1,)), ((), ())), preferred_element_type=F32) * ATTN_SCALE
            s = s + bias_ref[0, g, :, nb:]
            o, lse = _softmax_block(s, v)
            put(g, r, dil, o, lse)

            if n_blk > 1:
                def body(b, carry, q_ref=q_ref, k_ref=k_ref, v_ref=v_ref, g=g, dil=dil, r=r, span=span):
                    q = rows(q_ref, b * span + r, nb, dil).astype(BF16)
                    k = rows(k_ref, (b - 1) * span + r, 2 * nb, dil).astype(BF16)
                    v = rows(v_ref, (b - 1) * span + r, 2 * nb, dil).astype(BF16)
                    s = lax.dot_general(q, k, (((1,), (1,)), ((), ())), preferred_element_type=F32) * ATTN_SCALE
                    s = s + bias_ref[0, g]
                    o, lse = _softmax_block(s, v)
                    put(g, b * span + r, dil, o, lse)
                    return carry

                lax.fori_loop(1, n_blk, body, 0)

    chunk = 256

    def comb(c, carry):
        sl = pl.ds(pl.multiple_of(c * chunk, chunk), chunk)
        l0, l1, l2 = l_sc[0, sl, :], l_sc[1, sl, :], l_sc[2, sl, :]
        mx = jnp.maximum(jnp.maximum(l0, l1), l2)
        e0, e1, e2 = jnp.exp(l0 - mx), jnp.exp(l1 - mx), jnp.exp(l2 - mx)
        num = e0 * o_sc[0, sl, :] + e1 * o_sc[1, sl, :] + e2 * o_sc[2, sl, :]
        o_ref[sl, :] = (num / (e0 + e1 + e2)).astype(o_ref.dtype)
        return carry

    lax.fori_loop(0, seq // chunk, comb, 0)


def _prompt_attn(p, bias, batch, seq, col0):
    def spec(kind, g):
        base = col0 + kind * N_ATTN_HEADS + g * HEADS_PER_GROUP
        return pl.BlockSpec((seq, HEAD_DIM), lambda b, i, base=base: (b, base + i))

    in_specs = []
    for g in range(N_GROUPS):
        in_specs += [spec(0, g), spec(1, g), spec(2, g)]
    in_specs.append(pl.BlockSpec((1, N_GROUPS, N_BACK, 2 * N_BACK), lambda b, i: (i, 0, 0, 0)))
    return pl.pallas_call(
        functools.partial(_prompt_attn_kernel, seq=seq),
        grid=(batch, HEADS_PER_GROUP),
        in_specs=in_specs,
        out_specs=pl.BlockSpec((seq, HEAD_DIM), lambda b, i: (b, i)),
        out_shape=jax.ShapeDtypeStruct((batch * seq, ATTN_OUT_WIDTH), BF16),
        scratch_shapes=[pltpu.VMEM((N_GROUPS, seq, HEAD_DIM), F32), pltpu.VMEM((N_GROUPS, seq, HEAD_DIM), F32)],
        compiler_params=_cparams("arbitrary", "arbitrary"),
        name="prompt_attn",
    )(*([p] * 9), bias)


def _sample_attn_kernel(qkv_ref, c0_ref, c1_ref, c2_ref, b0_ref, b1_ref, b2_ref, bn_ref, o_ref, *, t_new):
    caches = (c0_ref, c1_ref, c2_ref)
    biases = (b0_ref, b1_ref, b2_ref)
    dn = (((1,), (1,)), ((), ()))
    for i in range(HEADS_PER_GROUP):
        outs, lses = [], []
        for g in range(N_GROUPS):
            h = g * HEADS_PER_GROUP + i
            q = qkv_ref[0, :, h * HEAD_DIM:(h + 1) * HEAD_DIM].astype(BF16)
            kn = qkv_ref[0, :, ATTN_WIDTH + h * HEAD_DIM:ATTN_WIDTH + (h + 1) * HEAD_DIM].astype(BF16)
            vn = qkv_ref[0, :, 2 * ATTN_WIDTH + h * HEAD_DIM:2 * ATTN_WIDTH + (h + 1) * HEAD_DIM].astype(BF16)
            c_ref = caches[g]
            if g < 2:
                n_rows = c_ref.shape[1] // CACHE_ROW_TILE
                kc = c_ref[0, pl.ds(i, n_rows, stride=CACHE_ROW_TILE), :].astype(BF16)
                vc = c_ref[0, pl.ds(HEADS_PER_GROUP + i, n_rows, stride=CACHE_ROW_TILE), :].astype(BF16)
            else:
                kc = jnp.concatenate(
                    [c_ref[0, :, t * CACHE_ROW_TILE + i, :] for t in range(t_new)], axis=0).astype(BF16)
                vc = jnp.concatenate(
                    [c_ref[0, :, t * CACHE_ROW_TILE + HEADS_PER_GROUP + i, :] for t in range(t_new)],
                    axis=0).astype(BF16)
            sc = lax.dot_general(q, kc, dn, preferred_element_type=F32) * ATTN_SCALE + biases[g][i]
            sn = lax.dot_general(q, kn, dn, preferred_element_type=F32) * ATTN_SCALE + bn_ref[g, i]
            m = jnp.maximum(jnp.max(sc, axis=-1, keepdims=True), jnp.max(sn, axis=-1, keepdims=True))
            pc = jnp.exp(sc - m)
            pn = jnp.exp(sn - m)
            den = jnp.sum(pc, axis=-1, keepdims=True) + jnp.sum(pn, axis=-1, keepdims=True)
            o = (jnp.dot(pc.astype(BF16), vc, preferred_element_type=F32)
                 + jnp.dot(pn.astype(BF16), vn, preferred_element_type=F32)) / den
            outs.append(o)
            lses.append(m + jnp.log(den))
        mx = jnp.maximum(jnp.maximum(lses[0], lses[1]), lses[2])
        es = [jnp.exp(l - mx) for l in lses]
        num = es[0] * outs[0] + es[1] * outs[1] + es[2] * outs[2]
        o_ref[0, :, i * HEAD_DIM:(i + 1) * HEAD_DIM] = (num / (es[0] + es[1] + es[2])).astype(o_ref.dtype)


def _sample_attn(qkv_s, c0, c1, c2v, bias_c, bias_n):
    db, t_new, _ = qkv_s.shape

    def full(a):
        return pl.BlockSpec(a.shape, lambda b, nd=a.ndim: (0,) * nd)

    return pl.pallas_call(
        functools.partial(_sample_attn_kernel, t_new=t_new),
        grid=(db,),
        in_specs=[
            pl.BlockSpec((1, t_new, qkv_s.shape[-1]), lambda b: (b, 0, 0)),
            pl.BlockSpec((1, c0.shape[1], HEAD_DIM), lambda b: (b, 0, 0)),
            pl.BlockSpec((1, c1.shape[1], HEAD_DIM), lambda b: (b, 0, 0)),
            pl.BlockSpec((1, c2v.shape[1], t_new * CACHE_ROW_TILE, HEAD_DIM), lambda b: (b, 0, 0, 0)),
            full(bias_c[0]), full(bias_c[1]), full(bias_c[2]), full(bias_n),
        ],
        out_specs=pl.BlockSpec((1, t_new, ATTN_OUT_WIDTH), lambda b: (b, 0, 0)),
        out_shape=jax.ShapeDtypeStruct((db, t_new, ATTN_OUT_WIDTH), BF16),
        compiler_params=_cparams("arbitrary"),
        name="sample_attn",
    )(qkv_s, c0, c1, c2v, bias_c[0], bias_c[1], bias_c[2], bias_n)


def _cache_roll_kernel(*refs, n_caches, t_new):
    caches = refs[:n_caches]
    news = refs[n_caches:2 * n_caches]
    outs = refs[2 * n_caches:3 * n_caches]
    for g in range(n_caches):
        rows = caches[g].shape[1]
        outs[g][0, :rows - t_new] = caches[g][0, t_new:]
        outs[g][0, rows - t_new:] = news[g][0]


def _cache_roll(caches, news):
    n = len(caches)
    db, t_new = news[0].shape[:2]

    def per_seq(a):
        return pl.BlockSpec((1,) + a.shape[1:], lambda b: (b, 0, 0, 0))

    return pl.pallas_call(
        functools.partial(_cache_roll_kernel, n_caches=n, t_new=t_new),
        grid=(db,),
        in_specs=[per_seq(c) for c in caches] + [per_seq(a) for a in news],
        out_specs=[per_seq(c) for c in caches],
        out_shape=[jax.ShapeDtypeStruct(c.shape, c.dtype) for c in caches],
        compiler_params=_cparams("arbitrary"),
        name="cache_roll",
    )(*caches, *news)


def _pool_kernel(hist_ref, z_ref, w_ref, s_ref, o_ref, zbuf, *, t_len, pos0, chunk):
    zbuf[0:POOL_HIST, :] = hist_ref[0]
    zbuf[POOL_HIST:, :] = z_ref[...].reshape(t_len, POOL_WIDTH)
    for c in range(t_len // chunk):
        base = POOL_HIST + c * chunk
        for g, w in enumerate(POOL_WINDOWS):
            sl = slice(g * PGW, (g + 1) * PGW)
            cur = zbuf[base:base + chunk, sl]
            tot = cur
            for i in range(1, w):
                tot = tot + zbuf[base - i:base - i + chunk, sl]
            pos = pos0 + c * chunk + lax.broadcasted_iota(jnp.int32, (chunk, PGW), 0)
            cnt = jnp.minimum(w, pos + 1).astype(F32)
            mixed = tot / cnt - cur
            y = jnp.dot(mixed.astype(BF16), w_ref[g], preferred_element_type=F32) * s_ref[:, sl]
            o_ref[0, c * chunk:(c + 1) * chunk, sl] = y.astype(o_ref.dtype)


def _pool_mix(hist, z, z_spec, n_seq, t_len, w_pool, pool_scale, pos0):
    chunk = min(t_len, 256)
    return pl.pallas_call(
        functools.partial(_pool_kernel, t_len=t_len, pos0=pos0, chunk=chunk),
        grid=(n_seq,),
        in_specs=[
            pl.BlockSpec((1, POOL_HIST, POOL_WIDTH), lambda i: (i, 0, 0)),
            z_spec,
            pl.BlockSpec(w_pool.shape, lambda i: (0, 0, 0)),
            pl.BlockSpec((1, POOL_WIDTH), lambda i: (0, 0)),
        ],
        out_specs=pl.BlockSpec((1, t_len, POOL_WIDTH), lambda i: (i, 0, 0)),
        out_shape=jax.ShapeDtypeStruct((n_seq, t_len, POOL_WIDTH), BF16),
        scratch_shapes=[pltpu.VMEM((POOL_HIST + t_len, POOL_WIDTH), F32)],
        compiler_params=_cparams("arbitrary"),
        name="pool_mix",
    )(hist, z, w_pool, pool_scale)


def _merge_kernel(h_ref, a_ref, p_ref, ga_ref, gb_ref, wa_ref, wb_ref, wo_ref, g2_ref, h1_ref, xn_ref):
    ya = jnp.dot(a_ref[...], wa_ref[...], preferred_element_type=F32)
    yb = jnp.dot(p_ref[...], wb_ref[...], preferred_element_type=F32)
    u = jax.nn.sigmoid(ga_ref[...]) * ya + jax.nn.sigmoid(gb_ref[...]) * yb
    h1 = h_ref[...] + jnp.dot(u.astype(BF16), wo_ref[...], preferred_element_type=F32)
    h1_ref[...] = h1
    xn_ref[...] = _rmsnorm_f32(h1, g2_ref[...])


def _merge(h, attn, pool, proj, wa, wb, wo, g2, tm):
    m, d = h.shape

    def const(a):
        return pl.BlockSpec(a.shape, lambda i, nd=a.ndim: (0,) * nd, pipeline_mode=pl.Buffered(1))

    return pl.pallas_call(
        _merge_kernel,
        grid=(m // tm,),
        in_specs=[
            pl.BlockSpec((tm, d), lambda i: (i, 0)),
            pl.BlockSpec((tm, attn.shape[1]), lambda i: (i, 0)),
            pl.BlockSpec((tm, pool.shape[1]), lambda i: (i, 0)),
            pl.BlockSpec((tm, d), lambda i: (i, 0)),
            pl.BlockSpec((tm, d), lambda i: (i, 1)),
            const(wa), const(wb), const(wo), const(g2),
        ],
        out_specs=[pl.BlockSpec((tm, d), lambda i: (i, 0)), pl.BlockSpec((tm, d), lambda i: (i, 0))],
        out_shape=[jax.ShapeDtypeStruct((m, d), F32), jax.ShapeDtypeStruct((m, d), F32)],
        compiler_params=_cparams("arbitrary"),
        name="merge",
    )(h, attn, pool, proj, proj, wa, wb, wo, g2)


def _router_kernel(x_ref, wh_ref, wl_ref, b_ref, e_ref, gate_ref, rank_ref, cnt_ref, carry):
    @pl.when(pl.program_id(0) == 0)
    def _():
        carry[...] = jnp.zeros_like(carry)

    x = x_ref[...]
    xh = x.astype(BF16)
    xl = (x - xh.astype(F32)).astype(BF16)
    logits = (jnp.dot(xh, wh_ref[...], preferred_element_type=F32)
              + jnp.dot(xh, wl_ref[...], preferred_element_type=F32)
              + jnp.dot(xl, wh_ref[...], preferred_element_type=F32)) + b_ref[...]
    n_e = logits.shape[-1]
    lane = lax.broadcasted_iota(jnp.int32, logits.shape, 1)
    vals = logits
    tops, idxs = [], []
    for _ in range(TOP_K):
        m = jnp.max(vals, axis=-1, keepdims=True)
        idx = jnp.min(jnp.where(vals == m, lane, n_e), axis=-1, keepdims=True)
        tops.append(m)
        idxs.append(idx)
        vals = jnp.where(lane == idx, -jnp.inf, vals)
    ex = [jnp.exp(t - tops[0]) for t in tops]
    den = ex[0] + ex[1] + ex[2] + ex[3]
    for k in range(TOP_K):
        e_ref[:, k:k + 1] = idxs[k]
        gate_ref[:, k:k + 1] = ex[k] / den

    onehots = [(lane == idx).astype(F32) for idx in idxs]
    cnt = onehots[0] + onehots[1] + onehots[2] + onehots[3]
    tm = cnt.shape[0]
    lower = (lax.broadcasted_iota(jnp.int32, (tm, tm), 0) > lax.broadcasted_iota(jnp.int32, (tm, tm), 1))
    prefix = jnp.dot(lower.astype(BF16), cnt.astype(BF16), preferred_element_type=F32)
    base = carry[...] + prefix
    for k in range(TOP_K):
        rank_ref[:, k:k + 1] = jnp.sum(onehots[k] * base, axis=-1, keepdims=True).astype(jnp.int32)
    carry[...] = carry[...] + jnp.sum(cnt, axis=0, keepdims=True)
    cnt_ref[...] = carry[...].astype(jnp.int32)


def _router(xn, wh, wl, b, tm):
    m, d = xn.shape
    n_e = wh.shape[1]
    tok_spec = pl.BlockSpec((tm, TOP_K), lambda i: (i, 0))
    return pl.pallas_call(
        _router_kernel,
        grid=(m // tm,),
        in_specs=[
            pl.BlockSpec((tm, d), lambda i: (i, 0)),
            pl.BlockSpec((d, n_e), lambda i: (0, 0)),
            pl.BlockSpec((d, n_e), lambda i: (0, 0)),
            pl.BlockSpec((1, n_e), lambda i: (0, 0)),
        ],
        out_specs=[tok_spec, tok_spec, tok_spec, pl.BlockSpec((1, n_e), lambda i: (0, 0))],
        out_shape=[jax.ShapeDtypeStruct((m, TOP_K), jnp.int32), jax.ShapeDtypeStruct((m, TOP_K), F32),
                   jax.ShapeDtypeStruct((m, TOP_K), jnp.int32), jax.ShapeDtypeStruct((1, n_e), jnp.int32)],
        scratch_shapes=[pltpu.VMEM((1, n_e), F32)],
        compiler_params=_cparams("arbitrary"),
        name="router",
    )(xn, wh, wl, b)


def _row_gather_kernel(nused_ref, tok_ref, src_hbm, o_ref, buf, sem, *, rows):
    t = pl.program_id(0)
    n_used = nused_ref[0]

    def issue(tile, slot):
        def body(i, carry):
            pltpu.make_async_copy(src_hbm.at[pl.ds(tok_ref[tile * rows + i], 1)],
                                  buf.at[slot, pl.ds(i, 1)], sem.at[slot]).start()
            return carry
        lax.fori_loop(0, rows, body, 0, unroll=DMA_ISSUE_UNROLL)

    @pl.when(jnp.logical_and(t == 0, n_used > 0))
    def _():
        issue(0, 0)

    @pl.when(t + 1 < n_used)
    def _():
        issue(t + 1, (t + 1) % 2)

    @pl.when(t < n_used)
    def _():
        slot = t % 2
        pltpu.make_async_copy(src_hbm.at[pl.ds(0, rows)], buf.at[slot], sem.at[slot]).wait()
        o_ref[...] = buf[slot].astype(o_ref.dtype)

    @pl.when(t >= n_used)
    def _():
        o_ref[...] = jnp.zeros_like(o_ref)


def _row_gather(n_used, row_idx, src, n_tiles, rows, out_dtype):
    d = src.shape[1]
    grid_spec = pltpu.PrefetchScalarGridSpec(
        num_scalar_prefetch=2,
        grid=(n_tiles,),
        in_specs=[pl.BlockSpec(memory_space=pl.ANY)],
        out_specs=pl.BlockSpec((rows, d), lambda t, *_: (t, 0)),
        scratch_shapes=[pltpu.VMEM((2, rows, d), src.dtype), pltpu.SemaphoreType.DMA((2,))],
    )
    return pl.pallas_call(
        functools.partial(_row_gather_kernel, rows=rows),
        grid_spec=grid_spec,
        out_shape=jax.ShapeDtypeStruct((n_tiles * rows, d), out_dtype),
        compiler_params=_cparams("arbitrary"),
        name="moe_row_gather",
    )(n_used, row_idx, src)


def _start_all(copies):
    for c in copies:
        c.start()


def _wait_all(copies):
    for c in copies:
        c.wait()


def _split_rows(n_rows, pieces):
    per = n_rows // pieces
    assert per * pieces == n_rows
    return [pl.ds(p * per, per) for p in range(pieces)]


def _prefetch_step_weights(step, n_steps, copies_for):
    slot = step % 2

    @pl.when(step == 0)
    def _():
        _start_all(copies_for(step, slot))

    @pl.when(step + 1 < n_steps)
    def _():
        _start_all(copies_for(step + 1, 1 - slot))

    _wait_all(copies_for(step, slot))
    return slot


def _expert_chunk_loop(n_chunks, in_copies, out_copies, compute, prepare):
    @pl.when(n_chunks > 0)
    def _():
        _start_all(in_copies(0, 0))
        prepare()

        def body(r, carry):
            slot = r % 2

            @pl.when(r + 1 < n_chunks)
            def _():
                _start_all(in_copies(r + 1, 1 - slot))

            _wait_all(in_copies(r, slot))

            @pl.when(r >= 2)
            def _():
                _wait_all(out_copies(r - 2, slot))

            compute(slot)
            _start_all(out_copies(r, slot))
            return carry

        lax.fori_loop(0, n_chunks, body, 0)

        @pl.when(n_chunks >= 2)
        def _():
            _wait_all(out_copies(n_chunks - 2, n_chunks % 2))

        _wait_all(out_copies(n_chunks - 1, (n_chunks - 1) % 2))


def _zero_fill_tiles(first, last, zbuf, dst, sem):
    zbuf[...] = jnp.zeros_like(zbuf)

    def copy(t):
        return pltpu.make_async_copy(zbuf, dst(t), sem)

    def start(t, carry):
        copy(t).start()
        return carry

    def wait(t, carry):
        copy(t).wait()
        return carry

    lax.fori_loop(first, last, start, 0)
    lax.fori_loop(first, last, wait, 0)


def _moe_up_kernel(tpe_ref, start_ref, xs_hbm, w_hbm, bg_ref, bu_ref, hid_hbm,
                   wbuf, wg_sc, wu_sc, xbuf, obuf, w_sem, in_sem, out_sem, *, rows):
    e = pl.program_id(0)
    n = pl.program_id(1)
    n_chunks = pl.num_programs(1)
    row0 = start_ref[e]
    d, nc = wg_sc.shape

    def w_copies(step, slot):
        e_, n_ = step // n_chunks, step % n_chunks
        copies = []
        for mat in range(2):
            cols = pl.ds(pl.multiple_of((mat * n_chunks + n_) * nc, nc), nc)
            for rws in _split_rows(d, DMA_SPLIT):
                copies.append(pltpu.make_async_copy(w_hbm.at[e_, rws, cols], wbuf.at[slot, mat, rws], w_sem.at[slot]))
        return copies

    w_slot = _prefetch_step_weights(e * n_chunks + n, pl.num_programs(0) * n_chunks, w_copies)

    def row_pieces(r, pieces):
        per = rows // pieces
        return [(pl.ds(pl.multiple_of(row0 + r * rows + p * per, per), per), pl.ds(p * per, per))
                for p in range(pieces)]

    def in_copies(r, slot):
        return [pltpu.make_async_copy(xs_hbm.at[src], xbuf.at[slot, dst], in_sem.at[slot])
                for src, dst in row_pieces(r, CHUNK_DMA_SPLIT)]

    def out_copies(r, slot):
        return [pltpu.make_async_copy(obuf.at[slot, src], hid_hbm.at[n, dst], out_sem.at[slot])
                for dst, src in row_pieces(r, 1)]

    def cast_weights():
        wg_sc[...] = wbuf[w_slot, 0].astype(BF16)
        wu_sc[...] = wbuf[w_slot, 1].astype(BF16)

    def compute(slot):
        x = xbuf[slot]
        g = jnp.dot(x, wg_sc[...], preferred_element_type=F32) + bg_ref[0]
        u = jnp.dot(x, wu_sc[...], preferred_element_type=F32) + bu_ref[0]
        g = jnp.minimum(g, SWIGLU_LIMIT)
        u = jnp.clip(u, -SWIGLU_LIMIT, SWIGLU_LIMIT)
        obuf[slot] = ((u + 1.0) * (g * jax.nn.sigmoid(SWIGLU_ALPHA * g))).astype(obuf.dtype)

    _expert_chunk_loop(tpe_ref[e], in_copies, out_copies, compute, cast_weights)

    @pl.when(e == pl.num_programs(0) - 1)
    def _():
        _zero_fill_tiles(row0 // rows + tpe_ref[e], hid_hbm.shape[1] // rows, obuf.at[0],
                         lambda t: hid_hbm.at[n, pl.ds(pl.multiple_of(t * rows, rows), rows)], out_sem.at[0])


def _moe_up(tiles_per_e, row_start, xs, w_gu, b_gu, rows, nc):
    r_pad, d = xs.shape
    n_e = w_gu.shape[0]
    d_ff = w_gu.shape[2] // 2
    n_chunks = d_ff // nc
    any_spec = pl.BlockSpec(memory_space=pl.ANY)
    grid_spec = pltpu.PrefetchScalarGridSpec(
        num_scalar_prefetch=2,
        grid=(n_e, n_chunks),
        in_specs=[
            any_spec,
            any_spec,
            pl.BlockSpec((1, 1, nc), lambda e, n, *_: (e, 0, n)),
            pl.BlockSpec((1, 1, nc), lambda e, n, *_: (e, 0, n_chunks + n)),
        ],
        out_specs=any_spec,
        scratch_shapes=[pltpu.VMEM((2, 2, d, nc), w_gu.dtype), pltpu.VMEM((d, nc), BF16), pltpu.VMEM((d, nc), BF16),
                        pltpu.VMEM((2, rows, d), xs.dtype), pltpu.VMEM((2, rows, nc), BF16),
                        pltpu.SemaphoreType.DMA((2,)), pltpu.SemaphoreType.DMA((2,)), pltpu.SemaphoreType.DMA((2,))],
    )
    return pl.pallas_call(
        functools.partial(_moe_up_kernel, rows=rows),
        grid_spec=grid_spec,
        out_shape=jax.ShapeDtypeStruct((n_chunks, r_pad, nc), BF16),
        compiler_params=_cparams("arbitrary", "arbitrary"),
        name="moe_up",
    )(tiles_per_e, row_start, xs, w_gu, b_gu, b_gu)


def _moe_down_kernel(tpe_ref, start_ref, hid_hbm, w_hbm, b_ref, y_hbm,
                     wbuf, w_sc, xbuf, obuf, w_sem, in_sem, out_sem, *, rows):
    e = pl.program_id(0)
    n = pl.program_id(1)
    n_chunks = pl.num_programs(1)
    row0 = start_ref[e]
    k_chunks, _, kc = xbuf.shape[1:]
    d_ff, nc = w_sc.shape

    def w_copies(step, slot):
        e_, n_ = step // n_chunks, step % n_chunks
        cols = pl.ds(pl.multiple_of(n_ * nc, nc), nc)
        return [pltpu.make_async_copy(w_hbm.at[e_, rws, cols], wbuf.at[slot, rws], w_sem.at[slot])
                for rws in _split_rows(d_ff, DMA_SPLIT)]

    w_slot = _prefetch_step_weights(e * n_chunks + n, pl.num_programs(0) * n_chunks, w_copies)

    def chunk_rows(r):
        return pl.ds(pl.multiple_of(row0 + r * rows, rows), rows)

    def in_copies(r, slot):
        return [pltpu.make_async_copy(hid_hbm.at[c, chunk_rows(r)], xbuf.at[slot, c], in_sem.at[slot])
                for c in range(k_chunks)]

    def out_copies(r, slot):
        cols = pl.ds(pl.multiple_of(n * nc, nc), nc)
        per = rows // CHUNK_DMA_SPLIT
        return [pltpu.make_async_copy(
            obuf.at[slot, pl.ds(p * per, per)],
            y_hbm.at[pl.ds(pl.multiple_of(row0 + r * rows + p * per, per), per), cols], out_sem.at[slot])
            for p in range(CHUNK_DMA_SPLIT)]

    def cast_weights():
        w_sc[...] = wbuf[w_slot].astype(BF16)

    def compute(slot):
        acc = b_ref[0] + jnp.dot(xbuf[slot, 0], w_sc[0:kc, :], preferred_element_type=F32)
        for c in range(1, k_chunks):
            acc = acc + jnp.dot(xbuf[slot, c], w_sc[c * kc:(c + 1) * kc, :], preferred_element_type=F32)
        obuf[slot] = acc

    _expert_chunk_loop(tpe_ref[e], in_copies, out_copies, compute, cast_weights)

    @pl.when(e == pl.num_programs(0) - 1)
    def _():
        cols = pl.ds(pl.multiple_of(n * nc, nc), nc)
        _zero_fill_tiles(row0 // rows + tpe_ref[e], y_hbm.shape[0] // rows, obuf.at[0],
                         lambda t: y_hbm.at[pl.ds(pl.multiple_of(t * rows, rows), rows), cols], out_sem.at[0])


def _moe_down(tiles_per_e, row_start, hid, w_dn, b_dn, rows, nc):
    k_chunks, r_pad, kc = hid.shape
    n_e, d_ff, d = w_dn.shape
    n_chunks = d // nc
    any_spec = pl.BlockSpec(memory_space=pl.ANY)
    grid_spec = pltpu.PrefetchScalarGridSpec(
        num_scalar_prefetch=2,
        grid=(n_e, n_chunks),
        in_specs=[
            any_spec,
            any_spec,
            pl.BlockSpec((1, 1, nc), lambda e, n, *_: (e, 0, n)),
        ],
        out_specs=any_spec,
        scratch_shapes=[pltpu.VMEM((2, d_ff, nc), w_dn.dtype), pltpu.VMEM((d_ff, nc), BF16),
                        pltpu.VMEM((2, k_chunks, rows, kc), hid.dtype), pltpu.VMEM((2, rows, nc), F32),
                        pltpu.SemaphoreType.DMA((2,)), pltpu.SemaphoreType.DMA((2,)), pltpu.SemaphoreType.DMA((2,))],
    )
    return pl.pallas_call(
        functools.partial(_moe_down_kernel, rows=rows),
        grid_spec=grid_spec,
        out_shape=jax.ShapeDtypeStruct((r_pad, d), F32),
        compiler_params=_cparams("arbitrary", "arbitrary"),
        name="moe_down",
    )(tiles_per_e, row_start, hid, w_dn, b_dn)


def _combine_kernel(dest_ref, h_ref, gate_ref, gf_ref, y_hbm, o_ref, buf, sem, *, rows, final_norm):
    t = pl.program_id(0)
    n_t = pl.num_programs(0)
    n_dma = rows * TOP_K

    def issue(tile, slot):
        def body(i, carry):
            pltpu.make_async_copy(y_hbm.at[pl.ds(dest_ref[tile * n_dma + i], 1)],
                                  buf.at[slot, pl.ds(i, 1)], sem.at[slot]).start()
            return carry
        lax.fori_loop(0, n_dma, body, 0, unroll=DMA_ISSUE_UNROLL)

    @pl.when(t == 0)
    def _():
        issue(0, 0)

    @pl.when(t + 1 < n_t)
    def _():
        issue(t + 1, (t + 1) % 2)

    slot = t % 2
    pltpu.make_async_copy(y_hbm.at[pl.ds(0, n_dma)], buf.at[slot], sem.at[slot]).wait()

    acc = h_ref[...]
    for k in range(TOP_K):
        acc = acc + gate_ref[:, k:k + 1] * buf[slot, k * rows:(k + 1) * rows, :]
    o_ref[...] = _rmsnorm_f32(acc, gf_ref[...]) if final_norm else acc


def _combine(dest_km, h1, gate, gf, y, rows, final_norm):
    m, d = h1.shape
    grid_spec = pltpu.PrefetchScalarGridSpec(
        num_scalar_prefetch=1,
        grid=(m // rows,),
        in_specs=[
            pl.BlockSpec((rows, d), lambda t, *_: (t, 0)),
            pl.BlockSpec((rows, TOP_K), lambda t, *_: (t, 0)),
            pl.BlockSpec((1, d), lambda t, *_: (0, 0)),
            pl.BlockSpec(memory_space=pl.ANY),
        ],
        out_specs=pl.BlockSpec((rows, d), lambda t, *_: (t, 0)),
        scratch_shapes=[pltpu.VMEM((2, rows * TOP_K, d), F32), pltpu.SemaphoreType.DMA((2,))],
    )
    return pl.pallas_call(
        functools.partial(_combine_kernel, rows=rows, final_norm=final_norm),
        grid_spec=grid_spec,
        out_shape=jax.ShapeDtypeStruct((m, d), F32),
        compiler_params=_cparams("arbitrary"),
        name="moe_combine",
    )(dest_km, h1, gate, gf, y)


def _routing_tables(top_e, rank, counts, rows, n_tiles):
    m = top_e.shape[0]
    n_experts = counts.shape[0]
    tiles_per_e = (counts + rows - 1) // rows
    tile_end = jnp.cumsum(tiles_per_e)
    pad_start = (tile_end - tiles_per_e) * rows
    experts = jnp.arange(n_experts, dtype=jnp.int32)
    dest = rank + jnp.sum(jnp.where(top_e[..., None] == experts, pad_start, 0), axis=-1)
    dest = dest.astype(jnp.int32).reshape(-1)
    n_used = tile_end[-1:].astype(jnp.int32)
    tok = jnp.arange(m * TOP_K, dtype=jnp.int32) // TOP_K
    row_tok = jnp.zeros((n_tiles * rows,), jnp.int32).at[dest].set(tok)
    return dest, row_tok, tiles_per_e.astype(jnp.int32), pad_start.astype(jnp.int32), n_used


def _largest_divisor(n, cap, mult):
    best = mult
    for c in range(mult, cap + 1, mult):
        if n % c == 0:
            best = c
    return best


def kernel(x_prompt, x_sample, cache_kv_w128, cache_kv_w512, cache_kv_w2048, state_pool, rel_bias, norm1, w_in,
           w_branch_a, w_branch_b, w_out, w_pool, pool_scale, norm2, router_w, router_b, w_gate_up, b_gate_up,
           w_down, b_down, norm_f):
    batch, seq, d = x_prompt.shape
    db, t_new, _ = x_sample.shape
    depth = norm1.shape[0]
    n_experts = router_w.shape[-1]
    caches = (cache_kv_w128, cache_kv_w512, cache_kv_w2048)
    m_p, m_s = batch * seq, db * t_new
    m = m_p + m_s
    tm = _largest_divisor(m, 640, 8)

    h = jnp.concatenate([x_prompt.reshape(m_p, d), x_sample.reshape(m_s, d)], axis=0)
    bias_p = _prompt_bias_tiles(rel_bias)
    bias_sc, bias_sn = _sample_bias_tiles(rel_bias, t_new)

    cuts = [ATTN_WIDTH, 2 * ATTN_WIDTH, 3 * ATTN_WIDTH, 3 * ATTN_WIDTH + POOL_WIDTH, 3 * ATTN_WIDTH + POOL_WIDTH + d]
    qkv0 = 2 * d
    z0 = qkv0 + 3 * ATTN_WIDTH

    kv_p = [[] for _ in range(N_GROUPS)]
    kv_s = [[] for _ in range(N_GROUPS)]
    pool_p, pool_s = [], []
    for l in range(depth):
        wq, wk, wv, wz, wga, wgb = jnp.split(w_in[l], cuts, axis=-1)
        w_perm = jnp.concatenate([wga, wgb, wq, wk, wv, wz], axis=-1).astype(BF16)
        proj = _inproj(h, norm1[l][None, :], w_perm, tm, _largest_divisor(w_perm.shape[1], 1536, 128))

        k0, v0 = qkv0 + ATTN_WIDTH, qkv0 + 2 * ATTN_WIDTH
        kv_w = HEADS_PER_GROUP * HEAD_DIM
        proj_s = proj[m_p:]
        z_s = proj_s[:, z0:z0 + POOL_WIDTH].reshape(db, t_new, POOL_WIDTH)

        attn_p = _prompt_attn(proj, bias_p, batch, seq, qkv0 // HEAD_DIM)
        qkv_s = proj_s[:, qkv0:qkv0 + 3 * ATTN_WIDTH].reshape(db, t_new, 3 * ATTN_WIDTH)
        c_rows = [caches[g][l].reshape(db, -1, CACHE_ROW_TILE, HEAD_DIM) for g in range(N_GROUPS)]
        c0 = c_rows[0].reshape(db, -1, HEAD_DIM)
        c1 = c_rows[1].reshape(db, -1, HEAD_DIM)
        dil2 = ATTN_GROUPS[2][1]
        c2v = c_rows[2].reshape(db, -1, dil2 * CACHE_ROW_TILE, HEAD_DIM)
        attn_s = _sample_attn(qkv_s, c0, c1, c2v, bias_sc, bias_sn)
        attn = jnp.concatenate([attn_p, attn_s.reshape(m_s, ATTN_OUT_WIDTH)], axis=0)

        news = []
        for g, (window, _) in enumerate(ATTN_GROUPS):
            keep = min(window, seq)
            kc, vc = k0 + g * kv_w, v0 + g * kv_w

            def kv_rows(a, r0, r1, kc=kc, vc=vc):
                return jnp.stack([a[r0:r1, kc:kc + kv_w].reshape(-1, HEADS_PER_GROUP, HEAD_DIM),
                                  a[r0:r1, vc:vc + kv_w].reshape(-1, HEADS_PER_GROUP, HEAD_DIM)], axis=1)

            kv_p[g].append(jnp.stack([kv_rows(proj, (b + 1) * seq - keep, (b + 1) * seq) for b in range(batch)]))
            news.append(kv_rows(proj_s, 0, m_s).reshape(db, t_new, CACHE_ROW_TILE, HEAD_DIM))
        rolled = _cache_roll(c_rows, news)
        for g in range(N_GROUPS):
            kv_s[g].append(rolled[g].reshape(caches[g][l].shape))

        w_pool_b = w_pool[l].astype(BF16)
        scale = pool_scale[l][None, :]
        hist_p = jnp.zeros((batch, POOL_HIST, POOL_WIDTH), F32)
        hist_s = jnp.concatenate([jnp.zeros((db, POOL_HIST - POOL_BUF, POOL_WIDTH), F32), state_pool[l]], axis=1)
        pool = jnp.concatenate([
            _pool_mix(hist_p, proj, pl.BlockSpec((seq, POOL_WIDTH), lambda i: (i, z0 // POOL_WIDTH)),
                      batch, seq, w_pool_b, scale, 0).reshape(m_p, POOL_WIDTH),
            _pool_mix(hist_s, z_s, pl.BlockSpec((1, t_new, POOL_WIDTH), lambda i: (i, 0, 0)),
                      db, t_new, w_pool_b, scale, PAST_LEN).reshape(m_s, POOL_WIDTH),
        ], axis=0)
        pool_p.append(jnp.stack([proj[(b + 1) * seq - POOL_BUF:(b + 1) * seq, z0:z0 + POOL_WIDTH]
                                 for b in range(batch)]))
        pool_s.append(jnp.concatenate([state_pool[l], z_s], axis=1)[:, t_new:])

        h1, xn2 = _merge(h, attn, pool, proj, w_branch_a[l].astype(BF16), w_branch_b[l].astype(BF16),
                         w_out[l].astype(BF16), norm2[l][None, :], _largest_divisor(m, 320, 8))

        rw = router_w[l]
        rw_hi = rw.astype(BF16)
        rw_lo = (rw - rw_hi.astype(F32)).astype(BF16)
        top_e, gate, rank, counts = _router(xn2, rw_hi, rw_lo, router_b[l][None, :], tm)
        rows = MOE_ROW_TILE
        n_tiles = (m * TOP_K + n_experts * (rows - 1)) // rows
        dest, row_tok, tiles_per_e, row_start, n_used = _routing_tables(top_e, rank, counts[0], rows, n_tiles)
        xs = _row_gather(n_used, row_tok, xn2, n_tiles, rows, BF16)
        hid = _moe_up(tiles_per_e, row_start, xs, w_gate_up[l], b_gate_up[l][:, None, :], rows, 512)
        y = _moe_down(tiles_per_e, row_start, hid, w_down[l], b_down[l][:, None, :], rows, 1024)
        crow = _largest_divisor(m, 128, 8)
        dest_km = dest.reshape(m // crow, crow, TOP_K).transpose(0, 2, 1).reshape(-1)
        h = _combine(dest_km, h1, gate, norm_f[None, :], y, crow, final_norm=(l == depth - 1))

    y_prompt = h[:m_p].reshape(batch, seq, d)
    y_sample = h[m_p:].reshape(db, t_new, d)
    return (y_prompt, y_sample,
            jnp.stack(kv_p[0]), jnp.stack(kv_p[1]), jnp.stack(kv_p[2]), jnp.stack(pool_p),
            jnp.stack(kv_s[0]), jnp.stack(kv_s[1]), jnp.stack(kv_s[2]), jnp.stack(pool_s))
```

```python
import functools
import math

import jax
import jax.numpy as jnp
from jax import lax
from jax.experimental import pallas as pl
from jax.experimental.pallas import tpu as pltpu

F32 = jnp.float32
BF16 = jnp.bfloat16

HEAD_DIM = 128
HEADS_PER_GROUP = 4
ATTN_GROUPS = ((128, 1), (512, 4), (2048, 16))
N_GROUPS = len(ATTN_GROUPS)
N_ATTN_HEADS = N_GROUPS * HEADS_PER_GROUP
ATTN_WIDTH = N_ATTN_HEADS * HEAD_DIM
ATTN_OUT_WIDTH = HEADS_PER_GROUP * HEAD_DIM
ATTN_SCALE = HEAD_DIM ** -0.5
N_BACK = 128
N_REL_BUCKETS = 32
REL_MAX_DIST = 2048
POOL_WINDOWS = (2, 4, 8, 16)
POOL_GROUPS = len(POOL_WINDOWS)
PGW = 128
POOL_WIDTH = POOL_GROUPS * PGW
POOL_BUF = max(POOL_WINDOWS) - 1
POOL_HIST = 16
TOP_K = 4
SWIGLU_LIMIT = 7.0
SWIGLU_ALPHA = 1.702
NORM_EPS = 1e-5
NEG_INF = -1e30
PAST_LEN = 8192
CACHE_ROW_TILE = 2 * HEADS_PER_GROUP

VMEM_LIMIT_BYTES = 56 * 1024 * 1024
MOE_ROW_TILE = 256
DMA_ISSUE_UNROLL = 8
DMA_SPLIT = 8
CHUNK_DMA_SPLIT = 4
ATTN_INTERLEAVE = 4


def _cparams(*sem):
    return pltpu.CompilerParams(dimension_semantics=sem, vmem_limit_bytes=VMEM_LIMIT_BYTES)


def _rmsnorm_f32(x, g):
    return x * lax.rsqrt(jnp.mean(x * x, axis=-1, keepdims=True) + NORM_EPS) * g


def _inproj_kernel(*refs):
    x_refs, (g_ref, w_ref, o_ref) = refs[:CHUNK_DMA_SPLIT], refs[CHUNK_DMA_SPLIT:]
    xs = [r[...] for r in x_refs]
    kc = xs[0].shape[1]
    sq = xs[0] * xs[0]
    for x in xs[1:]:
        sq = sq + x * x
    inv = lax.rsqrt(jnp.sum(sq, axis=-1, keepdims=True) / (kc * len(xs)) + NORM_EPS)
    acc = None
    for k, x in enumerate(xs):
        xn = (x * inv * g_ref[:, k * kc:(k + 1) * kc]).astype(BF16)
        part = jnp.dot(xn, w_ref[k * kc:(k + 1) * kc, :], preferred_element_type=F32)
        acc = part if acc is None else acc + part
    o_ref[...] = acc


def _inproj(x, g, w, tm, tn):
    m, d = x.shape
    n = w.shape[1]
    kc = d // CHUNK_DMA_SPLIT
    return pl.pallas_call(
        _inproj_kernel,
        grid=(n // tn, m // tm),
        in_specs=[pl.BlockSpec((tm, kc), lambda j, i, k=k: (i, k)) for k in range(CHUNK_DMA_SPLIT)] + [
            pl.BlockSpec((1, d), lambda j, i: (0, 0)),
            pl.BlockSpec((d, tn), lambda j, i: (0, j)),
        ],
        out_specs=pl.BlockSpec((tm, tn), lambda j, i: (i, j)),
        out_shape=jax.ShapeDtypeStruct((m, n), F32),
        compiler_params=_cparams("arbitrary", "arbitrary"),
        name="inproj",
    )(*([x] * CHUNK_DMA_SPLIT), g, w)


def _t5_bucket(dist):
    max_exact = N_REL_BUCKETS // 2
    df = jnp.maximum(dist, 1).astype(F32)
    large = max_exact + (jnp.log(df / max_exact) / math.log(REL_MAX_DIST / max_exact)
                         * (N_REL_BUCKETS - max_exact)).astype(jnp.int32)
    large = jnp.minimum(large, N_REL_BUCKETS - 1)
    return jnp.where(dist < max_exact, dist, large)


def _step_bias(rel_bias, g):
    _, dil = ATTN_GROUPS[g]
    buckets = _t5_bucket(jnp.arange(N_BACK + 1, dtype=jnp.int32) * dil)
    tab = rel_bias[buckets].astype(F32)
    return tab[:, g * HEADS_PER_GROUP:(g + 1) * HEADS_PER_GROUP].T


def _prompt_bias_tiles(rel_bias):
    nb = N_BACK
    period = 3 * nb - 1
    tiles = []
    for g in range(N_GROUPS):
        bj = _step_bias(rel_bias, g)
        h = bj.shape[0]
        pad = jnp.full((h, nb - 1), NEG_INF, F32)
        v = jnp.concatenate([pad, bj[:, ::-1], pad], axis=1)
        skew = jnp.tile(v, (1, nb + 1))[:, :nb * (period + 1)].reshape(h, nb, period + 1)
        tiles.append(skew[:, ::-1, :2 * nb])
    return jnp.stack(tiles, axis=1)


def _sample_bias_tiles(rel_bias, t_new):
    tq = jnp.arange(t_new)[:, None]
    outs = []
    new = []
    for g, (window, dil) in enumerate(ATTN_GROUPS):
        bj = _step_bias(rel_bias, g)
        if g < 2:
            p = jnp.arange(window)[None, :]
            dist = window + tq - p
            ok = (dist % dil == 0) & (dist // dil <= N_BACK)
            outs.append(jnp.where(ok[None], bj[:, jnp.clip(dist // dil, 0, N_BACK)], NEG_INF))
        else:
            col = jnp.arange(t_new * N_BACK)[None, :]
            blk, r = col // N_BACK, col % N_BACK
            ok = blk == tq
            outs.append(jnp.where(ok[None], bj[:, jnp.broadcast_to(N_BACK - r, (t_new, t_new * N_BACK))], NEG_INF))
        tk = jnp.arange(t_new)[None, :]
        d = tq - tk
        okn = (d >= 0) & (d % dil == 0) & (d // dil <= N_BACK)
        new.append(jnp.where(okn[None], bj[:, jnp.clip(d // dil, 0, N_BACK)], NEG_INF))
    return outs, jnp.stack(new, axis=0)


def _softmax_block(s, v):
    m = jnp.max(s, axis=-1, keepdims=True)
    p = jnp.exp(s - m)
    den = jnp.sum(p, axis=-1, keepdims=True)
    o = jnp.dot(p.astype(BF16), v, preferred_element_type=F32) / den
    return o, m + jnp.log(den)


def _prompt_attn_kernel(q0, k0, v0, q1, k1, v1, q2, k2, v2, bias_ref, o_ref, o_sc, l_sc, *, seq):
    qkv = ((q0, k0, v0), (q1, k1, v1), (q2, k2, v2))
    nb = N_BACK

    def rows(ref, start, size, dil):
        if dil == 1:
            return ref[pl.ds(start, size), :]
        return ref[pl.ds(start, size, stride=dil), :]

    def put(g, start, dil, o, lse):
        idx = pl.ds(start, nb) if dil == 1 else pl.ds(start, nb, stride=dil)
        o_sc[g, idx, :] = o
        l_sc[g, idx, :] = jnp.broadcast_to(lse, (nb, HEAD_DIM))

    dn = (((1,), (1,)), ((), ()))

    def attend(g, dil, q_starts, kv_starts, n_kv):
        q_ref, k_ref, v_ref = qkv[g]
        bias = bias_ref[0, g] if n_kv == 2 * nb else bias_ref[0, g, :, nb:]
        qs = [rows(q_ref, st, nb, dil).astype(BF16) for st in q_starts]
        ks = [rows(k_ref, st, n_kv, dil).astype(BF16) for st in kv_starts]
        ss = [lax.dot_general(q, k, dn, preferred_element_type=F32) * ATTN_SCALE + bias for q, k in zip(qs, ks)]
        ms = [jnp.max(s, axis=-1, keepdims=True) for s in ss]
        ps = [jnp.exp(s - m) for s, m in zip(ss, ms)]
        dens = [jnp.sum(p, axis=-1, keepdims=True) for p in ps]
        vs = [rows(v_ref, st, n_kv, dil).astype(BF16) for st in kv_starts]
        for st, p, v, m, den in zip(q_starts, ps, vs, ms, dens):
            o = jnp.dot(p.astype(BF16), v, preferred_element_type=F32) / den
            put(g, st, dil, o, m + jnp.log(den))

    for g, (_, dil) in enumerate(ATTN_GROUPS):
        n_blk = seq // (dil * nb)
        span = dil * nb
        for r0 in range(0, dil, ATTN_INTERLEAVE):
            starts = list(range(r0, min(r0 + ATTN_INTERLEAVE, dil)))
            attend(g, dil, starts, starts, nb)
        if n_blk > 1:
            per_iter = max(c for c in range(1, max(1, ATTN_INTERLEAVE // dil) + 1) if (n_blk - 1) % c == 0)

            def body(it, carry, g=g, dil=dil, span=span, per_iter=per_iter):
                q_starts, kv_starts = [], []
                for j in range(per_iter):
                    b = 1 + it * per_iter + j
                    for r in range(dil):
                        q_starts.append(b * span + r)
                        kv_starts.append((b - 1) * span + r)
                attend(g, dil, q_starts, kv_starts, 2 * nb)
                return carry

            lax.fori_loop(0, (n_blk - 1) // per_iter, body, 0)

    chunk = 256

    def comb(c, carry):
        sl = pl.ds(pl.multiple_of(c * chunk, chunk), chunk)
        l0, l1, l2 = l_sc[0, sl, :], l_sc[1, sl, :], l_sc[2, sl, :]
        mx = jnp.maximum(jnp.maximum(l0, l1), l2)
        e0, e1, e2 = jnp.exp(l0 - mx), jnp.exp(l1 - mx), jnp.exp(l2 - mx)
        num = e0 * o_sc[0, sl, :] + e1 * o_sc[1, sl, :] + e2 * o_sc[2, sl, :]
        o_ref[sl, :] = (num / (e0 + e1 + e2)).astype(o_ref.dtype)
        return carry

    lax.fori_loop(0, seq // chunk, comb, 0)


def _prompt_attn(p, bias, batch, seq, col0):
    def spec(kind, g):
        base = col0 + kind * N_ATTN_HEADS + g * HEADS_PER_GROUP
        return pl.BlockSpec((seq, HEAD_DIM), lambda b, i, base=base: (b, base + i))

    in_specs = []
    for g in range(N_GROUPS):
        in_specs += [spec(0, g), spec(1, g), spec(2, g)]
    in_specs.append(pl.BlockSpec((1, N_GROUPS, N_BACK, 2 * N_BACK), lambda b, i: (i, 0, 0, 0)))
    return pl.pallas_call(
        functools.partial(_prompt_attn_kernel, seq=seq),
        grid=(batch, HEADS_PER_GROUP),
        in_specs=in_specs,
        out_specs=pl.BlockSpec((seq, HEAD_DIM), lambda b, i: (b, i)),
        out_shape=jax.ShapeDtypeStruct((batch * seq, ATTN_OUT_WIDTH), BF16),
        scratch_shapes=[pltpu.VMEM((N_GROUPS, seq, HEAD_DIM), F32), pltpu.VMEM((N_GROUPS, seq, HEAD_DIM), F32)],
        compiler_params=_cparams("arbitrary", "arbitrary"),
        name="prompt_attn",
    )(*([p] * 9), bias)


def _sample_attn_kernel(qkv_ref, c0_ref, c1_ref, c2_ref, b0_ref, b1_ref, b2_ref, bn_ref, o_ref, *, t_new):
    caches = (c0_ref, c1_ref, c2_ref)
    biases = (b0_ref, b1_ref, b2_ref)
    dn = (((1,), (1,)), ((), ()))
    for i in range(HEADS_PER_GROUP):
        outs, lses = [], []
        for g in range(N_GROUPS):
            h = g * HEADS_PER_GROUP + i
            q = qkv_ref[0, :, h * HEAD_DIM:(h + 1) * HEAD_DIM].astype(BF16)
            kn = qkv_ref[0, :, ATTN_WIDTH + h * HEAD_DIM:ATTN_WIDTH + (h + 1) * HEAD_DIM].astype(BF16)
            vn = qkv_ref[0, :, 2 * ATTN_WIDTH + h * HEAD_DIM:2 * ATTN_WIDTH + (h + 1) * HEAD_DIM].astype(BF16)
            c_ref = caches[g]
            if g < 2:
                n_rows = c_ref.shape[1] // CACHE_ROW_TILE
                kc = c_ref[0, pl.ds(i, n_rows, stride=CACHE_ROW_TILE), :].astype(BF16)
                vc = c_ref[0, pl.ds(HEADS_PER_GROUP + i, n_rows, stride=CACHE_ROW_TILE), :].astype(BF16)
            else:
                kc = jnp.concatenate(
                    [c_ref[0, :, t * CACHE_ROW_TILE + i, :] for t in range(t_new)], axis=0).astype(BF16)
                vc = jnp.concatenate(
                    [c_ref[0, :, t * CACHE_ROW_TILE + HEADS_PER_GROUP + i, :] for t in range(t_new)],
                    axis=0).astype(BF16)
            sc = lax.dot_general(q, kc, dn, preferred_element_type=F32) * ATTN_SCALE + biases[g][i]
            sn = lax.dot_general(q, kn, dn, preferred_element_type=F32) * ATTN_SCALE + bn_ref[g, i]
            m = jnp.maximum(jnp.max(sc, axis=-1, keepdims=True), jnp.max(sn, axis=-1, keepdims=True))
            pc = jnp.exp(sc - m)
            pn = jnp.exp(sn - m)
            den = jnp.sum(pc, axis=-1, keepdims=True) + jnp.sum(pn, axis=-1, keepdims=True)
            o = (jnp.dot(pc.astype(BF16), vc, preferred_element_type=F32)
                 + jnp.dot(pn.astype(BF16), vn, preferred_element_type=F32)) / den
            outs.append(o)
            lses.append(m + jnp.log(den))
        mx = jnp.maximum(jnp.maximum(lses[0], lses[1]), lses[2])
        es = [jnp.exp(l - mx) for l in lses]
        num = es[0] * outs[0] + es[1] * outs[1] + es[2] * outs[2]
        o_ref[0, :, i * HEAD_DIM:(i + 1) * HEAD_DIM] = (num / (es[0] + es[1] + es[2])).astype(o_ref.dtype)


def _sample_attn(qkv_s, c0, c1, c2v, bias_c, bias_n):
    db, t_new, _ = qkv_s.shape

    def full(a):
        return pl.BlockSpec(a.shape, lambda b, nd=a.ndim: (0,) * nd)

    return pl.pallas_call(
        functools.partial(_sample_attn_kernel, t_new=t_new),
        grid=(db,),
        in_specs=[
            pl.BlockSpec((1, t_new, qkv_s.shape[-1]), lambda b: (b, 0, 0)),
            pl.BlockSpec((1, c0.shape[1], HEAD_DIM), lambda b: (b, 0, 0)),
            pl.BlockSpec((1, c1.shape[1], HEAD_DIM), lambda b: (b, 0, 0)),
            pl.BlockSpec((1, c2v.shape[1], t_new * CACHE_ROW_TILE, HEAD_DIM), lambda b: (b, 0, 0, 0)),
            full(bias_c[0]), full(bias_c[1]), full(bias_c[2]), full(bias_n),
        ],
        out_specs=pl.BlockSpec((1, t_new, ATTN_OUT_WIDTH), lambda b: (b, 0, 0)),
        out_shape=jax.ShapeDtypeStruct((db, t_new, ATTN_OUT_WIDTH), BF16),
        compiler_params=_cparams("arbitrary"),
        name="sample_attn",
    )(qkv_s, c0, c1, c2v, bias_c[0], bias_c[1], bias_c[2], bias_n)


def _cache_roll_kernel(*refs, n_caches, t_new):
    caches = refs[:n_caches]
    news = refs[n_caches:2 * n_caches]
    outs = refs[2 * n_caches:3 * n_caches]
    for g in range(n_caches):
        rows = caches[g].shape[1]
        outs[g][0, :rows - t_new] = caches[g][0, t_new:]
        outs[g][0, rows - t_new:] = news[g][0]


def _cache_roll(caches, news):
    n = len(caches)
    db, t_new = news[0].shape[:2]

    def per_seq(a):
        return pl.BlockSpec((1,) + a.shape[1:], lambda b: (b, 0, 0, 0))

    return pl.pallas_call(
        functools.partial(_cache_roll_kernel, n_caches=n, t_new=t_new),
        grid=(db,),
        in_specs=[per_seq(c) for c in caches] + [per_seq(a) for a in news],
        out_specs=[per_seq(c) for c in caches],
        out_shape=[jax.ShapeDtypeStruct(c.shape, c.dtype) for c in caches],
        compiler_params=_cparams("arbitrary"),
        name="cache_roll",
    )(*caches, *news)


def _pool_kernel(hist_ref, z_ref, w_ref, s_ref, o_ref, zbuf, *, t_len, pos0, chunk):
    zbuf[0:POOL_HIST, :] = hist_ref[0]
    zbuf[POOL_HIST:, :] = z_ref[...].reshape(t_len, POOL_WIDTH)
    for c in range(t_len // chunk):
        base = POOL_HIST + c * chunk
        for g, w in enumerate(POOL_WINDOWS):
            sl = slice(g * PGW, (g + 1) * PGW)
            cur = zbuf[base:base + chunk, sl]
            tot = cur
            for i in range(1, w):
                tot = tot + zbuf[base - i:base - i + chunk, sl]
            pos = pos0 + c * chunk + lax.broadcasted_iota(jnp.int32, (chunk, PGW), 0)
            cnt = jnp.minimum(w, pos + 1).astype(F32)
            mixed = tot / cnt - cur
            y = jnp.dot(mixed.astype(BF16), w_ref[g], preferred_element_type=F32) * s_ref[:, sl]
            o_ref[0, c * chunk:(c + 1) * chunk, sl] = y.astype(o_ref.dtype)


def _pool_mix(hist, z, z_spec, n_seq, t_len, w_pool, pool_scale, pos0):
    chunk = min(t_len, 256)
    return pl.pallas_call(
        functools.partial(_pool_kernel, t_len=t_len, pos0=pos0, chunk=chunk),
        grid=(n_seq,),
        in_specs=[
            pl.BlockSpec((1, POOL_HIST, POOL_WIDTH), lambda i: (i, 0, 0)),
            z_spec,
            pl.BlockSpec(w_pool.shape, lambda i: (0, 0, 0)),
            pl.BlockSpec((1, POOL_WIDTH), lambda i: (0, 0)),
        ],
        out_specs=pl.BlockSpec((1, t_len, POOL_WIDTH), lambda i: (i, 0, 0)),
        out_shape=jax.ShapeDtypeStruct((n_seq, t_len, POOL_WIDTH), BF16),
        scratch_shapes=[pltpu.VMEM((POOL_HIST + t_len, POOL_WIDTH), F32)],
        compiler_params=_cparams("arbitrary"),
        name="pool_mix",
    )(hist, z, w_pool, pool_scale)


def _merge_kernel(h_ref, a_ref, p_ref, ga_ref, gb_ref, wa_ref, wb_ref, wo_ref, g2_ref, h1_ref, xn_ref):
    ya = jnp.dot(a_ref[...], wa_ref[...], preferred_element_type=F32)
    yb = jnp.dot(p_ref[...], wb_ref[...], preferred_element_type=F32)
    u = jax.nn.sigmoid(ga_ref[...]) * ya + jax.nn.sigmoid(gb_ref[...]) * yb
    h1 = h_ref[...] + jnp.dot(u.astype(BF16), wo_ref[...], preferred_element_type=F32)
    h1_ref[...] = h1
    xn_ref[...] = _rmsnorm_f32(h1, g2_ref[...])


def _merge(h, attn, pool, proj, wa, wb, wo, g2, tm):
    m, d = h.shape

    def const(a):
        return pl.BlockSpec(a.shape, lambda i, nd=a.ndim: (0,) * nd, pipeline_mode=pl.Buffered(1))

    return pl.pallas_call(
        _merge_kernel,
        grid=(m // tm,),
        in_specs=[
            pl.BlockSpec((tm, d), lambda i: (i, 0)),
            pl.BlockSpec((tm, attn.shape[1]), lambda i: (i, 0)),
            pl.BlockSpec((tm, pool.shape[1]), lambda i: (i, 0)),
            pl.BlockSpec((tm, d), lambda i: (i, 0)),
            pl.BlockSpec((tm, d), lambda i: (i, 1)),
            const(wa), const(wb), const(wo), const(g2),
        ],
        out_specs=[pl.BlockSpec((tm, d), lambda i: (i, 0)), pl.BlockSpec((tm, d), lambda i: (i, 0))],
        out_shape=[jax.ShapeDtypeStruct((m, d), F32), jax.ShapeDtypeStruct((m, d), F32)],
        compiler_params=_cparams("arbitrary"),
        name="merge",
    )(h, attn, pool, proj, proj, wa, wb, wo, g2)


def _router_kernel(x_ref, wh_ref, wl_ref, b_ref, e_ref, gate_ref, rank_ref, cnt_ref, carry):
    @pl.when(pl.program_id(0) == 0)
    def _():
        carry[...] = jnp.zeros_like(carry)

    x = x_ref[...]
    xh = x.astype(BF16)
    xl = (x - xh.astype(F32)).astype(BF16)
    logits = (jnp.dot(xh, wh_ref[...], preferred_element_type=F32)
              + jnp.dot(xh, wl_ref[...], preferred_element_type=F32)
              + jnp.dot(xl, wh_ref[...], preferred_element_type=F32)) + b_ref[...]
    n_e = logits.shape[-1]
    lane = lax.broadcasted_iota(jnp.int32, logits.shape, 1)
    vals = logits
    tops, idxs = [], []
    for _ in range(TOP_K):
        m = jnp.max(vals, axis=-1, keepdims=True)
        idx = jnp.min(jnp.where(vals == m, lane, n_e), axis=-1, keepdims=True)
        tops.append(m)
        idxs.append(idx)
        vals = jnp.where(lane == idx, -jnp.inf, vals)
    ex = [jnp.exp(t - tops[0]) for t in tops]
    den = ex[0] + ex[1] + ex[2] + ex[3]
    for k in range(TOP_K):
        e_ref[:, k:k + 1] = idxs[k]
        gate_ref[:, k:k + 1] = ex[k] / den

    onehots = [(lane == idx).astype(F32) for idx in idxs]
    cnt = onehots[0] + onehots[1] + onehots[2] + onehots[3]
    tm = cnt.shape[0]
    lower = (lax.broadcasted_iota(jnp.int32, (tm, tm), 0) > lax.broadcasted_iota(jnp.int32, (tm, tm), 1))
    prefix = jnp.dot(lower.astype(BF16), cnt.astype(BF16), preferred_element_type=F32)
    base = carry[...] + prefix
    for k in range(TOP_K):
        rank_ref[:, k:k + 1] = jnp.sum(onehots[k] * base, axis=-1, keepdims=True).astype(jnp.int32)
    carry[...] = carry[...] + jnp.sum(cnt, axis=0, keepdims=True)
    cnt_ref[...] = carry[...].astype(jnp.int32)


def _router(xn, wh, wl, b, tm):
    m, d = xn.shape
    n_e = wh.shape[1]
    tok_spec = pl.BlockSpec((tm, TOP_K), lambda i: (i, 0))
    return pl.pallas_call(
        _router_kernel,
        grid=(m // tm,),
        in_specs=[
            pl.BlockSpec((tm, d), lambda i: (i, 0)),
            pl.BlockSpec((d, n_e), lambda i: (0, 0)),
            pl.BlockSpec((d, n_e), lambda i: (0, 0)),
            pl.BlockSpec((1, n_e), lambda i: (0, 0)),
        ],
        out_specs=[tok_spec, tok_spec, tok_spec, pl.BlockSpec((1, n_e), lambda i: (0, 0))],
        out_shape=[jax.ShapeDtypeStruct((m, TOP_K), jnp.int32), jax.ShapeDtypeStruct((m, TOP_K), F32),
                   jax.ShapeDtypeStruct((m, TOP_K), jnp.int32), jax.ShapeDtypeStruct((1, n_e), jnp.int32)],
        scratch_shapes=[pltpu.VMEM((1, n_e), F32)],
        compiler_params=_cparams("arbitrary"),
        name="router",
    )(xn, wh, wl, b)


def _row_gather_kernel(nused_ref, tok_ref, src_hbm, o_ref, buf, sem, *, rows):
    t = pl.program_id(0)
    n_used = nused_ref[0]

    def issue(tile, slot):
        def body(j, carry):
            for u in range(DMA_ISSUE_UNROLL):
                i = j * DMA_ISSUE_UNROLL + u
                pltpu.make_async_copy(src_hbm.at[pl.ds(tok_ref[tile * rows + i], 1)],
                                      buf.at[slot, pl.ds(i, 1)], sem.at[slot]).start(priority=u % 2)
            return carry
        lax.fori_loop(0, rows // DMA_ISSUE_UNROLL, body, 0)

    @pl.when(jnp.logical_and(t == 0, n_used > 0))
    def _():
        issue(0, 0)

    @pl.when(t + 1 < n_used)
    def _():
        issue(t + 1, (t + 1) % 2)

    @pl.when(t < n_used)
    def _():
        slot = t % 2
        pltpu.make_async_copy(src_hbm.at[pl.ds(0, rows)], buf.at[slot], sem.at[slot]).wait()
        o_ref[...] = buf[slot].astype(o_ref.dtype)

    @pl.when(t >= n_used)
    def _():
        o_ref[...] = jnp.zeros_like(o_ref)


def _row_gather(n_used, row_idx, src, n_tiles, rows, out_dtype):
    d = src.shape[1]
    grid_spec = pltpu.PrefetchScalarGridSpec(
        num_scalar_prefetch=2,
        grid=(n_tiles,),
        in_specs=[pl.BlockSpec(memory_space=pl.ANY)],
        out_specs=pl.BlockSpec((rows, d), lambda t, *_: (t, 0)),
        scratch_shapes=[pltpu.VMEM((2, rows, d), src.dtype), pltpu.SemaphoreType.DMA((2,))],
    )
    return pl.pallas_call(
        functools.partial(_row_gather_kernel, rows=rows),
        grid_spec=grid_spec,
        out_shape=jax.ShapeDtypeStruct((n_tiles * rows, d), out_dtype),
        compiler_params=_cparams("arbitrary"),
        name="moe_row_gather",
    )(n_used, row_idx, src)


def _start_all(copies, priority=0):
    for c in copies:
        c.start(priority=priority)


def _wait_all(copies):
    for c in copies:
        c.wait()


def _split_rows(n_rows, pieces):
    per = n_rows // pieces
    assert per * pieces == n_rows
    return [pl.ds(p * per, per) for p in range(pieces)]


def _prefetch_step_weights(step, n_steps, copies_for):
    slot = step % 2

    @pl.when(step == 0)
    def _():
        _start_all(copies_for(step, slot), priority=1)

    @pl.when(step + 1 < n_steps)
    def _():
        _start_all(copies_for(step + 1, 1 - slot), priority=1)

    _wait_all(copies_for(step, slot))
    return slot


def _expert_chunk_loop(n_chunks, in_copies, out_copies, compute, prepare):
    @pl.when(n_chunks > 0)
    def _():
        _start_all(in_copies(0, 0))
        prepare()

        def body(r, carry):
            slot = r % 2

            @pl.when(r + 1 < n_chunks)
            def _():
                _start_all(in_copies(r + 1, 1 - slot))

            _wait_all(in_copies(r, slot))

            @pl.when(r >= 2)
            def _():
                _wait_all(out_copies(r - 2, slot))

            compute(slot)
            _start_all(out_copies(r, slot))
            return carry

        lax.fori_loop(0, n_chunks, body, 0)

        @pl.when(n_chunks >= 2)
        def _():
            _wait_all(out_copies(n_chunks - 2, n_chunks % 2))

        _wait_all(out_copies(n_chunks - 1, (n_chunks - 1) % 2))


def _zero_fill_tiles(first, last, zbuf, dst, sem):
    zbuf[...] = jnp.zeros_like(zbuf)

    def copy(t):
        return pltpu.make_async_copy(zbuf, dst(t), sem)

    def start(t, carry):
        copy(t).start()
        return carry

    def wait(t, carry):
        copy(t).wait()
        return carry

    lax.fori_loop(first, last, start, 0)
    lax.fori_loop(first, last, wait, 0)


def _moe_up_kernel(tpe_ref, start_ref, xs_hbm, w_hbm, bg_ref, bu_ref, hid_hbm,
                   wbuf, wg_sc, wu_sc, xbuf, obuf, w_sem, in_sem, out_sem, *, rows):
    e = pl.program_id(0)
    n = pl.program_id(1)
    n_chunks = pl.num_programs(1)
    row0 = start_ref[e]
    d, nc = wg_sc.shape

    def w_copies(step, slot):
        e_, n_ = step // n_chunks, step % n_chunks
        copies = []
        for mat in range(2):
            cols = pl.ds(pl.multiple_of((mat * n_chunks + n_) * nc, nc), nc)
            for rws in _split_rows(d, DMA_SPLIT):
                copies.append(pltpu.make_async_copy(w_hbm.at[e_, rws, cols], wbuf.at[slot, mat, rws], w_sem.at[slot]))
        return copies

    w_slot = _prefetch_step_weights(e * n_chunks + n, pl.num_programs(0) * n_chunks, w_copies)

    def row_pieces(r, pieces):
        per = rows // pieces
        return [(pl.ds(pl.multiple_of(row0 + r * rows + p * per, per), per), pl.ds(p * per, per))
                for p in range(pieces)]

    def in_copies(r, slot):
        return [pltpu.make_async_copy(xs_hbm.at[src], xbuf.at[slot, dst], in_sem.at[slot])
                for src, dst in row_pieces(r, CHUNK_DMA_SPLIT)]

    def out_copies(r, slot):
        return [pltpu.make_async_copy(obuf.at[slot, src], hid_hbm.at[n, dst], out_sem.at[slot])
                for dst, src in row_pieces(r, 1)]

    def cast_weights():
        wg_sc[...] = wbuf[w_slot, 0].astype(BF16)
        wu_sc[...] = wbuf[w_slot, 1].astype(BF16)

    def compute(slot):
        x = xbuf[slot]
        g = jnp.dot(x, wg_sc[...], preferred_element_type=F32) + bg_ref[0]
        u = jnp.dot(x, wu_sc[...], preferred_element_type=F32) + bu_ref[0]
        g = jnp.minimum(g, SWIGLU_LIMIT)
        u = jnp.clip(u, -SWIGLU_LIMIT, SWIGLU_LIMIT)
        obuf[slot] = ((u + 1.0) * (g * jax.nn.sigmoid(SWIGLU_ALPHA * g))).astype(obuf.dtype)

    _expert_chunk_loop(tpe_ref[e], in_copies, out_copies, compute, cast_weights)

    @pl.when(e == pl.num_programs(0) - 1)
    def _():
        _zero_fill_tiles(row0 // rows + tpe_ref[e], hid_hbm.shape[1] // rows, obuf.at[0],
                         lambda t: hid_hbm.at[n, pl.ds(pl.multiple_of(t * rows, rows), rows)], out_sem.at[0])


def _moe_up(tiles_per_e, row_start, xs, w_gu, b_gu, rows, nc):
    r_pad, d = xs.shape
    n_e = w_gu.shape[0]
    d_ff = w_gu.shape[2] // 2
    n_chunks = d_ff // nc
    any_spec = pl.BlockSpec(memory_space=pl.ANY)
    grid_spec = pltpu.PrefetchScalarGridSpec(
        num_scalar_prefetch=2,
        grid=(n_e, n_chunks),
        in_specs=[
            any_spec,
            any_spec,
            pl.BlockSpec((1, 1, nc), lambda e, n, *_: (e, 0, n)),
            pl.BlockSpec((1, 1, nc), lambda e, n, *_: (e, 0, n_chunks + n)),
        ],
        out_specs=any_spec,
        scratch_shapes=[pltpu.VMEM((2, 2, d, nc), w_gu.dtype), pltpu.VMEM((d, nc), BF16), pltpu.VMEM((d, nc), BF16),
                        pltpu.VMEM((2, rows, d), xs.dtype), pltpu.VMEM((2, rows, nc), BF16),
                        pltpu.SemaphoreType.DMA((2,)), pltpu.SemaphoreType.DMA((2,)), pltpu.SemaphoreType.DMA((2,))],
    )
    return pl.pallas_call(
        functools.partial(_moe_up_kernel, rows=rows),
        grid_spec=grid_spec,
        out_shape=jax.ShapeDtypeStruct((n_chunks, r_pad, nc), BF16),
        compiler_params=_cparams("arbitrary", "arbitrary"),
        name="moe_up",
    )(tiles_per_e, row_start, xs, w_gu, b_gu, b_gu)


def _moe_down_kernel(tpe_ref, start_ref, hid_hbm, w_hbm, b_ref, y_hbm,
                     wbuf, w_sc, xbuf, obuf, w_sem, in_sem, out_sem, *, rows):
    e = pl.program_id(0)
    n = pl.program_id(1)
    n_chunks = pl.num_programs(1)
    row0 = start_ref[e]
    k_chunks, _, kc = xbuf.shape[1:]
    d_ff, nc = w_sc.shape

    def w_copies(step, slot):
        e_, n_ = step // n_chunks, step % n_chunks
        cols = pl.ds(pl.multiple_of(n_ * nc, nc), nc)
        return [pltpu.make_async_copy(w_hbm.at[e_, rws, cols], wbuf.at[slot, rws], w_sem.at[slot])
                for rws in _split_rows(d_ff, DMA_SPLIT)]

    w_slot = _prefetch_step_weights(e * n_chunks + n, pl.num_programs(0) * n_chunks, w_copies)

    def chunk_rows(r):
        return pl.ds(pl.multiple_of(row0 + r * rows, rows), rows)

    def in_copies(r, slot):
        return [pltpu.make_async_copy(hid_hbm.at[c, chunk_rows(r)], xbuf.at[slot, c], in_sem.at[slot])
                for c in range(k_chunks)]

    def out_copies(r, slot):
        cols = pl.ds(pl.multiple_of(n * nc, nc), nc)
        per = rows // CHUNK_DMA_SPLIT
        return [pltpu.make_async_copy(
            obuf.at[slot, pl.ds(p * per, per)],
            y_hbm.at[pl.ds(pl.multiple_of(row0 + r * rows + p * per, per), per), cols], out_sem.at[slot])
            for p in range(CHUNK_DMA_SPLIT)]

    def cast_weights():
        w_sc[...] = wbuf[w_slot].astype(BF16)

    def compute(slot):
        acc = b_ref[0] + jnp.dot(xbuf[slot, 0], w_sc[0:kc, :], preferred_element_type=F32)
        for c in range(1, k_chunks):
            acc = acc + jnp.dot(xbuf[slot, c], w_sc[c * kc:(c + 1) * kc, :], preferred_element_type=F32)
        obuf[slot] = acc

    _expert_chunk_loop(tpe_ref[e], in_copies, out_copies, compute, cast_weights)

    @pl.when(e == pl.num_programs(0) - 1)
    def _():
        cols = pl.ds(pl.multiple_of(n * nc, nc), nc)
        _zero_fill_tiles(row0 // rows + tpe_ref[e], y_hbm.shape[0] // rows, obuf.at[0],
                         lambda t: y_hbm.at[pl.ds(pl.multiple_of(t * rows, rows), rows), cols], out_sem.at[0])


def _moe_down(tiles_per_e, row_start, hid, w_dn, b_dn, rows, nc):
    k_chunks, r_pad, kc = hid.shape
    n_e, d_ff, d = w_dn.shape
    n_chunks = d // nc
    any_spec = pl.BlockSpec(memory_space=pl.ANY)
    grid_spec = pltpu.PrefetchScalarGridSpec(
        num_scalar_prefetch=2,
        grid=(n_e, n_chunks),
        in_specs=[
            any_spec,
            any_spec,
            pl.BlockSpec((1, 1, nc), lambda e, n, *_: (e, 0, n)),
        ],
        out_specs=any_spec,
        scratch_shapes=[pltpu.VMEM((2, d_ff, nc), w_dn.dtype), pltpu.VMEM((d_ff, nc), BF16),
                        pltpu.VMEM((2, k_chunks, rows, kc), hid.dtype), pltpu.VMEM((2, rows, nc), F32),
                        pltpu.SemaphoreType.DMA((2,)), pltpu.SemaphoreType.DMA((2,)), pltpu.SemaphoreType.DMA((2,))],
    )
    return pl.pallas_call(
        functools.partial(_moe_down_kernel, rows=rows),
        grid_spec=grid_spec,
        out_shape=jax.ShapeDtypeStruct((r_pad, d), F32),
        compiler_params=_cparams("arbitrary", "arbitrary"),
        name="moe_down",
    )(tiles_per_e, row_start, hid, w_dn, b_dn)


def _combine_kernel(dest_ref, h_ref, gate_ref, gf_ref, y_hbm, o_ref, buf, sem, *, rows, final_norm):
    t = pl.program_id(0)
    n_t = pl.num_programs(0)
    n_dma = rows * TOP_K

    def issue(tile, slot):
        def body(j, carry):
            for u in range(DMA_ISSUE_UNROLL):
                i = j * DMA_ISSUE_UNROLL + u
                pltpu.make_async_copy(y_hbm.at[pl.ds(dest_ref[tile * n_dma + i], 1)],
                                      buf.at[slot, pl.ds(i, 1)], sem.at[slot]).start(priority=u % 2)
            return carry
        lax.fori_loop(0, n_dma // DMA_ISSUE_UNROLL, body, 0)

    @pl.when(t == 0)
    def _():
        issue(0, 0)

    @pl.when(t + 1 < n_t)
    def _():
        issue(t + 1, (t + 1) % 2)

    slot = t % 2
    pltpu.make_async_copy(y_hbm.at[pl.ds(0, n_dma)], buf.at[slot], sem.at[slot]).wait()

    acc = h_ref[...]
    for k in range(TOP_K):
        acc = acc + gate_ref[:, k:k + 1] * buf[slot, k * rows:(k + 1) * rows, :]
    o_ref[...] = _rmsnorm_f32(acc, gf_ref[...]) if final_norm else acc


def _combine(dest_km, h1, gate, gf, y, rows, final_norm):
    m, d = h1.shape
    grid_spec = pltpu.PrefetchScalarGridSpec(
        num_scalar_prefetch=1,
        grid=(m // rows,),
        in_specs=[
            pl.BlockSpec((rows, d), lambda t, *_: (t, 0)),
            pl.BlockSpec((rows, TOP_K), lambda t, *_: (t, 0)),
            pl.BlockSpec((1, d), lambda t, *_: (0, 0)),
            pl.BlockSpec(memory_space=pl.ANY),
        ],
        out_specs=pl.BlockSpec((rows, d), lambda t, *_: (t, 0)),
        scratch_shapes=[pltpu.VMEM((2, rows * TOP_K, d), F32), pltpu.SemaphoreType.DMA((2,))],
    )
    return pl.pallas_call(
        functools.partial(_combine_kernel, rows=rows, final_norm=final_norm),
        grid_spec=grid_spec,
        out_shape=jax.ShapeDtypeStruct((m, d), F32),
        compiler_params=_cparams("arbitrary"),
        name="moe_combine",
    )(dest_km, h1, gate, gf, y)


def _routing_tables(top_e, rank, counts, rows, n_tiles):
    m = top_e.shape[0]
    n_experts = counts.shape[0]
    tiles_per_e = (counts + rows - 1) // rows
    tile_end = jnp.cumsum(tiles_per_e)
    pad_start = (tile_end - tiles_per_e) * rows
    experts = jnp.arange(n_experts, dtype=jnp.int32)
    dest = rank + jnp.sum(jnp.where(top_e[..., None] == experts, pad_start, 0), axis=-1)
    dest = dest.astype(jnp.int32).reshape(-1)
    n_used = tile_end[-1:].astype(jnp.int32)
    tok = jnp.arange(m * TOP_K, dtype=jnp.int32) // TOP_K
    row_tok = jnp.zeros((n_tiles * rows,), jnp.int32).at[dest].set(tok)
    return dest, row_tok, tiles_per_e.astype(jnp.int32), pad_start.astype(jnp.int32), n_used


def _largest_divisor(n, cap, mult):
    best = mult
    for c in range(mult, cap + 1, mult):
        if n % c == 0:
            best = c
    return best


def kernel(x_prompt, x_sample, cache_kv_w128, cache_kv_w512, cache_kv_w2048, state_pool, rel_bias, norm1, w_in,
           w_branch_a, w_branch_b, w_out, w_pool, pool_scale, norm2, router_w, router_b, w_gate_up, b_gate_up,
           w_down, b_down, norm_f):
    batch, seq, d = x_prompt.shape
    db, t_new, _ = x_sample.shape
    depth = norm1.shape[0]
    n_experts = router_w.shape[-1]
    caches = (cache_kv_w128, cache_kv_w512, cache_kv_w2048)
    m_p, m_s = batch * seq, db * t_new
    m = m_p + m_s
    tm = _largest_divisor(m, 640, 8)

    h = jnp.concatenate([x_prompt.reshape(m_p, d), x_sample.reshape(m_s, d)], axis=0)
    bias_p = _prompt_bias_tiles(rel_bias)
    bias_sc, bias_sn = _sample_bias_tiles(rel_bias, t_new)

    cuts = [ATTN_WIDTH, 2 * ATTN_WIDTH, 3 * ATTN_WIDTH, 3 * ATTN_WIDTH + POOL_WIDTH, 3 * ATTN_WIDTH + POOL_WIDTH + d]
    qkv0 = 2 * d
    z0 = qkv0 + 3 * ATTN_WIDTH

    kv_p = [[] for _ in range(N_GROUPS)]
    kv_s = [[] for _ in range(N_GROUPS)]
    pool_p, pool_s = [], []
    for l in range(depth):
        wq, wk, wv, wz, wga, wgb = jnp.split(w_in[l], cuts, axis=-1)
        w_perm = jnp.concatenate([wga, wgb, wq, wk, wv, wz], axis=-1).astype(BF16)
        proj = _inproj(h, norm1[l][None, :], w_perm, tm, _largest_divisor(w_perm.shape[1], 1536, 128))

        k0, v0 = qkv0 + ATTN_WIDTH, qkv0 + 2 * ATTN_WIDTH
        kv_w = HEADS_PER_GROUP * HEAD_DIM
        proj_s = proj[m_p:]
        z_s = proj_s[:, z0:z0 + POOL_WIDTH].reshape(db, t_new, POOL_WIDTH)

        attn_p = _prompt_attn(proj, bias_p, batch, seq, qkv0 // HEAD_DIM)
        qkv_s = proj_s[:, qkv0:qkv0 + 3 * ATTN_WIDTH].reshape(db, t_new, 3 * ATTN_WIDTH)
        c_rows = [caches[g][l].reshape(db, -1, CACHE_ROW_TILE, HEAD_DIM) for g in range(N_GROUPS)]
        c0 = c_rows[0].reshape(db, -1, HEAD_DIM)
        c1 = c_rows[1].reshape(db, -1, HEAD_DIM)
        dil2 = ATTN_GROUPS[2][1]
        c2v = c_rows[2].reshape(db, -1, dil2 * CACHE_ROW_TILE, HEAD_DIM)
        attn_s = _sample_attn(qkv_s, c0, c1, c2v, bias_sc, bias_sn)
        attn = jnp.concatenate([attn_p, attn_s.reshape(m_s, ATTN_OUT_WIDTH)], axis=0)

        news = []
        for g, (window, _) in enumerate(ATTN_GROUPS):
            keep = min(window, seq)
            kc, vc = k0 + g * kv_w, v0 + g * kv_w

            def kv_rows(a, r0, r1, kc=kc, vc=vc):
                return jnp.stack([a[r0:r1, kc:kc + kv_w].reshape(-1, HEADS_PER_GROUP, HEAD_DIM),
                                  a[r0:r1, vc:vc + kv_w].reshape(-1, HEADS_PER_GROUP, HEAD_DIM)], axis=1)

            kv_p[g].append(jnp.stack([kv_rows(proj, (b + 1) * seq - keep, (b + 1) * seq) for b in range(batch)]))
            news.append(kv_rows(proj_s, 0, m_s).reshape(db, t_new, CACHE_ROW_TILE, HEAD_DIM))
        rolled = _cache_roll(c_rows, news)
        for g in range(N_GROUPS):
            kv_s[g].append(rolled[g].reshape(caches[g][l].shape))

        w_pool_b = w_pool[l].astype(BF16)
        scale = pool_scale[l][None, :]
        hist_p = jnp.zeros((batch, POOL_HIST, POOL_WIDTH), F32)
        hist_s = jnp.concatenate([jnp.zeros((db, POOL_HIST - POOL_BUF, POOL_WIDTH), F32), state_pool[l]], axis=1)
        pool = jnp.concatenate([
            _pool_mix(hist_p, proj, pl.BlockSpec((seq, POOL_WIDTH), lambda i: (i, z0 // POOL_WIDTH)),
                      batch, seq, w_pool_b, scale, 0).reshape(m_p, POOL_WIDTH),
            _pool_mix(hist_s, z_s, pl.BlockSpec((1, t_new, POOL_WIDTH), lambda i: (i, 0, 0)),
                      db, t_new, w_pool_b, scale, PAST_LEN).reshape(m_s, POOL_WIDTH),
        ], axis=0)
        pool_p.append(jnp.stack([proj[(b + 1) * seq - POOL_BUF:(b + 1) * seq, z0:z0 + POOL_WIDTH]
                                 for b in range(batch)]))
        pool_s.append(jnp.concatenate([state_pool[l], z_s], axis=1)[:, t_new:])

        h1, xn2 = _merge(h, attn, pool, proj, w_branch_a[l].astype(BF16), w_branch_b[l].astype(BF16),
                         w_out[l].astype(BF16), norm2[l][None, :], _largest_divisor(m, 320, 8))

        rw = router_w[l]
        rw_hi = rw.astype(BF16)
        rw_lo = (rw - rw_hi.astype(F32)).astype(BF16)
        top_e, gate, rank, counts = _router(xn2, rw_hi, rw_lo, router_b[l][None, :], tm)
        rows = MOE_ROW_TILE
        n_tiles = (m * TOP_K + n_experts * (rows - 1)) // rows
        dest, row_tok, tiles_per_e, row_start, n_used = _routing_tables(top_e, rank, counts[0], rows, n_tiles)
        xs = _row_gather(n_used, row_tok, xn2, n_tiles, rows, BF16)
        hid = _moe_up(tiles_per_e, row_start, xs, w_gate_up[l], b_gate_up[l][:, None, :], rows, 512)
        y = _moe_down(tiles_per_e, row_start, hid, w_down[l], b_down[l][:, None, :], rows, 1024)
        crow = _largest_divisor(m, 128, 8)
        dest_km = dest.reshape(m // crow, crow, TOP_K).transpose(0, 2, 1).reshape(-1)
        h = _combine(dest_km, h1, gate, norm_f[None, :], y, crow, final_norm=(l == depth - 1))

    y_prompt = h[:m_p].reshape(batch, seq, d)
    y_sample = h[m_p:].reshape(db, t_new, d)
    return (y_prompt, y_sample,
            jnp.stack(kv_p[0]), jnp.stack(kv_p[1]), jnp.stack(kv_p[2]), jnp.stack(pool_p),
            jnp.stack(kv_s[0]), jnp.stack(kv_s[1]), jnp.stack(kv_s[2]), jnp.stack(pool_s))
```

```python
import functools
import math

import jax
import jax.numpy as jnp
from jax import lax
from jax.experimental import pallas as pl
from jax.experimental.pallas import tpu as pltpu

F32 = jnp.float32
BF16 = jnp.bfloat16

HEAD_DIM = 128
HEADS_PER_GROUP = 4
ATTN_GROUPS = ((128, 1), (512, 4), (2048, 16))
N_GROUPS = len(ATTN_GROUPS)
N_ATTN_HEADS = N_GROUPS * HEADS_PER_GROUP
ATTN_WIDTH = N_ATTN_HEADS * HEAD_DIM
ATTN_OUT_WIDTH = HEADS_PER_GROUP * HEAD_DIM
ATTN_SCALE = HEAD_DIM ** -0.5
N_BACK = 128
N_REL_BUCKETS = 32
REL_MAX_DIST = 2048
POOL_WINDOWS = (2, 4, 8, 16)
POOL_GROUPS = len(POOL_WINDOWS)
PGW = 128
POOL_WIDTH = POOL_GROUPS * PGW
POOL_BUF = max(POOL_WINDOWS) - 1
POOL_HIST = 16
TOP_K = 4
SWIGLU_LIMIT = 7.0
SWIGLU_ALPHA = 1.702
NORM_EPS = 1e-5
NEG_INF = -1e30
PAST_LEN = 8192
CACHE_ROW_TILE = 2 * HEADS_PER_GROUP

VMEM_LIMIT_BYTES = 56 * 1024 * 1024
MOE_ROW_TILE = 256
DMA_ISSUE_UNROLL = 8
DMA_SPLIT = 8
CHUNK_DMA_SPLIT = 4
ATTN_INTERLEAVE = 4
LANES = 128
CHUNK_IN_SLOTS = 3
MOE_COL_BLOCK = 512


def _cparams(*sem):
    return pltpu.CompilerParams(dimension_semantics=sem, vmem_limit_bytes=VMEM_LIMIT_BYTES)


def _rmsnorm_f32(x, g):
    return x * lax.rsqrt(jnp.mean(x * x, axis=-1, keepdims=True) + NORM_EPS) * g


def _inproj_kernel(*refs):
    x_refs, (g_ref, w_ref, o_ref) = refs[:CHUNK_DMA_SPLIT], refs[CHUNK_DMA_SPLIT:]
    xs = [r[...] for r in x_refs]
    kc = xs[0].shape[1]
    sq = xs[0] * xs[0]
    for x in xs[1:]:
        sq = sq + x * x
    inv = lax.rsqrt(jnp.sum(sq, axis=-1, keepdims=True) / (kc * len(xs)) + NORM_EPS)
    acc = None
    for k, x in enumerate(xs):
        xn = (x * inv * g_ref[:, k * kc:(k + 1) * kc]).astype(BF16)
        part = jnp.dot(xn, w_ref[k * kc:(k + 1) * kc, :], preferred_element_type=F32)
        acc = part if acc is None else acc + part
    o_ref[...] = acc


def _inproj(x, g, w, tm, tn):
    m, d = x.shape
    n = w.shape[1]
    kc = d // CHUNK_DMA_SPLIT
    return pl.pallas_call(
        _inproj_kernel,
        grid=(n // tn, m // tm),
        in_specs=[pl.BlockSpec((tm, kc), lambda j, i, k=k: (i, k)) for k in range(CHUNK_DMA_SPLIT)] + [
            pl.BlockSpec((1, d), lambda j, i: (0, 0)),
            pl.BlockSpec((d, tn), lambda j, i: (0, j)),
        ],
        out_specs=pl.BlockSpec((tm, tn), lambda j, i: (i, j)),
        out_shape=jax.ShapeDtypeStruct((m, n), F32),
        compiler_params=_cparams("arbitrary", "arbitrary"),
        name="inproj",
    )(*([x] * CHUNK_DMA_SPLIT), g, w)


def _t5_bucket(dist):
    max_exact = N_REL_BUCKETS // 2
    df = jnp.maximum(dist, 1).astype(F32)
    large = max_exact + (jnp.log(df / max_exact) / math.log(REL_MAX_DIST / max_exact)
                         * (N_REL_BUCKETS - max_exact)).astype(jnp.int32)
    large = jnp.minimum(large, N_REL_BUCKETS - 1)
    return jnp.where(dist < max_exact, dist, large)


def _step_bias(rel_bias, g):
    _, dil = ATTN_GROUPS[g]
    buckets = _t5_bucket(jnp.arange(N_BACK + 1, dtype=jnp.int32) * dil)
    tab = rel_bias[buckets].astype(F32)
    return tab[:, g * HEADS_PER_GROUP:(g + 1) * HEADS_PER_GROUP].T


def _prompt_bias_tiles(rel_bias):
    nb = N_BACK
    period = 3 * nb - 1
    tiles = []
    for g in range(N_GROUPS):
        bj = _step_bias(rel_bias, g)
        h = bj.shape[0]
        pad = jnp.full((h, nb - 1), NEG_INF, F32)
        v = jnp.concatenate([pad, bj[:, ::-1], pad], axis=1)
        skew = jnp.tile(v, (1, nb + 1))[:, :nb * (period + 1)].reshape(h, nb, period + 1)
        tiles.append(skew[:, ::-1, :2 * nb])
    return jnp.stack(tiles, axis=1)


def _sample_bias_tiles(rel_bias, t_new):
    tq = jnp.arange(t_new)[:, None]
    outs = []
    new = []
    for g, (window, dil) in enumerate(ATTN_GROUPS):
        bj = _step_bias(rel_bias, g)
        if g < 2:
            p = jnp.arange(window)[None, :]
            dist = window + tq - p
            ok = (dist % dil == 0) & (dist // dil <= N_BACK)
            outs.append(jnp.where(ok[None], bj[:, jnp.clip(dist // dil, 0, N_BACK)], NEG_INF))
        else:
            col = jnp.arange(t_new * N_BACK)[None, :]
            blk, r = col // N_BACK, col % N_BACK
            ok = blk == tq
            outs.append(jnp.where(ok[None], bj[:, jnp.broadcast_to(N_BACK - r, (t_new, t_new * N_BACK))], NEG_INF))
        tk = jnp.arange(t_new)[None, :]
        d = tq - tk
        okn = (d >= 0) & (d % dil == 0) & (d // dil <= N_BACK)
        new.append(jnp.where(okn[None], bj[:, jnp.clip(d // dil, 0, N_BACK)], NEG_INF))
    return outs, jnp.stack(new, axis=0)


def _softmax_block(s, v):
    m = jnp.max(s, axis=-1, keepdims=True)
    p = jnp.exp(s - m)
    den = jnp.sum(p, axis=-1, keepdims=True)
    o = jnp.dot(p.astype(BF16), v, preferred_element_type=F32) / den
    return o, m + jnp.log(den)


def _prompt_attn_kernel(q0, k0, v0, q1, k1, v1, q2, k2, v2, bias_ref, o_ref, o_sc, l_sc, *, seq):
    qkv = ((q0, k0, v0), (q1, k1, v1), (q2, k2, v2))
    nb = N_BACK

    def rows(ref, start, size, dil):
        if dil == 1:
            return ref[pl.ds(start, size), :]
        return ref[pl.ds(start, size, stride=dil), :]

    def put(g, start, dil, o, lse):
        idx = pl.ds(start, nb) if dil == 1 else pl.ds(start, nb, stride=dil)
        o_sc[g, idx, :] = o
        l_sc[g, idx, :] = jnp.broadcast_to(lse, (nb, HEAD_DIM))

    dn = (((1,), (1,)), ((), ()))

    def attend(g, dil, q_starts, kv_starts, n_kv):
        q_ref, k_ref, v_ref = qkv[g]
        bias = bias_ref[0, g] if n_kv == 2 * nb else bias_ref[0, g, :, nb:]
        qs = [rows(q_ref, st, nb, dil).astype(BF16) for st in q_starts]
        ks = [rows(k_ref, st, n_kv, dil).astype(BF16) for st in kv_starts]
        ss = [lax.dot_general(q, k, dn, preferred_element_type=F32) * ATTN_SCALE + bias for q, k in zip(qs, ks)]
        ms = [jnp.max(s, axis=-1, keepdims=True) for s in ss]
        ps = [jnp.exp(s - m) for s, m in zip(ss, ms)]
        dens = [jnp.sum(p, axis=-1, keepdims=True) for p in ps]
        vs = [rows(v_ref, st, n_kv, dil).astype(BF16) for st in kv_starts]
        for st, p, v, m, den in zip(q_starts, ps, vs, ms, dens):
            o = jnp.dot(p.astype(BF16), v, preferred_element_type=F32) / den
            put(g, st, dil, o, m + jnp.log(den))

    for g, (_, dil) in enumerate(ATTN_GROUPS):
        n_blk = seq // (dil * nb)
        span = dil * nb
        for r0 in range(0, dil, ATTN_INTERLEAVE):
            starts = list(range(r0, min(r0 + ATTN_INTERLEAVE, dil)))
            attend(g, dil, starts, starts, nb)
        if n_blk > 1:
            per_iter = max(c for c in range(1, max(1, ATTN_INTERLEAVE // dil) + 1) if (n_blk - 1) % c == 0)

            def body(it, carry, g=g, dil=dil, span=span, per_iter=per_iter):
                q_starts, kv_starts = [], []
                for j in range(per_iter):
                    b = 1 + it * per_iter + j
                    for r in range(dil):
                        q_starts.append(b * span + r)
                        kv_starts.append((b - 1) * span + r)
                attend(g, dil, q_starts, kv_starts, 2 * nb)
                return carry

            lax.fori_loop(0, (n_blk - 1) // per_iter, body, 0)

    chunk = 256

    def comb(c, carry):
        sl = pl.ds(pl.multiple_of(c * chunk, chunk), chunk)
        l0, l1, l2 = l_sc[0, sl, :], l_sc[1, sl, :], l_sc[2, sl, :]
        mx = jnp.maximum(jnp.maximum(l0, l1), l2)
        e0, e1, e2 = jnp.exp(l0 - mx), jnp.exp(l1 - mx), jnp.exp(l2 - mx)
        num = e0 * o_sc[0, sl, :] + e1 * o_sc[1, sl, :] + e2 * o_sc[2, sl, :]
        o_ref[sl, :] = (num / (e0 + e1 + e2)).astype(o_ref.dtype)
        return carry

    lax.fori_loop(0, seq // chunk, comb, 0)


def _prompt_attn(p, bias, batch, seq, col0):
    def spec(kind, g):
        base = col0 + kind * N_ATTN_HEADS + g * HEADS_PER_GROUP
        return pl.BlockSpec((seq, HEAD_DIM), lambda b, i, base=base: (b, base + i))

    in_specs = []
    for g in range(N_GROUPS):
        in_specs += [spec(0, g), spec(1, g), spec(2, g)]
    in_specs.append(pl.BlockSpec((1, N_GROUPS, N_BACK, 2 * N_BACK), lambda b, i: (i, 0, 0, 0)))
    return pl.pallas_call(
        functools.partial(_prompt_attn_kernel, seq=seq),
        grid=(batch, HEADS_PER_GROUP),
        in_specs=in_specs,
        out_specs=pl.BlockSpec((seq, HEAD_DIM), lambda b, i: (b, i)),
        out_shape=jax.ShapeDtypeStruct((batch * seq, ATTN_OUT_WIDTH), BF16),
        scratch_shapes=[pltpu.VMEM((N_GROUPS, seq, HEAD_DIM), F32), pltpu.VMEM((N_GROUPS, seq, HEAD_DIM), F32)],
        compiler_params=_cparams("arbitrary", "arbitrary"),
        name="prompt_attn",
    )(*([p] * 9), bias)


def _sample_attn_kernel(qkv_ref, c0_ref, c1_ref, c2_ref, b0_ref, b1_ref, b2_ref, bn_ref, o_ref, *, t_new):
    caches = (c0_ref, c1_ref, c2_ref)
    biases = (b0_ref, b1_ref, b2_ref)
    dn = (((1,), (1,)), ((), ()))
    for i in range(HEADS_PER_GROUP):
        outs, lses = [], []
        for g in range(N_GROUPS):
            h = g * HEADS_PER_GROUP + i
            q = qkv_ref[0, :, h * HEAD_DIM:(h + 1) * HEAD_DIM].astype(BF16)
            kn = qkv_ref[0, :, ATTN_WIDTH + h * HEAD_DIM:ATTN_WIDTH + (h + 1) * HEAD_DIM].astype(BF16)
            vn = qkv_ref[0, :, 2 * ATTN_WIDTH + h * HEAD_DIM:2 * ATTN_WIDTH + (h + 1) * HEAD_DIM].astype(BF16)
            c_ref = caches[g]
            if g < 2:
                n_rows = c_ref.shape[1] // CACHE_ROW_TILE
                kc = c_ref[0, pl.ds(i, n_rows, stride=CACHE_ROW_TILE), :].astype(BF16)
                vc = c_ref[0, pl.ds(HEADS_PER_GROUP + i, n_rows, stride=CACHE_ROW_TILE), :].astype(BF16)
            else:
                kc = jnp.concatenate(
                    [c_ref[0, :, t * CACHE_ROW_TILE + i, :] for t in range(t_new)], axis=0).astype(BF16)
                vc = jnp.concatenate(
                    [c_ref[0, :, t * CACHE_ROW_TILE + HEADS_PER_GROUP + i, :] for t in range(t_new)],
                    axis=0).astype(BF16)
            sc = lax.dot_general(q, kc, dn, preferred_element_type=F32) * ATTN_SCALE + biases[g][i]
            sn = lax.dot_general(q, kn, dn, preferred_element_type=F32) * ATTN_SCALE + bn_ref[g, i]
            m = jnp.maximum(jnp.max(sc, axis=-1, keepdims=True), jnp.max(sn, axis=-1, keepdims=True))
            pc = jnp.exp(sc - m)
            pn = jnp.exp(sn - m)
            den = jnp.sum(pc, axis=-1, keepdims=True) + jnp.sum(pn, axis=-1, keepdims=True)
            o = (jnp.dot(pc.astype(BF16), vc, preferred_element_type=F32)
                 + jnp.dot(pn.astype(BF16), vn, preferred_element_type=F32)) / den
            outs.append(o)
            lses.append(m + jnp.log(den))
        mx = jnp.maximum(jnp.maximum(lses[0], lses[1]), lses[2])
        es = [jnp.exp(l - mx) for l in lses]
        num = es[0] * outs[0] + es[1] * outs[1] + es[2] * outs[2]
        o_ref[0, :, i * HEAD_DIM:(i + 1) * HEAD_DIM] = (num / (es[0] + es[1] + es[2])).astype(o_ref.dtype)


def _sample_attn(qkv_s, c0, c1, c2v, bias_c, bias_n):
    db, t_new, _ = qkv_s.shape

    def full(a):
        return pl.BlockSpec(a.shape, lambda b, nd=a.ndim: (0,) * nd)

    return pl.pallas_call(
        functools.partial(_sample_attn_kernel, t_new=t_new),
        grid=(db,),
        in_specs=[
            pl.BlockSpec((1, t_new, qkv_s.shape[-1]), lambda b: (b, 0, 0)),
            pl.BlockSpec((1, c0.shape[1], HEAD_DIM), lambda b: (b, 0, 0)),
            pl.BlockSpec((1, c1.shape[1], HEAD_DIM), lambda b: (b, 0, 0)),
            pl.BlockSpec((1, c2v.shape[1], t_new * CACHE_ROW_TILE, HEAD_DIM), lambda b: (b, 0, 0, 0)),
            full(bias_c[0]), full(bias_c[1]), full(bias_c[2]), full(bias_n),
        ],
        out_specs=pl.BlockSpec((1, t_new, ATTN_OUT_WIDTH), lambda b: (b, 0, 0)),
        out_shape=jax.ShapeDtypeStruct((db, t_new, ATTN_OUT_WIDTH), BF16),
        compiler_params=_cparams("arbitrary"),
        name="sample_attn",
    )(qkv_s, c0, c1, c2v, bias_c[0], bias_c[1], bias_c[2], bias_n)


def _cache_roll_kernel(*refs, n_caches, t_new):
    caches = refs[:n_caches]
    news = refs[n_caches:2 * n_caches]
    outs = refs[2 * n_caches:3 * n_caches]
    for g in range(n_caches):
        rows = caches[g].shape[1]
        outs[g][0, :rows - t_new] = caches[g][0, t_new:]
        outs[g][0, rows - t_new:] = news[g][0]


def _cache_roll(caches, news):
    n = len(caches)
    db, t_new = news[0].shape[:2]

    def per_seq(a):
        return pl.BlockSpec((1,) + a.shape[1:], lambda b: (b, 0, 0, 0))

    return pl.pallas_call(
        functools.partial(_cache_roll_kernel, n_caches=n, t_new=t_new),
        grid=(db,),
        in_specs=[per_seq(c) for c in caches] + [per_seq(a) for a in news],
        out_specs=[per_seq(c) for c in caches],
        out_shape=[jax.ShapeDtypeStruct(c.shape, c.dtype) for c in caches],
        compiler_params=_cparams("arbitrary"),
        name="cache_roll",
    )(*caches, *news)


def _pool_kernel(hist_ref, z_ref, w_ref, s_ref, o_ref, zbuf, *, t_len, pos0, chunk):
    zbuf[0:POOL_HIST, :] = hist_ref[0]
    zbuf[POOL_HIST:, :] = z_ref[...].reshape(t_len, POOL_WIDTH)
    for c in range(t_len // chunk):
        base = POOL_HIST + c * chunk
        for g, w in enumerate(POOL_WINDOWS):
            sl = slice(g * PGW, (g + 1) * PGW)
            cur = zbuf[base:base + chunk, sl]
            tot = cur
            for i in range(1, w):
                tot = tot + zbuf[base - i:base - i + chunk, sl]
            pos = pos0 + c * chunk + lax.broadcasted_iota(jnp.int32, (chunk, PGW), 0)
            cnt = jnp.minimum(w, pos + 1).astype(F32)
            mixed = tot / cnt - cur
            y = jnp.dot(mixed.astype(BF16), w_ref[g], preferred_element_type=F32) * s_ref[:, sl]
            o_ref[0, c * chunk:(c + 1) * chunk, sl] = y.astype(o_ref.dtype)


def _pool_mix(hist, z, z_spec, n_seq, t_len, w_pool, pool_scale, pos0):
    chunk = min(t_len, 256)
    return pl.pallas_call(
        functools.partial(_pool_kernel, t_len=t_len, pos0=pos0, chunk=chunk),
        grid=(n_seq,),
        in_specs=[
            pl.BlockSpec((1, POOL_HIST, POOL_WIDTH), lambda i: (i, 0, 0)),
            z_spec,
            pl.BlockSpec(w_pool.shape, lambda i: (0, 0, 0)),
            pl.BlockSpec((1, POOL_WIDTH), lambda i: (0, 0)),
        ],
        out_specs=pl.BlockSpec((1, t_len, POOL_WIDTH), lambda i: (i, 0, 0)),
        out_shape=jax.ShapeDtypeStruct((n_seq, t_len, POOL_WIDTH), BF16),
        scratch_shapes=[pltpu.VMEM((POOL_HIST + t_len, POOL_WIDTH), F32)],
        compiler_params=_cparams("arbitrary"),
        name="pool_mix",
    )(hist, z, w_pool, pool_scale)


def _merge_kernel(h_ref, a_ref, p_ref, ga_ref, gb_ref, wa_ref, wb_ref, wo_ref, g2_ref, rwh_ref, rwl_ref, rb_ref,
                  h1_ref, xn_ref, e_ref, gate_ref, rank_ref, cnt_ref, carry):
    ya = jnp.dot(a_ref[...], wa_ref[...], preferred_element_type=F32)
    yb = jnp.dot(p_ref[...], wb_ref[...], preferred_element_type=F32)
    u = jax.nn.sigmoid(ga_ref[...]) * ya + jax.nn.sigmoid(gb_ref[...]) * yb
    h1 = h_ref[...] + jnp.dot(u.astype(BF16), wo_ref[...], preferred_element_type=F32)
    h1_ref[...] = h1
    xn = _rmsnorm_f32(h1, g2_ref[...])
    tm, d = xn.shape
    groups = d // LANES
    for j in range(groups):
        xn_ref[pl.ds(j, tm, stride=groups), :] = xn[:, j * LANES:(j + 1) * LANES]
    _route_tile(xn, rwh_ref, rwl_ref, rb_ref, e_ref, gate_ref, rank_ref, cnt_ref, carry)


def _merge(h, attn, pool, proj, wa, wb, wo, g2, rwh, rwl, rb, tm):
    m, d = h.shape
    n_e = rwh.shape[1]

    def const(a):
        return pl.BlockSpec(a.shape, lambda i, nd=a.ndim: (0,) * nd, pipeline_mode=pl.Buffered(1))

    tok_spec = pl.BlockSpec((tm, TOP_K), lambda i: (i, 0))
    return pl.pallas_call(
        _merge_kernel,
        grid=(m // tm,),
        in_specs=[
            pl.BlockSpec((tm, d), lambda i: (i, 0)),
            pl.BlockSpec((tm, attn.shape[1]), lambda i: (i, 0)),
            pl.BlockSpec((tm, pool.shape[1]), lambda i: (i, 0)),
            pl.BlockSpec((tm, d), lambda i: (i, 0)),
            pl.BlockSpec((tm, d), lambda i: (i, 1)),
            const(wa), const(wb), const(wo), const(g2), const(rwh), const(rwl), const(rb),
        ],
        out_specs=[pl.BlockSpec((tm, d), lambda i: (i, 0)),
                   pl.BlockSpec((tm * d // LANES, LANES), lambda i: (i, 0)),
                   tok_spec, tok_spec, tok_spec, pl.BlockSpec((1, n_e), lambda i: (0, 0))],
        out_shape=[jax.ShapeDtypeStruct((m, d), F32), jax.ShapeDtypeStruct((m * d // LANES, LANES), F32),
                   jax.ShapeDtypeStruct((m, TOP_K), jnp.int32), jax.ShapeDtypeStruct((m, TOP_K), F32),
                   jax.ShapeDtypeStruct((m, TOP_K), jnp.int32), jax.ShapeDtypeStruct((1, n_e), jnp.int32)],
        scratch_shapes=[pltpu.VMEM((1, n_e), F32)],
        compiler_params=_cparams("arbitrary"),
        name="merge_route",
    )(h, attn, pool, proj, proj, wa, wb, wo, g2, rwh, rwl, rb)


def _route_tile(x, wh_ref, wl_ref, b_ref, e_ref, gate_ref, rank_ref, cnt_ref, carry):
    @pl.when(pl.program_id(0) == 0)
    def _():
        carry[...] = jnp.zeros_like(carry)

    xh = x.astype(BF16)
    xl = (x - xh.astype(F32)).astype(BF16)
    logits = (jnp.dot(xh, wh_ref[...], preferred_element_type=F32)
              + jnp.dot(xh, wl_ref[...], preferred_element_type=F32)
              + jnp.dot(xl, wh_ref[...], preferred_element_type=F32)) + b_ref[...]
    n_e = logits.shape[-1]
    lane = lax.broadcasted_iota(jnp.int32, logits.shape, 1)
    vals = logits
    tops, idxs = [], []
    for _ in range(TOP_K):
        m = jnp.max(vals, axis=-1, keepdims=True)
        idx = jnp.min(jnp.where(vals == m, lane, n_e), axis=-1, keepdims=True)
        tops.append(m)
        idxs.append(idx)
        vals = jnp.where(lane == idx, -jnp.inf, vals)
    ex = [jnp.exp(t - tops[0]) for t in tops]
    den = ex[0] + ex[1] + ex[2] + ex[3]
    for k in range(TOP_K):
        e_ref[:, k:k + 1] = idxs[k]
        gate_ref[:, k:k + 1] = ex[k] / den

    onehots = [(lane == idx).astype(F32) for idx in idxs]
    cnt = onehots[0] + onehots[1] + onehots[2] + onehots[3]
    tm = cnt.shape[0]
    lower = (lax.broadcasted_iota(jnp.int32, (tm, tm), 0) > lax.broadcasted_iota(jnp.int32, (tm, tm), 1))
    prefix = jnp.dot(lower.astype(BF16), cnt.astype(BF16), preferred_element_type=F32)
    base = carry[...] + prefix
    for k in range(TOP_K):
        rank_ref[:, k:k + 1] = jnp.sum(onehots[k] * base, axis=-1, keepdims=True).astype(jnp.int32)
    carry[...] = carry[...] + jnp.sum(cnt, axis=0, keepdims=True)
    cnt_ref[...] = carry[...].astype(jnp.int32)


def _row_gather_kernel(nused_ref, tok_ref, src_hbm, o_ref, buf, sem, *, rows):
    t = pl.program_id(0)
    n_used = nused_ref[0]
    groups = o_ref.shape[1] // LANES

    def issue(tile, slot):
        def body(j, carry):
            for u in range(DMA_ISSUE_UNROLL):
                i = j * DMA_ISSUE_UNROLL + u
                src = pl.ds(pl.multiple_of(tok_ref[tile * rows + i], groups), groups)
                dst = pl.ds(pl.multiple_of(i * groups, groups), groups)
                pltpu.make_async_copy(src_hbm.at[src], buf.at[slot, dst], sem.at[slot]).start(priority=u % 2)
            return carry
        lax.fori_loop(0, rows // DMA_ISSUE_UNROLL, body, 0)

    @pl.when(jnp.logical_and(t == 0, n_used > 0))
    def _():
        issue(0, 0)

    @pl.when(t + 1 < n_used)
    def _():
        issue(t + 1, (t + 1) % 2)

    @pl.when(t < n_used)
    def _():
        slot = t % 2
        pltpu.make_async_copy(src_hbm.at[pl.ds(0, rows * groups)], buf.at[slot], sem.at[slot]).wait()
        for j in range(groups):
            o_ref[:, j * LANES:(j + 1) * LANES] = buf[slot, pl.ds(j, rows, stride=groups), :].astype(o_ref.dtype)

    @pl.when(t >= n_used)
    def _():
        o_ref[...] = jnp.zeros_like(o_ref)


def _row_gather(n_used, row_idx, src, d, n_tiles, rows, out_dtype):
    groups = d // LANES
    grid_spec = pltpu.PrefetchScalarGridSpec(
        num_scalar_prefetch=2,
        grid=(n_tiles,),
        in_specs=[pl.BlockSpec(memory_space=pl.ANY)],
        out_specs=pl.BlockSpec((rows, d), lambda t, *_: (t, 0)),
        scratch_shapes=[pltpu.VMEM((2, rows * groups, LANES), src.dtype), pltpu.SemaphoreType.DMA((2,))],
    )
    return pl.pallas_call(
        functools.partial(_row_gather_kernel, rows=rows),
        grid_spec=grid_spec,
        out_shape=jax.ShapeDtypeStruct((n_tiles * rows, d), out_dtype),
        compiler_params=_cparams("arbitrary"),
        name="moe_row_gather",
    )(n_used, row_idx, src)


def _start_all(copies, priority=0):
    for c in copies:
        c.start(priority=priority)


def _wait_all(copies):
    for c in copies:
        c.wait()


def _split_rows(n_rows, pieces):
    per = n_rows // pieces
    assert per * pieces == n_rows
    return [pl.ds(p * per, per) for p in range(pieces)]


def _prefetch_step_weights(step, n_steps, copies_for):
    slot = step % 2

    @pl.when(step == 0)
    def _():
        _start_all(copies_for(step, slot), priority=1)

    @pl.when(step + 1 < n_steps)
    def _():
        _start_all(copies_for(step + 1, 1 - slot), priority=1)

    _wait_all(copies_for(step, slot))
    return slot


def _expert_chunk_loop(n_chunks, in_copies, out_copies, compute, prepare):
    @pl.when(n_chunks > 0)
    def _():
        _start_all(in_copies(0, 0))

        @pl.when(n_chunks > 1)
        def _():
            _start_all(in_copies(1, 1))

        prepare()

        def body(r, carry):
            in_slot = r % CHUNK_IN_SLOTS
            out_slot = r % 2

            @pl.when(r + 2 < n_chunks)
            def _():
                _start_all(in_copies(r + 2, (r + 2) % CHUNK_IN_SLOTS))

            _wait_all(in_copies(r, in_slot))

            @pl.when(r >= 2)
            def _():
                _wait_all(out_copies(r - 2, out_slot))

            compute(in_slot, out_slot)
            _start_all(out_copies(r, out_slot))
            return carry

        lax.fori_loop(0, n_chunks, body, 0)

        @pl.when(n_chunks >= 2)
        def _():
            _wait_all(out_copies(n_chunks - 2, n_chunks % 2))

        _wait_all(out_copies(n_chunks - 1, (n_chunks - 1) % 2))


def _zero_fill_tiles(first, last, zbuf, dst, sem):
    zbuf[...] = jnp.zeros_like(zbuf)

    def copy(t):
        return pltpu.make_async_copy(zbuf, dst(t), sem)

    def start(t, carry):
        copy(t).start()
        return carry

    def wait(t, carry):
        copy(t).wait()
        return carry

    lax.fori_loop(first, last, start, 0)
    lax.fori_loop(first, last, wait, 0)


def _moe_up_kernel(tpe_ref, start_ref, xs_hbm, w_hbm, bg_ref, bu_ref, hid_hbm,
                   wbuf, wg_sc, wu_sc, xbuf, obuf, w_sem, in_sem, out_sem, *, rows):
    e = pl.program_id(0)
    n = pl.program_id(1)
    n_chunks = pl.num_programs(1)
    row0 = start_ref[e]
    d, nc = wg_sc.shape

    def w_copies(step, slot):
        e_, n_ = step // n_chunks, step % n_chunks
        copies = []
        for mat in range(2):
            cols = pl.ds(pl.multiple_of((mat * n_chunks + n_) * nc, nc), nc)
            for rws in _split_rows(d, DMA_SPLIT):
                copies.append(pltpu.make_async_copy(w_hbm.at[e_, rws, cols], wbuf.at[slot, mat, rws], w_sem.at[slot]))
        return copies

    w_slot = _prefetch_step_weights(e * n_chunks + n, pl.num_programs(0) * n_chunks, w_copies)

    def row_pieces(r, pieces):
        per = rows // pieces
        return [(pl.ds(pl.multiple_of(row0 + r * rows + p * per, per), per), pl.ds(p * per, per))
                for p in range(pieces)]

    def in_copies(r, slot):
        return [pltpu.make_async_copy(xs_hbm.at[src], xbuf.at[slot, dst], in_sem.at[slot])
                for src, dst in row_pieces(r, CHUNK_DMA_SPLIT)]

    def out_copies(r, slot):
        return [pltpu.make_async_copy(obuf.at[slot, src], hid_hbm.at[n, dst], out_sem.at[slot])
                for dst, src in row_pieces(r, 1)]

    def cast_weights():
        wg_sc[...] = wbuf[w_slot, 0].astype(BF16)
        wu_sc[...] = wbuf[w_slot, 1].astype(BF16)

    def compute(in_slot, out_slot):
        x = xbuf[in_slot]
        for c0 in range(0, nc, MOE_COL_BLOCK):
            cs = slice(c0, c0 + MOE_COL_BLOCK)
            g = jnp.dot(x, wg_sc[:, cs], preferred_element_type=F32) + bg_ref[0, :, cs]
            u = jnp.dot(x, wu_sc[:, cs], preferred_element_type=F32) + bu_ref[0, :, cs]
            g = jnp.minimum(g, SWIGLU_LIMIT)
            u = jnp.clip(u, -SWIGLU_LIMIT, SWIGLU_LIMIT)
            obuf[out_slot, :, cs] = ((u + 1.0) * (g * jax.nn.sigmoid(SWIGLU_ALPHA * g))).astype(obuf.dtype)

    _expert_chunk_loop(tpe_ref[e], in_copies, out_copies, compute, cast_weights)

    @pl.when(e == pl.num_programs(0) - 1)
    def _():
        _zero_fill_tiles(row0 // rows + tpe_ref[e], hid_hbm.shape[1] // rows, obuf.at[0],
                         lambda t: hid_hbm.at[n, pl.ds(pl.multiple_of(t * rows, rows), rows)], out_sem.at[0])


def _moe_up(tiles_per_e, row_start, xs, w_gu, b_gu, rows, nc):
    r_pad, d = xs.shape
    n_e = w_gu.shape[0]
    d_ff = w_gu.shape[2] // 2
    n_chunks = d_ff // nc
    any_spec = pl.BlockSpec(memory_space=pl.ANY)
    grid_spec = pltpu.PrefetchScalarGridSpec(
        num_scalar_prefetch=2,
        grid=(n_e, n_chunks),
        in_specs=[
            any_spec,
            any_spec,
            pl.BlockSpec((1, 1, nc), lambda e, n, *_: (e, 0, n)),
            pl.BlockSpec((1, 1, nc), lambda e, n, *_: (e, 0, n_chunks + n)),
        ],
        out_specs=any_spec,
        scratch_shapes=[pltpu.VMEM((2, 2, d, nc), w_gu.dtype), pltpu.VMEM((d, nc), BF16), pltpu.VMEM((d, nc), BF16),
                        pltpu.VMEM((CHUNK_IN_SLOTS, rows, d), xs.dtype), pltpu.VMEM((2, rows, nc), BF16),
                        pltpu.SemaphoreType.DMA((2,)), pltpu.SemaphoreType.DMA((CHUNK_IN_SLOTS,)),
                        pltpu.SemaphoreType.DMA((2,))],
    )
    return pl.pallas_call(
        functools.partial(_moe_up_kernel, rows=rows),
        grid_spec=grid_spec,
        out_shape=jax.ShapeDtypeStruct((n_chunks, r_pad, nc), BF16),
        compiler_params=_cparams("arbitrary", "arbitrary"),
        name="moe_up",
    )(tiles_per_e, row_start, xs, w_gu, b_gu, b_gu)


def _moe_down_kernel(tpe_ref, start_ref, hid_hbm, w_hbm, b_ref, y_hbm,
                     wbuf, w_sc, xbuf, obuf, w_sem, in_sem, out_sem, *, rows):
    e = pl.program_id(0)
    n = pl.program_id(1)
    n_chunks = pl.num_programs(1)
    row0 = start_ref[e]
    k_chunks, _, kc = xbuf.shape[1:]
    d_ff, nc = w_sc.shape

    def w_copies(step, slot):
        e_, n_ = step // n_chunks, step % n_chunks
        cols = pl.ds(pl.multiple_of(n_ * nc, nc), nc)
        return [pltpu.make_async_copy(w_hbm.at[e_, rws, cols], wbuf.at[slot, rws], w_sem.at[slot])
                for rws in _split_rows(d_ff, DMA_SPLIT)]

    w_slot = _prefetch_step_weights(e * n_chunks + n, pl.num_programs(0) * n_chunks, w_copies)

    def chunk_rows(r):
        return pl.ds(pl.multiple_of(row0 + r * rows, rows), rows)

    def in_copies(r, slot):
        return [pltpu.make_async_copy(hid_hbm.at[c, chunk_rows(r)], xbuf.at[slot, c], in_sem.at[slot])
                for c in range(k_chunks)]

    lanes_per_row = nc // LANES
    per = rows * lanes_per_row // CHUNK_DMA_SPLIT

    def out_copies(r, slot):
        base = (row0 + r * rows) * lanes_per_row
        return [pltpu.make_async_copy(
            obuf.at[slot, pl.ds(p * per, per)],
            y_hbm.at[pl.ds(pl.multiple_of(base + p * per, per), per)], out_sem.at[slot])
            for p in range(CHUNK_DMA_SPLIT)]

    def cast_weights():
        w_sc[...] = wbuf[w_slot].astype(BF16)

    def compute(in_slot, out_slot):
        for c0 in range(0, nc, MOE_COL_BLOCK):
            cs = slice(c0, c0 + MOE_COL_BLOCK)
            acc = b_ref[0, :, cs] + jnp.dot(xbuf[in_slot, 0], w_sc[0:kc, cs], preferred_element_type=F32)
            for c in range(1, k_chunks):
                acc = acc + jnp.dot(xbuf[in_slot, c], w_sc[c * kc:(c + 1) * kc, cs], preferred_element_type=F32)
            for j in range(MOE_COL_BLOCK // LANES):
                lane_group = c0 // LANES + j
                obuf[out_slot, pl.ds(lane_group, rows, stride=lanes_per_row), :] = acc[:, j * LANES:(j + 1) * LANES]

    _expert_chunk_loop(tpe_ref[e], in_copies, out_copies, compute, cast_weights)

    @pl.when(e == pl.num_programs(0) - 1)
    def _():
        tile_rows = rows * lanes_per_row
        _zero_fill_tiles(row0 // rows + tpe_ref[e], y_hbm.shape[0] // tile_rows, obuf.at[0],
                         lambda t: y_hbm.at[pl.ds(pl.multiple_of(t * tile_rows, tile_rows), tile_rows)], out_sem.at[0])


def _moe_down(tiles_per_e, row_start, hid, w_dn, b_dn, rows):
    k_chunks, r_pad, kc = hid.shape
    n_e, d_ff, d = w_dn.shape
    nc = d
    n_chunks = 1
    any_spec = pl.BlockSpec(memory_space=pl.ANY)
    grid_spec = pltpu.PrefetchScalarGridSpec(
        num_scalar_prefetch=2,
        grid=(n_e, n_chunks),
        in_specs=[
            any_spec,
            any_spec,
            pl.BlockSpec((1, 1, nc), lambda e, n, *_: (e, 0, n)),
        ],
        out_specs=any_spec,
        scratch_shapes=[pltpu.VMEM((2, d_ff, nc), w_dn.dtype), pltpu.VMEM((d_ff, nc), BF16),
                        pltpu.VMEM((CHUNK_IN_SLOTS, k_chunks, rows, kc), hid.dtype),
                        pltpu.VMEM((2, rows * nc // LANES, LANES), F32),
                        pltpu.SemaphoreType.DMA((2,)), pltpu.SemaphoreType.DMA((CHUNK_IN_SLOTS,)),
                        pltpu.SemaphoreType.DMA((2,))],
    )
    return pl.pallas_call(
        functools.partial(_moe_down_kernel, rows=rows),
        grid_spec=grid_spec,
        out_shape=jax.ShapeDtypeStruct((r_pad * d // LANES, LANES), F32),
        compiler_params=_cparams("arbitrary", "arbitrary"),
        name="moe_down",
    )(tiles_per_e, row_start, hid, w_dn, b_dn)


def _combine_kernel(dest_ref, h_ref, gate_ref, gf_ref, y_hbm, o_ref, buf, sem, *, rows, final_norm):
    t = pl.program_id(0)
    n_t = pl.num_programs(0)
    n_dma = rows * TOP_K
    d = o_ref.shape[1]
    groups = d // LANES

    def issue(tile, slot):
        def body(j, carry):
            for u in range(DMA_ISSUE_UNROLL):
                i = j * DMA_ISSUE_UNROLL + u
                src = pl.ds(pl.multiple_of(dest_ref[tile * n_dma + i], groups), groups)
                dst = pl.ds(pl.multiple_of(i * groups, groups), groups)
                pltpu.make_async_copy(y_hbm.at[src], buf.at[slot, dst], sem.at[slot]).start(priority=u % 2)
            return carry
        lax.fori_loop(0, n_dma // DMA_ISSUE_UNROLL, body, 0)

    @pl.when(t == 0)
    def _():
        issue(0, 0)

    @pl.when(t + 1 < n_t)
    def _():
        issue(t + 1, (t + 1) % 2)

    slot = t % 2
    pltpu.make_async_copy(y_hbm.at[pl.ds(0, n_dma * groups)], buf.at[slot], sem.at[slot]).wait()

    gates = [gate_ref[:, k:k + 1] for k in range(TOP_K)]
    sq = jnp.zeros((rows, 1), F32)
    for j in range(groups):
        cs = slice(j * LANES, (j + 1) * LANES)
        acc = h_ref[:, cs]
        for k in range(TOP_K):
            acc = acc + gates[k] * buf[slot, pl.ds(k * rows * groups + j, rows, stride=groups), :]
        o_ref[:, cs] = acc
        sq = sq + jnp.sum(acc * acc, axis=-1, keepdims=True)
    if final_norm:
        o_ref[...] = o_ref[...] * lax.rsqrt(sq / d + NORM_EPS) * gf_ref[...]


def _combine(dest_km, h1, gate, gf, y, rows, final_norm):
    m, d = h1.shape
    grid_spec = pltpu.PrefetchScalarGridSpec(
        num_scalar_prefetch=1,
        grid=(m // rows,),
        in_specs=[
            pl.BlockSpec((rows, d), lambda t, *_: (t, 0)),
            pl.BlockSpec((rows, TOP_K), lambda t, *_: (t, 0)),
            pl.BlockSpec((1, d), lambda t, *_: (0, 0)),
            pl.BlockSpec(memory_space=pl.ANY),
        ],
        out_specs=pl.BlockSpec((rows, d), lambda t, *_: (t, 0)),
        scratch_shapes=[pltpu.VMEM((2, rows * TOP_K * d // LANES, LANES), F32), pltpu.SemaphoreType.DMA((2,))],
    )
    return pl.pallas_call(
        functools.partial(_combine_kernel, rows=rows, final_norm=final_norm),
        grid_spec=grid_spec,
        out_shape=jax.ShapeDtypeStruct((m, d), F32),
        compiler_params=_cparams("arbitrary"),
        name="moe_combine",
    )(dest_km, h1, gate, gf, y)


def _routing_tables(top_e, rank, counts, rows, n_tiles):
    m = top_e.shape[0]
    n_experts = counts.shape[0]
    tiles_per_e = (counts + rows - 1) // rows
    tile_end = jnp.cumsum(tiles_per_e)
    pad_start = (tile_end - tiles_per_e) * rows
    experts = jnp.arange(n_experts, dtype=jnp.int32)
    dest = rank + jnp.sum(jnp.where(top_e[..., None] == experts, pad_start, 0), axis=-1)
    dest = dest.astype(jnp.int32).reshape(-1)
    n_used = tile_end[-1:].astype(jnp.int32)
    tok = jnp.arange(m * TOP_K, dtype=jnp.int32) // TOP_K
    row_tok = jnp.zeros((n_tiles * rows,), jnp.int32).at[dest].set(tok)
    return dest, row_tok, tiles_per_e.astype(jnp.int32), pad_start.astype(jnp.int32), n_used


def _largest_divisor(n, cap, mult):
    best = mult
    for c in range(mult, cap + 1, mult):
        if n % c == 0:
            best = c
    return best


def kernel(x_prompt, x_sample, cache_kv_w128, cache_kv_w512, cache_kv_w2048, state_pool, rel_bias, norm1, w_in,
           w_branch_a, w_branch_b, w_out, w_pool, pool_scale, norm2, router_w, router_b, w_gate_up, b_gate_up,
           w_down, b_down, norm_f):
    batch, seq, d = x_prompt.shape
    db, t_new, _ = x_sample.shape
    depth = norm1.shape[0]
    n_experts = router_w.shape[-1]
    caches = (cache_kv_w128, cache_kv_w512, cache_kv_w2048)
    m_p, m_s = batch * seq, db * t_new
    m = m_p + m_s
    tm = _largest_divisor(m, 640, 8)

    h = jnp.concatenate([x_prompt.reshape(m_p, d), x_sample.reshape(m_s, d)], axis=0)
    bias_p = _prompt_bias_tiles(rel_bias)
    bias_sc, bias_sn = _sample_bias_tiles(rel_bias, t_new)

    cuts = [ATTN_WIDTH, 2 * ATTN_WIDTH, 3 * ATTN_WIDTH, 3 * ATTN_WIDTH + POOL_WIDTH, 3 * ATTN_WIDTH + POOL_WIDTH + d]
    qkv0 = 2 * d
    z0 = qkv0 + 3 * ATTN_WIDTH

    kv_p = [[] for _ in range(N_GROUPS)]
    kv_s = [[] for _ in range(N_GROUPS)]
    pool_p, pool_s = [], []
    for l in range(depth):
        wq, wk, wv, wz, wga, wgb = jnp.split(w_in[l], cuts, axis=-1)
        w_perm = jnp.concatenate([wga, wgb, wq, wk, wv, wz], axis=-1).astype(BF16)
        proj = _inproj(h, norm1[l][None, :], w_perm, tm, _largest_divisor(w_perm.shape[1], 1536, 128))

        k0, v0 = qkv0 + ATTN_WIDTH, qkv0 + 2 * ATTN_WIDTH
        kv_w = HEADS_PER_GROUP * HEAD_DIM
        proj_s = proj[m_p:]
        z_s = proj_s[:, z0:z0 + POOL_WIDTH].reshape(db, t_new, POOL_WIDTH)

        attn_p = _prompt_attn(proj, bias_p, batch, seq, qkv0 // HEAD_DIM)
        qkv_s = proj_s[:, qkv0:qkv0 + 3 * ATTN_WIDTH].reshape(db, t_new, 3 * ATTN_WIDTH)
        c_rows = [caches[g][l].reshape(db, -1, CACHE_ROW_TILE, HEAD_DIM) for g in range(N_GROUPS)]
        c0 = c_rows[0].reshape(db, -1, HEAD_DIM)
        c1 = c_rows[1].reshape(db, -1, HEAD_DIM)
        dil2 = ATTN_GROUPS[2][1]
        c2v = c_rows[2].reshape(db, -1, dil2 * CACHE_ROW_TILE, HEAD_DIM)
        attn_s = _sample_attn(qkv_s, c0, c1, c2v, bias_sc, bias_sn)
        attn = jnp.concatenate([attn_p, attn_s.reshape(m_s, ATTN_OUT_WIDTH)], axis=0)

        news = []
        for g, (window, _) in enumerate(ATTN_GROUPS):
            keep = min(window, seq)
            kc, vc = k0 + g * kv_w, v0 + g * kv_w

            def kv_rows(a, r0, r1, kc=kc, vc=vc):
                return jnp.stack([a[r0:r1, kc:kc + kv_w].reshape(-1, HEADS_PER_GROUP, HEAD_DIM),
                                  a[r0:r1, vc:vc + kv_w].reshape(-1, HEADS_PER_GROUP, HEAD_DIM)], axis=1)

            kv_p[g].append(jnp.stack([kv_rows(proj, (b + 1) * seq - keep, (b + 1) * seq) for b in range(batch)]))
            news.append(kv_rows(proj_s, 0, m_s).reshape(db, t_new, CACHE_ROW_TILE, HEAD_DIM))
        rolled = _cache_roll(c_rows, news)
        for g in range(N_GROUPS):
            kv_s[g].append(rolled[g].reshape(caches[g][l].shape))

        w_pool_b = w_pool[l].astype(BF16)
        scale = pool_scale[l][None, :]
        hist_p = jnp.zeros((batch, POOL_HIST, POOL_WIDTH), F32)
        hist_s = jnp.concatenate([jnp.zeros((db, POOL_HIST - POOL_BUF, POOL_WIDTH), F32), state_pool[l]], axis=1)
        pool = jnp.concatenate([
            _pool_mix(hist_p, proj, pl.BlockSpec((seq, POOL_WIDTH), lambda i: (i, z0 // POOL_WIDTH)),
                      batch, seq, w_pool_b, scale, 0).reshape(m_p, POOL_WIDTH),
            _pool_mix(hist_s, z_s, pl.BlockSpec((1, t_new, POOL_WIDTH), lambda i: (i, 0, 0)),
                      db, t_new, w_pool_b, scale, PAST_LEN).reshape(m_s, POOL_WIDTH),
        ], axis=0)
        pool_p.append(jnp.stack([proj[(b + 1) * seq - POOL_BUF:(b + 1) * seq, z0:z0 + POOL_WIDTH]
                                 for b in range(batch)]))
        pool_s.append(jnp.concatenate([state_pool[l], z_s], axis=1)[:, t_new:])

        rw = router_w[l]
        rw_hi = rw.astype(BF16)
        rw_lo = (rw - rw_hi.astype(F32)).astype(BF16)
        h1, xn_rows, top_e, gate, rank, counts = _merge(
            h, attn, pool, proj, w_branch_a[l].astype(BF16), w_branch_b[l].astype(BF16), w_out[l].astype(BF16),
            norm2[l][None, :], rw_hi, rw_lo, router_b[l][None, :], _largest_divisor(m, 320, 8))

        rows = MOE_ROW_TILE
        groups = d // LANES
        n_tiles = (m * TOP_K + n_experts * (rows - 1)) // rows
        dest, row_tok, tiles_per_e, row_start, n_used = _routing_tables(top_e, rank, counts[0], rows, n_tiles)
        xs = _row_gather(n_used, row_tok * groups, xn_rows, d, n_tiles, rows, BF16)
        hid = _moe_up(tiles_per_e, row_start, xs, w_gate_up[l], b_gate_up[l][:, None, :], rows, 1024)
        y = _moe_down(tiles_per_e, row_start, hid, w_down[l], b_down[l][:, None, :], rows)
        crow = _largest_divisor(m, 128, 8)
        dest_km = (dest * groups).reshape(m // crow, crow, TOP_K).transpose(0, 2, 1).reshape(-1)
        h = _combine(dest_km, h1, gate, norm_f[None, :], y, crow, final_norm=(l == depth - 1))

    y_prompt = h[:m_p].reshape(batch, seq, d)
    y_sample = h[m_p:].reshape(db, t_new, d)
    return (y_prompt, y_sample,
            jnp.stack(kv_p[0]), jnp.stack(kv_p[1]), jnp.stack(kv_p[2]), jnp.stack(pool_p),
            jnp.stack(kv_s[0]), jnp.stack(kv_s[1]), jnp.stack(kv_s[2]), jnp.stack(pool_s))
```

```python
import functools
import math

import jax
import jax.numpy as jnp
from jax import lax
from jax.experimental import pallas as pl
from jax.experimental.pallas import tpu as pltpu

F32 = jnp.float32
BF16 = jnp.bfloat16

HEAD_DIM = 128
HEADS_PER_GROUP = 4
ATTN_GROUPS = ((128, 1), (512, 4), (2048, 16))
N_GROUPS = len(ATTN_GROUPS)
N_ATTN_HEADS = N_GROUPS * HEADS_PER_GROUP
ATTN_WIDTH = N_ATTN_HEADS * HEAD_DIM
ATTN_OUT_WIDTH = HEADS_PER_GROUP * HEAD_DIM
ATTN_SCALE = HEAD_DIM ** -0.5
N_BACK = 128
N_REL_BUCKETS = 32
REL_MAX_DIST = 2048
POOL_WINDOWS = (2, 4, 8, 16)
POOL_GROUPS = len(POOL_WINDOWS)
PGW = 128
POOL_WIDTH = POOL_GROUPS * PGW
POOL_BUF = max(POOL_WINDOWS) - 1
POOL_HIST = 16
TOP_K = 4
SWIGLU_LIMIT = 7.0
SWIGLU_ALPHA = 1.702
NORM_EPS = 1e-5
NEG_INF = -1e30
PAST_LEN = 8192
CACHE_ROW_TILE = 2 * HEADS_PER_GROUP

VMEM_LIMIT_BYTES = 56 * 1024 * 1024
MOE_ROW_TILE = 256
DMA_ISSUE_UNROLL = 8
DMA_SPLIT = 8
CHUNK_DMA_SPLIT = 4
ATTN_INTERLEAVE = 4
LANES = 128
CHUNK_IN_SLOTS = 3
MOE_COL_BLOCK = 512


def _cparams(*sem):
    return pltpu.CompilerParams(dimension_semantics=sem, vmem_limit_bytes=VMEM_LIMIT_BYTES)


def _rmsnorm_f32(x, g):
    return x * lax.rsqrt(jnp.mean(x * x, axis=-1, keepdims=True) + NORM_EPS) * g


def _inproj_kernel(*refs):
    x_refs, (g_ref, w_ref, o_ref) = refs[:CHUNK_DMA_SPLIT], refs[CHUNK_DMA_SPLIT:]
    xs = [r[...] for r in x_refs]
    kc = xs[0].shape[1]
    sq = xs[0] * xs[0]
    for x in xs[1:]:
        sq = sq + x * x
    inv = lax.rsqrt(jnp.sum(sq, axis=-1, keepdims=True) / (kc * len(xs)) + NORM_EPS)
    acc = None
    for k, x in enumerate(xs):
        xn = (x * inv * g_ref[:, k * kc:(k + 1) * kc]).astype(BF16)
        part = jnp.dot(xn, w_ref[k * kc:(k + 1) * kc, :], preferred_element_type=F32)
        acc = part if acc is None else acc + part
    o_ref[...] = acc


def _inproj(x, g, w, tm, tn):
    m, d = x.shape
    n = w.shape[1]
    kc = d // CHUNK_DMA_SPLIT
    return pl.pallas_call(
        _inproj_kernel,
        grid=(n // tn, m // tm),
        in_specs=[pl.BlockSpec((tm, kc), lambda j, i, k=k: (i, k)) for k in range(CHUNK_DMA_SPLIT)] + [
            pl.BlockSpec((1, d), lambda j, i: (0, 0)),
            pl.BlockSpec((d, tn), lambda j, i: (0, j)),
        ],
        out_specs=pl.BlockSpec((tm, tn), lambda j, i: (i, j)),
        out_shape=jax.ShapeDtypeStruct((m, n), F32),
        compiler_params=_cparams("arbitrary", "arbitrary"),
        name="inproj",
    )(*([x] * CHUNK_DMA_SPLIT), g, w)


def _t5_bucket(dist):
    max_exact = N_REL_BUCKETS // 2
    df = jnp.maximum(dist, 1).astype(F32)
    large = max_exact + (jnp.log(df / max_exact) / math.log(REL_MAX_DIST / max_exact)
                         * (N_REL_BUCKETS - max_exact)).astype(jnp.int32)
    large = jnp.minimum(large, N_REL_BUCKETS - 1)
    return jnp.where(dist < max_exact, dist, large)


def _step_bias(rel_bias, g):
    _, dil = ATTN_GROUPS[g]
    buckets = _t5_bucket(jnp.arange(N_BACK + 1, dtype=jnp.int32) * dil)
    tab = rel_bias[buckets].astype(F32)
    return tab[:, g * HEADS_PER_GROUP:(g + 1) * HEADS_PER_GROUP].T


def _prompt_bias_tiles(rel_bias):
    nb = N_BACK
    period = 3 * nb - 1
    tiles = []
    for g in range(N_GROUPS):
        bj = _step_bias(rel_bias, g)
        h = bj.shape[0]
        pad = jnp.full((h, nb - 1), NEG_INF, F32)
        v = jnp.concatenate([pad, bj[:, ::-1], pad], axis=1)
        skew = jnp.tile(v, (1, nb + 1))[:, :nb * (period + 1)].reshape(h, nb, period + 1)
        tiles.append(skew[:, ::-1, :2 * nb])
    return jnp.stack(tiles, axis=1)


def _sample_bias_tiles(rel_bias, t_new):
    tq = jnp.arange(t_new)[:, None]
    outs = []
    new = []
    for g, (window, dil) in enumerate(ATTN_GROUPS):
        bj = _step_bias(rel_bias, g)
        if g < 2:
            p = jnp.arange(window)[None, :]
            dist = window + tq - p
            ok = (dist % dil == 0) & (dist // dil <= N_BACK)
            outs.append(jnp.where(ok[None], bj[:, jnp.clip(dist // dil, 0, N_BACK)], NEG_INF))
        else:
            col = jnp.arange(t_new * N_BACK)[None, :]
            blk, r = col // N_BACK, col % N_BACK
            ok = blk == tq
            outs.append(jnp.where(ok[None], bj[:, jnp.broadcast_to(N_BACK - r, (t_new, t_new * N_BACK))], NEG_INF))
        tk = jnp.arange(t_new)[None, :]
        d = tq - tk
        okn = (d >= 0) & (d % dil == 0) & (d // dil <= N_BACK)
        new.append(jnp.where(okn[None], bj[:, jnp.clip(d // dil, 0, N_BACK)], NEG_INF))
    return outs, jnp.stack(new, axis=0)


def _softmax_block(s, v):
    m = jnp.max(s, axis=-1, keepdims=True)
    p = jnp.exp(s - m)
    den = jnp.sum(p, axis=-1, keepdims=True)
    o = jnp.dot(p.astype(BF16), v, preferred_element_type=F32) / den
    return o, m + jnp.log(den)


def _prompt_attn_kernel(q0, k0, v0, q1, k1, v1, q2, k2, v2, bias_ref, o_ref, o_sc, l_sc, *, seq):
    qkv = ((q0, k0, v0), (q1, k1, v1), (q2, k2, v2))
    nb = N_BACK

    def rows(ref, start, size, dil):
        if dil == 1:
            return ref[pl.ds(start, size), :]
        return ref[pl.ds(start, size, stride=dil), :]

    def put(g, start, dil, o, lse):
        idx = pl.ds(start, nb) if dil == 1 else pl.ds(start, nb, stride=dil)
        o_sc[g, idx, :] = o
        l_sc[g, idx, :] = jnp.broadcast_to(lse, (nb, HEAD_DIM))

    dn = (((1,), (1,)), ((), ()))

    def attend(g, dil, q_starts, kv_starts, n_kv):
        q_ref, k_ref, v_ref = qkv[g]
        bias = bias_ref[0, g] if n_kv == 2 * nb else bias_ref[0, g, :, nb:]
        qs = [rows(q_ref, st, nb, dil).astype(BF16) for st in q_starts]
        ks = [rows(k_ref, st, n_kv, dil).astype(BF16) for st in kv_starts]
        ss = [lax.dot_general(q, k, dn, preferred_element_type=F32) * ATTN_SCALE + bias for q, k in zip(qs, ks)]
        ms = [jnp.max(s, axis=-1, keepdims=True) for s in ss]
        ps = [jnp.exp(s - m) for s, m in zip(ss, ms)]
        dens = [jnp.sum(p, axis=-1, keepdims=True) for p in ps]
        vs = [rows(v_ref, st, n_kv, dil).astype(BF16) for st in kv_starts]
        for st, p, v, m, den in zip(q_starts, ps, vs, ms, dens):
            o = jnp.dot(p.astype(BF16), v, preferred_element_type=F32) / den
            put(g, st, dil, o, m + jnp.log(den))

    for g, (_, dil) in enumerate(ATTN_GROUPS):
        n_blk = seq // (dil * nb)
        span = dil * nb
        for r0 in range(0, dil, ATTN_INTERLEAVE):
            starts = list(range(r0, min(r0 + ATTN_INTERLEAVE, dil)))
            attend(g, dil, starts, starts, nb)
        if n_blk > 1:
            per_iter = max(c for c in range(1, max(1, ATTN_INTERLEAVE // dil) + 1) if (n_blk - 1) % c == 0)

            def body(it, carry, g=g, dil=dil, span=span, per_iter=per_iter):
                q_starts, kv_starts = [], []
                for j in range(per_iter):
                    b = 1 + it * per_iter + j
                    for r in range(dil):
                        q_starts.append(b * span + r)
                        kv_starts.append((b - 1) * span + r)
                attend(g, dil, q_starts, kv_starts, 2 * nb)
                return carry

            lax.fori_loop(0, (n_blk - 1) // per_iter, body, 0)

    chunk = 256

    def comb(c, carry):
        sl = pl.ds(pl.multiple_of(c * chunk, chunk), chunk)
        l0, l1, l2 = l_sc[0, sl, :], l_sc[1, sl, :], l_sc[2, sl, :]
        mx = jnp.maximum(jnp.maximum(l0, l1), l2)
        e0, e1, e2 = jnp.exp(l0 - mx), jnp.exp(l1 - mx), jnp.exp(l2 - mx)
        num = e0 * o_sc[0, sl, :] + e1 * o_sc[1, sl, :] + e2 * o_sc[2, sl, :]
        o_ref[sl, :] = (num / (e0 + e1 + e2)).astype(o_ref.dtype)
        return carry

    lax.fori_loop(0, seq // chunk, comb, 0)


def _prompt_attn(p, bias, batch, seq, col0):
    def spec(kind, g):
        base = col0 + kind * N_ATTN_HEADS + g * HEADS_PER_GROUP
        return pl.BlockSpec((seq, HEAD_DIM), lambda b, i, base=base: (b, base + i))

    in_specs = []
    for g in range(N_GROUPS):
        in_specs += [spec(0, g), spec(1, g), spec(2, g)]
    in_specs.append(pl.BlockSpec((1, N_GROUPS, N_BACK, 2 * N_BACK), lambda b, i: (i, 0, 0, 0)))
    return pl.pallas_call(
        functools.partial(_prompt_attn_kernel, seq=seq),
        grid=(batch, HEADS_PER_GROUP),
        in_specs=in_specs,
        out_specs=pl.BlockSpec((seq, HEAD_DIM), lambda b, i: (b, i)),
        out_shape=jax.ShapeDtypeStruct((batch * seq, ATTN_OUT_WIDTH), BF16),
        scratch_shapes=[pltpu.VMEM((N_GROUPS, seq, HEAD_DIM), F32), pltpu.VMEM((N_GROUPS, seq, HEAD_DIM), F32)],
        compiler_params=_cparams("arbitrary", "arbitrary"),
        name="prompt_attn",
    )(*([p] * 9), bias)


def _sample_attn_kernel(qkv_ref, c0_ref, c1_ref, c2_ref, n0_ref, n1_ref, n2_ref, b0_ref, b1_ref, b2_ref, bn_ref,
                        o_ref, r0_ref, r1_ref, r2_ref, *, t_new):
    caches = (c0_ref, c1_ref, c2_ref)
    biases = (b0_ref, b1_ref, b2_ref)
    dn = (((1,), (1,)), ((), ()))

    shift = t_new * CACHE_ROW_TILE
    for c_ref, n_ref, r_ref in ((c0_ref, n0_ref, r0_ref), (c1_ref, n1_ref, r1_ref)):
        keep = c_ref.shape[1] - shift
        r_ref[0, :keep] = c_ref[0, shift:]
        r_ref[0, keep:] = n_ref[0]
    n_r, width = c2_ref.shape[1], c2_ref.shape[2]
    r2_ref[0, :, :width - shift] = c2_ref[0, :, shift:]
    r2_ref[0, :n_r - 1, width - shift:] = c2_ref[0, 1:, :shift]
    r2_ref[0, n_r - 1:, width - shift:] = n2_ref[0]

    for i in range(HEADS_PER_GROUP):
        outs, lses = [], []
        for g in range(N_GROUPS):
            h = g * HEADS_PER_GROUP + i
            q = qkv_ref[0, :, h * HEAD_DIM:(h + 1) * HEAD_DIM].astype(BF16)
            kn = qkv_ref[0, :, ATTN_WIDTH + h * HEAD_DIM:ATTN_WIDTH + (h + 1) * HEAD_DIM].astype(BF16)
            vn = qkv_ref[0, :, 2 * ATTN_WIDTH + h * HEAD_DIM:2 * ATTN_WIDTH + (h + 1) * HEAD_DIM].astype(BF16)
            c_ref = caches[g]
            if g < 2:
                n_rows = c_ref.shape[1] // CACHE_ROW_TILE
                kc = c_ref[0, pl.ds(i, n_rows, stride=CACHE_ROW_TILE), :].astype(BF16)
                vc = c_ref[0, pl.ds(HEADS_PER_GROUP + i, n_rows, stride=CACHE_ROW_TILE), :].astype(BF16)
            else:
                kc = jnp.concatenate(
                    [c_ref[0, :, t * CACHE_ROW_TILE + i, :] for t in range(t_new)], axis=0).astype(BF16)
                vc = jnp.concatenate(
                    [c_ref[0, :, t * CACHE_ROW_TILE + HEADS_PER_GROUP + i, :] for t in range(t_new)],
                    axis=0).astype(BF16)
            sc = lax.dot_general(q, kc, dn, preferred_element_type=F32) * ATTN_SCALE + biases[g][i]
            sn = lax.dot_general(q, kn, dn, preferred_element_type=F32) * ATTN_SCALE + bn_ref[g, i]
            m = jnp.maximum(jnp.max(sc, axis=-1, keepdims=True), jnp.max(sn, axis=-1, keepdims=True))
            pc = jnp.exp(sc - m)
            pn = jnp.exp(sn - m)
            den = jnp.sum(pc, axis=-1, keepdims=True) + jnp.sum(pn, axis=-1, keepdims=True)
            o = (jnp.dot(pc.astype(BF16), vc, preferred_element_type=F32)
                 + jnp.dot(pn.astype(BF16), vn, preferred_element_type=F32)) / den
            outs.append(o)
            lses.append(m + jnp.log(den))
        mx = jnp.maximum(jnp.maximum(lses[0], lses[1]), lses[2])
        es = [jnp.exp(l - mx) for l in lses]
        num = es[0] * outs[0] + es[1] * outs[1] + es[2] * outs[2]
        o_ref[0, :, i * HEAD_DIM:(i + 1) * HEAD_DIM] = (num / (es[0] + es[1] + es[2])).astype(o_ref.dtype)


def _sample_attn(qkv_s, c0, c1, c2v, news, bias_c, bias_n):
    db, t_new, _ = qkv_s.shape

    def full(a):
        return pl.BlockSpec(a.shape, lambda b, nd=a.ndim: (0,) * nd)

    def per_seq(a):
        return pl.BlockSpec((1,) + a.shape[1:], lambda b, nd=a.ndim: (b,) + (0,) * (nd - 1))

    return pl.pallas_call(
        functools.partial(_sample_attn_kernel, t_new=t_new),
        grid=(db,),
        in_specs=[per_seq(qkv_s), per_seq(c0), per_seq(c1), per_seq(c2v), per_seq(news[0]), per_seq(news[1]),
                  per_seq(news[2]), full(bias_c[0]), full(bias_c[1]), full(bias_c[2]), full(bias_n)],
        out_specs=[pl.BlockSpec((1, t_new, ATTN_OUT_WIDTH), lambda b: (b, 0, 0)),
                   per_seq(c0), per_seq(c1), per_seq(c2v)],
        out_shape=[jax.ShapeDtypeStruct((db, t_new, ATTN_OUT_WIDTH), BF16)]
                  + [jax.ShapeDtypeStruct(c.shape, c.dtype) for c in (c0, c1, c2v)],
        compiler_params=_cparams("arbitrary"),
        name="sample_attn",
    )(qkv_s, c0, c1, c2v, *news, bias_c[0], bias_c[1], bias_c[2], bias_n)


def _pool_kernel(hist_ref, z_ref, w_ref, s_ref, o_ref, zbuf, *, t_len, pos0, chunk):
    zbuf[0:POOL_HIST, :] = hist_ref[0]
    zbuf[POOL_HIST:, :] = z_ref[...].reshape(t_len, POOL_WIDTH)
    for c in range(t_len // chunk):
        base = POOL_HIST + c * chunk
        for g, w in enumerate(POOL_WINDOWS):
            sl = slice(g * PGW, (g + 1) * PGW)
            cur = zbuf[base:base + chunk, sl]
            tot = cur
            for i in range(1, w):
                tot = tot + zbuf[base - i:base - i + chunk, sl]
            pos = pos0 + c * chunk + lax.broadcasted_iota(jnp.int32, (chunk, PGW), 0)
            cnt = jnp.minimum(w, pos + 1).astype(F32)
            mixed = tot / cnt - cur
            y = jnp.dot(mixed.astype(BF16), w_ref[g], preferred_element_type=F32) * s_ref[:, sl]
            o_ref[0, c * chunk:(c + 1) * chunk, sl] = y.astype(o_ref.dtype)


def _pool_mix(hist, z, z_spec, n_seq, t_len, w_pool, pool_scale, pos0):
    chunk = min(t_len, 256)
    return pl.pallas_call(
        functools.partial(_pool_kernel, t_len=t_len, pos0=pos0, chunk=chunk),
        grid=(n_seq,),
        in_specs=[
            pl.BlockSpec((1, POOL_HIST, POOL_WIDTH), lambda i: (i, 0, 0)),
            z_spec,
            pl.BlockSpec(w_pool.shape, lambda i: (0, 0, 0)),
            pl.BlockSpec((1, POOL_WIDTH), lambda i: (0, 0)),
        ],
        out_specs=pl.BlockSpec((1, t_len, POOL_WIDTH), lambda i: (i, 0, 0)),
        out_shape=jax.ShapeDtypeStruct((n_seq, t_len, POOL_WIDTH), BF16),
        scratch_shapes=[pltpu.VMEM((POOL_HIST + t_len, POOL_WIDTH), F32)],
        compiler_params=_cparams("arbitrary"),
        name="pool_mix",
    )(hist, z, w_pool, pool_scale)


def _merge_kernel(h_ref, a_ref, p_ref, ga_ref, gb_ref, wa_ref, wb_ref, wo_ref, g2_ref, rwh_ref, rwl_ref, rb_ref,
                  h1_ref, xn_ref, e_ref, gate_ref, rank_ref, cnt_ref, carry):
    ya = jnp.dot(a_ref[...], wa_ref[...], preferred_element_type=F32)
    yb = jnp.dot(p_ref[...], wb_ref[...], preferred_element_type=F32)
    u = jax.nn.sigmoid(ga_ref[...]) * ya + jax.nn.sigmoid(gb_ref[...]) * yb
    h1 = h_ref[...] + jnp.dot(u.astype(BF16), wo_ref[...], preferred_element_type=F32)
    h1_ref[...] = h1
    xn = _rmsnorm_f32(h1, g2_ref[...])
    tm, d = xn.shape
    groups = d // LANES
    for j in range(groups):
        xn_ref[pl.ds(j, tm, stride=groups), :] = xn[:, j * LANES:(j + 1) * LANES]
    _route_tile(xn, rwh_ref, rwl_ref, rb_ref, e_ref, gate_ref, rank_ref, cnt_ref, carry)


def _merge(h, attn, pool, proj, wa, wb, wo, g2, rwh, rwl, rb, tm):
    m, d = h.shape
    n_e = rwh.shape[1]

    def const(a):
        return pl.BlockSpec(a.shape, lambda i, nd=a.ndim: (0,) * nd, pipeline_mode=pl.Buffered(1))

    tok_spec = pl.BlockSpec((tm, TOP_K), lambda i: (i, 0))
    return pl.pallas_call(
        _merge_kernel,
        grid=(m // tm,),
        in_specs=[
            pl.BlockSpec((tm, d), lambda i: (i, 0)),
            pl.BlockSpec((tm, attn.shape[1]), lambda i: (i, 0)),
            pl.BlockSpec((tm, pool.shape[1]), lambda i: (i, 0)),
            pl.BlockSpec((tm, d), lambda i: (i, 0)),
            pl.BlockSpec((tm, d), lambda i: (i, 1)),
            const(wa), const(wb), const(wo), const(g2), const(rwh), const(rwl), const(rb),
        ],
        out_specs=[pl.BlockSpec((tm, d), lambda i: (i, 0)),
                   pl.BlockSpec((tm * d // LANES, LANES), lambda i: (i, 0)),
                   tok_spec, tok_spec, tok_spec, pl.BlockSpec((1, n_e), lambda i: (0, 0))],
        out_shape=[jax.ShapeDtypeStruct((m, d), F32), jax.ShapeDtypeStruct((m * d // LANES, LANES), F32),
                   jax.ShapeDtypeStruct((m, TOP_K), jnp.int32), jax.ShapeDtypeStruct((m, TOP_K), F32),
                   jax.ShapeDtypeStruct((m, TOP_K), jnp.int32), jax.ShapeDtypeStruct((1, n_e), jnp.int32)],
        scratch_shapes=[pltpu.VMEM((1, n_e), F32)],
        compiler_params=_cparams("arbitrary"),
        name="merge_route",
    )(h, attn, pool, proj, proj, wa, wb, wo, g2, rwh, rwl, rb)


def _route_tile(x, wh_ref, wl_ref, b_ref, e_ref, gate_ref, rank_ref, cnt_ref, carry):
    @pl.when(pl.program_id(0) == 0)
    def _():
        carry[...] = jnp.zeros_like(carry)

    xh = x.astype(BF16)
    xl = (x - xh.astype(F32)).astype(BF16)
    logits = (jnp.dot(xh, wh_ref[...], preferred_element_type=F32)
              + jnp.dot(xh, wl_ref[...], preferred_element_type=F32)
              + jnp.dot(xl, wh_ref[...], preferred_element_type=F32)) + b_ref[...]
    n_e = logits.shape[-1]
    lane = lax.broadcasted_iota(jnp.int32, logits.shape, 1)
    vals = logits
    tops, idxs = [], []
    for _ in range(TOP_K):
        m = jnp.max(vals, axis=-1, keepdims=True)
        idx = jnp.min(jnp.where(vals == m, lane, n_e), axis=-1, keepdims=True)
        tops.append(m)
        idxs.append(idx)
        vals = jnp.where(lane == idx, -jnp.inf, vals)
    ex = [jnp.exp(t - tops[0]) for t in tops]
    den = ex[0] + ex[1] + ex[2] + ex[3]
    for k in range(TOP_K):
        e_ref[:, k:k + 1] = idxs[k]
        gate_ref[:, k:k + 1] = ex[k] / den

    onehots = [(lane == idx).astype(F32) for idx in idxs]
    cnt = onehots[0] + onehots[1] + onehots[2] + onehots[3]
    tm = cnt.shape[0]
    lower = (lax.broadcasted_iota(jnp.int32, (tm, tm), 0) > lax.broadcasted_iota(jnp.int32, (tm, tm), 1))
    prefix = jnp.dot(lower.astype(BF16), cnt.astype(BF16), preferred_element_type=F32)
    base = carry[...] + prefix
    for k in range(TOP_K):
        rank_ref[:, k:k + 1] = jnp.sum(onehots[k] * base, axis=-1, keepdims=True).astype(jnp.int32)
    carry[...] = carry[...] + jnp.sum(cnt, axis=0, keepdims=True)
    cnt_ref[...] = carry[...].astype(jnp.int32)


def _row_gather_kernel(nused_ref, tok_ref, src_hbm, o_ref, buf, sem, *, rows):
    t = pl.program_id(0)
    n_used = nused_ref[0]
    groups = o_ref.shape[1] // LANES

    def issue(tile, slot):
        def body(j, carry):
            for u in range(DMA_ISSUE_UNROLL):
                i = j * DMA_ISSUE_UNROLL + u
                src = pl.ds(pl.multiple_of(tok_ref[tile * rows + i], groups), groups)
                dst = pl.ds(pl.multiple_of(i * groups, groups), groups)
                pltpu.make_async_copy(src_hbm.at[src], buf.at[slot, dst], sem.at[slot]).start(priority=u % 2)
            return carry
        lax.fori_loop(0, rows // DMA_ISSUE_UNROLL, body, 0)

    @pl.when(jnp.logical_and(t == 0, n_used > 0))
    def _():
        issue(0, 0)

    @pl.when(t + 1 < n_used)
    def _():
        issue(t + 1, (t + 1) % 2)

    @pl.when(t < n_used)
    def _():
        slot = t % 2
        pltpu.make_async_copy(src_hbm.at[pl.ds(0, rows * groups)], buf.at[slot], sem.at[slot]).wait()
        for j in range(groups):
            o_ref[:, j * LANES:(j + 1) * LANES] = buf[slot, pl.ds(j, rows, stride=groups), :].astype(o_ref.dtype)

    @pl.when(t >= n_used)
    def _():
        o_ref[...] = jnp.zeros_like(o_ref)


def _row_gather(n_used, row_idx, src, d, n_tiles, rows, out_dtype):
    groups = d // LANES
    grid_spec = pltpu.PrefetchScalarGridSpec(
        num_scalar_prefetch=2,
        grid=(n_tiles,),
        in_specs=[pl.BlockSpec(memory_space=pl.ANY)],
        out_specs=pl.BlockSpec((rows, d), lambda t, *_: (t, 0)),
        scratch_shapes=[pltpu.VMEM((2, rows * groups, LANES), src.dtype), pltpu.SemaphoreType.DMA((2,))],
    )
    return pl.pallas_call(
        functools.partial(_row_gather_kernel, rows=rows),
        grid_spec=grid_spec,
        out_shape=jax.ShapeDtypeStruct((n_tiles * rows, d), out_dtype),
        compiler_params=_cparams("arbitrary"),
        name="moe_row_gather",
    )(n_used, row_idx, src)


def _start_all(copies, priority=0):
    for c in copies:
        c.start(priority=priority)


def _wait_all(copies):
    for c in copies:
        c.wait()


def _split_rows(n_rows, pieces):
    per = n_rows // pieces
    assert per * pieces == n_rows
    return [pl.ds(p * per, per) for p in range(pieces)]


def _prefetch_step_weights(step, n_steps, copies_for):
    slot = step % 2

    @pl.when(step == 0)
    def _():
        _start_all(copies_for(step, slot), priority=1)

    @pl.when(step + 1 < n_steps)
    def _():
        _start_all(copies_for(step + 1, 1 - slot), priority=1)

    _wait_all(copies_for(step, slot))
    return slot


def _expert_chunk_loop(step, n_steps, n_chunks, next_n_chunks, in_copies, next_in_copies, out_copies, compute,
                       prepare):
    def start_first_two(count, copies):
        @pl.when(count > 0)
        def _():
            _start_all(copies(0, 0))

        @pl.when(count > 1)
        def _():
            _start_all(copies(1, 1))

    @pl.when(step == 0)
    def _():
        start_first_two(n_chunks, in_copies)

    @pl.when(n_chunks > 0)
    def _():
        prepare()

        def body(r, carry):
            in_slot = r % CHUNK_IN_SLOTS
            out_slot = r % 2

            @pl.when(r + 2 < n_chunks)
            def _():
                _start_all(in_copies(r + 2, (r + 2) % CHUNK_IN_SLOTS))

            _wait_all(in_copies(r, in_slot))

            @pl.when(r >= 2)
            def _():
                _wait_all(out_copies(r - 2, out_slot))

            compute(in_slot, out_slot)
            _start_all(out_copies(r, out_slot))
            return carry

        lax.fori_loop(0, n_chunks, body, 0)

    @pl.when(step + 1 < n_steps)
    def _():
        start_first_two(next_n_chunks, next_in_copies)

    @pl.when(n_chunks >= 2)
    def _():
        _wait_all(out_copies(n_chunks - 2, n_chunks % 2))

    @pl.when(n_chunks >= 1)
    def _():
        _wait_all(out_copies(n_chunks - 1, (n_chunks - 1) % 2))


def _zero_fill_tiles(first, last, zbuf, dst, sem):
    zbuf[...] = jnp.zeros_like(zbuf)

    def copy(t):
        return pltpu.make_async_copy(zbuf, dst(t), sem)

    def start(t, carry):
        copy(t).start()
        return carry

    def wait(t, carry):
        copy(t).wait()
        return carry

    lax.fori_loop(first, last, start, 0)
    lax.fori_loop(first, last, wait, 0)


def _moe_up_kernel(tpe_ref, start_ref, xs_hbm, w_hbm, bg_ref, bu_ref, hid_hbm,
                   wbuf, wg_sc, wu_sc, xbuf, obuf, w_sem, in_sem, out_sem, *, rows):
    e = pl.program_id(0)
    n = pl.program_id(1)
    n_chunks = pl.num_programs(1)
    row0 = start_ref[e]
    d, nc = wg_sc.shape

    def w_copies(step, slot):
        e_, n_ = step // n_chunks, step % n_chunks
        copies = []
        for mat in range(2):
            cols = pl.ds(pl.multiple_of((mat * n_chunks + n_) * nc, nc), nc)
            for rws in _split_rows(d, DMA_SPLIT):
                copies.append(pltpu.make_async_copy(w_hbm.at[e_, rws, cols], wbuf.at[slot, mat, rws], w_sem.at[slot]))
        return copies

    step, n_steps = e * n_chunks + n, pl.num_programs(0) * n_chunks
    w_slot = _prefetch_step_weights(step, n_steps, w_copies)
    next_e = jnp.minimum((step + 1) // n_chunks, pl.num_programs(0) - 1)

    def row_pieces(first_row, r, pieces):
        per = rows // pieces
        return [(pl.ds(pl.multiple_of(first_row + r * rows + p * per, per), per), pl.ds(p * per, per))
                for p in range(pieces)]

    def in_copies_from(first_row):
        def in_copies(r, slot):
            return [pltpu.make_async_copy(xs_hbm.at[src], xbuf.at[slot, dst], in_sem.at[slot])
                    for src, dst in row_pieces(first_row, r, CHUNK_DMA_SPLIT)]
        return in_copies

    def out_copies(r, slot):
        return [pltpu.make_async_copy(obuf.at[slot, src], hid_hbm.at[n, dst], out_sem.at[slot])
                for dst, src in row_pieces(row0, r, 1)]

    def cast_weights():
        wg_sc[...] = wbuf[w_slot, 0].astype(BF16)
        wu_sc[...] = wbuf[w_slot, 1].astype(BF16)

    def compute(in_slot, out_slot):
        x = xbuf[in_slot]
        for c0 in range(0, nc, MOE_COL_BLOCK):
            cs = slice(c0, c0 + MOE_COL_BLOCK)
            g = jnp.dot(x, wg_sc[:, cs], preferred_element_type=F32) + bg_ref[0, :, cs]
            u = jnp.dot(x, wu_sc[:, cs], preferred_element_type=F32) + bu_ref[0, :, cs]
            g = jnp.minimum(g, SWIGLU_LIMIT)
            u = jnp.clip(u, -SWIGLU_LIMIT, SWIGLU_LIMIT)
            obuf[out_slot, :, cs] = ((u + 1.0) * (g * jax.nn.sigmoid(SWIGLU_ALPHA * g))).astype(obuf.dtype)

    _expert_chunk_loop(step, n_steps, tpe_ref[e], tpe_ref[next_e], in_copies_from(row0),
                       in_copies_from(start_ref[next_e]), out_copies, compute, cast_weights)

    @pl.when(e == pl.num_programs(0) - 1)
    def _():
        _zero_fill_tiles(row0 // rows + tpe_ref[e], hid_hbm.shape[1] // rows, obuf.at[0],
                         lambda t: hid_hbm.at[n, pl.ds(pl.multiple_of(t * rows, rows), rows)], out_sem.at[0])


def _moe_up(tiles_per_e, row_start, xs, w_gu, b_gu, rows, nc):
    r_pad, d = xs.shape
    n_e = w_gu.shape[0]
    d_ff = w_gu.shape[2] // 2
    n_chunks = d_ff // nc
    any_spec = pl.BlockSpec(memory_space=pl.ANY)
    grid_spec = pltpu.PrefetchScalarGridSpec(
        num_scalar_prefetch=2,
        grid=(n_e, n_chunks),
        in_specs=[
            any_spec,
            any_spec,
            pl.BlockSpec((1, 1, nc), lambda e, n, *_: (e, 0, n)),
            pl.BlockSpec((1, 1, nc), lambda e, n, *_: (e, 0, n_chunks + n)),
        ],
        out_specs=any_spec,
        scratch_shapes=[pltpu.VMEM((2, 2, d, nc), w_gu.dtype), pltpu.VMEM((d, nc), BF16), pltpu.VMEM((d, nc), BF16),
                        pltpu.VMEM((CHUNK_IN_SLOTS, rows, d), xs.dtype), pltpu.VMEM((2, rows, nc), BF16),
                        pltpu.SemaphoreType.DMA((2,)), pltpu.SemaphoreType.DMA((CHUNK_IN_SLOTS,)),
                        pltpu.SemaphoreType.DMA((2,))],
    )
    return pl.pallas_call(
        functools.partial(_moe_up_kernel, rows=rows),
        grid_spec=grid_spec,
        out_shape=jax.ShapeDtypeStruct((n_chunks, r_pad, nc), BF16),
        compiler_params=_cparams("arbitrary", "arbitrary"),
        name="moe_up",
    )(tiles_per_e, row_start, xs, w_gu, b_gu, b_gu)


def _moe_down_kernel(tpe_ref, start_ref, hid_hbm, w_hbm, b_ref, y_hbm,
                     wbuf, w_sc, xbuf, obuf, w_sem, in_sem, out_sem, *, rows):
    e = pl.program_id(0)
    n = pl.program_id(1)
    n_chunks = pl.num_programs(1)
    row0 = start_ref[e]
    k_chunks, _, kc = xbuf.shape[1:]
    d_ff, nc = w_sc.shape

    def w_copies(step, slot):
        e_, n_ = step // n_chunks, step % n_chunks
        cols = pl.ds(pl.multiple_of(n_ * nc, nc), nc)
        return [pltpu.make_async_copy(w_hbm.at[e_, rws, cols], wbuf.at[slot, rws], w_sem.at[slot])
                for rws in _split_rows(d_ff, DMA_SPLIT)]

    step, n_steps = e * n_chunks + n, pl.num_programs(0) * n_chunks
    w_slot = _prefetch_step_weights(step, n_steps, w_copies)
    next_e = jnp.minimum((step + 1) // n_chunks, pl.num_programs(0) - 1)

    def in_copies_from(first_row):
        def in_copies(r, slot):
            chunk = pl.ds(pl.multiple_of(first_row + r * rows, rows), rows)
            return [pltpu.make_async_copy(hid_hbm.at[c, chunk], xbuf.at[slot, c], in_sem.at[slot])
                    for c in range(k_chunks)]
        return in_copies

    lanes_per_row = nc // LANES
    per = rows * lanes_per_row // CHUNK_DMA_SPLIT

    def out_copies(r, slot):
        base = (row0 + r * rows) * lanes_per_row
        return [pltpu.make_async_copy(
            obuf.at[slot, pl.ds(p * per, per)],
            y_hbm.at[pl.ds(pl.multiple_of(base + p * per, per), per)], out_sem.at[slot])
            for p in range(CHUNK_DMA_SPLIT)]

    def cast_weights():
        w_sc[...] = wbuf[w_slot].astype(BF16)

    def compute(in_slot, out_slot):
        for c0 in range(0, nc, MOE_COL_BLOCK):
            cs = slice(c0, c0 + MOE_COL_BLOCK)
            acc = b_ref[0, :, cs] + jnp.dot(xbuf[in_slot, 0], w_sc[0:kc, cs], preferred_element_type=F32)
            for c in range(1, k_chunks):
                acc = acc + jnp.dot(xbuf[in_slot, c], w_sc[c * kc:(c + 1) * kc, cs], preferred_element_type=F32)
            for j in range(MOE_COL_BLOCK // LANES):
                lane_group = c0 // LANES + j
                obuf[out_slot, pl.ds(lane_group, rows, stride=lanes_per_row), :] = acc[:, j * LANES:(j + 1) * LANES]

    _expert_chunk_loop(step, n_steps, tpe_ref[e], tpe_ref[next_e], in_copies_from(row0),
                       in_copies_from(start_ref[next_e]), out_copies, compute, cast_weights)

    @pl.when(e == pl.num_programs(0) - 1)
    def _():
        tile_rows = rows * lanes_per_row
        _zero_fill_tiles(row0 // rows + tpe_ref[e], y_hbm.shape[0] // tile_rows, obuf.at[0],
                         lambda t: y_hbm.at[pl.ds(pl.multiple_of(t * tile_rows, tile_rows), tile_rows)], out_sem.at[0])


def _moe_down(tiles_per_e, row_start, hid, w_dn, b_dn, rows):
    k_chunks, r_pad, kc = hid.shape
    n_e, d_ff, d = w_dn.shape
    nc = d
    n_chunks = 1
    any_spec = pl.BlockSpec(memory_space=pl.ANY)
    grid_spec = pltpu.PrefetchScalarGridSpec(
        num_scalar_prefetch=2,
        grid=(n_e, n_chunks),
        in_specs=[
            any_spec,
            any_spec,
            pl.BlockSpec((1, 1, nc), lambda e, n, *_: (e, 0, n)),
        ],
        out_specs=any_spec,
        scratch_shapes=[pltpu.VMEM((2, d_ff, nc), w_dn.dtype), pltpu.VMEM((d_ff, nc), BF16),
                        pltpu.VMEM((CHUNK_IN_SLOTS, k_chunks, rows, kc), hid.dtype),
                        pltpu.VMEM((2, rows * nc // LANES, LANES), F32),
                        pltpu.SemaphoreType.DMA((2,)), pltpu.SemaphoreType.DMA((CHUNK_IN_SLOTS,)),
                        pltpu.SemaphoreType.DMA((2,))],
    )
    return pl.pallas_call(
        functools.partial(_moe_down_kernel, rows=rows),
        grid_spec=grid_spec,
        out_shape=jax.ShapeDtypeStruct((r_pad * d // LANES, LANES), F32),
        compiler_params=_cparams("arbitrary", "arbitrary"),
        name="moe_down",
    )(tiles_per_e, row_start, hid, w_dn, b_dn)


def _combine_kernel(dest_ref, h_ref, gate_ref, gf_ref, y_hbm, *rest, rows, final_norm, n_head_tiles, split):
    if split:
        o_ref, o_tail_ref, buf, sem = rest
    else:
        (o_ref, buf, sem), o_tail_ref = rest, None
    t = pl.program_id(0)
    n_t = pl.num_programs(0)
    n_dma = rows * TOP_K
    d = o_ref.shape[1]
    groups = d // LANES

    def issue(tile, slot):
        def body(j, carry):
            for u in range(DMA_ISSUE_UNROLL):
                i = j * DMA_ISSUE_UNROLL + u
                src = pl.ds(pl.multiple_of(dest_ref[tile * n_dma + i], groups), groups)
                dst = pl.ds(pl.multiple_of(i * groups, groups), groups)
                pltpu.make_async_copy(y_hbm.at[src], buf.at[slot, dst], sem.at[slot]).start(priority=u % 2)
            return carry
        lax.fori_loop(0, n_dma // DMA_ISSUE_UNROLL, body, 0)

    @pl.when(t == 0)
    def _():
        issue(0, 0)

    @pl.when(t + 1 < n_t)
    def _():
        issue(t + 1, (t + 1) % 2)

    slot = t % 2
    pltpu.make_async_copy(y_hbm.at[pl.ds(0, n_dma * groups)], buf.at[slot], sem.at[slot]).wait()

    def finish(o_ref):
        gates = [gate_ref[:, k:k + 1] for k in range(TOP_K)]
        sq = jnp.zeros((rows, 1), F32)
        for j in range(groups):
            cs = slice(j * LANES, (j + 1) * LANES)
            acc = h_ref[:, cs]
            for k in range(TOP_K):
                acc = acc + gates[k] * buf[slot, pl.ds(k * rows * groups + j, rows, stride=groups), :]
            o_ref[:, cs] = acc
            sq = sq + jnp.sum(acc * acc, axis=-1, keepdims=True)
        if final_norm:
            o_ref[...] = o_ref[...] * lax.rsqrt(sq / d + NORM_EPS) * gf_ref[...]

    if o_tail_ref is None:
        finish(o_ref)
    else:
        pl.when(t < n_head_tiles)(lambda: finish(o_ref))
        pl.when(t >= n_head_tiles)(lambda: finish(o_tail_ref))


def _combine(dest_km, h1, gate, gf, y, rows, final_norm, split_at=None):
    m, d = h1.shape
    n_head = m // rows if split_at is None else split_at // rows
    out_specs = [pl.BlockSpec((rows, d), lambda t, *_: (jnp.minimum(t, n_head - 1), 0))]
    out_shape = [jax.ShapeDtypeStruct((n_head * rows, d), F32)]
    if split_at is not None:
        assert split_at % rows == 0 and (m - split_at) % rows == 0
        out_specs.append(pl.BlockSpec((rows, d), lambda t, *_: (jnp.maximum(t - n_head, 0), 0)))
        out_shape.append(jax.ShapeDtypeStruct((m - split_at, d), F32))
    grid_spec = pltpu.PrefetchScalarGridSpec(
        num_scalar_prefetch=1,
        grid=(m // rows,),
        in_specs=[
            pl.BlockSpec((rows, d), lambda t, *_: (t, 0)),
            pl.BlockSpec((rows, TOP_K), lambda t, *_: (t, 0)),
            pl.BlockSpec((1, d), lambda t, *_: (0, 0)),
            pl.BlockSpec(memory_space=pl.ANY),
        ],
        out_specs=out_specs,
        scratch_shapes=[pltpu.VMEM((2, rows * TOP_K * d // LANES, LANES), F32), pltpu.SemaphoreType.DMA((2,))],
    )
    return pl.pallas_call(
        functools.partial(_combine_kernel, rows=rows, final_norm=final_norm, n_head_tiles=n_head,
                          split=split_at is not None),
        grid_spec=grid_spec,
        out_shape=out_shape,
        compiler_params=_cparams("arbitrary"),
        name="moe_combine",
    )(dest_km, h1, gate, gf, y)


def _routing_tables(top_e, rank, counts, rows, n_tiles):
    m = top_e.shape[0]
    n_experts = counts.shape[0]
    tiles_per_e = (counts + rows - 1) // rows
    tile_end = jnp.cumsum(tiles_per_e)
    pad_start = (tile_end - tiles_per_e) * rows
    experts = jnp.arange(n_experts, dtype=jnp.int32)
    dest = rank + jnp.sum(jnp.where(top_e[..., None] == experts, pad_start, 0), axis=-1)
    dest = dest.astype(jnp.int32).reshape(-1)
    n_used = tile_end[-1:].astype(jnp.int32)
    tok = jnp.arange(m * TOP_K, dtype=jnp.int32) // TOP_K
    row_tok = jnp.zeros((n_tiles * rows,), jnp.int32).at[dest].set(tok)
    return dest, row_tok, tiles_per_e.astype(jnp.int32), pad_start.astype(jnp.int32), n_used


def _largest_divisor(n, cap, mult):
    best = mult
    for c in range(mult, cap + 1, mult):
        if n % c == 0:
            best = c
    return best


def kernel(x_prompt, x_sample, cache_kv_w128, cache_kv_w512, cache_kv_w2048, state_pool, rel_bias, norm1, w_in,
           w_branch_a, w_branch_b, w_out, w_pool, pool_scale, norm2, router_w, router_b, w_gate_up, b_gate_up,
           w_down, b_down, norm_f):
    batch, seq, d = x_prompt.shape
    db, t_new, _ = x_sample.shape
    depth = norm1.shape[0]
    n_experts = router_w.shape[-1]
    caches = (cache_kv_w128, cache_kv_w512, cache_kv_w2048)
    m_p, m_s = batch * seq, db * t_new
    m = m_p + m_s
    tm = _largest_divisor(m, 640, 8)

    h = jnp.concatenate([x_prompt.reshape(m_p, d), x_sample.reshape(m_s, d)], axis=0)
    bias_p = _prompt_bias_tiles(rel_bias)
    bias_sc, bias_sn = _sample_bias_tiles(rel_bias, t_new)

    cuts = [ATTN_WIDTH, 2 * ATTN_WIDTH, 3 * ATTN_WIDTH, 3 * ATTN_WIDTH + POOL_WIDTH, 3 * ATTN_WIDTH + POOL_WIDTH + d]
    qkv0 = 2 * d
    z0 = qkv0 + 3 * ATTN_WIDTH

    kv_p = [[] for _ in range(N_GROUPS)]
    kv_s = [[] for _ in range(N_GROUPS)]
    pool_p, pool_s = [], []
    for l in range(depth):
        wq, wk, wv, wz, wga, wgb = jnp.split(w_in[l], cuts, axis=-1)
        w_perm = jnp.concatenate([wga, wgb, wq, wk, wv, wz], axis=-1).astype(BF16)
        proj = _inproj(h, norm1[l][None, :], w_perm, tm, _largest_divisor(w_perm.shape[1], 1536, 128))

        k0, v0 = qkv0 + ATTN_WIDTH, qkv0 + 2 * ATTN_WIDTH
        kv_w = HEADS_PER_GROUP * HEAD_DIM
        proj_s = proj[m_p:]
        z_s = proj_s[:, z0:z0 + POOL_WIDTH].reshape(db, t_new, POOL_WIDTH)

        attn_p = _prompt_attn(proj, bias_p, batch, seq, qkv0 // HEAD_DIM)
        qkv_s = proj_s[:, qkv0:qkv0 + 3 * ATTN_WIDTH].reshape(db, t_new, 3 * ATTN_WIDTH)
        c0 = caches[0][l].reshape(db, -1, HEAD_DIM)
        c1 = caches[1][l].reshape(db, -1, HEAD_DIM)
        dil2 = ATTN_GROUPS[2][1]
        c2v = caches[2][l].reshape(db, -1, dil2 * CACHE_ROW_TILE, HEAD_DIM)

        news = []
        for g, (window, _) in enumerate(ATTN_GROUPS):
            keep = min(window, seq)
            kc, vc = k0 + g * kv_w, v0 + g * kv_w

            def kv_rows(a, r0, r1, kc=kc, vc=vc):
                return jnp.stack([a[r0:r1, kc:kc + kv_w].reshape(-1, HEADS_PER_GROUP, HEAD_DIM),
                                  a[r0:r1, vc:vc + kv_w].reshape(-1, HEADS_PER_GROUP, HEAD_DIM)], axis=1)

            kv_p[g].append(jnp.stack([kv_rows(proj, (b + 1) * seq - keep, (b + 1) * seq) for b in range(batch)]))
            news.append(kv_rows(proj_s, 0, m_s).reshape(db, t_new * CACHE_ROW_TILE, HEAD_DIM))
        news[2] = news[2][:, None]
        attn_s, *rolled = _sample_attn(qkv_s, c0, c1, c2v, news, bias_sc, bias_sn)
        attn = jnp.concatenate([attn_p, attn_s.reshape(m_s, ATTN_OUT_WIDTH)], axis=0)
        for g in range(N_GROUPS):
            kv_s[g].append(rolled[g].reshape(caches[g][l].shape))

        w_pool_b = w_pool[l].astype(BF16)
        scale = pool_scale[l][None, :]
        hist_p = jnp.zeros((batch, POOL_HIST, POOL_WIDTH), F32)
        hist_s = jnp.concatenate([jnp.zeros((db, POOL_HIST - POOL_BUF, POOL_WIDTH), F32), state_pool[l]], axis=1)
        pool = jnp.concatenate([
            _pool_mix(hist_p, proj, pl.BlockSpec((seq, POOL_WIDTH), lambda i: (i, z0 // POOL_WIDTH)),
                      batch, seq, w_pool_b, scale, 0).reshape(m_p, POOL_WIDTH),
            _pool_mix(hist_s, z_s, pl.BlockSpec((1, t_new, POOL_WIDTH), lambda i: (i, 0, 0)),
                      db, t_new, w_pool_b, scale, PAST_LEN).reshape(m_s, POOL_WIDTH),
        ], axis=0)
        pool_p.append(jnp.stack([proj[(b + 1) * seq - POOL_BUF:(b + 1) * seq, z0:z0 + POOL_WIDTH]
                                 for b in range(batch)]))
        pool_s.append(jnp.concatenate([state_pool[l], z_s], axis=1)[:, t_new:])

        rw = router_w[l]
        rw_hi = rw.astype(BF16)
        rw_lo = (rw - rw_hi.astype(F32)).astype(BF16)
        h1, xn_rows, top_e, gate, rank, counts = _merge(
            h, attn, pool, proj, w_branch_a[l].astype(BF16), w_branch_b[l].astype(BF16), w_out[l].astype(BF16),
            norm2[l][None, :], rw_hi, rw_lo, router_b[l][None, :], _largest_divisor(m, 320, 8))

        rows = MOE_ROW_TILE
        groups = d // LANES
        n_tiles = (m * TOP_K + n_experts * (rows - 1)) // rows
        dest, row_tok, tiles_per_e, row_start, n_used = _routing_tables(top_e, rank, counts[0], rows, n_tiles)
        xs = _row_gather(n_used, row_tok * groups, xn_rows, d, n_tiles, rows, BF16)
        hid = _moe_up(tiles_per_e, row_start, xs, w_gate_up[l], b_gate_up[l][:, None, :], rows, 1024)
        y = _moe_down(tiles_per_e, row_start, hid, w_down[l], b_down[l][:, None, :], rows)
        crow = _largest_divisor(math.gcd(m_p, m_s), 128, 8)
        dest_km = (dest * groups).reshape(m // crow, crow, TOP_K).transpose(0, 2, 1).reshape(-1)
        if l == depth - 1:
            h_p, h_s = _combine(dest_km, h1, gate, norm_f[None, :], y, crow, final_norm=True, split_at=m_p)
        else:
            h, = _combine(dest_km, h1, gate, norm_f[None, :], y, crow, final_norm=False)

    y_prompt = h_p.reshape(batch, seq, d)
    y_sample = h_s.reshape(db, t_new, d)
    return (y_prompt, y_sample,
            jnp.stack(kv_p[0]), jnp.stack(kv_p[1]), jnp.stack(kv_p[2]), jnp.stack(pool_p),
            jnp.stack(kv_s[0]), jnp.stack(kv_s[1]), jnp.stack(kv_s[2]), jnp.stack(pool_s))
```

```python
import functools
import math

import jax
import jax.numpy as jnp
from jax import lax
from jax.experimental import pallas as pl
from jax.experimental.pallas import tpu as pltpu

F32 = jnp.float32
BF16 = jnp.bfloat16

HEAD_DIM = 128
HEADS_PER_GROUP = 4
ATTN_GROUPS = ((128, 1), (512, 4), (2048, 16))
N_GROUPS = len(ATTN_GROUPS)
N_ATTN_HEADS = N_GROUPS * HEADS_PER_GROUP
ATTN_WIDTH = N_ATTN_HEADS * HEAD_DIM
ATTN_OUT_WIDTH = HEADS_PER_GROUP * HEAD_DIM
ATTN_SCALE = HEAD_DIM ** -0.5
N_BACK = 128
N_REL_BUCKETS = 32
REL_MAX_DIST = 2048
POOL_WINDOWS = (2, 4, 8, 16)
POOL_GROUPS = len(POOL_WINDOWS)
PGW = 128
POOL_WIDTH = POOL_GROUPS * PGW
POOL_BUF = max(POOL_WINDOWS) - 1
POOL_HIST = 16
TOP_K = 4
SWIGLU_LIMIT = 7.0
SWIGLU_ALPHA = 1.702
NORM_EPS = 1e-5
NEG_INF = -1e30
PAST_LEN = 8192
CACHE_ROW_TILE = 2 * HEADS_PER_GROUP

VMEM_LIMIT_BYTES = 56 * 1024 * 1024
MOE_ROW_TILE = 256
DMA_ISSUE_UNROLL = 8
DMA_SPLIT = 8
CHUNK_DMA_SPLIT = 4
ATTN_INTERLEAVE = 4
LANES = 128
CHUNK_IN_SLOTS = 3
MOE_COL_BLOCK = 512


def _cparams(*sem):
    return pltpu.CompilerParams(dimension_semantics=sem, vmem_limit_bytes=VMEM_LIMIT_BYTES)


def _rmsnorm_f32(x, g):
    return x * lax.rsqrt(jnp.mean(x * x, axis=-1, keepdims=True) + NORM_EPS) * g


def _inproj_kernel(*refs):
    x_refs, (g_ref, w_ref, o_ref) = refs[:CHUNK_DMA_SPLIT], refs[CHUNK_DMA_SPLIT:]
    xs = [r[...] for r in x_refs]
    kc = xs[0].shape[1]
    sq = xs[0] * xs[0]
    for x in xs[1:]:
        sq = sq + x * x
    inv = lax.rsqrt(jnp.sum(sq, axis=-1, keepdims=True) / (kc * len(xs)) + NORM_EPS)
    acc = None
    for k, x in enumerate(xs):
        xn = (x * inv * g_ref[:, k * kc:(k + 1) * kc]).astype(BF16)
        part = jnp.dot(xn, w_ref[k * kc:(k + 1) * kc, :], preferred_element_type=F32)
        acc = part if acc is None else acc + part
    o_ref[...] = acc


def _inproj(x, g, w, tm, tn):
    m, d = x.shape
    n = w.shape[1]
    kc = d // CHUNK_DMA_SPLIT
    return pl.pallas_call(
        _inproj_kernel,
        grid=(n // tn, m // tm),
        in_specs=[pl.BlockSpec((tm, kc), lambda j, i, k=k: (i, k)) for k in range(CHUNK_DMA_SPLIT)] + [
            pl.BlockSpec((1, d), lambda j, i: (0, 0)),
            pl.BlockSpec((d, tn), lambda j, i: (0, j)),
        ],
        out_specs=pl.BlockSpec((tm, tn), lambda j, i: (i, j)),
        out_shape=jax.ShapeDtypeStruct((m, n), F32),
        compiler_params=_cparams("arbitrary", "arbitrary"),
        name="inproj",
    )(*([x] * CHUNK_DMA_SPLIT), g, w)


def _t5_bucket(dist):
    max_exact = N_REL_BUCKETS // 2
    df = jnp.maximum(dist, 1).astype(F32)
    large = max_exact + (jnp.log(df / max_exact) / math.log(REL_MAX_DIST / max_exact)
                         * (N_REL_BUCKETS - max_exact)).astype(jnp.int32)
    large = jnp.minimum(large, N_REL_BUCKETS - 1)
    return jnp.where(dist < max_exact, dist, large)


def _step_bias(rel_bias, g):
    _, dil = ATTN_GROUPS[g]
    buckets = _t5_bucket(jnp.arange(N_BACK + 1, dtype=jnp.int32) * dil)
    tab = rel_bias[buckets].astype(F32)
    return tab[:, g * HEADS_PER_GROUP:(g + 1) * HEADS_PER_GROUP].T


def _prompt_bias_tiles(rel_bias):
    nb = N_BACK
    period = 3 * nb - 1
    tiles = []
    for g in range(N_GROUPS):
        bj = _step_bias(rel_bias, g)
        h = bj.shape[0]
        pad = jnp.full((h, nb - 1), NEG_INF, F32)
        v = jnp.concatenate([pad, bj[:, ::-1], pad], axis=1)
        skew = jnp.tile(v, (1, nb + 1))[:, :nb * (period + 1)].reshape(h, nb, period + 1)
        tiles.append(skew[:, ::-1, :2 * nb])
    return jnp.stack(tiles, axis=1)


def _sample_bias_tiles(rel_bias, t_new):
    tq = jnp.arange(t_new)[:, None]
    outs = []
    new = []
    for g, (window, dil) in enumerate(ATTN_GROUPS):
        bj = _step_bias(rel_bias, g)
        if g < 2:
            p = jnp.arange(window)[None, :]
            dist = window + tq - p
            ok = (dist % dil == 0) & (dist // dil <= N_BACK)
            outs.append(jnp.where(ok[None], bj[:, jnp.clip(dist // dil, 0, N_BACK)], NEG_INF))
        else:
            col = jnp.arange(t_new * N_BACK)[None, :]
            blk, r = col // N_BACK, col % N_BACK
            ok = blk == tq
            outs.append(jnp.where(ok[None], bj[:, jnp.broadcast_to(N_BACK - r, (t_new, t_new * N_BACK))], NEG_INF))
        tk = jnp.arange(t_new)[None, :]
        d = tq - tk
        okn = (d >= 0) & (d % dil == 0) & (d // dil <= N_BACK)
        new.append(jnp.where(okn[None], bj[:, jnp.clip(d // dil, 0, N_BACK)], NEG_INF))
    return outs, jnp.stack(new, axis=0)


def _softmax_block(s, v):
    m = jnp.max(s, axis=-1, keepdims=True)
    p = jnp.exp(s - m)
    den = jnp.sum(p, axis=-1, keepdims=True)
    o = jnp.dot(p.astype(BF16), v, preferred_element_type=F32) / den
    return o, m + jnp.log(den)


def _prompt_attn_kernel(q0, k0, v0, q1, k1, v1, q2, k2, v2, bias_ref, o_ref, o_sc, l_sc, *, seq):
    qkv = ((q0, k0, v0), (q1, k1, v1), (q2, k2, v2))
    nb = N_BACK

    def rows(ref, start, size, dil):
        if dil == 1:
            return ref[pl.ds(start, size), :]
        return ref[pl.ds(start, size, stride=dil), :]

    def put(g, start, dil, o, lse):
        idx = pl.ds(start, nb) if dil == 1 else pl.ds(start, nb, stride=dil)
        o_sc[g, idx, :] = o
        l_sc[g, idx, :] = jnp.broadcast_to(lse, (nb, HEAD_DIM))

    dn = (((1,), (1,)), ((), ()))

    def attend(g, dil, q_starts, kv_starts, n_kv):
        q_ref, k_ref, v_ref = qkv[g]
        bias = bias_ref[0, g] if n_kv == 2 * nb else bias_ref[0, g, :, nb:]
        qs = [rows(q_ref, st, nb, dil).astype(BF16) for st in q_starts]
        ks = [rows(k_ref, st, n_kv, dil).astype(BF16) for st in kv_starts]
        ss = [lax.dot_general(q, k, dn, preferred_element_type=F32) * ATTN_SCALE + bias for q, k in zip(qs, ks)]
        ms = [jnp.max(s, axis=-1, keepdims=True) for s in ss]
        ps = [jnp.exp(s - m) for s, m in zip(ss, ms)]
        dens = [jnp.sum(p, axis=-1, keepdims=True) for p in ps]
        vs = [rows(v_ref, st, n_kv, dil).astype(BF16) for st in kv_starts]
        for st, p, v, m, den in zip(q_starts, ps, vs, ms, dens):
            o = jnp.dot(p.astype(BF16), v, preferred_element_type=F32) / den
            put(g, st, dil, o, m + jnp.log(den))

    for g, (_, dil) in enumerate(ATTN_GROUPS):
        n_blk = seq // (dil * nb)
        span = dil * nb
        for r0 in range(0, dil, ATTN_INTERLEAVE):
            starts = list(range(r0, min(r0 + ATTN_INTERLEAVE, dil)))
            attend(g, dil, starts, starts, nb)
        if n_blk > 1:
            per_iter = max(c for c in range(1, max(1, ATTN_INTERLEAVE // dil) + 1) if (n_blk - 1) % c == 0)

            def body(it, carry, g=g, dil=dil, span=span, per_iter=per_iter):
                q_starts, kv_starts = [], []
                for j in range(per_iter):
                    b = 1 + it * per_iter + j
                    for r in range(dil):
                        q_starts.append(b * span + r)
                        kv_starts.append((b - 1) * span + r)
                attend(g, dil, q_starts, kv_starts, 2 * nb)
                return carry

            lax.fori_loop(0, (n_blk - 1) // per_iter, body, 0)

    chunk = 256

    def comb(c, carry):
        sl = pl.ds(pl.multiple_of(c * chunk, chunk), chunk)
        l0, l1, l2 = l_sc[0, sl, :], l_sc[1, sl, :], l_sc[2, sl, :]
        mx = jnp.maximum(jnp.maximum(l0, l1), l2)
        e0, e1, e2 = jnp.exp(l0 - mx), jnp.exp(l1 - mx), jnp.exp(l2 - mx)
        num = e0 * o_sc[0, sl, :] + e1 * o_sc[1, sl, :] + e2 * o_sc[2, sl, :]
        o_ref[sl, :] = (num / (e0 + e1 + e2)).astype(o_ref.dtype)
        return carry

    lax.fori_loop(0, seq // chunk, comb, 0)


def _prompt_attn(p, bias, batch, seq, col0):
    def spec(kind, g):
        base = col0 + kind * N_ATTN_HEADS + g * HEADS_PER_GROUP
        return pl.BlockSpec((seq, HEAD_DIM), lambda b, i, base=base: (b, base + i))

    in_specs = []
    for g in range(N_GROUPS):
        in_specs += [spec(0, g), spec(1, g), spec(2, g)]
    in_specs.append(pl.BlockSpec((1, N_GROUPS, N_BACK, 2 * N_BACK), lambda b, i: (i, 0, 0, 0)))
    return pl.pallas_call(
        functools.partial(_prompt_attn_kernel, seq=seq),
        grid=(batch, HEADS_PER_GROUP),
        in_specs=in_specs,
        out_specs=pl.BlockSpec((seq, HEAD_DIM), lambda b, i: (b, i)),
        out_shape=jax.ShapeDtypeStruct((batch * seq, ATTN_OUT_WIDTH), BF16),
        scratch_shapes=[pltpu.VMEM((N_GROUPS, seq, HEAD_DIM), F32), pltpu.VMEM((N_GROUPS, seq, HEAD_DIM), F32)],
        compiler_params=_cparams("arbitrary", "arbitrary"),
        name="prompt_attn",
    )(*([p] * 9), bias)


def _sample_attn_kernel(qkv_ref, c0_ref, c1_ref, c2_ref, n0_ref, n1_ref, n2_ref, b0_ref, b1_ref, b2_ref, bn_ref,
                        o_ref, r0_ref, r1_ref, r2_ref, *, t_new):
    caches = (c0_ref, c1_ref, c2_ref)
    biases = (b0_ref, b1_ref, b2_ref)
    dn = (((1,), (1,)), ((), ()))

    shift = t_new * CACHE_ROW_TILE
    for c_ref, n_ref, r_ref in ((c0_ref, n0_ref, r0_ref), (c1_ref, n1_ref, r1_ref)):
        keep = c_ref.shape[1] - shift
        r_ref[0, :keep] = c_ref[0, shift:]
        r_ref[0, keep:] = n_ref[0]
    n_r, width = c2_ref.shape[1], c2_ref.shape[2]
    r2_ref[0, :, :width - shift] = c2_ref[0, :, shift:]
    r2_ref[0, :n_r - 1, width - shift:] = c2_ref[0, 1:, :shift]
    r2_ref[0, n_r - 1:, width - shift:] = n2_ref[0]

    for i in range(HEADS_PER_GROUP):
        outs, lses = [], []
        for g in range(N_GROUPS):
            h = g * HEADS_PER_GROUP + i
            q = qkv_ref[0, :, h * HEAD_DIM:(h + 1) * HEAD_DIM].astype(BF16)
            kn = qkv_ref[0, :, ATTN_WIDTH + h * HEAD_DIM:ATTN_WIDTH + (h + 1) * HEAD_DIM].astype(BF16)
            vn = qkv_ref[0, :, 2 * ATTN_WIDTH + h * HEAD_DIM:2 * ATTN_WIDTH + (h + 1) * HEAD_DIM].astype(BF16)
            c_ref = caches[g]
            if g < 2:
                n_rows = c_ref.shape[1] // CACHE_ROW_TILE
                kc = c_ref[0, pl.ds(i, n_rows, stride=CACHE_ROW_TILE), :].astype(BF16)
                vc = c_ref[0, pl.ds(HEADS_PER_GROUP + i, n_rows, stride=CACHE_ROW_TILE), :].astype(BF16)
            else:
                kc = jnp.concatenate(
                    [c_ref[0, :, t * CACHE_ROW_TILE + i, :] for t in range(t_new)], axis=0).astype(BF16)
                vc = jnp.concatenate(
                    [c_ref[0, :, t * CACHE_ROW_TILE + HEADS_PER_GROUP + i, :] for t in range(t_new)],
                    axis=0).astype(BF16)
            sc = lax.dot_general(q, kc, dn, preferred_element_type=F32) * ATTN_SCALE + biases[g][i]
            sn = lax.dot_general(q, kn, dn, preferred_element_type=F32) * ATTN_SCALE + bn_ref[g, i]
            m = jnp.maximum(jnp.max(sc, axis=-1, keepdims=True), jnp.max(sn, axis=-1, keepdims=True))
            pc = jnp.exp(sc - m)
            pn = jnp.exp(sn - m)
            den = jnp.sum(pc, axis=-1, keepdims=True) + jnp.sum(pn, axis=-1, keepdims=True)
            o = (jnp.dot(pc.astype(BF16), vc, preferred_element_type=F32)
                 + jnp.dot(pn.astype(BF16), vn, preferred_element_type=F32)) / den
            outs.append(o)
            lses.append(m + jnp.log(den))
        mx = jnp.maximum(jnp.maximum(lses[0], lses[1]), lses[2])
        es = [jnp.exp(l - mx) for l in lses]
        num = es[0] * outs[0] + es[1] * outs[1] + es[2] * outs[2]
        o_ref[0, :, i * HEAD_DIM:(i + 1) * HEAD_DIM] = (num / (es[0] + es[1] + es[2])).astype(o_ref.dtype)


def _sample_attn(qkv_s, c0, c1, c2v, news, bias_c, bias_n):
    db, t_new, _ = qkv_s.shape

    def full(a):
        return pl.BlockSpec(a.shape, lambda b, nd=a.ndim: (0,) * nd)

    def per_seq(a):
        return pl.BlockSpec((1,) + a.shape[1:], lambda b, nd=a.ndim: (b,) + (0,) * (nd - 1))

    return pl.pallas_call(
        functools.partial(_sample_attn_kernel, t_new=t_new),
        grid=(db,),
        in_specs=[per_seq(qkv_s), per_seq(c0), per_seq(c1), per_seq(c2v), per_seq(news[0]), per_seq(news[1]),
                  per_seq(news[2]), full(bias_c[0]), full(bias_c[1]), full(bias_c[2]), full(bias_n)],
        out_specs=[pl.BlockSpec((1, t_new, ATTN_OUT_WIDTH), lambda b: (b, 0, 0)),
                   per_seq(c0), per_seq(c1), per_seq(c2v)],
        out_shape=[jax.ShapeDtypeStruct((db, t_new, ATTN_OUT_WIDTH), BF16)]
                  + [jax.ShapeDtypeStruct(c.shape, c.dtype) for c in (c0, c1, c2v)],
        compiler_params=_cparams("arbitrary"),
        name="sample_attn",
    )(qkv_s, c0, c1, c2v, *news, bias_c[0], bias_c[1], bias_c[2], bias_n)


def _pool_kernel(hist_ref, z_ref, w_ref, s_ref, o_ref, zbuf, *, t_len, pos0, chunk):
    zbuf[0:POOL_HIST, :] = hist_ref[0]
    zbuf[POOL_HIST:, :] = z_ref[...].reshape(t_len, POOL_WIDTH)
    for c in range(t_len // chunk):
        base = POOL_HIST + c * chunk
        for g, w in enumerate(POOL_WINDOWS):
            sl = slice(g * PGW, (g + 1) * PGW)
            cur = zbuf[base:base + chunk, sl]
            tot = cur
            for i in range(1, w):
                tot = tot + zbuf[base - i:base - i + chunk, sl]
            pos = pos0 + c * chunk + lax.broadcasted_iota(jnp.int32, (chunk, PGW), 0)
            cnt = jnp.minimum(w, pos + 1).astype(F32)
            mixed = tot / cnt - cur
            y = jnp.dot(mixed.astype(BF16), w_ref[g], preferred_element_type=F32) * s_ref[:, sl]
            o_ref[0, c * chunk:(c + 1) * chunk, sl] = y.astype(o_ref.dtype)


def _pool_mix(hist, z, z_spec, n_seq, t_len, w_pool, pool_scale, pos0):
    chunk = min(t_len, 256)
    return pl.pallas_call(
        functools.partial(_pool_kernel, t_len=t_len, pos0=pos0, chunk=chunk),
        grid=(n_seq,),
        in_specs=[
            pl.BlockSpec((1, POOL_HIST, POOL_WIDTH), lambda i: (i, 0, 0)),
            z_spec,
            pl.BlockSpec(w_pool.shape, lambda i: (0, 0, 0)),
            pl.BlockSpec((1, POOL_WIDTH), lambda i: (0, 0)),
        ],
        out_specs=pl.BlockSpec((1, t_len, POOL_WIDTH), lambda i: (i, 0, 0)),
        out_shape=jax.ShapeDtypeStruct((n_seq, t_len, POOL_WIDTH), BF16),
        scratch_shapes=[pltpu.VMEM((POOL_HIST + t_len, POOL_WIDTH), F32)],
        compiler_params=_cparams("arbitrary"),
        name="pool_mix",
    )(hist, z, w_pool, pool_scale)


def _merge_kernel(h_ref, a_ref, p_ref, ga_ref, gb_ref, wa_ref, wb_ref, wo_ref, g2_ref, rwh_ref, rwl_ref, rb_ref,
                  h1_ref, xn_ref, e_ref, gate_ref, rank_ref, cnt_ref, carry):
    ya = jnp.dot(a_ref[...], wa_ref[...], preferred_element_type=F32)
    yb = jnp.dot(p_ref[...], wb_ref[...], preferred_element_type=F32)
    u = jax.nn.sigmoid(ga_ref[...]) * ya + jax.nn.sigmoid(gb_ref[...]) * yb
    h1 = h_ref[...] + jnp.dot(u.astype(BF16), wo_ref[...], preferred_element_type=F32)
    h1_ref[...] = h1
    xn = _rmsnorm_f32(h1, g2_ref[...])
    tm, d = xn.shape
    half = d // (2 * LANES)
    for j in range(half):
        lo = _bf16_bits(xn[:, j * LANES:(j + 1) * LANES])
        hi = _bf16_bits(xn[:, (j + half) * LANES:(j + half + 1) * LANES])
        xn_ref[pl.ds(j, tm, stride=half), :] = (hi & jnp.uint32(0xFFFF0000)) | (lo >> 16)
    _route_tile(xn, rwh_ref, rwl_ref, rb_ref, e_ref, gate_ref, rank_ref, cnt_ref, carry)


def _bf16_bits(x):
    return lax.bitcast_convert_type(x.astype(BF16).astype(F32), jnp.uint32)


def _merge(h, attn, pool, proj, wa, wb, wo, g2, rwh, rwl, rb, tm):
    m, d = h.shape
    n_e = rwh.shape[1]

    def const(a):
        return pl.BlockSpec(a.shape, lambda i, nd=a.ndim: (0,) * nd, pipeline_mode=pl.Buffered(1))

    tok_spec = pl.BlockSpec((tm, TOP_K), lambda i: (i, 0))
    return pl.pallas_call(
        _merge_kernel,
        grid=(m // tm,),
        in_specs=[
            pl.BlockSpec((tm, d), lambda i: (i, 0)),
            pl.BlockSpec((tm, attn.shape[1]), lambda i: (i, 0)),
            pl.BlockSpec((tm, pool.shape[1]), lambda i: (i, 0)),
            pl.BlockSpec((tm, d), lambda i: (i, 0)),
            pl.BlockSpec((tm, d), lambda i: (i, 1)),
            const(wa), const(wb), const(wo), const(g2), const(rwh), const(rwl), const(rb),
        ],
        out_specs=[pl.BlockSpec((tm, d), lambda i: (i, 0)),
                   pl.BlockSpec((tm * d // (2 * LANES), LANES), lambda i: (i, 0)),
                   tok_spec, tok_spec, tok_spec, pl.BlockSpec((1, n_e), lambda i: (0, 0))],
        out_shape=[jax.ShapeDtypeStruct((m, d), F32), jax.ShapeDtypeStruct((m * d // (2 * LANES), LANES), jnp.uint32),
                   jax.ShapeDtypeStruct((m, TOP_K), jnp.int32), jax.ShapeDtypeStruct((m, TOP_K), F32),
                   jax.ShapeDtypeStruct((m, TOP_K), jnp.int32), jax.ShapeDtypeStruct((1, n_e), jnp.int32)],
        scratch_shapes=[pltpu.VMEM((1, n_e), F32)],
        compiler_params=_cparams("arbitrary"),
        name="merge_route",
    )(h, attn, pool, proj, proj, wa, wb, wo, g2, rwh, rwl, rb)


def _route_tile(x, wh_ref, wl_ref, b_ref, e_ref, gate_ref, rank_ref, cnt_ref, carry):
    @pl.when(pl.program_id(0) == 0)
    def _():
        carry[...] = jnp.zeros_like(carry)

    xh = x.astype(BF16)
    xl = (x - xh.astype(F32)).astype(BF16)
    logits = (jnp.dot(xh, wh_ref[...], preferred_element_type=F32)
              + jnp.dot(xh, wl_ref[...], preferred_element_type=F32)
              + jnp.dot(xl, wh_ref[...], preferred_element_type=F32)) + b_ref[...]
    n_e = logits.shape[-1]
    lane = lax.broadcasted_iota(jnp.int32, logits.shape, 1)
    vals = logits
    tops, idxs = [], []
    for _ in range(TOP_K):
        m = jnp.max(vals, axis=-1, keepdims=True)
        idx = jnp.min(jnp.where(vals == m, lane, n_e), axis=-1, keepdims=True)
        tops.append(m)
        idxs.append(idx)
        vals = jnp.where(lane == idx, -jnp.inf, vals)
    ex = [jnp.exp(t - tops[0]) for t in tops]
    den = ex[0] + ex[1] + ex[2] + ex[3]
    for k in range(TOP_K):
        e_ref[:, k:k + 1] = idxs[k]
        gate_ref[:, k:k + 1] = ex[k] / den

    onehots = [(lane == idx).astype(F32) for idx in idxs]
    cnt = onehots[0] + onehots[1] + onehots[2] + onehots[3]
    tm = cnt.shape[0]
    lower = (lax.broadcasted_iota(jnp.int32, (tm, tm), 0) > lax.broadcasted_iota(jnp.int32, (tm, tm), 1))
    prefix = jnp.dot(lower.astype(BF16), cnt.astype(BF16), preferred_element_type=F32)
    base = carry[...] + prefix
    for k in range(TOP_K):
        rank_ref[:, k:k + 1] = jnp.sum(onehots[k] * base, axis=-1, keepdims=True).astype(jnp.int32)
    carry[...] = carry[...] + jnp.sum(cnt, axis=0, keepdims=True)
    cnt_ref[...] = carry[...].astype(jnp.int32)


def _row_gather_kernel(nused_ref, tok_ref, src_hbm, o_ref, buf, sem, *, rows):
    t = pl.program_id(0)
    n_used = nused_ref[0]
    groups = o_ref.shape[1] // (2 * LANES)

    def issue(tile, slot):
        def body(j, carry):
            for u in range(DMA_ISSUE_UNROLL):
                i = j * DMA_ISSUE_UNROLL + u
                src = pl.ds(pl.multiple_of(tok_ref[tile * rows + i], groups), groups)
                dst = pl.ds(pl.multiple_of(i * groups, groups), groups)
                pltpu.make_async_copy(src_hbm.at[src], buf.at[slot, dst], sem.at[slot]).start(priority=u % 2)
            return carry
        lax.fori_loop(0, rows // DMA_ISSUE_UNROLL, body, 0)

    @pl.when(jnp.logical_and(t == 0, n_used > 0))
    def _():
        issue(0, 0)

    @pl.when(t + 1 < n_used)
    def _():
        issue(t + 1, (t + 1) % 2)

    @pl.when(t < n_used)
    def _():
        slot = t % 2
        pltpu.make_async_copy(src_hbm.at[pl.ds(0, rows * groups)], buf.at[slot], sem.at[slot]).wait()
        for j in range(groups):
            w = buf[slot, pl.ds(j, rows, stride=groups), :]
            lo = lax.bitcast_convert_type(w << 16, F32)
            hi = lax.bitcast_convert_type(w & jnp.uint32(0xFFFF0000), F32)
            o_ref[:, j * LANES:(j + 1) * LANES] = lo.astype(o_ref.dtype)
            o_ref[:, (j + groups) * LANES:(j + groups + 1) * LANES] = hi.astype(o_ref.dtype)

    @pl.when(t >= n_used)
    def _():
        o_ref[...] = jnp.zeros_like(o_ref)


def _row_gather(n_used, row_idx, src, d, n_tiles, rows, out_dtype):
    groups = d // (2 * LANES)
    grid_spec = pltpu.PrefetchScalarGridSpec(
        num_scalar_prefetch=2,
        grid=(n_tiles,),
        in_specs=[pl.BlockSpec(memory_space=pl.ANY)],
        out_specs=pl.BlockSpec((rows, d), lambda t, *_: (t, 0)),
        scratch_shapes=[pltpu.VMEM((2, rows * groups, LANES), src.dtype), pltpu.SemaphoreType.DMA((2,))],
    )
    return pl.pallas_call(
        functools.partial(_row_gather_kernel, rows=rows),
        grid_spec=grid_spec,
        out_shape=jax.ShapeDtypeStruct((n_tiles * rows, d), out_dtype),
        compiler_params=_cparams("arbitrary"),
        name="moe_row_gather",
    )(n_used, row_idx, src)


def _start_all(copies, priority=0):
    for c in copies:
        c.start(priority=priority)


def _wait_all(copies):
    for c in copies:
        c.wait()


def _split_rows(n_rows, pieces):
    per = n_rows // pieces
    assert per * pieces == n_rows
    return [pl.ds(p * per, per) for p in range(pieces)]


def _prefetch_step_weights(step, n_steps, copies_for):
    slot = step % 2

    @pl.when(step == 0)
    def _():
        _start_all(copies_for(step, slot), priority=1)

    @pl.when(step + 1 < n_steps)
    def _():
        _start_all(copies_for(step + 1, 1 - slot), priority=1)

    _wait_all(copies_for(step, slot))
    return slot


def _expert_chunk_loop(step, n_steps, n_chunks, next_n_chunks, in_copies, next_in_copies, out_copies, compute):
    def start_first_two(count, copies):
        @pl.when(count > 0)
        def _():
            _start_all(copies(0, 0))

        @pl.when(count > 1)
        def _():
            _start_all(copies(1, 1))

    @pl.when(step == 0)
    def _():
        start_first_two(n_chunks, in_copies)

    @pl.when(n_chunks > 0)
    def _():
        def chunk(r, first):
            in_slot = r % CHUNK_IN_SLOTS
            out_slot = r % 2

            @pl.when(r + 2 < n_chunks)
            def _():
                _start_all(in_copies(r + 2, (r + 2) % CHUNK_IN_SLOTS))

            _wait_all(in_copies(r, in_slot))

            if not first:
                @pl.when(r >= 2)
                def _():
                    _wait_all(out_copies(r - 2, out_slot))

            compute(in_slot, out_slot, first)
            _start_all(out_copies(r, out_slot))

        chunk(0, True)

        def body(r, carry):
            chunk(r, False)
            return carry

        lax.fori_loop(1, n_chunks, body, 0)

    @pl.when(step + 1 < n_steps)
    def _():
        start_first_two(next_n_chunks, next_in_copies)

    @pl.when(n_chunks >= 2)
    def _():
        _wait_all(out_copies(n_chunks - 2, n_chunks % 2))

    @pl.when(n_chunks >= 1)
    def _():
        _wait_all(out_copies(n_chunks - 1, (n_chunks - 1) % 2))


def _zero_fill_tiles(first, last, zbuf, dst, sem):
    zbuf[...] = jnp.zeros_like(zbuf)

    def copy(t):
        return pltpu.make_async_copy(zbuf, dst(t), sem)

    def start(t, carry):
        copy(t).start()
        return carry

    def wait(t, carry):
        copy(t).wait()
        return carry

    lax.fori_loop(first, last, start, 0)
    lax.fori_loop(first, last, wait, 0)


def _moe_up_kernel(tpe_ref, start_ref, xs_hbm, w_hbm, bg_ref, bu_ref, hid_hbm,
                   wbuf, wg_sc, wu_sc, xbuf, obuf, w_sem, in_sem, out_sem, *, rows):
    e = pl.program_id(0)
    n = pl.program_id(1)
    n_chunks = pl.num_programs(1)
    row0 = start_ref[e]
    d, nc = wg_sc.shape

    def w_copies(step, slot):
        e_, n_ = step // n_chunks, step % n_chunks
        copies = []
        for mat in range(2):
            cols = pl.ds(pl.multiple_of((mat * n_chunks + n_) * nc, nc), nc)
            for rws in _split_rows(d, DMA_SPLIT):
                copies.append(pltpu.make_async_copy(w_hbm.at[e_, rws, cols], wbuf.at[slot, mat, rws], w_sem.at[slot]))
        return copies

    step, n_steps = e * n_chunks + n, pl.num_programs(0) * n_chunks
    w_slot = _prefetch_step_weights(step, n_steps, w_copies)
    next_e = jnp.minimum((step + 1) // n_chunks, pl.num_programs(0) - 1)

    def row_pieces(first_row, r, pieces):
        per = rows // pieces
        return [(pl.ds(pl.multiple_of(first_row + r * rows + p * per, per), per), pl.ds(p * per, per))
                for p in range(pieces)]

    def in_copies_from(first_row):
        def in_copies(r, slot):
            return [pltpu.make_async_copy(xs_hbm.at[src], xbuf.at[slot, dst], in_sem.at[slot])
                    for src, dst in row_pieces(first_row, r, CHUNK_DMA_SPLIT)]
        return in_copies

    def out_copies(r, slot):
        return [pltpu.make_async_copy(obuf.at[slot, src], hid_hbm.at[n, dst], out_sem.at[slot])
                for dst, src in row_pieces(row0, r, 1)]

    def compute(in_slot, out_slot, first):
        x = xbuf[in_slot]
        for c0 in range(0, nc, MOE_COL_BLOCK):
            cs = slice(c0, c0 + MOE_COL_BLOCK)
            if first:
                wg_sc[:, cs] = wbuf[w_slot, 0, :, cs].astype(BF16)
                wu_sc[:, cs] = wbuf[w_slot, 1, :, cs].astype(BF16)
            g = jnp.dot(x, wg_sc[:, cs], preferred_element_type=F32) + bg_ref[0, :, cs]
            u = jnp.dot(x, wu_sc[:, cs], preferred_element_type=F32) + bu_ref[0, :, cs]
            g = jnp.minimum(g, SWIGLU_LIMIT)
            u = jnp.clip(u, -SWIGLU_LIMIT, SWIGLU_LIMIT)
            obuf[out_slot, :, cs] = ((u + 1.0) * (g * jax.nn.sigmoid(SWIGLU_ALPHA * g))).astype(obuf.dtype)

    _expert_chunk_loop(step, n_steps, tpe_ref[e], tpe_ref[next_e], in_copies_from(row0),
                       in_copies_from(start_ref[next_e]), out_copies, compute)

    @pl.when(e == pl.num_programs(0) - 1)
    def _():
        _zero_fill_tiles(row0 // rows + tpe_ref[e], hid_hbm.shape[1] // rows, obuf.at[0],
                         lambda t: hid_hbm.at[n, pl.ds(pl.multiple_of(t * rows, rows), rows)], out_sem.at[0])


def _moe_up(tiles_per_e, row_start, xs, w_gu, b_gu, rows, nc):
    r_pad, d = xs.shape
    n_e = w_gu.shape[0]
    d_ff = w_gu.shape[2] // 2
    n_chunks = d_ff // nc
    any_spec = pl.BlockSpec(memory_space=pl.ANY)
    grid_spec = pltpu.PrefetchScalarGridSpec(
        num_scalar_prefetch=2,
        grid=(n_e, n_chunks),
        in_specs=[
            any_spec,
            any_spec,
            pl.BlockSpec((1, 1, nc), lambda e, n, *_: (e, 0, n)),
            pl.BlockSpec((1, 1, nc), lambda e, n, *_: (e, 0, n_chunks + n)),
        ],
        out_specs=any_spec,
        scratch_shapes=[pltpu.VMEM((2, 2, d, nc), w_gu.dtype), pltpu.VMEM((d, nc), BF16), pltpu.VMEM((d, nc), BF16),
                        pltpu.VMEM((CHUNK_IN_SLOTS, rows, d), xs.dtype), pltpu.VMEM((2, rows, nc), BF16),
                        pltpu.SemaphoreType.DMA((2,)), pltpu.SemaphoreType.DMA((CHUNK_IN_SLOTS,)),
                        pltpu.SemaphoreType.DMA((2,))],
    )
    return pl.pallas_call(
        functools.partial(_moe_up_kernel, rows=rows),
        grid_spec=grid_spec,
        out_shape=jax.ShapeDtypeStruct((n_chunks, r_pad, nc), BF16),
        compiler_params=_cparams("arbitrary", "arbitrary"),
        name="moe_up",
    )(tiles_per_e, row_start, xs, w_gu, b_gu, b_gu)


def _moe_down_kernel(tpe_ref, start_ref, hid_hbm, w_hbm, b_ref, y_hbm,
                     wbuf, w_sc, xbuf, obuf, w_sem, in_sem, out_sem, *, rows):
    e = pl.program_id(0)
    n = pl.program_id(1)
    n_chunks = pl.num_programs(1)
    row0 = start_ref[e]
    k_chunks, _, kc = xbuf.shape[1:]
    d_ff, nc = w_sc.shape

    def w_copies(step, slot):
        e_, n_ = step // n_chunks, step % n_chunks
        cols = pl.ds(pl.multiple_of(n_ * nc, nc), nc)
        return [pltpu.make_async_copy(w_hbm.at[e_, rws, cols], wbuf.at[slot, rws], w_sem.at[slot])
                for rws in _split_rows(d_ff, DMA_SPLIT)]

    step, n_steps = e * n_chunks + n, pl.num_programs(0) * n_chunks
    w_slot = _prefetch_step_weights(step, n_steps, w_copies)
    next_e = jnp.minimum((step + 1) // n_chunks, pl.num_programs(0) - 1)

    def in_copies_from(first_row):
        def in_copies(r, slot):
            chunk = pl.ds(pl.multiple_of(first_row + r * rows, rows), rows)
            return [pltpu.make_async_copy(hid_hbm.at[c, chunk], xbuf.at[slot, c], in_sem.at[slot])
                    for c in range(k_chunks)]
        return in_copies

    lanes_per_row = nc // LANES
    per = rows * lanes_per_row // CHUNK_DMA_SPLIT

    def out_copies(r, slot):
        base = (row0 + r * rows) * lanes_per_row
        return [pltpu.make_async_copy(
            obuf.at[slot, pl.ds(p * per, per)],
            y_hbm.at[pl.ds(pl.multiple_of(base + p * per, per), per)], out_sem.at[slot])
            for p in range(CHUNK_DMA_SPLIT)]

    def compute(in_slot, out_slot, first):
        for c0 in range(0, nc, MOE_COL_BLOCK):
            cs = slice(c0, c0 + MOE_COL_BLOCK)
            if first:
                w_sc[:, cs] = wbuf[w_slot, :, cs].astype(BF16)
            acc = b_ref[0, :, cs] + jnp.dot(xbuf[in_slot, 0], w_sc[0:kc, cs], preferred_element_type=F32)
            for c in range(1, k_chunks):
                acc = acc + jnp.dot(xbuf[in_slot, c], w_sc[c * kc:(c + 1) * kc, cs], preferred_element_type=F32)
            for j in range(MOE_COL_BLOCK // LANES):
                lane_group = c0 // LANES + j
                obuf[out_slot, pl.ds(lane_group, rows, stride=lanes_per_row), :] = acc[:, j * LANES:(j + 1) * LANES]

    _expert_chunk_loop(step, n_steps, tpe_ref[e], tpe_ref[next_e], in_copies_from(row0),
                       in_copies_from(start_ref[next_e]), out_copies, compute)

    @pl.when(e == pl.num_programs(0) - 1)
    def _():
        tile_rows = rows * lanes_per_row
        _zero_fill_tiles(row0 // rows + tpe_ref[e], y_hbm.shape[0] // tile_rows, obuf.at[0],
                         lambda t: y_hbm.at[pl.ds(pl.multiple_of(t * tile_rows, tile_rows), tile_rows)], out_sem.at[0])


def _moe_down(tiles_per_e, row_start, hid, w_dn, b_dn, rows):
    k_chunks, r_pad, kc = hid.shape
    n_e, d_ff, d = w_dn.shape
    nc = d
    n_chunks = 1
    any_spec = pl.BlockSpec(memory_space=pl.ANY)
    grid_spec = pltpu.PrefetchScalarGridSpec(
        num_scalar_prefetch=2,
        grid=(n_e, n_chunks),
        in_specs=[
            any_spec,
            any_spec,
            pl.BlockSpec((1, 1, nc), lambda e, n, *_: (e, 0, n)),
        ],
        out_specs=any_spec,
        scratch_shapes=[pltpu.VMEM((2, d_ff, nc), w_dn.dtype), pltpu.VMEM((d_ff, nc), BF16),
                        pltpu.VMEM((CHUNK_IN_SLOTS, k_chunks, rows, kc), hid.dtype),
                        pltpu.VMEM((2, rows * nc // LANES, LANES), F32),
                        pltpu.SemaphoreType.DMA((2,)), pltpu.SemaphoreType.DMA((CHUNK_IN_SLOTS,)),
                        pltpu.SemaphoreType.DMA((2,))],
    )
    return pl.pallas_call(
        functools.partial(_moe_down_kernel, rows=rows),
        grid_spec=grid_spec,
        out_shape=jax.ShapeDtypeStruct((r_pad * d // LANES, LANES), F32),
        compiler_params=_cparams("arbitrary", "arbitrary"),
        name="moe_down",
    )(tiles_per_e, row_start, hid, w_dn, b_dn)


def _combine_kernel(dest_ref, h_ref, gate_ref, gf_ref, y_hbm, *rest, rows, final_norm, n_head_tiles, split):
    if split:
        o_ref, o_tail_ref, buf, sem = rest
    else:
        (o_ref, buf, sem), o_tail_ref = rest, None
    t = pl.program_id(0)
    n_t = pl.num_programs(0)
    n_dma = rows * TOP_K
    d = o_ref.shape[1]
    groups = d // LANES

    def issue(tile, slot):
        def body(j, carry):
            for u in range(DMA_ISSUE_UNROLL):
                i = j * DMA_ISSUE_UNROLL + u
                src = pl.ds(pl.multiple_of(dest_ref[tile * n_dma + i], groups), groups)
                dst = pl.ds(pl.multiple_of(i * groups, groups), groups)
                pltpu.make_async_copy(y_hbm.at[src], buf.at[slot, dst], sem.at[slot]).start(priority=u % 2)
            return carry
        lax.fori_loop(0, n_dma // DMA_ISSUE_UNROLL, body, 0)

    @pl.when(t == 0)
    def _():
        issue(0, 0)

    @pl.when(t + 1 < n_t)
    def _():
        issue(t + 1, (t + 1) % 2)

    slot = t % 2
    pltpu.make_async_copy(y_hbm.at[pl.ds(0, n_dma * groups)], buf.at[slot], sem.at[slot]).wait()

    def finish(o_ref):
        gates = [gate_ref[:, k:k + 1] for k in range(TOP_K)]
        sq = jnp.zeros((rows, 1), F32)
        for j in range(groups):
            cs = slice(j * LANES, (j + 1) * LANES)
            acc = h_ref[:, cs]
            for k in range(TOP_K):
                acc = acc + gates[k] * buf[slot, pl.ds(k * rows * groups + j, rows, stride=groups), :]
            o_ref[:, cs] = acc
            sq = sq + jnp.sum(acc * acc, axis=-1, keepdims=True)
        if final_norm:
            o_ref[...] = o_ref[...] * lax.rsqrt(sq / d + NORM_EPS) * gf_ref[...]

    if o_tail_ref is None:
        finish(o_ref)
    else:
        pl.when(t < n_head_tiles)(lambda: finish(o_ref))
        pl.when(t >= n_head_tiles)(lambda: finish(o_tail_ref))


def _combine(dest_km, h1, gate, gf, y, rows, final_norm, split_at=None):
    m, d = h1.shape
    n_head = m // rows if split_at is None else split_at // rows
    out_specs = [pl.BlockSpec((rows, d), lambda t, *_: (jnp.minimum(t, n_head - 1), 0))]
    out_shape = [jax.ShapeDtypeStruct((n_head * rows, d), F32)]
    if split_at is not None:
        assert split_at % rows == 0 and (m - split_at) % rows == 0
        out_specs.append(pl.BlockSpec((rows, d), lambda t, *_: (jnp.maximum(t - n_head, 0), 0)))
        out_shape.append(jax.ShapeDtypeStruct((m - split_at, d), F32))
    grid_spec = pltpu.PrefetchScalarGridSpec(
        num_scalar_prefetch=1,
        grid=(m // rows,),
        in_specs=[
            pl.BlockSpec((rows, d), lambda t, *_: (t, 0)),
            pl.BlockSpec((rows, TOP_K), lambda t, *_: (t, 0)),
            pl.BlockSpec((1, d), lambda t, *_: (0, 0)),
            pl.BlockSpec(memory_space=pl.ANY),
        ],
        out_specs=out_specs,
        scratch_shapes=[pltpu.VMEM((2, rows * TOP_K * d // LANES, LANES), F32), pltpu.SemaphoreType.DMA((2,))],
    )
    return pl.pallas_call(
        functools.partial(_combine_kernel, rows=rows, final_norm=final_norm, n_head_tiles=n_head,
                          split=split_at is not None),
        grid_spec=grid_spec,
        out_shape=out_shape,
        compiler_params=_cparams("arbitrary"),
        name="moe_combine",
    )(dest_km, h1, gate, gf, y)


def _routing_tables(top_e, rank, counts, rows, n_tiles):
    m = top_e.shape[0]
    n_experts = counts.shape[0]
    tiles_per_e = (counts + rows - 1) // rows
    tile_end = jnp.cumsum(tiles_per_e)
    pad_start = (tile_end - tiles_per_e) * rows
    experts = jnp.arange(n_experts, dtype=jnp.int32)
    dest = rank + jnp.sum(jnp.where(top_e[..., None] == experts, pad_start, 0), axis=-1)
    dest = dest.astype(jnp.int32).reshape(-1)
    n_used = tile_end[-1:].astype(jnp.int32)
    tok = jnp.arange(m * TOP_K, dtype=jnp.int32) // TOP_K
    row_tok = jnp.zeros((n_tiles * rows,), jnp.int32).at[dest].set(tok)
    return dest, row_tok, tiles_per_e.astype(jnp.int32), pad_start.astype(jnp.int32), n_used


def _largest_divisor(n, cap, mult):
    best = mult
    for c in range(mult, cap + 1, mult):
        if n % c == 0:
            best = c
    return best


def kernel(x_prompt, x_sample, cache_kv_w128, cache_kv_w512, cache_kv_w2048, state_pool, rel_bias, norm1, w_in,
           w_branch_a, w_branch_b, w_out, w_pool, pool_scale, norm2, router_w, router_b, w_gate_up, b_gate_up,
           w_down, b_down, norm_f):
    batch, seq, d = x_prompt.shape
    db, t_new, _ = x_sample.shape
    depth = norm1.shape[0]
    n_experts = router_w.shape[-1]
    caches = (cache_kv_w128, cache_kv_w512, cache_kv_w2048)
    m_p, m_s = batch * seq, db * t_new
    m = m_p + m_s
    tm = _largest_divisor(m, 640, 8)

    h = jnp.concatenate([x_prompt.reshape(m_p, d), x_sample.reshape(m_s, d)], axis=0)
    bias_p = _prompt_bias_tiles(rel_bias)
    bias_sc, bias_sn = _sample_bias_tiles(rel_bias, t_new)

    cuts = [ATTN_WIDTH, 2 * ATTN_WIDTH, 3 * ATTN_WIDTH, 3 * ATTN_WIDTH + POOL_WIDTH, 3 * ATTN_WIDTH + POOL_WIDTH + d]
    qkv0 = 2 * d
    z0 = qkv0 + 3 * ATTN_WIDTH

    kv_p = [[] for _ in range(N_GROUPS)]
    kv_s = [[] for _ in range(N_GROUPS)]
    pool_p, pool_s = [], []
    for l in range(depth):
        wq, wk, wv, wz, wga, wgb = jnp.split(w_in[l], cuts, axis=-1)
        w_perm = jnp.concatenate([wga, wgb, wq, wk, wv, wz], axis=-1).astype(BF16)
        proj = _inproj(h, norm1[l][None, :], w_perm, tm, _largest_divisor(w_perm.shape[1], 1536, 128))

        k0, v0 = qkv0 + ATTN_WIDTH, qkv0 + 2 * ATTN_WIDTH
        kv_w = HEADS_PER_GROUP * HEAD_DIM
        proj_s = proj[m_p:]
        z_s = proj_s[:, z0:z0 + POOL_WIDTH].reshape(db, t_new, POOL_WIDTH)

        attn_p = _prompt_attn(proj, bias_p, batch, seq, qkv0 // HEAD_DIM)
        qkv_s = proj_s[:, qkv0:qkv0 + 3 * ATTN_WIDTH].reshape(db, t_new, 3 * ATTN_WIDTH)
        c0 = caches[0][l].reshape(db, -1, HEAD_DIM)
        c1 = caches[1][l].reshape(db, -1, HEAD_DIM)
        dil2 = ATTN_GROUPS[2][1]
        c2v = caches[2][l].reshape(db, -1, dil2 * CACHE_ROW_TILE, HEAD_DIM)

        news = []
        for g, (window, _) in enumerate(ATTN_GROUPS):
            keep = min(window, seq)
            kc, vc = k0 + g * kv_w, v0 + g * kv_w

            def kv_rows(a, r0, r1, kc=kc, vc=vc):
                return jnp.stack([a[r0:r1, kc:kc + kv_w].reshape(-1, HEADS_PER_GROUP, HEAD_DIM),
                                  a[r0:r1, vc:vc + kv_w].reshape(-1, HEADS_PER_GROUP, HEAD_DIM)], axis=1)

            kv_p[g].append(jnp.stack([kv_rows(proj, (b + 1) * seq - keep, (b + 1) * seq) for b in range(batch)]))
            news.append(kv_rows(proj_s, 0, m_s).reshape(db, t_new * CACHE_ROW_TILE, HEAD_DIM))
        news[2] = news[2][:, None]
        attn_s, *rolled = _sample_attn(qkv_s, c0, c1, c2v, news, bias_sc, bias_sn)
        attn = jnp.concatenate([attn_p, attn_s.reshape(m_s, ATTN_OUT_WIDTH)], axis=0)
        for g in range(N_GROUPS):
            kv_s[g].append(rolled[g].reshape(caches[g][l].shape))

        w_pool_b = w_pool[l].astype(BF16)
        scale = pool_scale[l][None, :]
        hist_p = jnp.zeros((batch, POOL_HIST, POOL_WIDTH), F32)
        hist_s = jnp.concatenate([jnp.zeros((db, POOL_HIST - POOL_BUF, POOL_WIDTH), F32), state_pool[l]], axis=1)
        pool = jnp.concatenate([
            _pool_mix(hist_p, proj, pl.BlockSpec((seq, POOL_WIDTH), lambda i: (i, z0 // POOL_WIDTH)),
                      batch, seq, w_pool_b, scale, 0).reshape(m_p, POOL_WIDTH),
            _pool_mix(hist_s, z_s, pl.BlockSpec((1, t_new, POOL_WIDTH), lambda i: (i, 0, 0)),
                      db, t_new, w_pool_b, scale, PAST_LEN).reshape(m_s, POOL_WIDTH),
        ], axis=0)
        pool_p.append(jnp.stack([proj[(b + 1) * seq - POOL_BUF:(b + 1) * seq, z0:z0 + POOL_WIDTH]
                                 for b in range(batch)]))
        pool_s.append(jnp.concatenate([state_pool[l], z_s], axis=1)[:, t_new:])

        rw = router_w[l]
        rw_hi = rw.astype(BF16)
        rw_lo = (rw - rw_hi.astype(F32)).astype(BF16)
        h1, xn_rows, top_e, gate, rank, counts = _merge(
            h, attn, pool, proj, w_branch_a[l].astype(BF16), w_branch_b[l].astype(BF16), w_out[l].astype(BF16),
            norm2[l][None, :], rw_hi, rw_lo, router_b[l][None, :], _largest_divisor(m, 320, 8))

        rows = MOE_ROW_TILE
        groups = d // LANES
        n_tiles = (m * TOP_K + n_experts * (rows - 1)) // rows
        dest, row_tok, tiles_per_e, row_start, n_used = _routing_tables(top_e, rank, counts[0], rows, n_tiles)
        xs = _row_gather(n_used, row_tok * (groups // 2), xn_rows, d, n_tiles, rows, BF16)
        hid = _moe_up(tiles_per_e, row_start, xs, w_gate_up[l], b_gate_up[l][:, None, :], rows, 1024)
        y = _moe_down(tiles_per_e, row_start, hid, w_down[l], b_down[l][:, None, :], rows)
        crow = _largest_divisor(math.gcd(m_p, m_s), 128, 8)
        dest_km = (dest * groups).reshape(m // crow, crow, TOP_K).transpose(0, 2, 1).reshape(-1)
        if l == depth - 1:
            h_p, h_s = _combine(dest_km, h1, gate, norm_f[None, :], y, crow, final_norm=True, split_at=m_p)
        else:
            h, = _combine(dest_km, h1, gate, norm_f[None, :], y, crow, final_norm=False)

    y_prompt = h_p.reshape(batch, seq, d)
    y_sample = h_s.reshape(db, t_new, d)
    return (y_prompt, y_sample,
            jnp.stack(kv_p[0]), jnp.stack(kv_p[1]), jnp.stack(kv_p[2]), jnp.stack(pool_p),
            jnp.stack(kv_s[0]), jnp.stack(kv_s[1]), jnp.stack(kv_s[2]), jnp.stack(pool_s))
```

```python
import functools
import math

import jax
import jax.numpy as jnp
from jax import lax
from jax.experimental import pallas as pl
from jax.experimental.pallas import tpu as pltpu

F32 = jnp.float32
BF16 = jnp.bfloat16

HEAD_DIM = 128
HEADS_PER_GROUP = 4
ATTN_GROUPS = ((128, 1), (512, 4), (2048, 16))
N_GROUPS = len(ATTN_GROUPS)
N_ATTN_HEADS = N_GROUPS * HEADS_PER_GROUP
ATTN_WIDTH = N_ATTN_HEADS * HEAD_DIM
ATTN_OUT_WIDTH = HEADS_PER_GROUP * HEAD_DIM
ATTN_SCALE = HEAD_DIM ** -0.5
N_BACK = 128
N_REL_BUCKETS = 32
REL_MAX_DIST = 2048
POOL_WINDOWS = (2, 4, 8, 16)
POOL_GROUPS = len(POOL_WINDOWS)
PGW = 128
POOL_WIDTH = POOL_GROUPS * PGW
POOL_BUF = max(POOL_WINDOWS) - 1
POOL_HIST = 16
TOP_K = 4
SWIGLU_LIMIT = 7.0
SWIGLU_ALPHA = 1.702
NORM_EPS = 1e-5
NEG_INF = -1e30
PAST_LEN = 8192
CACHE_ROW_TILE = 2 * HEADS_PER_GROUP

VMEM_LIMIT_BYTES = 56 * 1024 * 1024
MOE_ROW_TILE = 256
DMA_ISSUE_UNROLL = 8
DMA_SPLIT = 8
CHUNK_DMA_SPLIT = 4
ATTN_INTERLEAVE = 4
LANES = 128
CHUNK_IN_SLOTS = 3
MOE_COL_BLOCK = 512
WEIGHT_PAIRS_PER_CHUNK = 2


def _cparams(*sem):
    return pltpu.CompilerParams(dimension_semantics=sem, vmem_limit_bytes=VMEM_LIMIT_BYTES)


def _rmsnorm_f32(x, g):
    return x * lax.rsqrt(jnp.mean(x * x, axis=-1, keepdims=True) + NORM_EPS) * g


def _inproj_kernel(*refs):
    x_refs, (g_ref, w_ref, o_ref) = refs[:CHUNK_DMA_SPLIT], refs[CHUNK_DMA_SPLIT:]
    xs = [r[...] for r in x_refs]
    kc = xs[0].shape[1]
    sq = xs[0] * xs[0]
    for x in xs[1:]:
        sq = sq + x * x
    inv = lax.rsqrt(jnp.sum(sq, axis=-1, keepdims=True) / (kc * len(xs)) + NORM_EPS)
    acc = None
    for k, x in enumerate(xs):
        xn = (x * inv * g_ref[:, k * kc:(k + 1) * kc]).astype(BF16)
        part = jnp.dot(xn, w_ref[k * kc:(k + 1) * kc, :], preferred_element_type=F32)
        acc = part if acc is None else acc + part
    o_ref[...] = acc


def _inproj(x, g, w, tm, tn):
    m, d = x.shape
    n = w.shape[1]
    kc = d // CHUNK_DMA_SPLIT
    return pl.pallas_call(
        _inproj_kernel,
        grid=(n // tn, m // tm),
        in_specs=[pl.BlockSpec((tm, kc), lambda j, i, k=k: (i, k)) for k in range(CHUNK_DMA_SPLIT)] + [
            pl.BlockSpec((1, d), lambda j, i: (0, 0)),
            pl.BlockSpec((d, tn), lambda j, i: (0, j)),
        ],
        out_specs=pl.BlockSpec((tm, tn), lambda j, i: (i, j)),
        out_shape=jax.ShapeDtypeStruct((m, n), F32),
        compiler_params=_cparams("arbitrary", "arbitrary"),
        name="inproj",
    )(*([x] * CHUNK_DMA_SPLIT), g, w)


def _t5_bucket(dist):
    max_exact = N_REL_BUCKETS // 2
    df = jnp.maximum(dist, 1).astype(F32)
    large = max_exact + (jnp.log(df / max_exact) / math.log(REL_MAX_DIST / max_exact)
                         * (N_REL_BUCKETS - max_exact)).astype(jnp.int32)
    large = jnp.minimum(large, N_REL_BUCKETS - 1)
    return jnp.where(dist < max_exact, dist, large)


def _step_bias(rel_bias, g):
    _, dil = ATTN_GROUPS[g]
    buckets = _t5_bucket(jnp.arange(N_BACK + 1, dtype=jnp.int32) * dil)
    tab = rel_bias[buckets].astype(F32)
    return tab[:, g * HEADS_PER_GROUP:(g + 1) * HEADS_PER_GROUP].T


def _prompt_bias_tiles(rel_bias):
    nb = N_BACK
    period = 3 * nb - 1
    tiles = []
    for g in range(N_GROUPS):
        bj = _step_bias(rel_bias, g)
        h = bj.shape[0]
        pad = jnp.full((h, nb - 1), NEG_INF, F32)
        v = jnp.concatenate([pad, bj[:, ::-1], pad], axis=1)
        skew = jnp.tile(v, (1, nb + 1))[:, :nb * (period + 1)].reshape(h, nb, period + 1)
        tiles.append(skew[:, ::-1, :2 * nb])
    return jnp.stack(tiles, axis=1)


def _sample_bias_tiles(rel_bias, t_new):
    tq = jnp.arange(t_new)[:, None]
    outs = []
    new = []
    for g, (window, dil) in enumerate(ATTN_GROUPS):
        bj = _step_bias(rel_bias, g)
        if g < 2:
            p = jnp.arange(window)[None, :]
            dist = window + tq - p
            ok = (dist % dil == 0) & (dist // dil <= N_BACK)
            outs.append(jnp.where(ok[None], bj[:, jnp.clip(dist // dil, 0, N_BACK)], NEG_INF))
        else:
            col = jnp.arange(t_new * N_BACK)[None, :]
            blk, r = col // N_BACK, col % N_BACK
            ok = blk == tq
            outs.append(jnp.where(ok[None], bj[:, jnp.broadcast_to(N_BACK - r, (t_new, t_new * N_BACK))], NEG_INF))
        tk = jnp.arange(t_new)[None, :]
        d = tq - tk
        okn = (d >= 0) & (d % dil == 0) & (d // dil <= N_BACK)
        new.append(jnp.where(okn[None], bj[:, jnp.clip(d // dil, 0, N_BACK)], NEG_INF))
    return outs, jnp.stack(new, axis=0)


def _softmax_block(s, v):
    m = jnp.max(s, axis=-1, keepdims=True)
    p = jnp.exp(s - m)
    den = jnp.sum(p, axis=-1, keepdims=True)
    o = jnp.dot(p.astype(BF16), v, preferred_element_type=F32) / den
    return o, m + jnp.log(den)


def _prompt_attn_kernel(q0, k0, v0, q1, k1, v1, q2, k2, v2, bias_ref, o_ref, o_sc, l_sc, *, seq):
    qkv = ((q0, k0, v0), (q1, k1, v1), (q2, k2, v2))
    nb = N_BACK

    def rows(ref, start, size, dil):
        if dil == 1:
            return ref[pl.ds(start, size), :]
        return ref[pl.ds(start, size, stride=dil), :]

    def put(g, start, dil, o, lse):
        idx = pl.ds(start, nb) if dil == 1 else pl.ds(start, nb, stride=dil)
        o_sc[g, idx, :] = o
        l_sc[g, idx, :] = jnp.broadcast_to(lse, (nb, HEAD_DIM))

    dn = (((1,), (1,)), ((), ()))

    def attend(g, dil, q_starts, kv_starts, n_kv):
        q_ref, k_ref, v_ref = qkv[g]
        bias = bias_ref[0, g] if n_kv == 2 * nb else bias_ref[0, g, :, nb:]
        qs = [rows(q_ref, st, nb, dil).astype(BF16) for st in q_starts]
        ks = [rows(k_ref, st, n_kv, dil).astype(BF16) for st in kv_starts]
        ss = [lax.dot_general(q, k, dn, preferred_element_type=F32) * ATTN_SCALE + bias for q, k in zip(qs, ks)]
        ms = [jnp.max(s, axis=-1, keepdims=True) for s in ss]
        ps = [jnp.exp(s - m) for s, m in zip(ss, ms)]
        dens = [jnp.sum(p, axis=-1, keepdims=True) for p in ps]
        vs = [rows(v_ref, st, n_kv, dil).astype(BF16) for st in kv_starts]
        for st, p, v, m, den in zip(q_starts, ps, vs, ms, dens):
            o = jnp.dot(p.astype(BF16), v, preferred_element_type=F32) / den
            put(g, st, dil, o, m + jnp.log(den))

    for g, (_, dil) in enumerate(ATTN_GROUPS):
        n_blk = seq // (dil * nb)
        span = dil * nb
        for r0 in range(0, dil, ATTN_INTERLEAVE):
            starts = list(range(r0, min(r0 + ATTN_INTERLEAVE, dil)))
            attend(g, dil, starts, starts, nb)
        if n_blk > 1:
            per_iter = max(c for c in range(1, max(1, ATTN_INTERLEAVE // dil) + 1) if (n_blk - 1) % c == 0)

            def body(it, carry, g=g, dil=dil, span=span, per_iter=per_iter):
                q_starts, kv_starts = [], []
                for j in range(per_iter):
                    b = 1 + it * per_iter + j
                    for r in range(dil):
                        q_starts.append(b * span + r)
                        kv_starts.append((b - 1) * span + r)
                attend(g, dil, q_starts, kv_starts, 2 * nb)
                return carry

            lax.fori_loop(0, (n_blk - 1) // per_iter, body, 0)

    chunk = 256

    def comb(c, carry):
        sl = pl.ds(pl.multiple_of(c * chunk, chunk), chunk)
        l0, l1, l2 = l_sc[0, sl, :], l_sc[1, sl, :], l_sc[2, sl, :]
        mx = jnp.maximum(jnp.maximum(l0, l1), l2)
        e0, e1, e2 = jnp.exp(l0 - mx), jnp.exp(l1 - mx), jnp.exp(l2 - mx)
        num = e0 * o_sc[0, sl, :] + e1 * o_sc[1, sl, :] + e2 * o_sc[2, sl, :]
        o_ref[sl, :] = (num / (e0 + e1 + e2)).astype(o_ref.dtype)
        return carry

    lax.fori_loop(0, seq // chunk, comb, 0)


def _prompt_attn(p, bias, batch, seq, col0):
    def spec(kind, g):
        base = col0 + kind * N_ATTN_HEADS + g * HEADS_PER_GROUP
        return pl.BlockSpec((seq, HEAD_DIM), lambda b, i, base=base: (b, base + i))

    in_specs = []
    for g in range(N_GROUPS):
        in_specs += [spec(0, g), spec(1, g), spec(2, g)]
    in_specs.append(pl.BlockSpec((1, N_GROUPS, N_BACK, 2 * N_BACK), lambda b, i: (i, 0, 0, 0)))
    return pl.pallas_call(
        functools.partial(_prompt_attn_kernel, seq=seq),
        grid=(batch, HEADS_PER_GROUP),
        in_specs=in_specs,
        out_specs=pl.BlockSpec((seq, HEAD_DIM), lambda b, i: (b, i)),
        out_shape=jax.ShapeDtypeStruct((batch * seq, ATTN_OUT_WIDTH), BF16),
        scratch_shapes=[pltpu.VMEM((N_GROUPS, seq, HEAD_DIM), F32), pltpu.VMEM((N_GROUPS, seq, HEAD_DIM), F32)],
        compiler_params=_cparams("arbitrary", "arbitrary"),
        name="prompt_attn",
    )(*([p] * 9), bias)


def _sample_attn_kernel(qkv_ref, c0_ref, c1_ref, c2_ref, n0_ref, n1_ref, n2_ref, b0_ref, b1_ref, b2_ref, bn_ref,
                        o_ref, r0_ref, r1_ref, r2_ref, *, t_new):
    caches = (c0_ref, c1_ref, c2_ref)
    biases = (b0_ref, b1_ref, b2_ref)
    dn = (((1,), (1,)), ((), ()))

    shift = t_new * CACHE_ROW_TILE
    for c_ref, n_ref, r_ref in ((c0_ref, n0_ref, r0_ref), (c1_ref, n1_ref, r1_ref)):
        keep = c_ref.shape[1] - shift
        r_ref[0, :keep] = c_ref[0, shift:]
        r_ref[0, keep:] = n_ref[0]
    n_r, width = c2_ref.shape[1], c2_ref.shape[2]
    r2_ref[0, :, :width - shift] = c2_ref[0, :, shift:]
    r2_ref[0, :n_r - 1, width - shift:] = c2_ref[0, 1:, :shift]
    r2_ref[0, n_r - 1:, width - shift:] = n2_ref[0]

    for i in range(HEADS_PER_GROUP):
        outs, lses = [], []
        for g in range(N_GROUPS):
            h = g * HEADS_PER_GROUP + i
            q = qkv_ref[0, :, h * HEAD_DIM:(h + 1) * HEAD_DIM].astype(BF16)
            kn = qkv_ref[0, :, ATTN_WIDTH + h * HEAD_DIM:ATTN_WIDTH + (h + 1) * HEAD_DIM].astype(BF16)
            vn = qkv_ref[0, :, 2 * ATTN_WIDTH + h * HEAD_DIM:2 * ATTN_WIDTH + (h + 1) * HEAD_DIM].astype(BF16)
            c_ref = caches[g]
            if g < 2:
                n_rows = c_ref.shape[1] // CACHE_ROW_TILE
                kc = c_ref[0, pl.ds(i, n_rows, stride=CACHE_ROW_TILE), :].astype(BF16)
                vc = c_ref[0, pl.ds(HEADS_PER_GROUP + i, n_rows, stride=CACHE_ROW_TILE), :].astype(BF16)
            else:
                kc = jnp.concatenate(
                    [c_ref[0, :, t * CACHE_ROW_TILE + i, :] for t in range(t_new)], axis=0).astype(BF16)
                vc = jnp.concatenate(
                    [c_ref[0, :, t * CACHE_ROW_TILE + HEADS_PER_GROUP + i, :] for t in range(t_new)],
                    axis=0).astype(BF16)
            sc = lax.dot_general(q, kc, dn, preferred_element_type=F32) * ATTN_SCALE + biases[g][i]
            sn = lax.dot_general(q, kn, dn, preferred_element_type=F32) * ATTN_SCALE + bn_ref[g, i]
            m = jnp.maximum(jnp.max(sc, axis=-1, keepdims=True), jnp.max(sn, axis=-1, keepdims=True))
            pc = jnp.exp(sc - m)
            pn = jnp.exp(sn - m)
            den = jnp.sum(pc, axis=-1, keepdims=True) + jnp.sum(pn, axis=-1, keepdims=True)
            o = (jnp.dot(pc.astype(BF16), vc, preferred_element_type=F32)
                 + jnp.dot(pn.astype(BF16), vn, preferred_element_type=F32)) / den
            outs.append(o)
            lses.append(m + jnp.log(den))
        mx = jnp.maximum(jnp.maximum(lses[0], lses[1]), lses[2])
        es = [jnp.exp(l - mx) for l in lses]
        num = es[0] * outs[0] + es[1] * outs[1] + es[2] * outs[2]
        o_ref[0, :, i * HEAD_DIM:(i + 1) * HEAD_DIM] = (num / (es[0] + es[1] + es[2])).astype(o_ref.dtype)


def _sample_attn(qkv_s, c0, c1, c2v, news, bias_c, bias_n):
    db, t_new, _ = qkv_s.shape

    def full(a):
        return pl.BlockSpec(a.shape, lambda b, nd=a.ndim: (0,) * nd)

    def per_seq(a):
        return pl.BlockSpec((1,) + a.shape[1:], lambda b, nd=a.ndim: (b,) + (0,) * (nd - 1))

    return pl.pallas_call(
        functools.partial(_sample_attn_kernel, t_new=t_new),
        grid=(db,),
        in_specs=[per_seq(qkv_s), per_seq(c0), per_seq(c1), per_seq(c2v), per_seq(news[0]), per_seq(news[1]),
                  per_seq(news[2]), full(bias_c[0]), full(bias_c[1]), full(bias_c[2]), full(bias_n)],
        out_specs=[pl.BlockSpec((1, t_new, ATTN_OUT_WIDTH), lambda b: (b, 0, 0)),
                   per_seq(c0), per_seq(c1), per_seq(c2v)],
        out_shape=[jax.ShapeDtypeStruct((db, t_new, ATTN_OUT_WIDTH), BF16)]
                  + [jax.ShapeDtypeStruct(c.shape, c.dtype) for c in (c0, c1, c2v)],
        compiler_params=_cparams("arbitrary"),
        name="sample_attn",
    )(qkv_s, c0, c1, c2v, *news, bias_c[0], bias_c[1], bias_c[2], bias_n)


def _pool_kernel(hist_ref, z_ref, w_ref, s_ref, o_ref, zbuf, *, t_len, pos0, chunk):
    zbuf[0:POOL_HIST, :] = hist_ref[0]
    zbuf[POOL_HIST:, :] = z_ref[...].reshape(t_len, POOL_WIDTH)
    for c in range(t_len // chunk):
        base = POOL_HIST + c * chunk
        for g, w in enumerate(POOL_WINDOWS):
            sl = slice(g * PGW, (g + 1) * PGW)
            cur = zbuf[base:base + chunk, sl]
            tot = cur
            for i in range(1, w):
                tot = tot + zbuf[base - i:base - i + chunk, sl]
            pos = pos0 + c * chunk + lax.broadcasted_iota(jnp.int32, (chunk, PGW), 0)
            cnt = jnp.minimum(w, pos + 1).astype(F32)
            mixed = tot / cnt - cur
            y = jnp.dot(mixed.astype(BF16), w_ref[g], preferred_element_type=F32) * s_ref[:, sl]
            o_ref[0, c * chunk:(c + 1) * chunk, sl] = y.astype(o_ref.dtype)


def _pool_mix(hist, z, z_spec, n_seq, t_len, w_pool, pool_scale, pos0):
    chunk = min(t_len, 256)
    return pl.pallas_call(
        functools.partial(_pool_kernel, t_len=t_len, pos0=pos0, chunk=chunk),
        grid=(n_seq,),
        in_specs=[
            pl.BlockSpec((1, POOL_HIST, POOL_WIDTH), lambda i: (i, 0, 0)),
            z_spec,
            pl.BlockSpec(w_pool.shape, lambda i: (0, 0, 0)),
            pl.BlockSpec((1, POOL_WIDTH), lambda i: (0, 0)),
        ],
        out_specs=pl.BlockSpec((1, t_len, POOL_WIDTH), lambda i: (i, 0, 0)),
        out_shape=jax.ShapeDtypeStruct((n_seq, t_len, POOL_WIDTH), BF16),
        scratch_shapes=[pltpu.VMEM((POOL_HIST + t_len, POOL_WIDTH), F32)],
        compiler_params=_cparams("arbitrary"),
        name="pool_mix",
    )(hist, z, w_pool, pool_scale)


def _merge_kernel(h_ref, a_ref, p_ref, ga_ref, gb_ref, wa_ref, wb_ref, wo_ref, g2_ref, rwh_ref, rwl_ref, rb_ref,
                  h1_ref, xn_ref, e_ref, gate_ref, rank_ref, cnt_ref, carry):
    ya = jnp.dot(a_ref[...], wa_ref[...], preferred_element_type=F32)
    yb = jnp.dot(p_ref[...], wb_ref[...], preferred_element_type=F32)
    u = jax.nn.sigmoid(ga_ref[...]) * ya + jax.nn.sigmoid(gb_ref[...]) * yb
    h1 = h_ref[...] + jnp.dot(u.astype(BF16), wo_ref[...], preferred_element_type=F32)
    h1_ref[...] = h1
    xn = _rmsnorm_f32(h1, g2_ref[...])
    tm, d = xn.shape
    half = d // (2 * LANES)
    for j in range(half):
        lo = _bf16_bits(xn[:, j * LANES:(j + 1) * LANES])
        hi = _bf16_bits(xn[:, (j + half) * LANES:(j + half + 1) * LANES])
        xn_ref[pl.ds(j, tm, stride=half), :] = (hi & jnp.uint32(0xFFFF0000)) | (lo >> 16)
    _route_tile(xn, rwh_ref, rwl_ref, rb_ref, e_ref, gate_ref, rank_ref, cnt_ref, carry)


def _bf16_bits(x):
    return lax.bitcast_convert_type(x.astype(BF16).astype(F32), jnp.uint32)


def _merge(h, attn, pool, proj, wa, wb, wo, g2, rwh, rwl, rb, tm):
    m, d = h.shape
    n_e = rwh.shape[1]

    def const(a):
        return pl.BlockSpec(a.shape, lambda i, nd=a.ndim: (0,) * nd, pipeline_mode=pl.Buffered(1))

    tok_spec = pl.BlockSpec((tm, TOP_K), lambda i: (i, 0))
    return pl.pallas_call(
        _merge_kernel,
        grid=(m // tm,),
        in_specs=[
            pl.BlockSpec((tm, d), lambda i: (i, 0)),
            pl.BlockSpec((tm, attn.shape[1]), lambda i: (i, 0)),
            pl.BlockSpec((tm, pool.shape[1]), lambda i: (i, 0)),
            pl.BlockSpec((tm, d), lambda i: (i, 0)),
            pl.BlockSpec((tm, d), lambda i: (i, 1)),
            const(wa), const(wb), const(wo), const(g2), const(rwh), const(rwl), const(rb),
        ],
        out_specs=[pl.BlockSpec((tm, d), lambda i: (i, 0)),
                   pl.BlockSpec((tm * d // (2 * LANES), LANES), lambda i: (i, 0)),
                   tok_spec, tok_spec, tok_spec, pl.BlockSpec((1, n_e), lambda i: (0, 0))],
        out_shape=[jax.ShapeDtypeStruct((m, d), F32), jax.ShapeDtypeStruct((m * d // (2 * LANES), LANES), jnp.uint32),
                   jax.ShapeDtypeStruct((m, TOP_K), jnp.int32), jax.ShapeDtypeStruct((m, TOP_K), F32),
                   jax.ShapeDtypeStruct((m, TOP_K), jnp.int32), jax.ShapeDtypeStruct((1, n_e), jnp.int32)],
        scratch_shapes=[pltpu.VMEM((1, n_e), F32)],
        compiler_params=_cparams("arbitrary"),
        name="merge_route",
    )(h, attn, pool, proj, proj, wa, wb, wo, g2, rwh, rwl, rb)


def _route_tile(x, wh_ref, wl_ref, b_ref, e_ref, gate_ref, rank_ref, cnt_ref, carry):
    @pl.when(pl.program_id(0) == 0)
    def _():
        carry[...] = jnp.zeros_like(carry)

    xh = x.astype(BF16)
    xl = (x - xh.astype(F32)).astype(BF16)
    logits = (jnp.dot(xh, wh_ref[...], preferred_element_type=F32)
              + jnp.dot(xh, wl_ref[...], preferred_element_type=F32)
              + jnp.dot(xl, wh_ref[...], preferred_element_type=F32)) + b_ref[...]
    n_e = logits.shape[-1]
    lane = lax.broadcasted_iota(jnp.int32, logits.shape, 1)
    vals = logits
    tops, idxs = [], []
    for _ in range(TOP_K):
        m = jnp.max(vals, axis=-1, keepdims=True)
        idx = jnp.min(jnp.where(vals == m, lane, n_e), axis=-1, keepdims=True)
        tops.append(m)
        idxs.append(idx)
        vals = jnp.where(lane == idx, -jnp.inf, vals)
    ex = [jnp.exp(t - tops[0]) for t in tops]
    den = ex[0] + ex[1] + ex[2] + ex[3]
    for k in range(TOP_K):
        e_ref[:, k:k + 1] = idxs[k]
        gate_ref[:, k:k + 1] = ex[k] / den

    onehots = [(lane == idx).astype(F32) for idx in idxs]
    cnt = onehots[0] + onehots[1] + onehots[2] + onehots[3]
    tm = cnt.shape[0]
    lower = (lax.broadcasted_iota(jnp.int32, (tm, tm), 0) > lax.broadcasted_iota(jnp.int32, (tm, tm), 1))
    prefix = jnp.dot(lower.astype(BF16), cnt.astype(BF16), preferred_element_type=F32)
    base = carry[...] + prefix
    for k in range(TOP_K):
        rank_ref[:, k:k + 1] = jnp.sum(onehots[k] * base, axis=-1, keepdims=True).astype(jnp.int32)
    carry[...] = carry[...] + jnp.sum(cnt, axis=0, keepdims=True)
    cnt_ref[...] = carry[...].astype(jnp.int32)


def _row_gather_kernel(nused_ref, tok_ref, src_hbm, o_ref, buf, sem, *, rows):
    t = pl.program_id(0)
    n_used = nused_ref[0]
    groups = o_ref.shape[1] // (2 * LANES)

    def issue(tile, slot):
        def body(j, carry):
            for u in range(DMA_ISSUE_UNROLL):
                i = j * DMA_ISSUE_UNROLL + u
                src = pl.ds(pl.multiple_of(tok_ref[tile * rows + i], groups), groups)
                dst = pl.ds(pl.multiple_of(i * groups, groups), groups)
                pltpu.make_async_copy(src_hbm.at[src], buf.at[slot, dst], sem.at[slot]).start(priority=u % 2)
            return carry
        lax.fori_loop(0, rows // DMA_ISSUE_UNROLL, body, 0)

    @pl.when(jnp.logical_and(t == 0, n_used > 0))
    def _():
        issue(0, 0)

    @pl.when(t + 1 < n_used)
    def _():
        issue(t + 1, (t + 1) % 2)

    @pl.when(t < n_used)
    def _():
        slot = t % 2
        pltpu.make_async_copy(src_hbm.at[pl.ds(0, rows * groups)], buf.at[slot], sem.at[slot]).wait()
        for j in range(groups):
            w = buf[slot, pl.ds(j, rows, stride=groups), :]
            lo = lax.bitcast_convert_type(w << 16, F32)
            hi = lax.bitcast_convert_type(w & jnp.uint32(0xFFFF0000), F32)
            o_ref[:, j * LANES:(j + 1) * LANES] = lo.astype(o_ref.dtype)
            o_ref[:, (j + groups) * LANES:(j + groups + 1) * LANES] = hi.astype(o_ref.dtype)

    @pl.when(t >= n_used)
    def _():
        o_ref[...] = jnp.zeros_like(o_ref)


def _row_gather(n_used, row_idx, src, d, n_tiles, rows, out_dtype):
    groups = d // (2 * LANES)
    grid_spec = pltpu.PrefetchScalarGridSpec(
        num_scalar_prefetch=2,
        grid=(n_tiles,),
        in_specs=[pl.BlockSpec(memory_space=pl.ANY)],
        out_specs=pl.BlockSpec((rows, d), lambda t, *_: (t, 0)),
        scratch_shapes=[pltpu.VMEM((2, rows * groups, LANES), src.dtype), pltpu.SemaphoreType.DMA((2,))],
    )
    return pl.pallas_call(
        functools.partial(_row_gather_kernel, rows=rows),
        grid_spec=grid_spec,
        out_shape=jax.ShapeDtypeStruct((n_tiles * rows, d), out_dtype),
        compiler_params=_cparams("arbitrary"),
        name="moe_row_gather",
    )(n_used, row_idx, src)


def _start_all(copies):
    for c in copies:
        c.start()


def _wait_all(copies):
    for c in copies:
        c.wait()


def _start_weight_pairs(piece, step, slot, q_lo, q_hi):
    def body(q, carry):
        piece(step, slot, 2 * q).start(priority=0)
        piece(step, slot, 2 * q + 1).start(priority=1)
        return carry
    lax.fori_loop(q_lo, q_hi, body, 0)


def _wait_step_weights(step, piece, n_pieces):
    slot = step % 2

    @pl.when(step == 0)
    def _():
        _start_weight_pairs(piece, step, slot, 0, n_pieces // 2)

    for p in range(n_pieces):
        piece(step, slot, p).wait()
    return slot


def _expert_chunk_loop(step, n_steps, n_chunks, next_n_chunks, in_copies, next_in_copies, out_copies, compute,
                       weight_piece, n_weight_pieces):
    def start_first_two(count, copies):
        @pl.when(count > 0)
        def _():
            _start_all(copies(0, 0))

        @pl.when(count > 1)
        def _():
            _start_all(copies(1, 1))

    n_pairs = n_weight_pieces // 2

    def next_weights(q_lo, q_hi):
        @pl.when(step + 1 < n_steps)
        def _():
            _start_weight_pairs(weight_piece, step + 1, (step + 1) % 2, q_lo, q_hi)

    @pl.when(step == 0)
    def _():
        start_first_two(n_chunks, in_copies)

    @pl.when(n_chunks > 0)
    def _():
        def chunk(r, first):
            in_slot = r % CHUNK_IN_SLOTS
            out_slot = r % 2

            @pl.when(r + 2 < n_chunks)
            def _():
                _start_all(in_copies(r + 2, (r + 2) % CHUNK_IN_SLOTS))

            next_weights(jnp.minimum(r * WEIGHT_PAIRS_PER_CHUNK, n_pairs),
                         jnp.minimum((r + 1) * WEIGHT_PAIRS_PER_CHUNK, n_pairs))
            _wait_all(in_copies(r, in_slot))

            if not first:
                @pl.when(r >= 2)
                def _():
                    _wait_all(out_copies(r - 2, out_slot))

            compute(in_slot, out_slot, first)
            _start_all(out_copies(r, out_slot))

        chunk(0, True)

        def body(r, carry):
            chunk(r, False)
            return carry

        lax.fori_loop(1, n_chunks, body, 0)

    @pl.when(step + 1 < n_steps)
    def _():
        start_first_two(next_n_chunks, next_in_copies)

    next_weights(jnp.minimum(n_chunks * WEIGHT_PAIRS_PER_CHUNK, n_pairs), n_pairs)

    @pl.when(n_chunks >= 2)
    def _():
        _wait_all(out_copies(n_chunks - 2, n_chunks % 2))

    @pl.when(n_chunks >= 1)
    def _():
        _wait_all(out_copies(n_chunks - 1, (n_chunks - 1) % 2))


def _zero_fill_tiles(first, last, zbuf, dst, sem):
    zbuf[...] = jnp.zeros_like(zbuf)

    def copy(t):
        return pltpu.make_async_copy(zbuf, dst(t), sem)

    def start(t, carry):
        copy(t).start()
        return carry

    def wait(t, carry):
        copy(t).wait()
        return carry

    lax.fori_loop(first, last, start, 0)
    lax.fori_loop(first, last, wait, 0)


def _moe_up_kernel(tpe_ref, start_ref, xs_hbm, w_hbm, bg_ref, bu_ref, hid_hbm,
                   wbuf, wg_sc, wu_sc, xbuf, obuf, w_sem, in_sem, out_sem, *, rows):
    e = pl.program_id(0)
    n = pl.program_id(1)
    n_chunks = pl.num_programs(1)
    row0 = start_ref[e]
    d, nc = wg_sc.shape

    piece_rows = d // DMA_SPLIT
    n_pieces = 2 * DMA_SPLIT

    def w_piece(step, slot, p):
        e_, n_ = step // n_chunks, step % n_chunks
        mat, blk = p // DMA_SPLIT, p % DMA_SPLIT
        cols = pl.ds(pl.multiple_of((mat * n_chunks + n_) * nc, nc), nc)
        rws = pl.ds(pl.multiple_of(blk * piece_rows, piece_rows), piece_rows)
        return pltpu.make_async_copy(w_hbm.at[e_, rws, cols], wbuf.at[slot, mat, rws], w_sem.at[slot])

    step, n_steps = e * n_chunks + n, pl.num_programs(0) * n_chunks
    w_slot = _wait_step_weights(step, w_piece, n_pieces)
    next_e = jnp.minimum((step + 1) // n_chunks, pl.num_programs(0) - 1)

    def row_pieces(first_row, r, pieces):
        per = rows // pieces
        return [(pl.ds(pl.multiple_of(first_row + r * rows + p * per, per), per), pl.ds(p * per, per))
                for p in range(pieces)]

    def in_copies_from(first_row):
        def in_copies(r, slot):
            return [pltpu.make_async_copy(xs_hbm.at[src], xbuf.at[slot, dst], in_sem.at[slot])
                    for src, dst in row_pieces(first_row, r, CHUNK_DMA_SPLIT)]
        return in_copies

    def out_copies(r, slot):
        return [pltpu.make_async_copy(obuf.at[slot, src], hid_hbm.at[n, dst], out_sem.at[slot])
                for dst, src in row_pieces(row0, r, 1)]

    def compute(in_slot, out_slot, first):
        x = xbuf[in_slot]
        for c0 in range(0, nc, MOE_COL_BLOCK):
            cs = slice(c0, c0 + MOE_COL_BLOCK)
            if first:
                wg_sc[:, cs] = wbuf[w_slot, 0, :, cs].astype(BF16)
                wu_sc[:, cs] = wbuf[w_slot, 1, :, cs].astype(BF16)
            g = jnp.dot(x, wg_sc[:, cs], preferred_element_type=F32) + bg_ref[0, :, cs]
            u = jnp.dot(x, wu_sc[:, cs], preferred_element_type=F32) + bu_ref[0, :, cs]
            g = jnp.minimum(g, SWIGLU_LIMIT)
            u = jnp.clip(u, -SWIGLU_LIMIT, SWIGLU_LIMIT)
            obuf[out_slot, :, cs] = ((u + 1.0) * (g * jax.nn.sigmoid(SWIGLU_ALPHA * g))).astype(obuf.dtype)

    _expert_chunk_loop(step, n_steps, tpe_ref[e], tpe_ref[next_e], in_copies_from(row0),
                       in_copies_from(start_ref[next_e]), out_copies, compute, w_piece, n_pieces)

    @pl.when(e == pl.num_programs(0) - 1)
    def _():
        _zero_fill_tiles(row0 // rows + tpe_ref[e], hid_hbm.shape[1] // rows, obuf.at[0],
                         lambda t: hid_hbm.at[n, pl.ds(pl.multiple_of(t * rows, rows), rows)], out_sem.at[0])


def _moe_up(tiles_per_e, row_start, xs, w_gu, b_gu, rows, nc):
    r_pad, d = xs.shape
    n_e = w_gu.shape[0]
    d_ff = w_gu.shape[2] // 2
    n_chunks = d_ff // nc
    any_spec = pl.BlockSpec(memory_space=pl.ANY)
    grid_spec = pltpu.PrefetchScalarGridSpec(
        num_scalar_prefetch=2,
        grid=(n_e, n_chunks),
        in_specs=[
            any_spec,
            any_spec,
            pl.BlockSpec((1, 1, nc), lambda e, n, *_: (e, 0, n)),
            pl.BlockSpec((1, 1, nc), lambda e, n, *_: (e, 0, n_chunks + n)),
        ],
        out_specs=any_spec,
        scratch_shapes=[pltpu.VMEM((2, 2, d, nc), w_gu.dtype), pltpu.VMEM((d, nc), BF16), pltpu.VMEM((d, nc), BF16),
                        pltpu.VMEM((CHUNK_IN_SLOTS, rows, d), xs.dtype), pltpu.VMEM((2, rows, nc), BF16),
                        pltpu.SemaphoreType.DMA((2,)), pltpu.SemaphoreType.DMA((CHUNK_IN_SLOTS,)),
                        pltpu.SemaphoreType.DMA((2,))],
    )
    return pl.pallas_call(
        functools.partial(_moe_up_kernel, rows=rows),
        grid_spec=grid_spec,
        out_shape=jax.ShapeDtypeStruct((n_chunks, r_pad, nc), BF16),
        compiler_params=_cparams("arbitrary", "arbitrary"),
        name="moe_up",
    )(tiles_per_e, row_start, xs, w_gu, b_gu, b_gu)


def _moe_down_kernel(tpe_ref, start_ref, hid_hbm, w_hbm, b_ref, y_hbm,
                     wbuf, w_sc, xbuf, obuf, w_sem, in_sem, out_sem, *, rows):
    e = pl.program_id(0)
    n = pl.program_id(1)
    n_chunks = pl.num_programs(1)
    row0 = start_ref[e]
    k_chunks, _, kc = xbuf.shape[1:]
    d_ff, nc = w_sc.shape

    piece_rows = d_ff // DMA_SPLIT

    def w_piece(step, slot, p):
        e_, n_ = step // n_chunks, step % n_chunks
        cols = pl.ds(pl.multiple_of(n_ * nc, nc), nc)
        rws = pl.ds(pl.multiple_of(p * piece_rows, piece_rows), piece_rows)
        return pltpu.make_async_copy(w_hbm.at[e_, rws, cols], wbuf.at[slot, rws], w_sem.at[slot])

    step, n_steps = e * n_chunks + n, pl.num_programs(0) * n_chunks
    w_slot = _wait_step_weights(step, w_piece, DMA_SPLIT)
    next_e = jnp.minimum((step + 1) // n_chunks, pl.num_programs(0) - 1)

    def in_copies_from(first_row):
        def in_copies(r, slot):
            chunk = pl.ds(pl.multiple_of(first_row + r * rows, rows), rows)
            return [pltpu.make_async_copy(hid_hbm.at[c, chunk], xbuf.at[slot, c], in_sem.at[slot])
                    for c in range(k_chunks)]
        return in_copies

    lanes_per_row = nc // LANES
    per = rows * lanes_per_row // CHUNK_DMA_SPLIT

    def out_copies(r, slot):
        base = (row0 + r * rows) * lanes_per_row
        return [pltpu.make_async_copy(
            obuf.at[slot, pl.ds(p * per, per)],
            y_hbm.at[pl.ds(pl.multiple_of(base + p * per, per), per)], out_sem.at[slot])
            for p in range(CHUNK_DMA_SPLIT)]

    def compute(in_slot, out_slot, first):
        for c0 in range(0, nc, MOE_COL_BLOCK):
            cs = slice(c0, c0 + MOE_COL_BLOCK)
            if first:
                w_sc[:, cs] = wbuf[w_slot, :, cs].astype(BF16)
            acc = b_ref[0, :, cs] + jnp.dot(xbuf[in_slot, 0], w_sc[0:kc, cs], preferred_element_type=F32)
            for c in range(1, k_chunks):
                acc = acc + jnp.dot(xbuf[in_slot, c], w_sc[c * kc:(c + 1) * kc, cs], preferred_element_type=F32)
            for j in range(MOE_COL_BLOCK // LANES):
                lane_group = c0 // LANES + j
                obuf[out_slot, pl.ds(lane_group, rows, stride=lanes_per_row), :] = acc[:, j * LANES:(j + 1) * LANES]

    _expert_chunk_loop(step, n_steps, tpe_ref[e], tpe_ref[next_e], in_copies_from(row0),
                       in_copies_from(start_ref[next_e]), out_copies, compute, w_piece, DMA_SPLIT)

    @pl.when(e == pl.num_programs(0) - 1)
    def _():
        tile_rows = rows * lanes_per_row
        _zero_fill_tiles(row0 // rows + tpe_ref[e], y_hbm.shape[0] // tile_rows, obuf.at[0],
                         lambda t: y_hbm.at[pl.ds(pl.multiple_of(t * tile_rows, tile_rows), tile_rows)], out_sem.at[0])


def _moe_down(tiles_per_e, row_start, hid, w_dn, b_dn, rows):
    k_chunks, r_pad, kc = hid.shape
    n_e, d_ff, d = w_dn.shape
    nc = d
    n_chunks = 1
    any_spec = pl.BlockSpec(memory_space=pl.ANY)
    grid_spec = pltpu.PrefetchScalarGridSpec(
        num_scalar_prefetch=2,
        grid=(n_e, n_chunks),
        in_specs=[
            any_spec,
            any_spec,
            pl.BlockSpec((1, 1, nc), lambda e, n, *_: (e, 0, n)),
        ],
        out_specs=any_spec,
        scratch_shapes=[pltpu.VMEM((2, d_ff, nc), w_dn.dtype), pltpu.VMEM((d_ff, nc), BF16),
                        pltpu.VMEM((CHUNK_IN_SLOTS, k_chunks, rows, kc), hid.dtype),
                        pltpu.VMEM((2, rows * nc // LANES, LANES), F32),
                        pltpu.SemaphoreType.DMA((2,)), pltpu.SemaphoreType.DMA((CHUNK_IN_SLOTS,)),
                        pltpu.SemaphoreType.DMA((2,))],
    )
    return pl.pallas_call(
        functools.partial(_moe_down_kernel, rows=rows),
        grid_spec=grid_spec,
        out_shape=jax.ShapeDtypeStruct((r_pad * d // LANES, LANES), F32),
        compiler_params=_cparams("arbitrary", "arbitrary"),
        name="moe_down",
    )(tiles_per_e, row_start, hid, w_dn, b_dn)


def _combine_kernel(dest_ref, h_ref, gate_ref, gf_ref, y_hbm, *rest, rows, final_norm, n_head_tiles, split):
    if split:
        o_ref, o_tail_ref, buf, sem = rest
    else:
        (o_ref, buf, sem), o_tail_ref = rest, None
    t = pl.program_id(0)
    n_t = pl.num_programs(0)
    n_dma = rows * TOP_K
    d = o_ref.shape[1]
    groups = d // LANES

    def issue(tile, slot):
        def body(j, carry):
            for u in range(DMA_ISSUE_UNROLL):
                i = j * DMA_ISSUE_UNROLL + u
                src = pl.ds(pl.multiple_of(dest_ref[tile * n_dma + i], groups), groups)
                dst = pl.ds(pl.multiple_of(i * groups, groups), groups)
                pltpu.make_async_copy(y_hbm.at[src], buf.at[slot, dst], sem.at[slot]).start(priority=u % 2)
            return carry
        lax.fori_loop(0, n_dma // DMA_ISSUE_UNROLL, body, 0)

    @pl.when(t == 0)
    def _():
        issue(0, 0)

    @pl.when(t + 1 < n_t)
    def _():
        issue(t + 1, (t + 1) % 2)

    slot = t % 2
    pltpu.make_async_copy(y_hbm.at[pl.ds(0, n_dma * groups)], buf.at[slot], sem.at[slot]).wait()

    def finish(o_ref):
        gates = [gate_ref[:, k:k + 1] for k in range(TOP_K)]
        sq = jnp.zeros((rows, 1), F32)
        for j in range(groups):
            cs = slice(j * LANES, (j + 1) * LANES)
            acc = h_ref[:, cs]
            for k in range(TOP_K):
                acc = acc + gates[k] * buf[slot, pl.ds(k * rows * groups + j, rows, stride=groups), :]
            o_ref[:, cs] = acc
            sq = sq + jnp.sum(acc * acc, axis=-1, keepdims=True)
        if final_norm:
            o_ref[...] = o_ref[...] * lax.rsqrt(sq / d + NORM_EPS) * gf_ref[...]

    if o_tail_ref is None:
        finish(o_ref)
    else:
        pl.when(t < n_head_tiles)(lambda: finish(o_ref))
        pl.when(t >= n_head_tiles)(lambda: finish(o_tail_ref))


def _combine(dest_km, h1, gate, gf, y, rows, final_norm, split_at=None):
    m, d = h1.shape
    n_head = m // rows if split_at is None else split_at // rows
    out_specs = [pl.BlockSpec((rows, d), lambda t, *_: (jnp.minimum(t, n_head - 1), 0))]
    out_shape = [jax.ShapeDtypeStruct((n_head * rows, d), F32)]
    if split_at is not None:
        assert split_at % rows == 0 and (m - split_at) % rows == 0
        out_specs.append(pl.BlockSpec((rows, d), lambda t, *_: (jnp.maximum(t - n_head, 0), 0)))
        out_shape.append(jax.ShapeDtypeStruct((m - split_at, d), F32))
    grid_spec = pltpu.PrefetchScalarGridSpec(
        num_scalar_prefetch=1,
        grid=(m // rows,),
        in_specs=[
            pl.BlockSpec((rows, d), lambda t, *_: (t, 0)),
            pl.BlockSpec((rows, TOP_K), lambda t, *_: (t, 0)),
            pl.BlockSpec((1, d), lambda t, *_: (0, 0)),
            pl.BlockSpec(memory_space=pl.ANY),
        ],
        out_specs=out_specs,
        scratch_shapes=[pltpu.VMEM((2, rows * TOP_K * d // LANES, LANES), F32), pltpu.SemaphoreType.DMA((2,))],
    )
    return pl.pallas_call(
        functools.partial(_combine_kernel, rows=rows, final_norm=final_norm, n_head_tiles=n_head,
                          split=split_at is not None),
        grid_spec=grid_spec,
        out_shape=out_shape,
        compiler_params=_cparams("arbitrary"),
        name="moe_combine",
    )(dest_km, h1, gate, gf, y)


def _routing_tables(top_e, rank, counts, rows, n_tiles):
    m = top_e.shape[0]
    n_experts = counts.shape[0]
    tiles_per_e = (counts + rows - 1) // rows
    tile_end = jnp.cumsum(tiles_per_e)
    pad_start = (tile_end - tiles_per_e) * rows
    experts = jnp.arange(n_experts, dtype=jnp.int32)
    dest = rank + jnp.sum(jnp.where(top_e[..., None] == experts, pad_start, 0), axis=-1)
    dest = dest.astype(jnp.int32).reshape(-1)
    n_used = tile_end[-1:].astype(jnp.int32)
    tok = jnp.arange(m * TOP_K, dtype=jnp.int32) // TOP_K
    row_tok = jnp.zeros((n_tiles * rows,), jnp.int32).at[dest].set(tok)
    return dest, row_tok, tiles_per_e.astype(jnp.int32), pad_start.astype(jnp.int32), n_used


def _largest_divisor(n, cap, mult):
    best = mult
    for c in range(mult, cap + 1, mult):
        if n % c == 0:
            best = c
    return best


def kernel(x_prompt, x_sample, cache_kv_w128, cache_kv_w512, cache_kv_w2048, state_pool, rel_bias, norm1, w_in,
           w_branch_a, w_branch_b, w_out, w_pool, pool_scale, norm2, router_w, router_b, w_gate_up, b_gate_up,
           w_down, b_down, norm_f):
    batch, seq, d = x_prompt.shape
    db, t_new, _ = x_sample.shape
    depth = norm1.shape[0]
    n_experts = router_w.shape[-1]
    caches = (cache_kv_w128, cache_kv_w512, cache_kv_w2048)
    m_p, m_s = batch * seq, db * t_new
    m = m_p + m_s
    tm = _largest_divisor(m, 640, 8)

    h = jnp.concatenate([x_prompt.reshape(m_p, d), x_sample.reshape(m_s, d)], axis=0)
    bias_p = _prompt_bias_tiles(rel_bias)
    bias_sc, bias_sn = _sample_bias_tiles(rel_bias, t_new)

    cuts = [ATTN_WIDTH, 2 * ATTN_WIDTH, 3 * ATTN_WIDTH, 3 * ATTN_WIDTH + POOL_WIDTH, 3 * ATTN_WIDTH + POOL_WIDTH + d]
    qkv0 = 2 * d
    z0 = qkv0 + 3 * ATTN_WIDTH

    kv_p = [[] for _ in range(N_GROUPS)]
    kv_s = [[] for _ in range(N_GROUPS)]
    pool_p, pool_s = [], []
    for l in range(depth):
        wq, wk, wv, wz, wga, wgb = jnp.split(w_in[l], cuts, axis=-1)
        w_perm = jnp.concatenate([wga, wgb, wq, wk, wv, wz], axis=-1).astype(BF16)
        proj = _inproj(h, norm1[l][None, :], w_perm, tm, _largest_divisor(w_perm.shape[1], 1536, 128))

        k0, v0 = qkv0 + ATTN_WIDTH, qkv0 + 2 * ATTN_WIDTH
        kv_w = HEADS_PER_GROUP * HEAD_DIM
        proj_s = proj[m_p:]
        z_s = proj_s[:, z0:z0 + POOL_WIDTH].reshape(db, t_new, POOL_WIDTH)

        attn_p = _prompt_attn(proj, bias_p, batch, seq, qkv0 // HEAD_DIM)
        qkv_s = proj_s[:, qkv0:qkv0 + 3 * ATTN_WIDTH].reshape(db, t_new, 3 * ATTN_WIDTH)
        c0 = caches[0][l].reshape(db, -1, HEAD_DIM)
        c1 = caches[1][l].reshape(db, -1, HEAD_DIM)
        dil2 = ATTN_GROUPS[2][1]
        c2v = caches[2][l].reshape(db, -1, dil2 * CACHE_ROW_TILE, HEAD_DIM)

        news = []
        for g, (window, _) in enumerate(ATTN_GROUPS):
            keep = min(window, seq)
            kc, vc = k0 + g * kv_w, v0 + g * kv_w

            def kv_rows(a, r0, r1, kc=kc, vc=vc):
                return jnp.stack([a[r0:r1, kc:kc + kv_w].reshape(-1, HEADS_PER_GROUP, HEAD_DIM),
                                  a[r0:r1, vc:vc + kv_w].reshape(-1, HEADS_PER_GROUP, HEAD_DIM)], axis=1)

            kv_p[g].append(jnp.stack([kv_rows(proj, (b + 1) * seq - keep, (b + 1) * seq) for b in range(batch)]))
            news.append(kv_rows(proj_s, 0, m_s).reshape(db, t_new * CACHE_ROW_TILE, HEAD_DIM))
        news[2] = news[2][:, None]
        attn_s, *rolled = _sample_attn(qkv_s, c0, c1, c2v, news, bias_sc, bias_sn)
        attn = jnp.concatenate([attn_p, attn_s.reshape(m_s, ATTN_OUT_WIDTH)], axis=0)
        for g in range(N_GROUPS):
            kv_s[g].append(rolled[g].reshape(caches[g][l].shape))

        w_pool_b = w_pool[l].astype(BF16)
        scale = pool_scale[l][None, :]
        hist_p = jnp.zeros((batch, POOL_HIST, POOL_WIDTH), F32)
        hist_s = jnp.concatenate([jnp.zeros((db, POOL_HIST - POOL_BUF, POOL_WIDTH), F32), state_pool[l]], axis=1)
        pool = jnp.concatenate([
            _pool_mix(hist_p, proj, pl.BlockSpec((seq, POOL_WIDTH), lambda i: (i, z0 // POOL_WIDTH)),
                      batch, seq, w_pool_b, scale, 0).reshape(m_p, POOL_WIDTH),
            _pool_mix(hist_s, z_s, pl.BlockSpec((1, t_new, POOL_WIDTH), lambda i: (i, 0, 0)),
                      db, t_new, w_pool_b, scale, PAST_LEN).reshape(m_s, POOL_WIDTH),
        ], axis=0)
        pool_p.append(jnp.stack([proj[(b + 1) * seq - POOL_BUF:(b + 1) * seq, z0:z0 + POOL_WIDTH]
                                 for b in range(batch)]))
        pool_s.append(jnp.concatenate([state_pool[l], z_s], axis=1)[:, t_new:])

        rw = router_w[l]
        rw_hi = rw.astype(BF16)
        rw_lo = (rw - rw_hi.astype(F32)).astype(BF16)
        h1, xn_rows, top_e, gate, rank, counts = _merge(
            h, attn, pool, proj, w_branch_a[l].astype(BF16), w_branch_b[l].astype(BF16), w_out[l].astype(BF16),
            norm2[l][None, :], rw_hi, rw_lo, router_b[l][None, :], _largest_divisor(m, 320, 8))

        rows = MOE_ROW_TILE
        groups = d // LANES
        n_tiles = (m * TOP_K + n_experts * (rows - 1)) // rows
        dest, row_tok, tiles_per_e, row_start, n_used = _routing_tables(top_e, rank, counts[0], rows, n_tiles)
        xs = _row_gather(n_used, row_tok * (groups // 2), xn_rows, d, n_tiles, rows, BF16)
        hid = _moe_up(tiles_per_e, row_start, xs, w_gate_up[l], b_gate_up[l][:, None, :], rows, 1024)
        y = _moe_down(tiles_per_e, row_start, hid, w_down[l], b_down[l][:, None, :], rows)
        crow = _largest_divisor(math.gcd(m_p, m_s), 128, 8)
        dest_km = (dest * groups).reshape(m // crow, crow, TOP_K).transpose(0, 2, 1).reshape(-1)
        if l == depth - 1:
            h_p, h_s = _combine(dest_km, h1, gate, norm_f[None, :], y, crow, final_norm=True, split_at=m_p)
        else:
            h, = _combine(dest_km, h1, gate, norm_f[None, :], y, crow, final_norm=False)

    y_prompt = h_p.reshape(batch, seq, d)
    y_sample = h_s.reshape(db, t_new, d)
    return (y_prompt, y_sample,
            jnp.stack(kv_p[0]), jnp.stack(kv_p[1]), jnp.stack(kv_p[2]), jnp.stack(pool_p),
            jnp.stack(kv_s[0]), jnp.stack(kv_s[1]), jnp.stack(kv_s[2]), jnp.stack(pool_s))
```

```python
import functools
import math

import jax
import jax.numpy as jnp
from jax import lax
from jax.experimental import pallas as pl
from jax.experimental.pallas import tpu as pltpu

F32 = jnp.float32
BF16 = jnp.bfloat16

HEAD_DIM = 128
HEADS_PER_GROUP = 4
ATTN_GROUPS = ((128, 1), (512, 4), (2048, 16))
N_GROUPS = len(ATTN_GROUPS)
N_ATTN_HEADS = N_GROUPS * HEADS_PER_GROUP
ATTN_WIDTH = N_ATTN_HEADS * HEAD_DIM
ATTN_OUT_WIDTH = HEADS_PER_GROUP * HEAD_DIM
ATTN_SCALE = HEAD_DIM ** -0.5
N_BACK = 128
N_REL_BUCKETS = 32
REL_MAX_DIST = 2048
POOL_WINDOWS = (2, 4, 8, 16)
POOL_GROUPS = len(POOL_WINDOWS)
PGW = 128
POOL_WIDTH = POOL_GROUPS * PGW
POOL_BUF = max(POOL_WINDOWS) - 1
POOL_HIST = 16
TOP_K = 4
SWIGLU_LIMIT = 7.0
SWIGLU_ALPHA = 1.702
NORM_EPS = 1e-5
NEG_INF = -1e30
PAST_LEN = 8192
CACHE_ROW_TILE = 2 * HEADS_PER_GROUP

VMEM_LIMIT_BYTES = 56 * 1024 * 1024
MOE_ROW_TILE = 256
DMA_ISSUE_UNROLL = 8
DMA_SPLIT = 8
ATTN_INTERLEAVE = 8
LANES = 128
CHUNK_IN_SLOTS = 3
MOE_COL_BLOCK = 512
WEIGHT_PAIRS_PER_CHUNK = 2


def _cparams(*sem):
    return pltpu.CompilerParams(dimension_semantics=sem, vmem_limit_bytes=VMEM_LIMIT_BYTES)


def _rmsnorm_f32(x, g):
    return x * lax.rsqrt(jnp.mean(x * x, axis=-1, keepdims=True) + NORM_EPS) * g


def _inproj_kernel(x_ref, g_ref, w_ref, o_ref):
    xn = _rmsnorm_f32(x_ref[...], g_ref[...]).astype(BF16)
    o_ref[...] = jnp.dot(xn, w_ref[...], preferred_element_type=F32)


def _inproj(x, g, w, tm, tn):
    m, d = x.shape
    n = w.shape[1]
    return pl.pallas_call(
        _inproj_kernel,
        grid=(n // tn, m // tm),
        in_specs=[
            pl.BlockSpec((tm, d), lambda j, i: (i, 0)),
            pl.BlockSpec((1, d), lambda j, i: (0, 0)),
            pl.BlockSpec((d, tn), lambda j, i: (0, j)),
        ],
        out_specs=pl.BlockSpec((tm, tn), lambda j, i: (i, j)),
        out_shape=jax.ShapeDtypeStruct((m, n), F32),
        compiler_params=_cparams("arbitrary", "arbitrary"),
        name="inproj",
    )(x, g, w)


def _t5_bucket(dist):
    max_exact = N_REL_BUCKETS // 2
    df = jnp.maximum(dist, 1).astype(F32)
    large = max_exact + (jnp.log(df / max_exact) / math.log(REL_MAX_DIST / max_exact)
                         * (N_REL_BUCKETS - max_exact)).astype(jnp.int32)
    large = jnp.minimum(large, N_REL_BUCKETS - 1)
    return jnp.where(dist < max_exact, dist, large)


def _step_bias(rel_bias, g):
    _, dil = ATTN_GROUPS[g]
    buckets = _t5_bucket(jnp.arange(N_BACK + 1, dtype=jnp.int32) * dil)
    tab = rel_bias[buckets].astype(F32)
    return tab[:, g * HEADS_PER_GROUP:(g + 1) * HEADS_PER_GROUP].T


def _prompt_bias_tiles(rel_bias):
    nb = N_BACK
    period = 3 * nb - 1
    tiles = []
    for g in range(N_GROUPS):
        bj = _step_bias(rel_bias, g)
        h = bj.shape[0]
        pad = jnp.full((h, nb - 1), NEG_INF, F32)
        v = jnp.concatenate([pad, bj[:, ::-1], pad], axis=1)
        skew = jnp.tile(v, (1, nb + 1))[:, :nb * (period + 1)].reshape(h, nb, period + 1)
        tiles.append(skew[:, ::-1, :2 * nb])
    return jnp.stack(tiles, axis=1)


def _sample_bias_tiles(rel_bias, t_new):
    tq = jnp.arange(t_new)[:, None]
    outs = []
    new = []
    for g, (window, dil) in enumerate(ATTN_GROUPS):
        bj = _step_bias(rel_bias, g)
        if g < 2:
            p = jnp.arange(window)[None, :]
            dist = window + tq - p
            ok = (dist % dil == 0) & (dist // dil <= N_BACK)
            outs.append(jnp.where(ok[None], bj[:, jnp.clip(dist // dil, 0, N_BACK)], NEG_INF))
        else:
            col = jnp.arange(t_new * N_BACK)[None, :]
            blk, r = col // N_BACK, col % N_BACK
            ok = blk == tq
            outs.append(jnp.where(ok[None], bj[:, jnp.broadcast_to(N_BACK - r, (t_new, t_new * N_BACK))], NEG_INF))
        tk = jnp.arange(t_new)[None, :]
        d = tq - tk
        okn = (d >= 0) & (d % dil == 0) & (d // dil <= N_BACK)
        new.append(jnp.where(okn[None], bj[:, jnp.clip(d // dil, 0, N_BACK)], NEG_INF))
    return outs, jnp.stack(new, axis=0)


def _softmax_block(s, v):
    m = jnp.max(s, axis=-1, keepdims=True)
    p = jnp.exp(s - m)
    den = jnp.sum(p, axis=-1, keepdims=True)
    o = jnp.dot(p.astype(BF16), v, preferred_element_type=F32) / den
    return o, m + jnp.log(den)


def _prompt_attn_kernel(q0, k0, v0, q1, k1, v1, q2, k2, v2, bias_ref, o_ref, o_sc, l_sc, *, seq):
    qkv = ((q0, k0, v0), (q1, k1, v1), (q2, k2, v2))
    nb = N_BACK

    def rows(ref, start, size, dil):
        if dil == 1:
            return ref[pl.ds(start, size), :]
        return ref[pl.ds(start, size, stride=dil), :]

    def put(g, start, dil, o, lse):
        idx = pl.ds(start, nb) if dil == 1 else pl.ds(start, nb, stride=dil)
        o_sc[g, idx, :] = o
        l_sc[g, idx, :] = jnp.broadcast_to(lse, (nb, HEAD_DIM))

    dn = (((1,), (1,)), ((), ()))

    def attend(g, dil, q_starts, kv_starts, n_kv):
        q_ref, k_ref, v_ref = qkv[g]
        bias = bias_ref[0, g] if n_kv == 2 * nb else bias_ref[0, g, :, nb:]
        qs = [rows(q_ref, st, nb, dil).astype(BF16) for st in q_starts]
        ks = [rows(k_ref, st, n_kv, dil).astype(BF16) for st in kv_starts]
        ss = [lax.dot_general(q, k, dn, preferred_element_type=F32) * ATTN_SCALE + bias for q, k in zip(qs, ks)]
        ms = [jnp.max(s, axis=-1, keepdims=True) for s in ss]
        ps = [jnp.exp(s - m) for s, m in zip(ss, ms)]
        dens = [jnp.sum(p, axis=-1, keepdims=True) for p in ps]
        vs = [rows(v_ref, st, n_kv, dil).astype(BF16) for st in kv_starts]
        for st, p, v, m, den in zip(q_starts, ps, vs, ms, dens):
            o = jnp.dot(p.astype(BF16), v, preferred_element_type=F32) / den
            put(g, st, dil, o, m + jnp.log(den))

    for g, (_, dil) in enumerate(ATTN_GROUPS):
        n_blk = seq // (dil * nb)
        span = dil * nb
        for r0 in range(0, dil, ATTN_INTERLEAVE):
            starts = list(range(r0, min(r0 + ATTN_INTERLEAVE, dil)))
            attend(g, dil, starts, starts, nb)
        if n_blk > 1:
            per_iter = max(c for c in range(1, max(1, ATTN_INTERLEAVE // dil) + 1) if (n_blk - 1) % c == 0)

            def body(it, carry, g=g, dil=dil, span=span, per_iter=per_iter):
                q_starts, kv_starts = [], []
                for j in range(per_iter):
                    b = 1 + it * per_iter + j
                    for r in range(dil):
                        q_starts.append(b * span + r)
                        kv_starts.append((b - 1) * span + r)
                attend(g, dil, q_starts, kv_starts, 2 * nb)
                return carry

            lax.fori_loop(0, (n_blk - 1) // per_iter, body, 0)

    chunk = 256

    def comb(c, carry):
        sl = pl.ds(pl.multiple_of(c * chunk, chunk), chunk)
        l0, l1, l2 = l_sc[0, sl, :], l_sc[1, sl, :], l_sc[2, sl, :]
        mx = jnp.maximum(jnp.maximum(l0, l1), l2)
        e0, e1, e2 = jnp.exp(l0 - mx), jnp.exp(l1 - mx), jnp.exp(l2 - mx)
        num = e0 * o_sc[0, sl, :] + e1 * o_sc[1, sl, :] + e2 * o_sc[2, sl, :]
        o_ref[sl, :] = (num / (e0 + e1 + e2)).astype(o_ref.dtype)
        return carry

    lax.fori_loop(0, seq // chunk, comb, 0)


def _prompt_attn(p, bias, batch, seq, col0):
    def spec(kind, g):
        base = col0 + kind * N_ATTN_HEADS + g * HEADS_PER_GROUP
        return pl.BlockSpec((seq, HEAD_DIM), lambda b, i, base=base: (b, base + i))

    in_specs = []
    for g in range(N_GROUPS):
        in_specs += [spec(0, g), spec(1, g), spec(2, g)]
    in_specs.append(pl.BlockSpec((1, N_GROUPS, N_BACK, 2 * N_BACK), lambda b, i: (i, 0, 0, 0)))
    return pl.pallas_call(
        functools.partial(_prompt_attn_kernel, seq=seq),
        grid=(batch, HEADS_PER_GROUP),
        in_specs=in_specs,
        out_specs=pl.BlockSpec((seq, HEAD_DIM), lambda b, i: (b, i)),
        out_shape=jax.ShapeDtypeStruct((batch * seq, ATTN_OUT_WIDTH), BF16),
        scratch_shapes=[pltpu.VMEM((N_GROUPS, seq, HEAD_DIM), F32), pltpu.VMEM((N_GROUPS, seq, HEAD_DIM), F32)],
        compiler_params=_cparams("arbitrary", "arbitrary"),
        name="prompt_attn",
    )(*([p] * 9), bias)


def _sample_attn_kernel(qkv_ref, c0_ref, c1_ref, c2_ref, n0_ref, n1_ref, n2_ref, b0_ref, b1_ref, b2_ref, bn_ref,
                        o_ref, r0_ref, r1_ref, r2_ref, *, t_new):
    caches = (c0_ref, c1_ref, c2_ref)
    biases = (b0_ref, b1_ref, b2_ref)
    dn = (((1,), (1,)), ((), ()))

    shift = t_new * CACHE_ROW_TILE
    for c_ref, n_ref, r_ref in ((c0_ref, n0_ref, r0_ref), (c1_ref, n1_ref, r1_ref)):
        keep = c_ref.shape[1] - shift
        r_ref[0, :keep] = c_ref[0, shift:]
        r_ref[0, keep:] = n_ref[0]
    n_r, width = c2_ref.shape[1], c2_ref.shape[2]
    r2_ref[0, :, :width - shift] = c2_ref[0, :, shift:]
    r2_ref[0, :n_r - 1, width - shift:] = c2_ref[0, 1:, :shift]
    r2_ref[0, n_r - 1:, width - shift:] = n2_ref[0]

    for i in range(HEADS_PER_GROUP):
        outs, lses = [], []
        for g in range(N_GROUPS):
            h = g * HEADS_PER_GROUP + i
            q = qkv_ref[0, :, h * HEAD_DIM:(h + 1) * HEAD_DIM].astype(BF16)
            kn = qkv_ref[0, :, ATTN_WIDTH + h * HEAD_DIM:ATTN_WIDTH + (h + 1) * HEAD_DIM].astype(BF16)
            vn = qkv_ref[0, :, 2 * ATTN_WIDTH + h * HEAD_DIM:2 * ATTN_WIDTH + (h + 1) * HEAD_DIM].astype(BF16)
            c_ref = caches[g]
            if g < 2:
                n_rows = c_ref.shape[1] // CACHE_ROW_TILE
                kc = c_ref[0, pl.ds(i, n_rows, stride=CACHE_ROW_TILE), :].astype(BF16)
                vc = c_ref[0, pl.ds(HEADS_PER_GROUP + i, n_rows, stride=CACHE_ROW_TILE), :].astype(BF16)
            else:
                kc = jnp.concatenate(
                    [c_ref[0, :, t * CACHE_ROW_TILE + i, :] for t in range(t_new)], axis=0).astype(BF16)
                vc = jnp.concatenate(
                    [c_ref[0, :, t * CACHE_ROW_TILE + HEADS_PER_GROUP + i, :] for t in range(t_new)],
                    axis=0).astype(BF16)
            sc = lax.dot_general(q, kc, dn, preferred_element_type=F32) * ATTN_SCALE + biases[g][i]
            sn = lax.dot_general(q, kn, dn, preferred_element_type=F32) * ATTN_SCALE + bn_ref[g, i]
            m = jnp.maximum(jnp.max(sc, axis=-1, keepdims=True), jnp.max(sn, axis=-1, keepdims=True))
            pc = jnp.exp(sc - m)
            pn = jnp.exp(sn - m)
            den = jnp.sum(pc, axis=-1, keepdims=True) + jnp.sum(pn, axis=-1, keepdims=True)
            o = (jnp.dot(pc.astype(BF16), vc, preferred_element_type=F32)
                 + jnp.dot(pn.astype(BF16), vn, preferred_element_type=F32)) / den
            outs.append(o)
            lses.append(m + jnp.log(den))
        mx = jnp.maximum(jnp.maximum(lses[0], lses[1]), lses[2])
        es = [jnp.exp(l - mx) for l in lses]
        num = es[0] * outs[0] + es[1] * outs[1] + es[2] * outs[2]
        o_ref[0, :, i * HEAD_DIM:(i + 1) * HEAD_DIM] = (num / (es[0] + es[1] + es[2])).astype(o_ref.dtype)


def _sample_attn(qkv_s, c0, c1, c2v, news, bias_c, bias_n):
    db, t_new, _ = qkv_s.shape

    def full(a):
        return pl.BlockSpec(a.shape, lambda b, nd=a.ndim: (0,) * nd)

    def per_seq(a):
        return pl.BlockSpec((1,) + a.shape[1:], lambda b, nd=a.ndim: (b,) + (0,) * (nd - 1))

    return pl.pallas_call(
        functools.partial(_sample_attn_kernel, t_new=t_new),
        grid=(db,),
        in_specs=[per_seq(qkv_s), per_seq(c0), per_seq(c1), per_seq(c2v), per_seq(news[0]), per_seq(news[1]),
                  per_seq(news[2]), full(bias_c[0]), full(bias_c[1]), full(bias_c[2]), full(bias_n)],
        out_specs=[pl.BlockSpec((1, t_new, ATTN_OUT_WIDTH), lambda b: (b, 0, 0)),
                   per_seq(c0), per_seq(c1), per_seq(c2v)],
        out_shape=[jax.ShapeDtypeStruct((db, t_new, ATTN_OUT_WIDTH), BF16)]
                  + [jax.ShapeDtypeStruct(c.shape, c.dtype) for c in (c0, c1, c2v)],
        compiler_params=_cparams("arbitrary"),
        name="sample_attn",
    )(qkv_s, c0, c1, c2v, *news, bias_c[0], bias_c[1], bias_c[2], bias_n)


def _pool_kernel(hist_ref, z_ref, w_ref, s_ref, o_ref, zbuf, *, t_len, pos0, chunk):
    zbuf[0:POOL_HIST, :] = hist_ref[0]
    zbuf[POOL_HIST:, :] = z_ref[...].reshape(t_len, POOL_WIDTH)
    for c in range(t_len // chunk):
        base = POOL_HIST + c * chunk
        for g, w in enumerate(POOL_WINDOWS):
            sl = slice(g * PGW, (g + 1) * PGW)
            cur = zbuf[base:base + chunk, sl]
            tot = cur
            for i in range(1, w):
                tot = tot + zbuf[base - i:base - i + chunk, sl]
            pos = pos0 + c * chunk + lax.broadcasted_iota(jnp.int32, (chunk, PGW), 0)
            cnt = jnp.minimum(w, pos + 1).astype(F32)
            mixed = tot / cnt - cur
            y = jnp.dot(mixed.astype(BF16), w_ref[g], preferred_element_type=F32) * s_ref[:, sl]
            o_ref[0, c * chunk:(c + 1) * chunk, sl] = y.astype(o_ref.dtype)


def _pool_mix(hist, z, z_spec, n_seq, t_len, w_pool, pool_scale, pos0):
    chunk = min(t_len, 256)
    return pl.pallas_call(
        functools.partial(_pool_kernel, t_len=t_len, pos0=pos0, chunk=chunk),
        grid=(n_seq,),
        in_specs=[
            pl.BlockSpec((1, POOL_HIST, POOL_WIDTH), lambda i: (i, 0, 0)),
            z_spec,
            pl.BlockSpec(w_pool.shape, lambda i: (0, 0, 0)),
            pl.BlockSpec((1, POOL_WIDTH), lambda i: (0, 0)),
        ],
        out_specs=pl.BlockSpec((1, t_len, POOL_WIDTH), lambda i: (i, 0, 0)),
        out_shape=jax.ShapeDtypeStruct((n_seq, t_len, POOL_WIDTH), BF16),
        scratch_shapes=[pltpu.VMEM((POOL_HIST + t_len, POOL_WIDTH), F32)],
        compiler_params=_cparams("arbitrary"),
        name="pool_mix",
    )(hist, z, w_pool, pool_scale)


def _merge_kernel(h_ref, a_ref, p_ref, ga_ref, gb_ref, wa_ref, wb_ref, wo_ref, g2_ref, rwh_ref, rwl_ref, rb_ref,
                  h1_ref, xn_ref, e_ref, gate_ref, rank_ref, cnt_ref, carry):
    ya = jnp.dot(a_ref[...], wa_ref[...], preferred_element_type=F32)
    yb = jnp.dot(p_ref[...], wb_ref[...], preferred_element_type=F32)
    u = jax.nn.sigmoid(ga_ref[...]) * ya + jax.nn.sigmoid(gb_ref[...]) * yb
    h1 = h_ref[...] + jnp.dot(u.astype(BF16), wo_ref[...], preferred_element_type=F32)
    h1_ref[...] = h1
    xn = _rmsnorm_f32(h1, g2_ref[...])
    tm, d = xn.shape
    half = d // (2 * LANES)
    for j in range(half):
        lo = _bf16_bits(xn[:, j * LANES:(j + 1) * LANES])
        hi = _bf16_bits(xn[:, (j + half) * LANES:(j + half + 1) * LANES])
        xn_ref[pl.ds(j, tm, stride=half), :] = (hi & jnp.uint32(0xFFFF0000)) | (lo >> 16)
    _route_tile(xn, rwh_ref, rwl_ref, rb_ref, e_ref, gate_ref, rank_ref, cnt_ref, carry)


def _bf16_bits(x):
    return lax.bitcast_convert_type(x.astype(BF16).astype(F32), jnp.uint32)


def _merge(h, attn, pool, proj, wa, wb, wo, g2, rwh, rwl, rb, tm):
    m, d = h.shape
    n_e = rwh.shape[1]

    def const(a):
        return pl.BlockSpec(a.shape, lambda i, nd=a.ndim: (0,) * nd, pipeline_mode=pl.Buffered(1))

    tok_spec = pl.BlockSpec((tm, TOP_K), lambda i: (i, 0))
    return pl.pallas_call(
        _merge_kernel,
        grid=(m // tm,),
        in_specs=[
            pl.BlockSpec((tm, d), lambda i: (i, 0)),
            pl.BlockSpec((tm, attn.shape[1]), lambda i: (i, 0)),
            pl.BlockSpec((tm, pool.shape[1]), lambda i: (i, 0)),
            pl.BlockSpec((tm, d), lambda i: (i, 0)),
            pl.BlockSpec((tm, d), lambda i: (i, 1)),
            const(wa), const(wb), const(wo), const(g2), const(rwh), const(rwl), const(rb),
        ],
        out_specs=[pl.BlockSpec((tm, d), lambda i: (i, 0)),
                   pl.BlockSpec((tm * d // (2 * LANES), LANES), lambda i: (i, 0)),
                   tok_spec, tok_spec, tok_spec, pl.BlockSpec((1, n_e), lambda i: (0, 0))],
        out_shape=[jax.ShapeDtypeStruct((m, d), F32), jax.ShapeDtypeStruct((m * d // (2 * LANES), LANES), jnp.uint32),
                   jax.ShapeDtypeStruct((m, TOP_K), jnp.int32), jax.ShapeDtypeStruct((m, TOP_K), F32),
                   jax.ShapeDtypeStruct((m, TOP_K), jnp.int32), jax.ShapeDtypeStruct((1, n_e), jnp.int32)],
        scratch_shapes=[pltpu.VMEM((1, n_e), F32)],
        compiler_params=_cparams("arbitrary"),
        name="merge_route",
    )(h, attn, pool, proj, proj, wa, wb, wo, g2, rwh, rwl, rb)


def _route_tile(x, wh_ref, wl_ref, b_ref, e_ref, gate_ref, rank_ref, cnt_ref, carry):
    @pl.when(pl.program_id(0) == 0)
    def _():
        carry[...] = jnp.zeros_like(carry)

    xh = x.astype(BF16)
    xl = (x - xh.astype(F32)).astype(BF16)
    logits = (jnp.dot(xh, wh_ref[...], preferred_element_type=F32)
              + jnp.dot(xh, wl_ref[...], preferred_element_type=F32)
              + jnp.dot(xl, wh_ref[...], preferred_element_type=F32)) + b_ref[...]
    n_e = logits.shape[-1]
    lane = lax.broadcasted_iota(jnp.int32, logits.shape, 1)
    vals = logits
    tops, idxs = [], []
    for _ in range(TOP_K):
        m = jnp.max(vals, axis=-1, keepdims=True)
        idx = jnp.min(jnp.where(vals == m, lane, n_e), axis=-1, keepdims=True)
        tops.append(m)
        idxs.append(idx)
        vals = jnp.where(lane == idx, -jnp.inf, vals)
    ex = [jnp.exp(t - tops[0]) for t in tops]
    den = ex[0] + ex[1] + ex[2] + ex[3]
    for k in range(TOP_K):
        e_ref[:, k:k + 1] = idxs[k]
        gate_ref[:, k:k + 1] = ex[k] / den

    onehots = [(lane == idx).astype(F32) for idx in idxs]
    cnt = onehots[0] + onehots[1] + onehots[2] + onehots[3]
    tm = cnt.shape[0]
    lower = (lax.broadcasted_iota(jnp.int32, (tm, tm), 0) > lax.broadcasted_iota(jnp.int32, (tm, tm), 1))
    prefix = jnp.dot(lower.astype(BF16), cnt.astype(BF16), preferred_element_type=F32)
    base = carry[...] + prefix
    for k in range(TOP_K):
        rank_ref[:, k:k + 1] = jnp.sum(onehots[k] * base, axis=-1, keepdims=True).astype(jnp.int32)
    carry[...] = carry[...] + jnp.sum(cnt, axis=0, keepdims=True)
    cnt_ref[...] = carry[...].astype(jnp.int32)


def _row_gather_kernel(nused_ref, tok_ref, src_hbm, o_ref, buf, sem, *, rows):
    t = pl.program_id(0)
    n_used = nused_ref[0]
    groups = o_ref.shape[1] // (2 * LANES)

    def issue(tile, slot):
        def body(j, carry):
            for u in range(DMA_ISSUE_UNROLL):
                i = j * DMA_ISSUE_UNROLL + u
                src = pl.ds(pl.multiple_of(tok_ref[tile * rows + i], groups), groups)
                dst = pl.ds(pl.multiple_of(i * groups, groups), groups)
                pltpu.make_async_copy(src_hbm.at[src], buf.at[slot, dst], sem.at[slot]).start(priority=u % 2)
            return carry
        lax.fori_loop(0, rows // DMA_ISSUE_UNROLL, body, 0)

    @pl.when(jnp.logical_and(t == 0, n_used > 0))
    def _():
        issue(0, 0)

    @pl.when(t + 1 < n_used)
    def _():
        issue(t + 1, (t + 1) % 2)

    @pl.when(t < n_used)
    def _():
        slot = t % 2
        pltpu.make_async_copy(src_hbm.at[pl.ds(0, rows * groups)], buf.at[slot], sem.at[slot]).wait()
        for j in range(groups):
            w = buf[slot, pl.ds(j, rows, stride=groups), :]
            lo = lax.bitcast_convert_type(w << 16, F32)
            hi = lax.bitcast_convert_type(w & jnp.uint32(0xFFFF0000), F32)
            o_ref[:, j * LANES:(j + 1) * LANES] = lo.astype(o_ref.dtype)
            o_ref[:, (j + groups) * LANES:(j + groups + 1) * LANES] = hi.astype(o_ref.dtype)

    @pl.when(t >= n_used)
    def _():
        o_ref[...] = jnp.zeros_like(o_ref)


def _row_gather(n_used, row_idx, src, d, n_tiles, rows, out_dtype):
    groups = d // (2 * LANES)
    grid_spec = pltpu.PrefetchScalarGridSpec(
        num_scalar_prefetch=2,
        grid=(n_tiles,),
        in_specs=[pl.BlockSpec(memory_space=pl.ANY)],
        out_specs=pl.BlockSpec((rows, d), lambda t, *_: (t, 0)),
        scratch_shapes=[pltpu.VMEM((2, rows * groups, LANES), src.dtype), pltpu.SemaphoreType.DMA((2,))],
    )
    return pl.pallas_call(
        functools.partial(_row_gather_kernel, rows=rows),
        grid_spec=grid_spec,
        out_shape=jax.ShapeDtypeStruct((n_tiles * rows, d), out_dtype),
        compiler_params=_cparams("arbitrary"),
        name="moe_row_gather",
    )(n_used, row_idx, src)


def _start_all(copies):
    for c in copies:
        c.start()


def _wait_all(copies):
    for c in copies:
        c.wait()


def _start_weight_pairs(piece, step, slot, q_lo, q_hi):
    def body(q, carry):
        piece(step, slot, 2 * q).start(priority=0)
        piece(step, slot, 2 * q + 1).start(priority=1)
        return carry
    lax.fori_loop(q_lo, q_hi, body, 0)


def _wait_step_weights(step, piece, n_pieces, whole):
    slot = step % 2

    @pl.when(step == 0)
    def _():
        _start_weight_pairs(piece, step, slot, 0, n_pieces // 2)

    _wait_all(whole(step, slot))
    return slot


def _expert_chunk_loop(step, n_steps, n_chunks, next_n_chunks, in_copies, next_in_copies, out_copies, compute,
                       weight_piece, n_weight_pieces):
    def start_first_two(count, copies):
        @pl.when(count > 0)
        def _():
            _start_all(copies(0, 0))

        @pl.when(count > 1)
        def _():
            _start_all(copies(1, 1))

    n_pairs = n_weight_pieces // 2

    def next_weights(q_lo, q_hi):
        @pl.when(step + 1 < n_steps)
        def _():
            _start_weight_pairs(weight_piece, step + 1, (step + 1) % 2, q_lo, q_hi)

    @pl.when(step == 0)
    def _():
        start_first_two(n_chunks, in_copies)

    @pl.when(n_chunks > 0)
    def _():
        def chunk(r, first):
            in_slot = r % CHUNK_IN_SLOTS
            out_slot = r % 2

            @pl.when(r + 2 < n_chunks)
            def _():
                _start_all(in_copies(r + 2, (r + 2) % CHUNK_IN_SLOTS))

            next_weights(jnp.minimum(r * WEIGHT_PAIRS_PER_CHUNK, n_pairs),
                         jnp.minimum((r + 1) * WEIGHT_PAIRS_PER_CHUNK, n_pairs))
            _wait_all(in_copies(r, in_slot))

            if not first:
                @pl.when(r >= 2)
                def _():
                    _wait_all(out_copies(r - 2, out_slot))

            compute(in_slot, out_slot, first)
            _start_all(out_copies(r, out_slot))

        chunk(0, True)

        def body(r, carry):
            chunk(r, False)
            return carry

        lax.fori_loop(1, n_chunks, body, 0)

    @pl.when(step + 1 < n_steps)
    def _():
        start_first_two(next_n_chunks, next_in_copies)

    next_weights(jnp.minimum(n_chunks * WEIGHT_PAIRS_PER_CHUNK, n_pairs), n_pairs)

    @pl.when(n_chunks >= 2)
    def _():
        _wait_all(out_copies(n_chunks - 2, n_chunks % 2))

    @pl.when(n_chunks >= 1)
    def _():
        _wait_all(out_copies(n_chunks - 1, (n_chunks - 1) % 2))


def _zero_fill_tiles(first, last, zbuf, dst, sem):
    zbuf[...] = jnp.zeros_like(zbuf)

    def copy(t):
        return pltpu.make_async_copy(zbuf, dst(t), sem)

    def start(t, carry):
        copy(t).start()
        return carry

    def wait(t, carry):
        copy(t).wait()
        return carry

    lax.fori_loop(first, last, start, 0)
    lax.fori_loop(first, last, wait, 0)


def _moe_up_kernel(tpe_ref, start_ref, xs_hbm, w_hbm, bg_ref, bu_ref, hid_hbm,
                   wbuf, wg_sc, wu_sc, xbuf, obuf, w_sem, in_sem, out_sem, *, rows):
    e = pl.program_id(0)
    n = pl.program_id(1)
    n_chunks = pl.num_programs(1)
    row0 = start_ref[e]
    d, nc = wg_sc.shape

    piece_rows = d // DMA_SPLIT
    n_pieces = 2 * DMA_SPLIT

    def w_piece(step, slot, p):
        e_, n_ = step // n_chunks, step % n_chunks
        mat, blk = p // DMA_SPLIT, p % DMA_SPLIT
        cols = pl.ds(pl.multiple_of((mat * n_chunks + n_) * nc, nc), nc)
        rws = pl.ds(pl.multiple_of(blk * piece_rows, piece_rows), piece_rows)
        return pltpu.make_async_copy(w_hbm.at[e_, rws, cols], wbuf.at[slot, mat, rws], w_sem.at[slot])

    def w_whole(step, slot):
        e_, n_ = step // n_chunks, step % n_chunks
        return [pltpu.make_async_copy(
            w_hbm.at[e_, :, pl.ds(pl.multiple_of((mat * n_chunks + n_) * nc, nc), nc)], wbuf.at[slot, mat],
            w_sem.at[slot]) for mat in range(2)]

    step, n_steps = e * n_chunks + n, pl.num_programs(0) * n_chunks
    w_slot = _wait_step_weights(step, w_piece, n_pieces, w_whole)
    next_e = jnp.minimum((step + 1) // n_chunks, pl.num_programs(0) - 1)

    def chunk_rows(first_row, r):
        return pl.ds(pl.multiple_of(first_row + r * rows, rows), rows)

    def in_copies_from(first_row):
        def in_copies(r, slot):
            return [pltpu.make_async_copy(xs_hbm.at[chunk_rows(first_row, r)], xbuf.at[slot], in_sem.at[slot])]
        return in_copies

    def out_copies(r, slot):
        return [pltpu.make_async_copy(obuf.at[slot], hid_hbm.at[n, chunk_rows(row0, r)], out_sem.at[slot])]

    def compute(in_slot, out_slot, first):
        x = xbuf[in_slot]
        for c0 in range(0, nc, MOE_COL_BLOCK):
            cs = slice(c0, c0 + MOE_COL_BLOCK)
            if first:
                wg_sc[:, cs] = wbuf[w_slot, 0, :, cs].astype(BF16)
                wu_sc[:, cs] = wbuf[w_slot, 1, :, cs].astype(BF16)
            g = jnp.dot(x, wg_sc[:, cs], preferred_element_type=F32) + bg_ref[0, :, cs]
            u = jnp.dot(x, wu_sc[:, cs], preferred_element_type=F32) + bu_ref[0, :, cs]
            g = jnp.minimum(g, SWIGLU_LIMIT)
            u = jnp.clip(u, -SWIGLU_LIMIT, SWIGLU_LIMIT)
            obuf[out_slot, :, cs] = ((u + 1.0) * (g * jax.nn.sigmoid(SWIGLU_ALPHA * g))).astype(obuf.dtype)

    _expert_chunk_loop(step, n_steps, tpe_ref[e], tpe_ref[next_e], in_copies_from(row0),
                       in_copies_from(start_ref[next_e]), out_copies, compute, w_piece, n_pieces)

    @pl.when(e == pl.num_programs(0) - 1)
    def _():
        _zero_fill_tiles(row0 // rows + tpe_ref[e], hid_hbm.shape[1] // rows, obuf.at[0],
                         lambda t: hid_hbm.at[n, pl.ds(pl.multiple_of(t * rows, rows), rows)], out_sem.at[0])


def _moe_up(tiles_per_e, row_start, xs, w_gu, b_gu, rows, nc):
    r_pad, d = xs.shape
    n_e = w_gu.shape[0]
    d_ff = w_gu.shape[2] // 2
    n_chunks = d_ff // nc
    any_spec = pl.BlockSpec(memory_space=pl.ANY)
    grid_spec = pltpu.PrefetchScalarGridSpec(
        num_scalar_prefetch=2,
        grid=(n_e, n_chunks),
        in_specs=[
            any_spec,
            any_spec,
            pl.BlockSpec((1, 1, nc), lambda e, n, *_: (e, 0, n)),
            pl.BlockSpec((1, 1, nc), lambda e, n, *_: (e, 0, n_chunks + n)),
        ],
        out_specs=any_spec,
        scratch_shapes=[pltpu.VMEM((2, 2, d, nc), w_gu.dtype), pltpu.VMEM((d, nc), BF16), pltpu.VMEM((d, nc), BF16),
                        pltpu.VMEM((CHUNK_IN_SLOTS, rows, d), xs.dtype), pltpu.VMEM((2, rows, nc), BF16),
                        pltpu.SemaphoreType.DMA((2,)), pltpu.SemaphoreType.DMA((CHUNK_IN_SLOTS,)),
                        pltpu.SemaphoreType.DMA((2,))],
    )
    return pl.pallas_call(
        functools.partial(_moe_up_kernel, rows=rows),
        grid_spec=grid_spec,
        out_shape=jax.ShapeDtypeStruct((n_chunks, r_pad, nc), BF16),
        compiler_params=_cparams("arbitrary", "arbitrary"),
        name="moe_up",
    )(tiles_per_e, row_start, xs, w_gu, b_gu, b_gu)


def _moe_down_kernel(tpe_ref, start_ref, hid_hbm, w_hbm, b_ref, y_hbm,
                     wbuf, w_sc, xbuf, obuf, w_sem, in_sem, out_sem, *, rows):
    e = pl.program_id(0)
    n = pl.program_id(1)
    n_chunks = pl.num_programs(1)
    row0 = start_ref[e]
    k_chunks, _, kc = xbuf.shape[1:]
    d_ff, nc = w_sc.shape

    piece_rows = d_ff // DMA_SPLIT

    def w_piece(step, slot, p):
        e_, n_ = step // n_chunks, step % n_chunks
        cols = pl.ds(pl.multiple_of(n_ * nc, nc), nc)
        rws = pl.ds(pl.multiple_of(p * piece_rows, piece_rows), piece_rows)
        return pltpu.make_async_copy(w_hbm.at[e_, rws, cols], wbuf.at[slot, rws], w_sem.at[slot])

    def w_whole(step, slot):
        e_, n_ = step // n_chunks, step % n_chunks
        return [pltpu.make_async_copy(w_hbm.at[e_, :, pl.ds(pl.multiple_of(n_ * nc, nc), nc)], wbuf.at[slot],
                                      w_sem.at[slot])]

    step, n_steps = e * n_chunks + n, pl.num_programs(0) * n_chunks
    w_slot = _wait_step_weights(step, w_piece, DMA_SPLIT, w_whole)
    next_e = jnp.minimum((step + 1) // n_chunks, pl.num_programs(0) - 1)

    def in_copies_from(first_row):
        def in_copies(r, slot):
            chunk = pl.ds(pl.multiple_of(first_row + r * rows, rows), rows)
            return [pltpu.make_async_copy(hid_hbm.at[:, chunk], xbuf.at[slot], in_sem.at[slot])]
        return in_copies

    lanes_per_row = nc // LANES
    tile_rows = rows * lanes_per_row

    def out_copies(r, slot):
        dst = pl.ds(pl.multiple_of((row0 + r * rows) * lanes_per_row, tile_rows), tile_rows)
        return [pltpu.make_async_copy(obuf.at[slot], y_hbm.at[dst], out_sem.at[slot])]

    def compute(in_slot, out_slot, first):
        for c0 in range(0, nc, MOE_COL_BLOCK):
            cs = slice(c0, c0 + MOE_COL_BLOCK)
            if first:
                w_sc[:, cs] = wbuf[w_slot, :, cs].astype(BF16)
            acc = b_ref[0, :, cs] + jnp.dot(xbuf[in_slot, 0], w_sc[0:kc, cs], preferred_element_type=F32)
            for c in range(1, k_chunks):
                acc = acc + jnp.dot(xbuf[in_slot, c], w_sc[c * kc:(c + 1) * kc, cs], preferred_element_type=F32)
            for j in range(MOE_COL_BLOCK // LANES):
                lane_group = c0 // LANES + j
                obuf[out_slot, pl.ds(lane_group, rows, stride=lanes_per_row), :] = acc[:, j * LANES:(j + 1) * LANES]

    _expert_chunk_loop(step, n_steps, tpe_ref[e], tpe_ref[next_e], in_copies_from(row0),
                       in_copies_from(start_ref[next_e]), out_copies, compute, w_piece, DMA_SPLIT)

    @pl.when(e == pl.num_programs(0) - 1)
    def _():
        _zero_fill_tiles(row0 // rows + tpe_ref[e], y_hbm.shape[0] // tile_rows, obuf.at[0],
                         lambda t: y_hbm.at[pl.ds(pl.multiple_of(t * tile_rows, tile_rows), tile_rows)], out_sem.at[0])


def _moe_down(tiles_per_e, row_start, hid, w_dn, b_dn, rows):
    k_chunks, r_pad, kc = hid.shape
    n_e, d_ff, d = w_dn.shape
    nc = d
    n_chunks = 1
    any_spec = pl.BlockSpec(memory_space=pl.ANY)
    grid_spec = pltpu.PrefetchScalarGridSpec(
        num_scalar_prefetch=2,
        grid=(n_e, n_chunks),
        in_specs=[
            any_spec,
            any_spec,
            pl.BlockSpec((1, 1, nc), lambda e, n, *_: (e, 0, n)),
        ],
        out_specs=any_spec,
        scratch_shapes=[pltpu.VMEM((2, d_ff, nc), w_dn.dtype), pltpu.VMEM((d_ff, nc), BF16),
                        pltpu.VMEM((CHUNK_IN_SLOTS, k_chunks, rows, kc), hid.dtype),
                        pltpu.VMEM((2, rows * nc // LANES, LANES), F32),
                        pltpu.SemaphoreType.DMA((2,)), pltpu.SemaphoreType.DMA((CHUNK_IN_SLOTS,)),
                        pltpu.SemaphoreType.DMA((2,))],
    )
    return pl.pallas_call(
        functools.partial(_moe_down_kernel, rows=rows),
        grid_spec=grid_spec,
        out_shape=jax.ShapeDtypeStruct((r_pad * d // LANES, LANES), F32),
        compiler_params=_cparams("arbitrary", "arbitrary"),
        name="moe_down",
    )(tiles_per_e, row_start, hid, w_dn, b_dn)


def _combine_kernel(dest_ref, h_ref, gate_ref, gf_ref, y_hbm, *rest, rows, final_norm, n_head_tiles, split):
    if split:
        o_ref, o_tail_ref, buf, sem = rest
    else:
        (o_ref, buf, sem), o_tail_ref = rest, None
    t = pl.program_id(0)
    n_t = pl.num_programs(0)
    n_dma = rows * TOP_K
    d = o_ref.shape[1]
    groups = d // LANES

    def issue(tile, slot):
        def body(j, carry):
            for u in range(DMA_ISSUE_UNROLL):
                i = j * DMA_ISSUE_UNROLL + u
                src = pl.ds(pl.multiple_of(dest_ref[tile * n_dma + i], groups), groups)
                dst = pl.ds(pl.multiple_of(i * groups, groups), groups)
                pltpu.make_async_copy(y_hbm.at[src], buf.at[slot, dst], sem.at[slot]).start(priority=u % 2)
            return carry
        lax.fori_loop(0, n_dma // DMA_ISSUE_UNROLL, body, 0)

    @pl.when(t == 0)
    def _():
        issue(0, 0)

    @pl.when(t + 1 < n_t)
    def _():
        issue(t + 1, (t + 1) % 2)

    slot = t % 2
    pltpu.make_async_copy(y_hbm.at[pl.ds(0, n_dma * groups)], buf.at[slot], sem.at[slot]).wait()

    def finish(o_ref):
        gates = [gate_ref[:, k:k + 1] for k in range(TOP_K)]
        sq = jnp.zeros((rows, 1), F32)
        for j in range(groups):
            cs = slice(j * LANES, (j + 1) * LANES)
            acc = h_ref[:, cs]
            for k in range(TOP_K):
                acc = acc + gates[k] * buf[slot, pl.ds(k * rows * groups + j, rows, stride=groups), :]
            o_ref[:, cs] = acc
            sq = sq + jnp.sum(acc * acc, axis=-1, keepdims=True)
        if final_norm:
            o_ref[...] = o_ref[...] * lax.rsqrt(sq / d + NORM_EPS) * gf_ref[...]

    if o_tail_ref is None:
        finish(o_ref)
    else:
        pl.when(t < n_head_tiles)(lambda: finish(o_ref))
        pl.when(t >= n_head_tiles)(lambda: finish(o_tail_ref))


def _combine(dest_km, h1, gate, gf, y, rows, final_norm, split_at=None):
    m, d = h1.shape
    n_head = m // rows if split_at is None else split_at // rows
    out_specs = [pl.BlockSpec((rows, d), lambda t, *_: (jnp.minimum(t, n_head - 1), 0))]
    out_shape = [jax.ShapeDtypeStruct((n_head * rows, d), F32)]
    if split_at is not None:
        assert split_at % rows == 0 and (m - split_at) % rows == 0
        out_specs.append(pl.BlockSpec((rows, d), lambda t, *_: (jnp.maximum(t - n_head, 0), 0)))
        out_shape.append(jax.ShapeDtypeStruct((m - split_at, d), F32))
    grid_spec = pltpu.PrefetchScalarGridSpec(
        num_scalar_prefetch=1,
        grid=(m // rows,),
        in_specs=[
            pl.BlockSpec((rows, d), lambda t, *_: (t, 0)),
            pl.BlockSpec((rows, TOP_K), lambda t, *_: (t, 0)),
            pl.BlockSpec((1, d), lambda t, *_: (0, 0)),
            pl.BlockSpec(memory_space=pl.ANY),
        ],
        out_specs=out_specs,
        scratch_shapes=[pltpu.VMEM((2, rows * TOP_K * d // LANES, LANES), F32), pltpu.SemaphoreType.DMA((2,))],
    )
    return pl.pallas_call(
        functools.partial(_combine_kernel, rows=rows, final_norm=final_norm, n_head_tiles=n_head,
                          split=split_at is not None),
        grid_spec=grid_spec,
        out_shape=out_shape,
        compiler_params=_cparams("arbitrary"),
        name="moe_combine",
    )(dest_km, h1, gate, gf, y)


def _routing_tables(top_e, rank, counts, rows, n_tiles):
    m = top_e.shape[0]
    n_experts = counts.shape[0]
    tiles_per_e = (counts + rows - 1) // rows
    tile_end = jnp.cumsum(tiles_per_e)
    pad_start = (tile_end - tiles_per_e) * rows
    experts = jnp.arange(n_experts, dtype=jnp.int32)
    dest = rank + jnp.sum(jnp.where(top_e[..., None] == experts, pad_start, 0), axis=-1)
    dest = dest.astype(jnp.int32).reshape(-1)
    n_used = tile_end[-1:].astype(jnp.int32)
    tok = jnp.arange(m * TOP_K, dtype=jnp.int32) // TOP_K
    row_tok = jnp.zeros((n_tiles * rows,), jnp.int32).at[dest].set(tok)
    return dest, row_tok, tiles_per_e.astype(jnp.int32), pad_start.astype(jnp.int32), n_used


def _largest_divisor(n, cap, mult):
    best = mult
    for c in range(mult, cap + 1, mult):
        if n % c == 0:
            best = c
    return best


def kernel(x_prompt, x_sample, cache_kv_w128, cache_kv_w512, cache_kv_w2048, state_pool, rel_bias, norm1, w_in,
           w_branch_a, w_branch_b, w_out, w_pool, pool_scale, norm2, router_w, router_b, w_gate_up, b_gate_up,
           w_down, b_down, norm_f):
    batch, seq, d = x_prompt.shape
    db, t_new, _ = x_sample.shape
    depth = norm1.shape[0]
    n_experts = router_w.shape[-1]
    caches = (cache_kv_w128, cache_kv_w512, cache_kv_w2048)
    m_p, m_s = batch * seq, db * t_new
    m = m_p + m_s
    tm = _largest_divisor(m, 640, 8)

    h = jnp.concatenate([x_prompt.reshape(m_p, d), x_sample.reshape(m_s, d)], axis=0)
    bias_p = _prompt_bias_tiles(rel_bias)
    bias_sc, bias_sn = _sample_bias_tiles(rel_bias, t_new)

    cuts = [ATTN_WIDTH, 2 * ATTN_WIDTH, 3 * ATTN_WIDTH, 3 * ATTN_WIDTH + POOL_WIDTH, 3 * ATTN_WIDTH + POOL_WIDTH + d]
    qkv0 = 2 * d
    z0 = qkv0 + 3 * ATTN_WIDTH

    kv_p = [[] for _ in range(N_GROUPS)]
    kv_s = [[] for _ in range(N_GROUPS)]
    pool_p, pool_s = [], []
    for l in range(depth):
        wq, wk, wv, wz, wga, wgb = jnp.split(w_in[l], cuts, axis=-1)
        w_perm = jnp.concatenate([wga, wgb, wq, wk, wv, wz], axis=-1).astype(BF16)
        proj = _inproj(h, norm1[l][None, :], w_perm, tm, _largest_divisor(w_perm.shape[1], 1536, 128))

        k0, v0 = qkv0 + ATTN_WIDTH, qkv0 + 2 * ATTN_WIDTH
        kv_w = HEADS_PER_GROUP * HEAD_DIM
        proj_s = proj[m_p:]
        z_s = proj_s[:, z0:z0 + POOL_WIDTH].reshape(db, t_new, POOL_WIDTH)

        attn_p = _prompt_attn(proj, bias_p, batch, seq, qkv0 // HEAD_DIM)
        qkv_s = proj_s[:, qkv0:qkv0 + 3 * ATTN_WIDTH].reshape(db, t_new, 3 * ATTN_WIDTH)
        c0 = caches[0][l].reshape(db, -1, HEAD_DIM)
        c1 = caches[1][l].reshape(db, -1, HEAD_DIM)
        dil2 = ATTN_GROUPS[2][1]
        c2v = caches[2][l].reshape(db, -1, dil2 * CACHE_ROW_TILE, HEAD_DIM)

        news = []
        for g, (window, _) in enumerate(ATTN_GROUPS):
            keep = min(window, seq)
            kc, vc = k0 + g * kv_w, v0 + g * kv_w

            def kv_rows(a, r0, r1, kc=kc, vc=vc):
                return jnp.stack([a[r0:r1, kc:kc + kv_w].reshape(-1, HEADS_PER_GROUP, HEAD_DIM),
                                  a[r0:r1, vc:vc + kv_w].reshape(-1, HEADS_PER_GROUP, HEAD_DIM)], axis=1)

            kv_p[g].append(jnp.stack([kv_rows(proj, (b + 1) * seq - keep, (b + 1) * seq) for b in range(batch)]))
            news.append(kv_rows(proj_s, 0, m_s).reshape(db, t_new * CACHE_ROW_TILE, HEAD_DIM))
        news[2] = news[2][:, None]
        attn_s, *rolled = _sample_attn(qkv_s, c0, c1, c2v, news, bias_sc, bias_sn)
        attn = jnp.concatenate([attn_p, attn_s.reshape(m_s, ATTN_OUT_WIDTH)], axis=0)
        for g in range(N_GROUPS):
            kv_s[g].append(rolled[g].reshape(caches[g][l].shape))

        w_pool_b = w_pool[l].astype(BF16)
        scale = pool_scale[l][None, :]
        hist_p = jnp.zeros((batch, POOL_HIST, POOL_WIDTH), F32)
        hist_s = jnp.concatenate([jnp.zeros((db, POOL_HIST - POOL_BUF, POOL_WIDTH), F32), state_pool[l]], axis=1)
        pool = jnp.concatenate([
            _pool_mix(hist_p, proj, pl.BlockSpec((seq, POOL_WIDTH), lambda i: (i, z0 // POOL_WIDTH)),
                      batch, seq, w_pool_b, scale, 0).reshape(m_p, POOL_WIDTH),
            _pool_mix(hist_s, z_s, pl.BlockSpec((1, t_new, POOL_WIDTH), lambda i: (i, 0, 0)),
                      db, t_new, w_pool_b, scale, PAST_LEN).reshape(m_s, POOL_WIDTH),
        ], axis=0)
        pool_p.append(jnp.stack([proj[(b + 1) * seq - POOL_BUF:(b + 1) * seq, z0:z0 + POOL_WIDTH]
                                 for b in range(batch)]))
        pool_s.append(jnp.concatenate([state_pool[l], z_s], axis=1)[:, t_new:])

        rw = router_w[l]
        rw_hi = rw.astype(BF16)
        rw_lo = (rw - rw_hi.astype(F32)).astype(BF16)
        h1, xn_rows, top_e, gate, rank, counts = _merge(
            h, attn, pool, proj, w_branch_a[l].astype(BF16), w_branch_b[l].astype(BF16), w_out[l].astype(BF16),
            norm2[l][None, :], rw_hi, rw_lo, router_b[l][None, :], _largest_divisor(m, 320, 8))

        rows = MOE_ROW_TILE
        groups = d // LANES
        n_tiles = (m * TOP_K + n_experts * (rows - 1)) // rows
        dest, row_tok, tiles_per_e, row_start, n_used = _routing_tables(top_e, rank, counts[0], rows, n_tiles)
        xs = _row_gather(n_used, row_tok * (groups // 2), xn_rows, d, n_tiles, rows, BF16)
        hid = _moe_up(tiles_per_e, row_start, xs, w_gate_up[l], b_gate_up[l][:, None, :], rows, 1024)
        y = _moe_down(tiles_per_e, row_start, hid, w_down[l], b_down[l][:, None, :], rows)
        crow = _largest_divisor(math.gcd(m_p, m_s), 128, 8)
        dest_km = (dest * groups).reshape(m // crow, crow, TOP_K).transpose(0, 2, 1).reshape(-1)
        if l == depth - 1:
            h_p, h_s = _combine(dest_km, h1, gate, norm_f[None, :], y, crow, final_norm=True, split_at=m_p)
        else:
            h, = _combine(dest_km, h1, gate, norm_f[None, :], y, crow, final_norm=False)

    y_prompt = h_p.reshape(batch, seq, d)
    y_sample = h_s.reshape(db, t_new, d)
    return (y_prompt, y_sample,
            jnp.stack(kv_p[0]), jnp.stack(kv_p[1]), jnp.stack(kv_p[2]), jnp.stack(pool_p),
            jnp.stack(kv_s[0]), jnp.stack(kv_s[1]), jnp.stack(kv_s[2]), jnp.stack(pool_s))
```

```python
import functools
import math

import jax
import jax.numpy as jnp
from jax import lax
from jax.experimental import pallas as pl
from jax.experimental.pallas import tpu as pltpu

F32 = jnp.float32
BF16 = jnp.bfloat16

HEAD_DIM = 128
HEADS_PER_GROUP = 4
ATTN_GROUPS = ((128, 1), (512, 4), (2048, 16))
N_GROUPS = len(ATTN_GROUPS)
N_ATTN_HEADS = N_GROUPS * HEADS_PER_GROUP
ATTN_WIDTH = N_ATTN_HEADS * HEAD_DIM
ATTN_OUT_WIDTH = HEADS_PER_GROUP * HEAD_DIM
ATTN_SCALE = HEAD_DIM ** -0.5
N_BACK = 128
N_REL_BUCKETS = 32
REL_MAX_DIST = 2048
POOL_WINDOWS = (2, 4, 8, 16)
POOL_GROUPS = len(POOL_WINDOWS)
PGW = 128
POOL_WIDTH = POOL_GROUPS * PGW
POOL_BUF = max(POOL_WINDOWS) - 1
POOL_HIST = 16
TOP_K = 4
SWIGLU_LIMIT = 7.0
SWIGLU_ALPHA = 1.702
NORM_EPS = 1e-5
NEG_INF = -1e30
PAST_LEN = 8192
CACHE_ROW_TILE = 2 * HEADS_PER_GROUP

VMEM_LIMIT_BYTES = 56 * 1024 * 1024
MOE_ROW_TILE = 256
DMA_ISSUE_UNROLL = 8
DMA_SPLIT = 8
ATTN_INTERLEAVE = 8
LANES = 128
CHUNK_IN_SLOTS = 3
MOE_COL_BLOCK = 512
WEIGHT_PAIRS_PER_CHUNK = 2


def _cparams(*sem):
    return pltpu.CompilerParams(dimension_semantics=sem, vmem_limit_bytes=VMEM_LIMIT_BYTES)


def _rmsnorm_f32(x, g):
    return x * lax.rsqrt(jnp.mean(x * x, axis=-1, keepdims=True) + NORM_EPS) * g


def _inproj_kernel(x_ref, g_ref, w_ref, o_ref):
    xn = _rmsnorm_f32(x_ref[...], g_ref[...]).astype(BF16)
    o_ref[...] = jnp.dot(xn, w_ref[...], preferred_element_type=F32)


def _inproj(x, g, w, tm, tn):
    m, d = x.shape
    n = w.shape[1]
    return pl.pallas_call(
        _inproj_kernel,
        grid=(n // tn, m // tm),
        in_specs=[
            pl.BlockSpec((tm, d), lambda j, i: (i, 0)),
            pl.BlockSpec((1, d), lambda j, i: (0, 0)),
            pl.BlockSpec((d, tn), lambda j, i: (0, j)),
        ],
        out_specs=pl.BlockSpec((tm, tn), lambda j, i: (i, j)),
        out_shape=jax.ShapeDtypeStruct((m, n), F32),
        compiler_params=_cparams("arbitrary", "arbitrary"),
        name="inproj",
    )(x, g, w)


def _t5_bucket(dist):
    max_exact = N_REL_BUCKETS // 2
    df = jnp.maximum(dist, 1).astype(F32)
    large = max_exact + (jnp.log(df / max_exact) / math.log(REL_MAX_DIST / max_exact)
                         * (N_REL_BUCKETS - max_exact)).astype(jnp.int32)
    large = jnp.minimum(large, N_REL_BUCKETS - 1)
    return jnp.where(dist < max_exact, dist, large)


def _step_bias(rel_bias, g):
    _, dil = ATTN_GROUPS[g]
    buckets = _t5_bucket(jnp.arange(N_BACK + 1, dtype=jnp.int32) * dil)
    tab = rel_bias[buckets].astype(F32)
    return tab[:, g * HEADS_PER_GROUP:(g + 1) * HEADS_PER_GROUP].T


def _prompt_bias_tiles(rel_bias):
    nb = N_BACK
    period = 3 * nb - 1
    tiles = []
    for g in range(N_GROUPS):
        bj = _step_bias(rel_bias, g)
        h = bj.shape[0]
        pad = jnp.full((h, nb - 1), NEG_INF, F32)
        v = jnp.concatenate([pad, bj[:, ::-1], pad], axis=1)
        skew = jnp.tile(v, (1, nb + 1))[:, :nb * (period + 1)].reshape(h, nb, period + 1)
        tiles.append(skew[:, ::-1, :2 * nb])
    return jnp.stack(tiles, axis=1)


def _sample_bias_tiles(rel_bias, t_new):
    tq = jnp.arange(t_new)[:, None]
    outs = []
    new = []
    for g, (window, dil) in enumerate(ATTN_GROUPS):
        bj = _step_bias(rel_bias, g)
        if g < 2:
            p = jnp.arange(window)[None, :]
            dist = window + tq - p
            ok = (dist % dil == 0) & (dist // dil <= N_BACK)
            outs.append(jnp.where(ok[None], bj[:, jnp.clip(dist // dil, 0, N_BACK)], NEG_INF))
        else:
            col = jnp.arange(t_new * N_BACK)[None, :]
            blk, r = col // N_BACK, col % N_BACK
            ok = blk == tq
            outs.append(jnp.where(ok[None], bj[:, jnp.broadcast_to(N_BACK - r, (t_new, t_new * N_BACK))], NEG_INF))
        tk = jnp.arange(t_new)[None, :]
        d = tq - tk
        okn = (d >= 0) & (d % dil == 0) & (d // dil <= N_BACK)
        new.append(jnp.where(okn[None], bj[:, jnp.clip(d // dil, 0, N_BACK)], NEG_INF))
    return outs, jnp.stack(new, axis=0)


def _softmax_block(s, v):
    m = jnp.max(s, axis=-1, keepdims=True)
    p = jnp.exp(s - m)
    den = jnp.sum(p, axis=-1, keepdims=True)
    o = jnp.dot(p.astype(BF16), v, preferred_element_type=F32) / den
    return o, m + jnp.log(den)


def _prompt_attn_kernel(q0, k0, v0, q1, k1, v1, q2, k2, v2, bias_ref, o_ref, o_sc, l_sc, *, seq):
    qkv = ((q0, k0, v0), (q1, k1, v1), (q2, k2, v2))
    nb = N_BACK

    def rows(ref, start, size, dil):
        if dil == 1:
            return ref[pl.ds(start, size), :]
        return ref[pl.ds(start, size, stride=dil), :]

    def put(g, start, dil, o, lse):
        idx = pl.ds(start, nb) if dil == 1 else pl.ds(start, nb, stride=dil)
        o_sc[g, idx, :] = o
        l_sc[g, idx, :] = jnp.broadcast_to(lse, (nb, HEAD_DIM))

    dn = (((1,), (1,)), ((), ()))

    def attend(g, dil, q_starts, kv_starts, n_kv):
        q_ref, k_ref, v_ref = qkv[g]
        bias = bias_ref[0, g] if n_kv == 2 * nb else bias_ref[0, g, :, nb:]
        qs = [rows(q_ref, st, nb, dil).astype(BF16) for st in q_starts]
        ks = [rows(k_ref, st, n_kv, dil).astype(BF16) for st in kv_starts]
        ss = [lax.dot_general(q, k, dn, preferred_element_type=F32) * ATTN_SCALE + bias for q, k in zip(qs, ks)]
        ms = [jnp.max(s, axis=-1, keepdims=True) for s in ss]
        ps = [jnp.exp(s - m) for s, m in zip(ss, ms)]
        dens = [jnp.sum(p, axis=-1, keepdims=True) for p in ps]
        vs = [rows(v_ref, st, n_kv, dil).astype(BF16) for st in kv_starts]
        for st, p, v, m, den in zip(q_starts, ps, vs, ms, dens):
            o = jnp.dot(p.astype(BF16), v, preferred_element_type=F32) / den
            put(g, st, dil, o, m + jnp.log(den))

    for g, (_, dil) in enumerate(ATTN_GROUPS):
        n_blk = seq // (dil * nb)
        span = dil * nb
        for r0 in range(0, dil, ATTN_INTERLEAVE):
            starts = list(range(r0, min(r0 + ATTN_INTERLEAVE, dil)))
            attend(g, dil, starts, starts, nb)
        if n_blk > 1:
            per_iter = max(c for c in range(1, max(1, ATTN_INTERLEAVE // dil) + 1) if (n_blk - 1) % c == 0)

            def body(it, carry, g=g, dil=dil, span=span, per_iter=per_iter):
                q_starts, kv_starts = [], []
                for j in range(per_iter):
                    b = 1 + it * per_iter + j
                    for r in range(dil):
                        q_starts.append(b * span + r)
                        kv_starts.append((b - 1) * span + r)
                attend(g, dil, q_starts, kv_starts, 2 * nb)
                return carry

            lax.fori_loop(0, (n_blk - 1) // per_iter, body, 0)

    chunk = 256

    def comb(c, carry):
        sl = pl.ds(pl.multiple_of(c * chunk, chunk), chunk)
        l0, l1, l2 = l_sc[0, sl, :], l_sc[1, sl, :], l_sc[2, sl, :]
        mx = jnp.maximum(jnp.maximum(l0, l1), l2)
        e0, e1, e2 = jnp.exp(l0 - mx), jnp.exp(l1 - mx), jnp.exp(l2 - mx)
        num = e0 * o_sc[0, sl, :] + e1 * o_sc[1, sl, :] + e2 * o_sc[2, sl, :]
        o_ref[sl, :] = (num / (e0 + e1 + e2)).astype(o_ref.dtype)
        return carry

    lax.fori_loop(0, seq // chunk, comb, 0)


def _prompt_attn(p, bias, batch, seq, col0):
    def spec(kind, g):
        base = col0 + kind * N_ATTN_HEADS + g * HEADS_PER_GROUP
        return pl.BlockSpec((seq, HEAD_DIM), lambda b, i, base=base: (b, base + i))

    in_specs = []
    for g in range(N_GROUPS):
        in_specs += [spec(0, g), spec(1, g), spec(2, g)]
    in_specs.append(pl.BlockSpec((1, N_GROUPS, N_BACK, 2 * N_BACK), lambda b, i: (i, 0, 0, 0)))
    return pl.pallas_call(
        functools.partial(_prompt_attn_kernel, seq=seq),
        grid=(batch, HEADS_PER_GROUP),
        in_specs=in_specs,
        out_specs=pl.BlockSpec((seq, HEAD_DIM), lambda b, i: (b, i)),
        out_shape=jax.ShapeDtypeStruct((batch * seq, ATTN_OUT_WIDTH), BF16),
        scratch_shapes=[pltpu.VMEM((N_GROUPS, seq, HEAD_DIM), F32), pltpu.VMEM((N_GROUPS, seq, HEAD_DIM), F32)],
        compiler_params=_cparams("arbitrary", "arbitrary"),
        name="prompt_attn",
    )(*([p] * 9), bias)


def _sample_attn_kernel(qkv_ref, c0_ref, c1_ref, c2_ref, n0_ref, n1_ref, n2_ref, b0_ref, b1_ref, b2_ref, bn_ref,
                        o_ref, r0_ref, r1_ref, r2_ref, *, t_new):
    caches = (c0_ref, c1_ref, c2_ref)
    biases = (b0_ref, b1_ref, b2_ref)
    dn = (((1,), (1,)), ((), ()))

    shift = t_new * CACHE_ROW_TILE
    for c_ref, n_ref, r_ref in ((c0_ref, n0_ref, r0_ref), (c1_ref, n1_ref, r1_ref)):
        keep = c_ref.shape[1] - shift
        r_ref[0, :keep] = c_ref[0, shift:]
        r_ref[0, keep:] = n_ref[0]
    n_r, width = c2_ref.shape[1], c2_ref.shape[2]
    r2_ref[0, :, :width - shift] = c2_ref[0, :, shift:]
    r2_ref[0, :n_r - 1, width - shift:] = c2_ref[0, 1:, :shift]
    r2_ref[0, n_r - 1:, width - shift:] = n2_ref[0]

    for i in range(HEADS_PER_GROUP):
        outs, lses = [], []
        for g in range(N_GROUPS):
            h = g * HEADS_PER_GROUP + i
            q = qkv_ref[0, :, h * HEAD_DIM:(h + 1) * HEAD_DIM].astype(BF16)
            kn = qkv_ref[0, :, ATTN_WIDTH + h * HEAD_DIM:ATTN_WIDTH + (h + 1) * HEAD_DIM].astype(BF16)
            vn = qkv_ref[0, :, 2 * ATTN_WIDTH + h * HEAD_DIM:2 * ATTN_WIDTH + (h + 1) * HEAD_DIM].astype(BF16)
            c_ref = caches[g]
            if g < 2:
                n_rows = c_ref.shape[1] // CACHE_ROW_TILE
                kc = c_ref[0, pl.ds(i, n_rows, stride=CACHE_ROW_TILE), :].astype(BF16)
                vc = c_ref[0, pl.ds(HEADS_PER_GROUP + i, n_rows, stride=CACHE_ROW_TILE), :].astype(BF16)
            else:
                kc = jnp.concatenate(
                    [c_ref[0, :, t * CACHE_ROW_TILE + i, :] for t in range(t_new)], axis=0).astype(BF16)
                vc = jnp.concatenate(
                    [c_ref[0, :, t * CACHE_ROW_TILE + HEADS_PER_GROUP + i, :] for t in range(t_new)],
                    axis=0).astype(BF16)
            sc = lax.dot_general(q, kc, dn, preferred_element_type=F32) * ATTN_SCALE + biases[g][i]
            sn = lax.dot_general(q, kn, dn, preferred_element_type=F32) * ATTN_SCALE + bn_ref[g, i]
            m = jnp.maximum(jnp.max(sc, axis=-1, keepdims=True), jnp.max(sn, axis=-1, keepdims=True))
            pc = jnp.exp(sc - m)
            pn = jnp.exp(sn - m)
            den = jnp.sum(pc, axis=-1, keepdims=True) + jnp.sum(pn, axis=-1, keepdims=True)
            o = (jnp.dot(pc.astype(BF16), vc, preferred_element_type=F32)
                 + jnp.dot(pn.astype(BF16), vn, preferred_element_type=F32)) / den
            outs.append(o)
            lses.append(m + jnp.log(den))
        mx = jnp.maximum(jnp.maximum(lses[0], lses[1]), lses[2])
        es = [jnp.exp(l - mx) for l in lses]
        num = es[0] * outs[0] + es[1] * outs[1] + es[2] * outs[2]
        o_ref[0, :, i * HEAD_DIM:(i + 1) * HEAD_DIM] = (num / (es[0] + es[1] + es[2])).astype(o_ref.dtype)


def _sample_attn(qkv_s, c0, c1, c2v, news, bias_c, bias_n):
    db, t_new, _ = qkv_s.shape

    def full(a):
        return pl.BlockSpec(a.shape, lambda b, nd=a.ndim: (0,) * nd)

    def per_seq(a):
        return pl.BlockSpec((1,) + a.shape[1:], lambda b, nd=a.ndim: (b,) + (0,) * (nd - 1))

    return pl.pallas_call(
        functools.partial(_sample_attn_kernel, t_new=t_new),
        grid=(db,),
        in_specs=[per_seq(qkv_s), per_seq(c0), per_seq(c1), per_seq(c2v), per_seq(news[0]), per_seq(news[1]),
                  per_seq(news[2]), full(bias_c[0]), full(bias_c[1]), full(bias_c[2]), full(bias_n)],
        out_specs=[pl.BlockSpec((1, t_new, ATTN_OUT_WIDTH), lambda b: (b, 0, 0)),
                   per_seq(c0), per_seq(c1), per_seq(c2v)],
        out_shape=[jax.ShapeDtypeStruct((db, t_new, ATTN_OUT_WIDTH), BF16)]
                  + [jax.ShapeDtypeStruct(c.shape, c.dtype) for c in (c0, c1, c2v)],
        compiler_params=_cparams("arbitrary"),
        name="sample_attn",
    )(qkv_s, c0, c1, c2v, *news, bias_c[0], bias_c[1], bias_c[2], bias_n)


def _pool_kernel(hist_ref, z_ref, w_ref, s_ref, o_ref, zbuf, *, t_len, pos0, chunk):
    zbuf[0:POOL_HIST, :] = hist_ref[0]
    zbuf[POOL_HIST:, :] = z_ref[...].reshape(t_len, POOL_WIDTH)
    for c in range(t_len // chunk):
        base = POOL_HIST + c * chunk
        for g, w in enumerate(POOL_WINDOWS):
            sl = slice(g * PGW, (g + 1) * PGW)
            cur = zbuf[base:base + chunk, sl]
            tot = cur
            for i in range(1, w):
                tot = tot + zbuf[base - i:base - i + chunk, sl]
            pos = pos0 + c * chunk + lax.broadcasted_iota(jnp.int32, (chunk, PGW), 0)
            cnt = jnp.minimum(w, pos + 1).astype(F32)
            mixed = tot / cnt - cur
            y = jnp.dot(mixed.astype(BF16), w_ref[g], preferred_element_type=F32) * s_ref[:, sl]
            o_ref[0, c * chunk:(c + 1) * chunk, sl] = y.astype(o_ref.dtype)


def _pool_mix(hist, z, z_spec, n_seq, t_len, w_pool, pool_scale, pos0):
    chunk = min(t_len, 256)
    return pl.pallas_call(
        functools.partial(_pool_kernel, t_len=t_len, pos0=pos0, chunk=chunk),
        grid=(n_seq,),
        in_specs=[
            pl.BlockSpec((1, POOL_HIST, POOL_WIDTH), lambda i: (i, 0, 0)),
            z_spec,
            pl.BlockSpec(w_pool.shape, lambda i: (0, 0, 0)),
            pl.BlockSpec((1, POOL_WIDTH), lambda i: (0, 0)),
        ],
        out_specs=pl.BlockSpec((1, t_len, POOL_WIDTH), lambda i: (i, 0, 0)),
        out_shape=jax.ShapeDtypeStruct((n_seq, t_len, POOL_WIDTH), BF16),
        scratch_shapes=[pltpu.VMEM((POOL_HIST + t_len, POOL_WIDTH), F32)],
        compiler_params=_cparams("arbitrary"),
        name="pool_mix",
    )(hist, z, w_pool, pool_scale)


def _merge_kernel(h_ref, a_ref, p_ref, ga_ref, gb_ref, wa_ref, wb_ref, wo_ref, g2_ref, rwh_ref, rwl_ref, rb_ref,
                  h1_ref, xn_ref, e_ref, gate_ref, rank_ref, cnt_ref, carry):
    ya = jnp.dot(a_ref[...], wa_ref[...], preferred_element_type=F32)
    yb = jnp.dot(p_ref[...], wb_ref[...], preferred_element_type=F32)
    u = jax.nn.sigmoid(ga_ref[...]) * ya + jax.nn.sigmoid(gb_ref[...]) * yb
    h1 = h_ref[...] + jnp.dot(u.astype(BF16), wo_ref[...], preferred_element_type=F32)
    h1_ref[...] = h1
    xn = _rmsnorm_f32(h1, g2_ref[...])
    half = xn.shape[1] // 2
    lo = _bf16_bits(xn[:, :half])
    hi = _bf16_bits(xn[:, half:])
    xn_ref[...] = (hi & jnp.uint32(0xFFFF0000)) | (lo >> 16)
    _route_tile(xn, rwh_ref, rwl_ref, rb_ref, e_ref, gate_ref, rank_ref, cnt_ref, carry)


def _bf16_bits(x):
    return lax.bitcast_convert_type(x.astype(BF16).astype(F32), jnp.uint32)


def _merge(h, attn, pool, proj, wa, wb, wo, g2, rwh, rwl, rb, tm):
    m, d = h.shape
    n_e = rwh.shape[1]

    def const(a):
        return pl.BlockSpec(a.shape, lambda i, nd=a.ndim: (0,) * nd, pipeline_mode=pl.Buffered(1))

    tok_spec = pl.BlockSpec((tm, TOP_K), lambda i: (i, 0))
    return pl.pallas_call(
        _merge_kernel,
        grid=(m // tm,),
        in_specs=[
            pl.BlockSpec((tm, d), lambda i: (i, 0)),
            pl.BlockSpec((tm, attn.shape[1]), lambda i: (i, 0)),
            pl.BlockSpec((tm, pool.shape[1]), lambda i: (i, 0)),
            pl.BlockSpec((tm, d), lambda i: (i, 0)),
            pl.BlockSpec((tm, d), lambda i: (i, 1)),
            const(wa), const(wb), const(wo), const(g2), const(rwh), const(rwl), const(rb),
        ],
        out_specs=[pl.BlockSpec((tm, d), lambda i: (i, 0)),
                   pl.BlockSpec((tm, d // 2), lambda i: (i, 0)),
                   tok_spec, tok_spec, tok_spec, pl.BlockSpec((1, n_e), lambda i: (0, 0))],
        out_shape=[jax.ShapeDtypeStruct((m, d), F32), jax.ShapeDtypeStruct((m, d // 2), jnp.uint32),
                   jax.ShapeDtypeStruct((m, TOP_K), jnp.int32), jax.ShapeDtypeStruct((m, TOP_K), F32),
                   jax.ShapeDtypeStruct((m, TOP_K), jnp.int32), jax.ShapeDtypeStruct((1, n_e), jnp.int32)],
        scratch_shapes=[pltpu.VMEM((1, n_e), F32)],
        compiler_params=_cparams("arbitrary"),
        name="merge_route",
    )(h, attn, pool, proj, proj, wa, wb, wo, g2, rwh, rwl, rb)


def _route_tile(x, wh_ref, wl_ref, b_ref, e_ref, gate_ref, rank_ref, cnt_ref, carry):
    @pl.when(pl.program_id(0) == 0)
    def _():
        carry[...] = jnp.zeros_like(carry)

    xh = x.astype(BF16)
    xl = (x - xh.astype(F32)).astype(BF16)
    logits = (jnp.dot(xh, wh_ref[...], preferred_element_type=F32)
              + jnp.dot(xh, wl_ref[...], preferred_element_type=F32)
              + jnp.dot(xl, wh_ref[...], preferred_element_type=F32)) + b_ref[...]
    n_e = logits.shape[-1]
    lane = lax.broadcasted_iota(jnp.int32, logits.shape, 1)
    vals = logits
    tops, idxs = [], []
    for _ in range(TOP_K):
        m = jnp.max(vals, axis=-1, keepdims=True)
        idx = jnp.min(jnp.where(vals == m, lane, n_e), axis=-1, keepdims=True)
        tops.append(m)
        idxs.append(idx)
        vals = jnp.where(lane == idx, -jnp.inf, vals)
    ex = [jnp.exp(t - tops[0]) for t in tops]
    den = ex[0] + ex[1] + ex[2] + ex[3]
    for k in range(TOP_K):
        e_ref[:, k:k + 1] = idxs[k]
        gate_ref[:, k:k + 1] = ex[k] / den

    onehots = [(lane == idx).astype(F32) for idx in idxs]
    cnt = onehots[0] + onehots[1] + onehots[2] + onehots[3]
    tm = cnt.shape[0]
    lower = (lax.broadcasted_iota(jnp.int32, (tm, tm), 0) > lax.broadcasted_iota(jnp.int32, (tm, tm), 1))
    prefix = jnp.dot(lower.astype(BF16), cnt.astype(BF16), preferred_element_type=F32)
    base = carry[...] + prefix
    for k in range(TOP_K):
        rank_ref[:, k:k + 1] = jnp.sum(onehots[k] * base, axis=-1, keepdims=True).astype(jnp.int32)
    carry[...] = carry[...] + jnp.sum(cnt, axis=0, keepdims=True)
    cnt_ref[...] = carry[...].astype(jnp.int32)


def _row_scatter_kernel(dest_ref, x_ref, init_hbm, o_hbm, sem, *, rows):
    del init_hbm
    t = pl.program_id(0)
    n_copies = rows * TOP_K
    tokens_per_iter = DMA_ISSUE_UNROLL // TOP_K

    def body(j, carry):
        for u in range(DMA_ISSUE_UNROLL):
            src = pl.ds(j * tokens_per_iter + u // TOP_K, 1)
            dst = pl.ds(dest_ref[t * n_copies + j * DMA_ISSUE_UNROLL + u], 1)
            pltpu.make_async_copy(x_ref.at[src], o_hbm.at[dst], sem.at[0]).start(priority=u % 2)
        return carry

    lax.fori_loop(0, n_copies // DMA_ISSUE_UNROLL, body, 0)
    for _ in range(TOP_K):
        pltpu.make_async_copy(x_ref, o_hbm.at[pl.ds(0, rows)], sem.at[0]).wait()


def _row_scatter(dest, src, n_rows, rows):
    m, width = src.shape
    assert DMA_ISSUE_UNROLL % TOP_K == 0 and m % rows == 0
    grid_spec = pltpu.PrefetchScalarGridSpec(
        num_scalar_prefetch=1,
        grid=(m // rows,),
        in_specs=[pl.BlockSpec((rows, width), lambda t, *_: (t, 0)), pl.BlockSpec(memory_space=pl.ANY)],
        out_specs=pl.BlockSpec(memory_space=pl.ANY),
        scratch_shapes=[pltpu.SemaphoreType.DMA((1,))],
    )
    return pl.pallas_call(
        functools.partial(_row_scatter_kernel, rows=rows),
        grid_spec=grid_spec,
        out_shape=jax.ShapeDtypeStruct((n_rows, width), src.dtype),
        input_output_aliases={2: 0},
        compiler_params=_cparams("arbitrary"),
        name="moe_row_scatter",
    )(dest, src, jnp.zeros((n_rows, width), src.dtype))


def _start_all(copies):
    for c in copies:
        c.start()


def _wait_all(copies):
    for c in copies:
        c.wait()


def _start_weight_pairs(piece, step, slot, q_lo, q_hi):
    def body(q, carry):
        piece(step, slot, 2 * q).start(priority=0)
        piece(step, slot, 2 * q + 1).start(priority=1)
        return carry
    lax.fori_loop(q_lo, q_hi, body, 0)


def _wait_step_weights(step, piece, n_pieces, whole):
    slot = step % 2

    @pl.when(step == 0)
    def _():
        _start_weight_pairs(piece, step, slot, 0, n_pieces // 2)

    _wait_all(whole(step, slot))
    return slot


def _expert_chunk_loop(step, n_steps, n_chunks, next_n_chunks, in_copies, next_in_copies, out_copies, compute,
                       weight_piece, n_weight_pieces):
    def start_first_two(count, copies):
        @pl.when(count > 0)
        def _():
            _start_all(copies(0, 0))

        @pl.when(count > 1)
        def _():
            _start_all(copies(1, 1))

    n_pairs = n_weight_pieces // 2

    def next_weights(q_lo, q_hi):
        @pl.when(step + 1 < n_steps)
        def _():
            _start_weight_pairs(weight_piece, step + 1, (step + 1) % 2, q_lo, q_hi)

    @pl.when(step == 0)
    def _():
        start_first_two(n_chunks, in_copies)

    @pl.when(n_chunks > 0)
    def _():
        def chunk(r, first):
            in_slot = r % CHUNK_IN_SLOTS
            out_slot = r % 2

            @pl.when(r + 2 < n_chunks)
            def _():
                _start_all(in_copies(r + 2, (r + 2) % CHUNK_IN_SLOTS))

            next_weights(jnp.minimum(r * WEIGHT_PAIRS_PER_CHUNK, n_pairs),
                         jnp.minimum((r + 1) * WEIGHT_PAIRS_PER_CHUNK, n_pairs))
            _wait_all(in_copies(r, in_slot))

            if not first:
                @pl.when(r >= 2)
                def _():
                    _wait_all(out_copies(r - 2, out_slot))

            compute(in_slot, out_slot, first)
            _start_all(out_copies(r, out_slot))

        chunk(0, True)

        def body(r, carry):
            chunk(r, False)
            return carry

        lax.fori_loop(1, n_chunks, body, 0)

    @pl.when(step + 1 < n_steps)
    def _():
        start_first_two(next_n_chunks, next_in_copies)

    next_weights(jnp.minimum(n_chunks * WEIGHT_PAIRS_PER_CHUNK, n_pairs), n_pairs)

    @pl.when(n_chunks >= 2)
    def _():
        _wait_all(out_copies(n_chunks - 2, n_chunks % 2))

    @pl.when(n_chunks >= 1)
    def _():
        _wait_all(out_copies(n_chunks - 1, (n_chunks - 1) % 2))


def _zero_fill_tiles(first, last, zbuf, dst, sem):
    zbuf[...] = jnp.zeros_like(zbuf)

    def copy(t):
        return pltpu.make_async_copy(zbuf, dst(t), sem)

    def start(t, carry):
        copy(t).start()
        return carry

    def wait(t, carry):
        copy(t).wait()
        return carry

    lax.fori_loop(first, last, start, 0)
    lax.fori_loop(first, last, wait, 0)


def _moe_up_kernel(tpe_ref, start_ref, xs_hbm, w_hbm, bg_ref, bu_ref, hid_hbm,
                   wbuf, wg_sc, wu_sc, xbuf, obuf, w_sem, in_sem, out_sem, *, rows):
    e = pl.program_id(0)
    n = pl.program_id(1)
    n_chunks = pl.num_programs(1)
    row0 = start_ref[e]
    d, nc = wg_sc.shape

    piece_rows = d // DMA_SPLIT
    n_pieces = 2 * DMA_SPLIT

    def w_piece(step, slot, p):
        e_, n_ = step // n_chunks, step % n_chunks
        mat, blk = p // DMA_SPLIT, p % DMA_SPLIT
        cols = pl.ds(pl.multiple_of((mat * n_chunks + n_) * nc, nc), nc)
        rws = pl.ds(pl.multiple_of(blk * piece_rows, piece_rows), piece_rows)
        return pltpu.make_async_copy(w_hbm.at[e_, rws, cols], wbuf.at[slot, mat, rws], w_sem.at[slot])

    def w_whole(step, slot):
        e_, n_ = step // n_chunks, step % n_chunks
        return [pltpu.make_async_copy(
            w_hbm.at[e_, :, pl.ds(pl.multiple_of((mat * n_chunks + n_) * nc, nc), nc)], wbuf.at[slot, mat],
            w_sem.at[slot]) for mat in range(2)]

    step, n_steps = e * n_chunks + n, pl.num_programs(0) * n_chunks
    w_slot = _wait_step_weights(step, w_piece, n_pieces, w_whole)
    next_e = jnp.minimum((step + 1) // n_chunks, pl.num_programs(0) - 1)

    def chunk_rows(first_row, r):
        return pl.ds(pl.multiple_of(first_row + r * rows, rows), rows)

    def in_copies_from(first_row):
        def in_copies(r, slot):
            return [pltpu.make_async_copy(xs_hbm.at[chunk_rows(first_row, r)], xbuf.at[slot], in_sem.at[slot])]
        return in_copies

    def out_copies(r, slot):
        return [pltpu.make_async_copy(obuf.at[slot], hid_hbm.at[n, chunk_rows(row0, r)], out_sem.at[slot])]

    half = d // 2

    def compute(in_slot, out_slot, first):
        words = xbuf[in_slot]
        x_lo = lax.bitcast_convert_type(words << 16, F32).astype(BF16)
        x_hi = lax.bitcast_convert_type(words & jnp.uint32(0xFFFF0000), F32).astype(BF16)

        def x_dot(w_ref, cs):
            return (jnp.dot(x_lo, w_ref[:half, cs], preferred_element_type=F32)
                    + jnp.dot(x_hi, w_ref[half:, cs], preferred_element_type=F32))

        for c0 in range(0, nc, MOE_COL_BLOCK):
            cs = slice(c0, c0 + MOE_COL_BLOCK)
            if first:
                wg_sc[:, cs] = wbuf[w_slot, 0, :, cs].astype(BF16)
                wu_sc[:, cs] = wbuf[w_slot, 1, :, cs].astype(BF16)
            g = x_dot(wg_sc, cs) + bg_ref[0, :, cs]
            u = x_dot(wu_sc, cs) + bu_ref[0, :, cs]
            g = jnp.minimum(g, SWIGLU_LIMIT)
            u = jnp.clip(u, -SWIGLU_LIMIT, SWIGLU_LIMIT)
            obuf[out_slot, :, cs] = ((u + 1.0) * (g * jax.nn.sigmoid(SWIGLU_ALPHA * g))).astype(obuf.dtype)

    _expert_chunk_loop(step, n_steps, tpe_ref[e], tpe_ref[next_e], in_copies_from(row0),
                       in_copies_from(start_ref[next_e]), out_copies, compute, w_piece, n_pieces)

    @pl.when(e == pl.num_programs(0) - 1)
    def _():
        _zero_fill_tiles(row0 // rows + tpe_ref[e], hid_hbm.shape[1] // rows, obuf.at[0],
                         lambda t: hid_hbm.at[n, pl.ds(pl.multiple_of(t * rows, rows), rows)], out_sem.at[0])


def _moe_up(tiles_per_e, row_start, xs, w_gu, b_gu, rows, nc):
    r_pad = xs.shape[0]
    d = w_gu.shape[1]
    n_e = w_gu.shape[0]
    d_ff = w_gu.shape[2] // 2
    n_chunks = d_ff // nc
    any_spec = pl.BlockSpec(memory_space=pl.ANY)
    grid_spec = pltpu.PrefetchScalarGridSpec(
        num_scalar_prefetch=2,
        grid=(n_e, n_chunks),
        in_specs=[
            any_spec,
            any_spec,
            pl.BlockSpec((1, 1, nc), lambda e, n, *_: (e, 0, n)),
            pl.BlockSpec((1, 1, nc), lambda e, n, *_: (e, 0, n_chunks + n)),
        ],
        out_specs=any_spec,
        scratch_shapes=[pltpu.VMEM((2, 2, d, nc), w_gu.dtype), pltpu.VMEM((d, nc), BF16), pltpu.VMEM((d, nc), BF16),
                        pltpu.VMEM((CHUNK_IN_SLOTS, rows, d // 2), xs.dtype), pltpu.VMEM((2, rows, nc), BF16),
                        pltpu.SemaphoreType.DMA((2,)), pltpu.SemaphoreType.DMA((CHUNK_IN_SLOTS,)),
                        pltpu.SemaphoreType.DMA((2,))],
    )
    return pl.pallas_call(
        functools.partial(_moe_up_kernel, rows=rows),
        grid_spec=grid_spec,
        out_shape=jax.ShapeDtypeStruct((n_chunks, r_pad, nc), BF16),
        compiler_params=_cparams("arbitrary", "arbitrary"),
        name="moe_up",
    )(tiles_per_e, row_start, xs, w_gu, b_gu, b_gu)


def _moe_down_kernel(tpe_ref, start_ref, hid_hbm, w_hbm, b_ref, y_hbm,
                     wbuf, w_sc, xbuf, obuf, w_sem, in_sem, out_sem, *, rows):
    e = pl.program_id(0)
    n = pl.program_id(1)
    n_chunks = pl.num_programs(1)
    row0 = start_ref[e]
    k_chunks, _, kc = xbuf.shape[1:]
    d_ff, nc = w_sc.shape

    piece_rows = d_ff // DMA_SPLIT

    def w_piece(step, slot, p):
        e_, n_ = step // n_chunks, step % n_chunks
        cols = pl.ds(pl.multiple_of(n_ * nc, nc), nc)
        rws = pl.ds(pl.multiple_of(p * piece_rows, piece_rows), piece_rows)
        return pltpu.make_async_copy(w_hbm.at[e_, rws, cols], wbuf.at[slot, rws], w_sem.at[slot])

    def w_whole(step, slot):
        e_, n_ = step // n_chunks, step % n_chunks
        return [pltpu.make_async_copy(w_hbm.at[e_, :, pl.ds(pl.multiple_of(n_ * nc, nc), nc)], wbuf.at[slot],
                                      w_sem.at[slot])]

    step, n_steps = e * n_chunks + n, pl.num_programs(0) * n_chunks
    w_slot = _wait_step_weights(step, w_piece, DMA_SPLIT, w_whole)
    next_e = jnp.minimum((step + 1) // n_chunks, pl.num_programs(0) - 1)

    def in_copies_from(first_row):
        def in_copies(r, slot):
            chunk = pl.ds(pl.multiple_of(first_row + r * rows, rows), rows)
            return [pltpu.make_async_copy(hid_hbm.at[:, chunk], xbuf.at[slot], in_sem.at[slot])]
        return in_copies

    lanes_per_row = nc // LANES
    tile_rows = rows * lanes_per_row

    def out_copies(r, slot):
        dst = pl.ds(pl.multiple_of((row0 + r * rows) * lanes_per_row, tile_rows), tile_rows)
        return [pltpu.make_async_copy(obuf.at[slot], y_hbm.at[dst], out_sem.at[slot])]

    def compute(in_slot, out_slot, first):
        for c0 in range(0, nc, MOE_COL_BLOCK):
            cs = slice(c0, c0 + MOE_COL_BLOCK)
            if first:
                w_sc[:, cs] = wbuf[w_slot, :, cs].astype(BF16)
            acc = b_ref[0, :, cs] + jnp.dot(xbuf[in_slot, 0], w_sc[0:kc, cs], preferred_element_type=F32)
            for c in range(1, k_chunks):
                acc = acc + jnp.dot(xbuf[in_slot, c], w_sc[c * kc:(c + 1) * kc, cs], preferred_element_type=F32)
            for j in range(MOE_COL_BLOCK // LANES):
                lane_group = c0 // LANES + j
                obuf[out_slot, pl.ds(lane_group, rows, stride=lanes_per_row), :] = acc[:, j * LANES:(j + 1) * LANES]

    _expert_chunk_loop(step, n_steps, tpe_ref[e], tpe_ref[next_e], in_copies_from(row0),
                       in_copies_from(start_ref[next_e]), out_copies, compute, w_piece, DMA_SPLIT)

    @pl.when(e == pl.num_programs(0) - 1)
    def _():
        _zero_fill_tiles(row0 // rows + tpe_ref[e], y_hbm.shape[0] // tile_rows, obuf.at[0],
                         lambda t: y_hbm.at[pl.ds(pl.multiple_of(t * tile_rows, tile_rows), tile_rows)], out_sem.at[0])


def _moe_down(tiles_per_e, row_start, hid, w_dn, b_dn, rows):
    k_chunks, r_pad, kc = hid.shape
    n_e, d_ff, d = w_dn.shape
    nc = d
    n_chunks = 1
    any_spec = pl.BlockSpec(memory_space=pl.ANY)
    grid_spec = pltpu.PrefetchScalarGridSpec(
        num_scalar_prefetch=2,
        grid=(n_e, n_chunks),
        in_specs=[
            any_spec,
            any_spec,
            pl.BlockSpec((1, 1, nc), lambda e, n, *_: (e, 0, n)),
        ],
        out_specs=any_spec,
        scratch_shapes=[pltpu.VMEM((2, d_ff, nc), w_dn.dtype), pltpu.VMEM((d_ff, nc), BF16),
                        pltpu.VMEM((CHUNK_IN_SLOTS, k_chunks, rows, kc), hid.dtype),
                        pltpu.VMEM((2, rows * nc // LANES, LANES), F32),
                        pltpu.SemaphoreType.DMA((2,)), pltpu.SemaphoreType.DMA((CHUNK_IN_SLOTS,)),
                        pltpu.SemaphoreType.DMA((2,))],
    )
    return pl.pallas_call(
        functools.partial(_moe_down_kernel, rows=rows),
        grid_spec=grid_spec,
        out_shape=jax.ShapeDtypeStruct((r_pad * d // LANES, LANES), F32),
        compiler_params=_cparams("arbitrary", "arbitrary"),
        name="moe_down",
    )(tiles_per_e, row_start, hid, w_dn, b_dn)


def _combine_kernel(dest_ref, h_ref, gate_ref, gf_ref, y_hbm, *rest, rows, final_norm, n_head_tiles, split):
    if split:
        o_ref, o_tail_ref, buf, sem = rest
    else:
        (o_ref, buf, sem), o_tail_ref = rest, None
    t = pl.program_id(0)
    n_t = pl.num_programs(0)
    n_dma = rows * TOP_K
    d = o_ref.shape[1]
    groups = d // LANES

    def issue(tile, slot):
        def body(j, carry):
            for u in range(DMA_ISSUE_UNROLL):
                i = j * DMA_ISSUE_UNROLL + u
                src = pl.ds(pl.multiple_of(dest_ref[tile * n_dma + i], groups), groups)
                dst = pl.ds(pl.multiple_of(i * groups, groups), groups)
                pltpu.make_async_copy(y_hbm.at[src], buf.at[slot, dst], sem.at[slot]).start(priority=u % 2)
            return carry
        lax.fori_loop(0, n_dma // DMA_ISSUE_UNROLL, body, 0)

    @pl.when(t == 0)
    def _():
        issue(0, 0)

    @pl.when(t + 1 < n_t)
    def _():
        issue(t + 1, (t + 1) % 2)

    slot = t % 2
    pltpu.make_async_copy(y_hbm.at[pl.ds(0, n_dma * groups)], buf.at[slot], sem.at[slot]).wait()

    def finish(o_ref):
        gates = [gate_ref[:, k:k + 1] for k in range(TOP_K)]
        sq = jnp.zeros((rows, 1), F32)
        for j in range(groups):
            cs = slice(j * LANES, (j + 1) * LANES)
            acc = h_ref[:, cs]
            for k in range(TOP_K):
                acc = acc + gates[k] * buf[slot, pl.ds(k * rows * groups + j, rows, stride=groups), :]
            o_ref[:, cs] = acc
            sq = sq + jnp.sum(acc * acc, axis=-1, keepdims=True)
        if final_norm:
            o_ref[...] = o_ref[...] * lax.rsqrt(sq / d + NORM_EPS) * gf_ref[...]

    if o_tail_ref is None:
        finish(o_ref)
    else:
        pl.when(t < n_head_tiles)(lambda: finish(o_ref))
        pl.when(t >= n_head_tiles)(lambda: finish(o_tail_ref))


def _combine(dest_km, h1, gate, gf, y, rows, final_norm, split_at=None):
    m, d = h1.shape
    n_head = m // rows if split_at is None else split_at // rows
    out_specs = [pl.BlockSpec((rows, d), lambda t, *_: (jnp.minimum(t, n_head - 1), 0))]
    out_shape = [jax.ShapeDtypeStruct((n_head * rows, d), F32)]
    if split_at is not None:
        assert split_at % rows == 0 and (m - split_at) % rows == 0
        out_specs.append(pl.BlockSpec((rows, d), lambda t, *_: (jnp.maximum(t - n_head, 0), 0)))
        out_shape.append(jax.ShapeDtypeStruct((m - split_at, d), F32))
    grid_spec = pltpu.PrefetchScalarGridSpec(
        num_scalar_prefetch=1,
        grid=(m // rows,),
        in_specs=[
            pl.BlockSpec((rows, d), lambda t, *_: (t, 0)),
            pl.BlockSpec((rows, TOP_K), lambda t, *_: (t, 0)),
            pl.BlockSpec((1, d), lambda t, *_: (0, 0)),
            pl.BlockSpec(memory_space=pl.ANY),
        ],
        out_specs=out_specs,
        scratch_shapes=[pltpu.VMEM((2, rows * TOP_K * d // LANES, LANES), F32), pltpu.SemaphoreType.DMA((2,))],
    )
    return pl.pallas_call(
        functools.partial(_combine_kernel, rows=rows, final_norm=final_norm, n_head_tiles=n_head,
                          split=split_at is not None),
        grid_spec=grid_spec,
        out_shape=out_shape,
        compiler_params=_cparams("arbitrary"),
        name="moe_combine",
    )(dest_km, h1, gate, gf, y)


def _routing_tables(top_e, rank, counts, rows):
    n_experts = counts.shape[0]
    tiles_per_e = (counts + rows - 1) // rows
    tile_end = jnp.cumsum(tiles_per_e)
    pad_start = (tile_end - tiles_per_e) * rows
    experts = jnp.arange(n_experts, dtype=jnp.int32)
    dest = rank + jnp.sum(jnp.where(top_e[..., None] == experts, pad_start, 0), axis=-1)
    dest = dest.astype(jnp.int32).reshape(-1)
    return dest, tiles_per_e.astype(jnp.int32), pad_start.astype(jnp.int32)


def _largest_divisor(n, cap, mult):
    best = mult
    for c in range(mult, cap + 1, mult):
        if n % c == 0:
            best = c
    return best


def kernel(x_prompt, x_sample, cache_kv_w128, cache_kv_w512, cache_kv_w2048, state_pool, rel_bias, norm1, w_in,
           w_branch_a, w_branch_b, w_out, w_pool, pool_scale, norm2, router_w, router_b, w_gate_up, b_gate_up,
           w_down, b_down, norm_f):
    batch, seq, d = x_prompt.shape
    db, t_new, _ = x_sample.shape
    depth = norm1.shape[0]
    n_experts = router_w.shape[-1]
    caches = (cache_kv_w128, cache_kv_w512, cache_kv_w2048)
    m_p, m_s = batch * seq, db * t_new
    m = m_p + m_s
    tm = _largest_divisor(m, 640, 8)

    h = jnp.concatenate([x_prompt.reshape(m_p, d), x_sample.reshape(m_s, d)], axis=0)
    bias_p = _prompt_bias_tiles(rel_bias)
    bias_sc, bias_sn = _sample_bias_tiles(rel_bias, t_new)

    cuts = [ATTN_WIDTH, 2 * ATTN_WIDTH, 3 * ATTN_WIDTH, 3 * ATTN_WIDTH + POOL_WIDTH, 3 * ATTN_WIDTH + POOL_WIDTH + d]
    qkv0 = 2 * d
    z0 = qkv0 + 3 * ATTN_WIDTH

    kv_p = [[] for _ in range(N_GROUPS)]
    kv_s = [[] for _ in range(N_GROUPS)]
    pool_p, pool_s = [], []
    for l in range(depth):
        wq, wk, wv, wz, wga, wgb = jnp.split(w_in[l], cuts, axis=-1)
        w_perm = jnp.concatenate([wga, wgb, wq, wk, wv, wz], axis=-1).astype(BF16)
        proj = _inproj(h, norm1[l][None, :], w_perm, tm, _largest_divisor(w_perm.shape[1], 1536, 128))

        k0, v0 = qkv0 + ATTN_WIDTH, qkv0 + 2 * ATTN_WIDTH
        kv_w = HEADS_PER_GROUP * HEAD_DIM
        proj_s = proj[m_p:]
        z_s = proj_s[:, z0:z0 + POOL_WIDTH].reshape(db, t_new, POOL_WIDTH)

        attn_p = _prompt_attn(proj, bias_p, batch, seq, qkv0 // HEAD_DIM)
        qkv_s = proj_s[:, qkv0:qkv0 + 3 * ATTN_WIDTH].reshape(db, t_new, 3 * ATTN_WIDTH)
        c0 = caches[0][l].reshape(db, -1, HEAD_DIM)
        c1 = caches[1][l].reshape(db, -1, HEAD_DIM)
        dil2 = ATTN_GROUPS[2][1]
        c2v = caches[2][l].reshape(db, -1, dil2 * CACHE_ROW_TILE, HEAD_DIM)

        news = []
        for g, (window, _) in enumerate(ATTN_GROUPS):
            keep = min(window, seq)
            kc, vc = k0 + g * kv_w, v0 + g * kv_w

            def kv_rows(a, r0, r1, kc=kc, vc=vc):
                return jnp.stack([a[r0:r1, kc:kc + kv_w].reshape(-1, HEADS_PER_GROUP, HEAD_DIM),
                                  a[r0:r1, vc:vc + kv_w].reshape(-1, HEADS_PER_GROUP, HEAD_DIM)], axis=1)

            kv_p[g].append(jnp.stack([kv_rows(proj, (b + 1) * seq - keep, (b + 1) * seq) for b in range(batch)]))
            news.append(kv_rows(proj_s, 0, m_s).reshape(db, t_new * CACHE_ROW_TILE, HEAD_DIM))
        news[2] = news[2][:, None]
        attn_s, *rolled = _sample_attn(qkv_s, c0, c1, c2v, news, bias_sc, bias_sn)
        attn = jnp.concatenate([attn_p, attn_s.reshape(m_s, ATTN_OUT_WIDTH)], axis=0)
        for g in range(N_GROUPS):
            kv_s[g].append(rolled[g].reshape(caches[g][l].shape))

        w_pool_b = w_pool[l].astype(BF16)
        scale = pool_scale[l][None, :]
        hist_p = jnp.zeros((batch, POOL_HIST, POOL_WIDTH), F32)
        hist_s = jnp.concatenate([jnp.zeros((db, POOL_HIST - POOL_BUF, POOL_WIDTH), F32), state_pool[l]], axis=1)
        pool = jnp.concatenate([
            _pool_mix(hist_p, proj, pl.BlockSpec((seq, POOL_WIDTH), lambda i: (i, z0 // POOL_WIDTH)),
                      batch, seq, w_pool_b, scale, 0).reshape(m_p, POOL_WIDTH),
            _pool_mix(hist_s, z_s, pl.BlockSpec((1, t_new, POOL_WIDTH), lambda i: (i, 0, 0)),
                      db, t_new, w_pool_b, scale, PAST_LEN).reshape(m_s, POOL_WIDTH),
        ], axis=0)
        pool_p.append(jnp.stack([proj[(b + 1) * seq - POOL_BUF:(b + 1) * seq, z0:z0 + POOL_WIDTH]
                                 for b in range(batch)]))
        pool_s.append(jnp.concatenate([state_pool[l], z_s], axis=1)[:, t_new:])

        rw = router_w[l]
        rw_hi = rw.astype(BF16)
        rw_lo = (rw - rw_hi.astype(F32)).astype(BF16)
        h1, xn_rows, top_e, gate, rank, counts = _merge(
            h, attn, pool, proj, w_branch_a[l].astype(BF16), w_branch_b[l].astype(BF16), w_out[l].astype(BF16),
            norm2[l][None, :], rw_hi, rw_lo, router_b[l][None, :], _largest_divisor(m, 320, 8))

        rows = MOE_ROW_TILE
        groups = d // LANES
        n_tiles = (m * TOP_K + n_experts * (rows - 1)) // rows
        dest, tiles_per_e, row_start = _routing_tables(top_e, rank, counts[0], rows)
        xs = _row_scatter(dest, xn_rows, n_tiles * rows, _largest_divisor(m, 256, 8))
        hid = _moe_up(tiles_per_e, row_start, xs, w_gate_up[l], b_gate_up[l][:, None, :], rows, 1024)
        y = _moe_down(tiles_per_e, row_start, hid, w_down[l], b_down[l][:, None, :], rows)
        crow = _largest_divisor(math.gcd(m_p, m_s), 128, 8)
        dest_km = (dest * groups).reshape(m // crow, crow, TOP_K).transpose(0, 2, 1).reshape(-1)
        if l == depth - 1:
            h_p, h_s = _combine(dest_km, h1, gate, norm_f[None, :], y, crow, final_norm=True, split_at=m_p)
        else:
            h, = _combine(dest_km, h1, gate, norm_f[None, :], y, crow, final_norm=False)

    y_prompt = h_p.reshape(batch, seq, d)
    y_sample = h_s.reshape(db, t_new, d)
    return (y_prompt, y_sample,
            jnp.stack(kv_p[0]), jnp.stack(kv_p[1]), jnp.stack(kv_p[2]), jnp.stack(pool_p),
            jnp.stack(kv_s[0]), jnp.stack(kv_s[1]), jnp.stack(kv_s[2]), jnp.stack(pool_s))
```

```python
import functools
import math

import jax
import jax.numpy as jnp
from jax import lax
from jax.experimental import pallas as pl
from jax.experimental.pallas import tpu as pltpu

F32 = jnp.float32
BF16 = jnp.bfloat16

HEAD_DIM = 128
HEADS_PER_GROUP = 4
ATTN_GROUPS = ((128, 1), (512, 4), (2048, 16))
N_GROUPS = len(ATTN_GROUPS)
N_ATTN_HEADS = N_GROUPS * HEADS_PER_GROUP
ATTN_WIDTH = N_ATTN_HEADS * HEAD_DIM
ATTN_OUT_WIDTH = HEADS_PER_GROUP * HEAD_DIM
ATTN_SCALE = HEAD_DIM ** -0.5
N_BACK = 128
N_REL_BUCKETS = 32
REL_MAX_DIST = 2048
POOL_WINDOWS = (2, 4, 8, 16)
POOL_GROUPS = len(POOL_WINDOWS)
PGW = 128
POOL_WIDTH = POOL_GROUPS * PGW
POOL_BUF = max(POOL_WINDOWS) - 1
POOL_HIST = 16
TOP_K = 4
SWIGLU_LIMIT = 7.0
SWIGLU_ALPHA = 1.702
NORM_EPS = 1e-5
NEG_INF = -1e30
PAST_LEN = 8192
CACHE_ROW_TILE = 2 * HEADS_PER_GROUP

VMEM_LIMIT_BYTES = 56 * 1024 * 1024
MOE_ROW_TILE = 256
DMA_ISSUE_UNROLL = 8
DMA_SPLIT = 8
ATTN_INTERLEAVE = 8
LANES = 128
CHUNK_IN_SLOTS = 3
MOE_COL_BLOCK = 512
WEIGHT_PAIRS_PER_CHUNK = 2


def _cparams(*sem):
    return pltpu.CompilerParams(dimension_semantics=sem, vmem_limit_bytes=VMEM_LIMIT_BYTES)


def _rmsnorm_f32(x, g):
    return x * lax.rsqrt(jnp.mean(x * x, axis=-1, keepdims=True) + NORM_EPS) * g


def _inproj_kernel(x_ref, g_ref, w_ref, o_ref):
    xn = _rmsnorm_f32(x_ref[...], g_ref[...]).astype(BF16)
    o_ref[...] = jnp.dot(xn, w_ref[...], preferred_element_type=F32)


def _inproj(x, g, w, tm, tn):
    m, d = x.shape
    n = w.shape[1]
    return pl.pallas_call(
        _inproj_kernel,
        grid=(n // tn, m // tm),
        in_specs=[
            pl.BlockSpec((tm, d), lambda j, i: (i, 0)),
            pl.BlockSpec((1, d), lambda j, i: (0, 0)),
            pl.BlockSpec((d, tn), lambda j, i: (0, j)),
        ],
        out_specs=pl.BlockSpec((tm, tn), lambda j, i: (i, j)),
        out_shape=jax.ShapeDtypeStruct((m, n), F32),
        compiler_params=_cparams("arbitrary", "arbitrary"),
        name="inproj",
    )(x, g, w)


def _t5_bucket(dist):
    max_exact = N_REL_BUCKETS // 2
    df = jnp.maximum(dist, 1).astype(F32)
    large = max_exact + (jnp.log(df / max_exact) / math.log(REL_MAX_DIST / max_exact)
                         * (N_REL_BUCKETS - max_exact)).astype(jnp.int32)
    large = jnp.minimum(large, N_REL_BUCKETS - 1)
    return jnp.where(dist < max_exact, dist, large)


def _step_bias(rel_bias, g):
    _, dil = ATTN_GROUPS[g]
    buckets = _t5_bucket(jnp.arange(N_BACK + 1, dtype=jnp.int32) * dil)
    tab = rel_bias[buckets].astype(F32)
    return tab[:, g * HEADS_PER_GROUP:(g + 1) * HEADS_PER_GROUP].T


def _prompt_bias_tiles(rel_bias):
    nb = N_BACK
    period = 3 * nb - 1
    tiles = []
    for g in range(N_GROUPS):
        bj = _step_bias(rel_bias, g)
        h = bj.shape[0]
        pad = jnp.full((h, nb - 1), NEG_INF, F32)
        v = jnp.concatenate([pad, bj[:, ::-1], pad], axis=1)
        skew = jnp.tile(v, (1, nb + 1))[:, :nb * (period + 1)].reshape(h, nb, period + 1)
        tiles.append(skew[:, ::-1, :2 * nb])
    return jnp.stack(tiles, axis=1)


def _sample_bias_tiles(rel_bias, t_new):
    tq = jnp.arange(t_new)[:, None]
    outs = []
    new = []
    for g, (window, dil) in enumerate(ATTN_GROUPS):
        bj = _step_bias(rel_bias, g)
        if g < 2:
            p = jnp.arange(window)[None, :]
            dist = window + tq - p
            ok = (dist % dil == 0) & (dist // dil <= N_BACK)
            outs.append(jnp.where(ok[None], bj[:, jnp.clip(dist // dil, 0, N_BACK)], NEG_INF))
        else:
            col = jnp.arange(t_new * N_BACK)[None, :]
            blk, r = col // N_BACK, col % N_BACK
            ok = blk == tq
            outs.append(jnp.where(ok[None], bj[:, jnp.broadcast_to(N_BACK - r, (t_new, t_new * N_BACK))], NEG_INF))
        tk = jnp.arange(t_new)[None, :]
        d = tq - tk
        okn = (d >= 0) & (d % dil == 0) & (d // dil <= N_BACK)
        new.append(jnp.where(okn[None], bj[:, jnp.clip(d // dil, 0, N_BACK)], NEG_INF))
    return outs, jnp.stack(new, axis=0)


def _softmax_block(s, v):
    m = jnp.max(s, axis=-1, keepdims=True)
    p = jnp.exp(s - m)
    den = jnp.sum(p, axis=-1, keepdims=True)
    o = jnp.dot(p.astype(BF16), v, preferred_element_type=F32) / den
    return o, m + jnp.log(den)


def _prompt_attn_kernel(q0, k0, v0, q1, k1, v1, q2, k2, v2, bias_ref, o_ref, o_sc, l_sc, *, seq):
    qkv = ((q0, k0, v0), (q1, k1, v1), (q2, k2, v2))
    nb = N_BACK

    def rows(ref, start, size, dil):
        if dil == 1:
            return ref[pl.ds(start, size), :]
        return ref[pl.ds(start, size, stride=dil), :]

    def put(g, start, dil, o, lse):
        idx = pl.ds(start, nb) if dil == 1 else pl.ds(start, nb, stride=dil)
        o_sc[g, idx, :] = o
        l_sc[g, idx, :] = jnp.broadcast_to(lse, (nb, HEAD_DIM))

    dn = (((1,), (1,)), ((), ()))

    def attend(g, dil, q_starts, kv_starts, n_kv):
        q_ref, k_ref, v_ref = qkv[g]
        bias = bias_ref[0, g] if n_kv == 2 * nb else bias_ref[0, g, :, nb:]
        qs = [rows(q_ref, st, nb, dil).astype(BF16) for st in q_starts]
        ks = [rows(k_ref, st, n_kv, dil).astype(BF16) for st in kv_starts]
        ss = [lax.dot_general(q, k, dn, preferred_element_type=F32) * ATTN_SCALE + bias for q, k in zip(qs, ks)]
        ms = [jnp.max(s, axis=-1, keepdims=True) for s in ss]
        ps = [jnp.exp(s - m) for s, m in zip(ss, ms)]
        dens = [jnp.sum(p, axis=-1, keepdims=True) for p in ps]
        vs = [rows(v_ref, st, n_kv, dil).astype(BF16) for st in kv_starts]
        for st, p, v, m, den in zip(q_starts, ps, vs, ms, dens):
            o = jnp.dot(p.astype(BF16), v, preferred_element_type=F32) / den
            put(g, st, dil, o, m + jnp.log(den))

    for g, (_, dil) in enumerate(ATTN_GROUPS):
        n_blk = seq // (dil * nb)
        span = dil * nb
        for r0 in range(0, dil, ATTN_INTERLEAVE):
            starts = list(range(r0, min(r0 + ATTN_INTERLEAVE, dil)))
            attend(g, dil, starts, starts, nb)
        if n_blk > 1:
            per_iter = max(c for c in range(1, max(1, ATTN_INTERLEAVE // dil) + 1) if (n_blk - 1) % c == 0)

            def body(it, carry, g=g, dil=dil, span=span, per_iter=per_iter):
                q_starts, kv_starts = [], []
                for j in range(per_iter):
                    b = 1 + it * per_iter + j
                    for r in range(dil):
                        q_starts.append(b * span + r)
                        kv_starts.append((b - 1) * span + r)
                attend(g, dil, q_starts, kv_starts, 2 * nb)
                return carry

            lax.fori_loop(0, (n_blk - 1) // per_iter, body, 0)

    chunk = 256

    def comb(c, carry):
        sl = pl.ds(pl.multiple_of(c * chunk, chunk), chunk)
        l0, l1, l2 = l_sc[0, sl, :], l_sc[1, sl, :], l_sc[2, sl, :]
        mx = jnp.maximum(jnp.maximum(l0, l1), l2)
        e0, e1, e2 = jnp.exp(l0 - mx), jnp.exp(l1 - mx), jnp.exp(l2 - mx)
        num = e0 * o_sc[0, sl, :] + e1 * o_sc[1, sl, :] + e2 * o_sc[2, sl, :]
        o_ref[sl, :] = (num / (e0 + e1 + e2)).astype(o_ref.dtype)
        return carry

    lax.fori_loop(0, seq // chunk, comb, 0)


def _prompt_attn(p, bias, batch, seq, q_block0, kv_block0):
    def spec(kind, g):
        if kind == 0:
            base = q_block0 + g * HEADS_PER_GROUP
        else:
            base = kv_block0 + (2 * g + kind - 1) * HEADS_PER_GROUP
        return pl.BlockSpec((seq, HEAD_DIM), lambda b, i, base=base: (b, base + i))

    in_specs = []
    for g in range(N_GROUPS):
        in_specs += [spec(0, g), spec(1, g), spec(2, g)]
    in_specs.append(pl.BlockSpec((1, N_GROUPS, N_BACK, 2 * N_BACK), lambda b, i: (i, 0, 0, 0)))
    return pl.pallas_call(
        functools.partial(_prompt_attn_kernel, seq=seq),
        grid=(batch, HEADS_PER_GROUP),
        in_specs=in_specs,
        out_specs=pl.BlockSpec((seq, HEAD_DIM), lambda b, i: (b, i)),
        out_shape=jax.ShapeDtypeStruct((batch * seq, ATTN_OUT_WIDTH), BF16),
        scratch_shapes=[pltpu.VMEM((N_GROUPS, seq, HEAD_DIM), F32), pltpu.VMEM((N_GROUPS, seq, HEAD_DIM), F32)],
        compiler_params=_cparams("arbitrary", "arbitrary"),
        name="prompt_attn",
    )(*([p] * 9), bias)


def _sample_attn_kernel(qkv_ref, c0_ref, c1_ref, c2_ref, n0_ref, n1_ref, n2_ref, b0_ref, b1_ref, b2_ref, bn_ref,
                        o_ref, r0_ref, r1_ref, r2_ref, *, t_new):
    caches = (c0_ref, c1_ref, c2_ref)
    biases = (b0_ref, b1_ref, b2_ref)
    dn = (((1,), (1,)), ((), ()))

    shift = t_new * CACHE_ROW_TILE
    for c_ref, n_ref, r_ref in ((c0_ref, n0_ref, r0_ref), (c1_ref, n1_ref, r1_ref)):
        keep = c_ref.shape[1] - shift
        r_ref[0, :keep] = c_ref[0, shift:]
        r_ref[0, keep:] = n_ref[0]
    n_r, width = c2_ref.shape[1], c2_ref.shape[2]
    r2_ref[0, :, :width - shift] = c2_ref[0, :, shift:]
    r2_ref[0, :n_r - 1, width - shift:] = c2_ref[0, 1:, :shift]
    r2_ref[0, n_r - 1:, width - shift:] = n2_ref[0]

    for i in range(HEADS_PER_GROUP):
        outs, lses = [], []
        for g in range(N_GROUPS):
            h = g * HEADS_PER_GROUP + i
            q = qkv_ref[0, :, h * HEAD_DIM:(h + 1) * HEAD_DIM].astype(BF16)
            k_col = ATTN_WIDTH + (2 * g * HEADS_PER_GROUP + i) * HEAD_DIM
            v_col = k_col + HEADS_PER_GROUP * HEAD_DIM
            kn = qkv_ref[0, :, k_col:k_col + HEAD_DIM].astype(BF16)
            vn = qkv_ref[0, :, v_col:v_col + HEAD_DIM].astype(BF16)
            c_ref = caches[g]
            if g < 2:
                n_rows = c_ref.shape[1] // CACHE_ROW_TILE
                kc = c_ref[0, pl.ds(i, n_rows, stride=CACHE_ROW_TILE), :].astype(BF16)
                vc = c_ref[0, pl.ds(HEADS_PER_GROUP + i, n_rows, stride=CACHE_ROW_TILE), :].astype(BF16)
            else:
                kc = jnp.concatenate(
                    [c_ref[0, :, t * CACHE_ROW_TILE + i, :] for t in range(t_new)], axis=0).astype(BF16)
                vc = jnp.concatenate(
                    [c_ref[0, :, t * CACHE_ROW_TILE + HEADS_PER_GROUP + i, :] for t in range(t_new)],
                    axis=0).astype(BF16)
            sc = lax.dot_general(q, kc, dn, preferred_element_type=F32) * ATTN_SCALE + biases[g][i]
            sn = lax.dot_general(q, kn, dn, preferred_element_type=F32) * ATTN_SCALE + bn_ref[g, i]
            m = jnp.maximum(jnp.max(sc, axis=-1, keepdims=True), jnp.max(sn, axis=-1, keepdims=True))
            pc = jnp.exp(sc - m)
            pn = jnp.exp(sn - m)
            den = jnp.sum(pc, axis=-1, keepdims=True) + jnp.sum(pn, axis=-1, keepdims=True)
            o = (jnp.dot(pc.astype(BF16), vc, preferred_element_type=F32)
                 + jnp.dot(pn.astype(BF16), vn, preferred_element_type=F32)) / den
            outs.append(o)
            lses.append(m + jnp.log(den))
        mx = jnp.maximum(jnp.maximum(lses[0], lses[1]), lses[2])
        es = [jnp.exp(l - mx) for l in lses]
        num = es[0] * outs[0] + es[1] * outs[1] + es[2] * outs[2]
        o_ref[0, :, i * HEAD_DIM:(i + 1) * HEAD_DIM] = (num / (es[0] + es[1] + es[2])).astype(o_ref.dtype)


def _sample_attn(qkv_s, c0, c1, c2v, news, bias_c, bias_n):
    db, t_new, _ = qkv_s.shape

    def full(a):
        return pl.BlockSpec(a.shape, lambda b, nd=a.ndim: (0,) * nd)

    def per_seq(a):
        return pl.BlockSpec((1,) + a.shape[1:], lambda b, nd=a.ndim: (b,) + (0,) * (nd - 1))

    return pl.pallas_call(
        functools.partial(_sample_attn_kernel, t_new=t_new),
        grid=(db,),
        in_specs=[per_seq(qkv_s), per_seq(c0), per_seq(c1), per_seq(c2v), per_seq(news[0]), per_seq(news[1]),
                  per_seq(news[2]), full(bias_c[0]), full(bias_c[1]), full(bias_c[2]), full(bias_n)],
        out_specs=[pl.BlockSpec((1, t_new, ATTN_OUT_WIDTH), lambda b: (b, 0, 0)),
                   per_seq(c0), per_seq(c1), per_seq(c2v)],
        out_shape=[jax.ShapeDtypeStruct((db, t_new, ATTN_OUT_WIDTH), BF16)]
                  + [jax.ShapeDtypeStruct(c.shape, c.dtype) for c in (c0, c1, c2v)],
        compiler_params=_cparams("arbitrary"),
        name="sample_attn",
    )(qkv_s, c0, c1, c2v, *news, bias_c[0], bias_c[1], bias_c[2], bias_n)


def _pool_kernel(hist_ref, z_ref, w_ref, s_ref, o_ref, zbuf, *, t_len, pos0, chunk):
    zbuf[0:POOL_HIST, :] = hist_ref[0]
    zbuf[POOL_HIST:, :] = z_ref[...].reshape(t_len, POOL_WIDTH)
    for c in range(t_len // chunk):
        base = POOL_HIST + c * chunk
        for g, w in enumerate(POOL_WINDOWS):
            sl = slice(g * PGW, (g + 1) * PGW)
            cur = zbuf[base:base + chunk, sl]
            tot = cur
            for i in range(1, w):
                tot = tot + zbuf[base - i:base - i + chunk, sl]
            pos = pos0 + c * chunk + lax.broadcasted_iota(jnp.int32, (chunk, PGW), 0)
            cnt = jnp.minimum(w, pos + 1).astype(F32)
            mixed = tot / cnt - cur
            y = jnp.dot(mixed.astype(BF16), w_ref[g], preferred_element_type=F32) * s_ref[:, sl]
            o_ref[0, c * chunk:(c + 1) * chunk, sl] = y.astype(o_ref.dtype)


def _pool_mix(hist, z, z_spec, n_seq, t_len, w_pool, pool_scale, pos0):
    chunk = min(t_len, 256)
    return pl.pallas_call(
        functools.partial(_pool_kernel, t_len=t_len, pos0=pos0, chunk=chunk),
        grid=(n_seq,),
        in_specs=[
            pl.BlockSpec((1, POOL_HIST, POOL_WIDTH), lambda i: (i, 0, 0)),
            z_spec,
            pl.BlockSpec(w_pool.shape, lambda i: (0, 0, 0)),
            pl.BlockSpec((1, POOL_WIDTH), lambda i: (0, 0)),
        ],
        out_specs=pl.BlockSpec((1, t_len, POOL_WIDTH), lambda i: (i, 0, 0)),
        out_shape=jax.ShapeDtypeStruct((n_seq, t_len, POOL_WIDTH), BF16),
        scratch_shapes=[pltpu.VMEM((POOL_HIST + t_len, POOL_WIDTH), F32)],
        compiler_params=_cparams("arbitrary"),
        name="pool_mix",
    )(hist, z, w_pool, pool_scale)


def _merge_kernel(h_ref, a_ref, p_ref, ga_ref, gb_ref, wa_ref, wb_ref, wo_ref, g2_ref, rwh_ref, rwl_ref, rb_ref,
                  h1_ref, xn_ref, e_ref, gate_ref, rank_ref, cnt_ref, carry):
    ya = jnp.dot(a_ref[...], wa_ref[...], preferred_element_type=F32)
    yb = jnp.dot(p_ref[...], wb_ref[...], preferred_element_type=F32)
    u = jax.nn.sigmoid(ga_ref[...]) * ya + jax.nn.sigmoid(gb_ref[...]) * yb
    h1 = h_ref[...] + jnp.dot(u.astype(BF16), wo_ref[...], preferred_element_type=F32)
    h1_ref[...] = h1
    xn = _rmsnorm_f32(h1, g2_ref[...])
    half = xn.shape[1] // 2
    lo = _bf16_bits(xn[:, :half])
    hi = _bf16_bits(xn[:, half:])
    xn_ref[...] = (hi & jnp.uint32(0xFFFF0000)) | (lo >> 16)
    _route_tile(xn, rwh_ref, rwl_ref, rb_ref, e_ref, gate_ref, rank_ref, cnt_ref, carry)


def _bf16_bits(x):
    return lax.bitcast_convert_type(x.astype(BF16).astype(F32), jnp.uint32)


def _merge(h, attn, pool, proj, wa, wb, wo, g2, rwh, rwl, rb, tm):
    m, d = h.shape
    n_e = rwh.shape[1]

    def const(a):
        return pl.BlockSpec(a.shape, lambda i, nd=a.ndim: (0,) * nd, pipeline_mode=pl.Buffered(1))

    tok_spec = pl.BlockSpec((tm, TOP_K), lambda i: (i, 0))
    return pl.pallas_call(
        _merge_kernel,
        grid=(m // tm,),
        in_specs=[
            pl.BlockSpec((tm, d), lambda i: (i, 0)),
            pl.BlockSpec((tm, attn.shape[1]), lambda i: (i, 0)),
            pl.BlockSpec((tm, pool.shape[1]), lambda i: (i, 0)),
            pl.BlockSpec((tm, d), lambda i: (i, 0)),
            pl.BlockSpec((tm, d), lambda i: (i, 1)),
            const(wa), const(wb), const(wo), const(g2), const(rwh), const(rwl), const(rb),
        ],
        out_specs=[pl.BlockSpec((tm, d), lambda i: (i, 0)),
                   pl.BlockSpec((tm, d // 2), lambda i: (i, 0)),
                   tok_spec, tok_spec, tok_spec, pl.BlockSpec((1, n_e), lambda i: (0, 0))],
        out_shape=[jax.ShapeDtypeStruct((m, d), F32), jax.ShapeDtypeStruct((m, d // 2), jnp.uint32),
                   jax.ShapeDtypeStruct((m, TOP_K), jnp.int32), jax.ShapeDtypeStruct((m, TOP_K), F32),
                   jax.ShapeDtypeStruct((m, TOP_K), jnp.int32), jax.ShapeDtypeStruct((1, n_e), jnp.int32)],
        scratch_shapes=[pltpu.VMEM((1, n_e), F32)],
        compiler_params=_cparams("arbitrary"),
        name="merge_route",
    )(h, attn, pool, proj, proj, wa, wb, wo, g2, rwh, rwl, rb)


def _route_tile(x, wh_ref, wl_ref, b_ref, e_ref, gate_ref, rank_ref, cnt_ref, carry):
    @pl.when(pl.program_id(0) == 0)
    def _():
        carry[...] = jnp.zeros_like(carry)

    xh = x.astype(BF16)
    xl = (x - xh.astype(F32)).astype(BF16)
    logits = (jnp.dot(xh, wh_ref[...], preferred_element_type=F32)
              + jnp.dot(xh, wl_ref[...], preferred_element_type=F32)
              + jnp.dot(xl, wh_ref[...], preferred_element_type=F32)) + b_ref[...]
    n_e = logits.shape[-1]
    lane = lax.broadcasted_iota(jnp.int32, logits.shape, 1)
    vals = logits
    tops, idxs = [], []
    for _ in range(TOP_K):
        m = jnp.max(vals, axis=-1, keepdims=True)
        idx = jnp.min(jnp.where(vals == m, lane, n_e), axis=-1, keepdims=True)
        tops.append(m)
        idxs.append(idx)
        vals = jnp.where(lane == idx, -jnp.inf, vals)
    ex = [jnp.exp(t - tops[0]) for t in tops]
    den = ex[0] + ex[1] + ex[2] + ex[3]
    for k in range(TOP_K):
        e_ref[:, k:k + 1] = idxs[k]
        gate_ref[:, k:k + 1] = ex[k] / den

    onehots = [(lane == idx).astype(F32) for idx in idxs]
    cnt = onehots[0] + onehots[1] + onehots[2] + onehots[3]
    tm = cnt.shape[0]
    lower = (lax.broadcasted_iota(jnp.int32, (tm, tm), 0) > lax.broadcasted_iota(jnp.int32, (tm, tm), 1))
    prefix = jnp.dot(lower.astype(BF16), cnt.astype(BF16), preferred_element_type=F32)
    base = carry[...] + prefix
    for k in range(TOP_K):
        rank_ref[:, k:k + 1] = jnp.sum(onehots[k] * base, axis=-1, keepdims=True).astype(jnp.int32)
    carry[...] = carry[...] + jnp.sum(cnt, axis=0, keepdims=True)
    cnt_ref[...] = carry[...].astype(jnp.int32)


def _row_scatter_kernel(dest_ref, x_ref, init_hbm, o_hbm, sem, *, rows):
    del init_hbm
    t = pl.program_id(0)
    n_copies = rows * TOP_K
    tokens_per_iter = DMA_ISSUE_UNROLL // TOP_K

    def body(j, carry):
        for u in range(DMA_ISSUE_UNROLL):
            src = pl.ds(j * tokens_per_iter + u // TOP_K, 1)
            dst = pl.ds(dest_ref[t * n_copies + j * DMA_ISSUE_UNROLL + u], 1)
            pltpu.make_async_copy(x_ref.at[src], o_hbm.at[dst], sem.at[0]).start(priority=u % 2)
        return carry

    lax.fori_loop(0, n_copies // DMA_ISSUE_UNROLL, body, 0)
    for _ in range(TOP_K):
        pltpu.make_async_copy(x_ref, o_hbm.at[pl.ds(0, rows)], sem.at[0]).wait()


def _row_scatter(dest, src, n_rows, rows):
    m, width = src.shape
    assert DMA_ISSUE_UNROLL % TOP_K == 0 and m % rows == 0
    grid_spec = pltpu.PrefetchScalarGridSpec(
        num_scalar_prefetch=1,
        grid=(m // rows,),
        in_specs=[pl.BlockSpec((rows, width), lambda t, *_: (t, 0)), pl.BlockSpec(memory_space=pl.ANY)],
        out_specs=pl.BlockSpec(memory_space=pl.ANY),
        scratch_shapes=[pltpu.SemaphoreType.DMA((1,))],
    )
    return pl.pallas_call(
        functools.partial(_row_scatter_kernel, rows=rows),
        grid_spec=grid_spec,
        out_shape=jax.ShapeDtypeStruct((n_rows, width), src.dtype),
        input_output_aliases={2: 0},
        compiler_params=_cparams("arbitrary"),
        name="moe_row_scatter",
    )(dest, src, jnp.zeros((n_rows, width), src.dtype))


def _start_all(copies):
    for c in copies:
        c.start()


def _wait_all(copies):
    for c in copies:
        c.wait()


def _start_weight_pairs(piece, step, slot, q_lo, q_hi):
    def body(q, carry):
        piece(step, slot, 2 * q).start(priority=0)
        piece(step, slot, 2 * q + 1).start(priority=1)
        return carry
    lax.fori_loop(q_lo, q_hi, body, 0)


def _wait_step_weights(step, piece, n_pieces, whole):
    slot = step % 2

    @pl.when(step == 0)
    def _():
        _start_weight_pairs(piece, step, slot, 0, n_pieces // 2)

    _wait_all(whole(step, slot))
    return slot


def _expert_chunk_loop(step, n_steps, n_chunks, next_n_chunks, in_copies, next_in_copies, out_copies, compute,
                       weight_piece, n_weight_pieces):
    def start_first_two(count, copies):
        @pl.when(count > 0)
        def _():
            _start_all(copies(0, 0))

        @pl.when(count > 1)
        def _():
            _start_all(copies(1, 1))

    n_pairs = n_weight_pieces // 2

    def next_weights(q_lo, q_hi):
        @pl.when(step + 1 < n_steps)
        def _():
            _start_weight_pairs(weight_piece, step + 1, (step + 1) % 2, q_lo, q_hi)

    @pl.when(step == 0)
    def _():
        start_first_two(n_chunks, in_copies)

    @pl.when(n_chunks > 0)
    def _():
        def chunk(r, first):
            in_slot = r % CHUNK_IN_SLOTS
            out_slot = r % 2

            @pl.when(r + 2 < n_chunks)
            def _():
                _start_all(in_copies(r + 2, (r + 2) % CHUNK_IN_SLOTS))

            next_weights(jnp.minimum(r * WEIGHT_PAIRS_PER_CHUNK, n_pairs),
                         jnp.minimum((r + 1) * WEIGHT_PAIRS_PER_CHUNK, n_pairs))
            _wait_all(in_copies(r, in_slot))

            if not first:
                @pl.when(r >= 2)
                def _():
                    _wait_all(out_copies(r - 2, out_slot))

            compute(in_slot, out_slot, first)
            _start_all(out_copies(r, out_slot))

        chunk(0, True)

        def body(r, carry):
            chunk(r, False)
            return carry

        lax.fori_loop(1, n_chunks, body, 0)

    @pl.when(step + 1 < n_steps)
    def _():
        start_first_two(next_n_chunks, next_in_copies)

    next_weights(jnp.minimum(n_chunks * WEIGHT_PAIRS_PER_CHUNK, n_pairs), n_pairs)

    @pl.when(n_chunks >= 2)
    def _():
        _wait_all(out_copies(n_chunks - 2, n_chunks % 2))

    @pl.when(n_chunks >= 1)
    def _():
        _wait_all(out_copies(n_chunks - 1, (n_chunks - 1) % 2))


def _zero_fill_tiles(first, last, zbuf, dst, sem):
    zbuf[...] = jnp.zeros_like(zbuf)

    def copy(t):
        return pltpu.make_async_copy(zbuf, dst(t), sem)

    def start(t, carry):
        copy(t).start()
        return carry

    def wait(t, carry):
        copy(t).wait()
        return carry

    lax.fori_loop(first, last, start, 0)
    lax.fori_loop(first, last, wait, 0)


def _moe_up_kernel(tpe_ref, start_ref, xs_hbm, w_hbm, bg_ref, bu_ref, hid_hbm,
                   wbuf, wg_sc, wu_sc, xbuf, obuf, w_sem, in_sem, out_sem, *, rows):
    e = pl.program_id(0)
    n = pl.program_id(1)
    n_chunks = pl.num_programs(1)
    row0 = start_ref[e]
    d, nc = wg_sc.shape

    piece_rows = d // DMA_SPLIT
    n_pieces = 2 * DMA_SPLIT

    def w_piece(step, slot, p):
        e_, n_ = step // n_chunks, step % n_chunks
        mat, blk = p // DMA_SPLIT, p % DMA_SPLIT
        cols = pl.ds(pl.multiple_of((mat * n_chunks + n_) * nc, nc), nc)
        rws = pl.ds(pl.multiple_of(blk * piece_rows, piece_rows), piece_rows)
        return pltpu.make_async_copy(w_hbm.at[e_, rws, cols], wbuf.at[slot, mat, rws], w_sem.at[slot])

    def w_whole(step, slot):
        e_, n_ = step // n_chunks, step % n_chunks
        return [pltpu.make_async_copy(
            w_hbm.at[e_, :, pl.ds(pl.multiple_of((mat * n_chunks + n_) * nc, nc), nc)], wbuf.at[slot, mat],
            w_sem.at[slot]) for mat in range(2)]

    step, n_steps = e * n_chunks + n, pl.num_programs(0) * n_chunks
    w_slot = _wait_step_weights(step, w_piece, n_pieces, w_whole)
    next_e = jnp.minimum((step + 1) // n_chunks, pl.num_programs(0) - 1)

    def chunk_rows(first_row, r):
        return pl.ds(pl.multiple_of(first_row + r * rows, rows), rows)

    def in_copies_from(first_row):
        def in_copies(r, slot):
            return [pltpu.make_async_copy(xs_hbm.at[chunk_rows(first_row, r)], xbuf.at[slot], in_sem.at[slot])]
        return in_copies

    def out_copies(r, slot):
        return [pltpu.make_async_copy(obuf.at[slot], hid_hbm.at[n, chunk_rows(row0, r)], out_sem.at[slot])]

    half = d // 2

    def compute(in_slot, out_slot, first):
        words = xbuf[in_slot]
        x_lo = lax.bitcast_convert_type(words << 16, F32).astype(BF16)
        x_hi = lax.bitcast_convert_type(words & jnp.uint32(0xFFFF0000), F32).astype(BF16)

        def x_dot(w_ref, cs):
            return (jnp.dot(x_lo, w_ref[:half, cs], preferred_element_type=F32)
                    + jnp.dot(x_hi, w_ref[half:, cs], preferred_element_type=F32))

        for c0 in range(0, nc, MOE_COL_BLOCK):
            cs = slice(c0, c0 + MOE_COL_BLOCK)
            if first:
                wg_sc[:, cs] = wbuf[w_slot, 0, :, cs].astype(BF16)
                wu_sc[:, cs] = wbuf[w_slot, 1, :, cs].astype(BF16)
            g = x_dot(wg_sc, cs) + bg_ref[0, :, cs]
            u = x_dot(wu_sc, cs) + bu_ref[0, :, cs]
            g = jnp.minimum(g, SWIGLU_LIMIT)
            u = jnp.clip(u, -SWIGLU_LIMIT, SWIGLU_LIMIT)
            obuf[out_slot, :, cs] = ((u + 1.0) * (g * jax.nn.sigmoid(SWIGLU_ALPHA * g))).astype(obuf.dtype)

    _expert_chunk_loop(step, n_steps, tpe_ref[e], tpe_ref[next_e], in_copies_from(row0),
                       in_copies_from(start_ref[next_e]), out_copies, compute, w_piece, n_pieces)

    @pl.when(e == pl.num_programs(0) - 1)
    def _():
        _zero_fill_tiles(row0 // rows + tpe_ref[e], hid_hbm.shape[1] // rows, obuf.at[0],
                         lambda t: hid_hbm.at[n, pl.ds(pl.multiple_of(t * rows, rows), rows)], out_sem.at[0])


def _moe_up(tiles_per_e, row_start, xs, w_gu, b_gu, rows, nc):
    r_pad = xs.shape[0]
    d = w_gu.shape[1]
    n_e = w_gu.shape[0]
    d_ff = w_gu.shape[2] // 2
    n_chunks = d_ff // nc
    any_spec = pl.BlockSpec(memory_space=pl.ANY)
    grid_spec = pltpu.PrefetchScalarGridSpec(
        num_scalar_prefetch=2,
        grid=(n_e, n_chunks),
        in_specs=[
            any_spec,
            any_spec,
            pl.BlockSpec((1, 1, nc), lambda e, n, *_: (e, 0, n)),
            pl.BlockSpec((1, 1, nc), lambda e, n, *_: (e, 0, n_chunks + n)),
        ],
        out_specs=any_spec,
        scratch_shapes=[pltpu.VMEM((2, 2, d, nc), w_gu.dtype), pltpu.VMEM((d, nc), BF16), pltpu.VMEM((d, nc), BF16),
                        pltpu.VMEM((CHUNK_IN_SLOTS, rows, d // 2), xs.dtype), pltpu.VMEM((2, rows, nc), BF16),
                        pltpu.SemaphoreType.DMA((2,)), pltpu.SemaphoreType.DMA((CHUNK_IN_SLOTS,)),
                        pltpu.SemaphoreType.DMA((2,))],
    )
    return pl.pallas_call(
        functools.partial(_moe_up_kernel, rows=rows),
        grid_spec=grid_spec,
        out_shape=jax.ShapeDtypeStruct((n_chunks, r_pad, nc), BF16),
        compiler_params=_cparams("arbitrary", "arbitrary"),
        name="moe_up",
    )(tiles_per_e, row_start, xs, w_gu, b_gu, b_gu)


def _moe_down_kernel(tpe_ref, start_ref, hid_hbm, w_hbm, b_ref, y_hbm,
                     wbuf, w_sc, xbuf, obuf, w_sem, in_sem, out_sem, *, rows):
    e = pl.program_id(0)
    n = pl.program_id(1)
    n_chunks = pl.num_programs(1)
    row0 = start_ref[e]
    k_chunks, _, kc = xbuf.shape[1:]
    d_ff, nc = w_sc.shape

    piece_rows = d_ff // DMA_SPLIT

    def w_piece(step, slot, p):
        e_, n_ = step // n_chunks, step % n_chunks
        cols = pl.ds(pl.multiple_of(n_ * nc, nc), nc)
        rws = pl.ds(pl.multiple_of(p * piece_rows, piece_rows), piece_rows)
        return pltpu.make_async_copy(w_hbm.at[e_, rws, cols], wbuf.at[slot, rws], w_sem.at[slot])

    def w_whole(step, slot):
        e_, n_ = step // n_chunks, step % n_chunks
        return [pltpu.make_async_copy(w_hbm.at[e_, :, pl.ds(pl.multiple_of(n_ * nc, nc), nc)], wbuf.at[slot],
                                      w_sem.at[slot])]

    step, n_steps = e * n_chunks + n, pl.num_programs(0) * n_chunks
    w_slot = _wait_step_weights(step, w_piece, DMA_SPLIT, w_whole)
    next_e = jnp.minimum((step + 1) // n_chunks, pl.num_programs(0) - 1)

    def in_copies_from(first_row):
        def in_copies(r, slot):
            chunk = pl.ds(pl.multiple_of(first_row + r * rows, rows), rows)
            return [pltpu.make_async_copy(hid_hbm.at[:, chunk], xbuf.at[slot], in_sem.at[slot])]
        return in_copies

    lanes_per_row = nc // LANES
    tile_rows = rows * lanes_per_row

    def out_copies(r, slot):
        dst = pl.ds(pl.multiple_of((row0 + r * rows) * lanes_per_row, tile_rows), tile_rows)
        return [pltpu.make_async_copy(obuf.at[slot], y_hbm.at[dst], out_sem.at[slot])]

    def compute(in_slot, out_slot, first):
        for c0 in range(0, nc, MOE_COL_BLOCK):
            cs = slice(c0, c0 + MOE_COL_BLOCK)
            if first:
                w_sc[:, cs] = wbuf[w_slot, :, cs].astype(BF16)
            acc = b_ref[0, :, cs] + jnp.dot(xbuf[in_slot, 0], w_sc[0:kc, cs], preferred_element_type=F32)
            for c in range(1, k_chunks):
                acc = acc + jnp.dot(xbuf[in_slot, c], w_sc[c * kc:(c + 1) * kc, cs], preferred_element_type=F32)
            for j in range(MOE_COL_BLOCK // LANES):
                lane_group = c0 // LANES + j
                obuf[out_slot, pl.ds(lane_group, rows, stride=lanes_per_row), :] = acc[:, j * LANES:(j + 1) * LANES]

    _expert_chunk_loop(step, n_steps, tpe_ref[e], tpe_ref[next_e], in_copies_from(row0),
                       in_copies_from(start_ref[next_e]), out_copies, compute, w_piece, DMA_SPLIT)

    @pl.when(e == pl.num_programs(0) - 1)
    def _():
        _zero_fill_tiles(row0 // rows + tpe_ref[e], y_hbm.shape[0] // tile_rows, obuf.at[0],
                         lambda t: y_hbm.at[pl.ds(pl.multiple_of(t * tile_rows, tile_rows), tile_rows)], out_sem.at[0])


def _moe_down(tiles_per_e, row_start, hid, w_dn, b_dn, rows):
    k_chunks, r_pad, kc = hid.shape
    n_e, d_ff, d = w_dn.shape
    nc = d
    n_chunks = 1
    any_spec = pl.BlockSpec(memory_space=pl.ANY)
    grid_spec = pltpu.PrefetchScalarGridSpec(
        num_scalar_prefetch=2,
        grid=(n_e, n_chunks),
        in_specs=[
            any_spec,
            any_spec,
            pl.BlockSpec((1, 1, nc), lambda e, n, *_: (e, 0, n)),
        ],
        out_specs=any_spec,
        scratch_shapes=[pltpu.VMEM((2, d_ff, nc), w_dn.dtype), pltpu.VMEM((d_ff, nc), BF16),
                        pltpu.VMEM((CHUNK_IN_SLOTS, k_chunks, rows, kc), hid.dtype),
                        pltpu.VMEM((2, rows * nc // LANES, LANES), F32),
                        pltpu.SemaphoreType.DMA((2,)), pltpu.SemaphoreType.DMA((CHUNK_IN_SLOTS,)),
                        pltpu.SemaphoreType.DMA((2,))],
    )
    return pl.pallas_call(
        functools.partial(_moe_down_kernel, rows=rows),
        grid_spec=grid_spec,
        out_shape=jax.ShapeDtypeStruct((r_pad * d // LANES, LANES), F32),
        compiler_params=_cparams("arbitrary", "arbitrary"),
        name="moe_down",
    )(tiles_per_e, row_start, hid, w_dn, b_dn)


def _combine_kernel(dest_ref, h_ref, gate_ref, gf_ref, y_hbm, *rest, rows, final_norm, n_head_tiles, split):
    if split:
        o_ref, o_tail_ref, buf, sem = rest
    else:
        (o_ref, buf, sem), o_tail_ref = rest, None
    t = pl.program_id(0)
    n_t = pl.num_programs(0)
    n_dma = rows * TOP_K
    d = o_ref.shape[1]
    groups = d // LANES

    def issue(tile, slot):
        def body(j, carry):
            for u in range(DMA_ISSUE_UNROLL):
                i = j * DMA_ISSUE_UNROLL + u
                src = pl.ds(pl.multiple_of(dest_ref[tile * n_dma + i], groups), groups)
                dst = pl.ds(pl.multiple_of(i * groups, groups), groups)
                pltpu.make_async_copy(y_hbm.at[src], buf.at[slot, dst], sem.at[slot]).start(priority=u % 2)
            return carry
        lax.fori_loop(0, n_dma // DMA_ISSUE_UNROLL, body, 0)

    @pl.when(t == 0)
    def _():
        issue(0, 0)

    @pl.when(t + 1 < n_t)
    def _():
        issue(t + 1, (t + 1) % 2)

    slot = t % 2
    pltpu.make_async_copy(y_hbm.at[pl.ds(0, n_dma * groups)], buf.at[slot], sem.at[slot]).wait()

    def finish(o_ref):
        gates = [gate_ref[:, k:k + 1] for k in range(TOP_K)]
        sq = jnp.zeros((rows, 1), F32)
        for j in range(groups):
            cs = slice(j * LANES, (j + 1) * LANES)
            acc = h_ref[:, cs]
            for k in range(TOP_K):
                acc = acc + gates[k] * buf[slot, pl.ds(k * rows * groups + j, rows, stride=groups), :]
            o_ref[:, cs] = acc
            sq = sq + jnp.sum(acc * acc, axis=-1, keepdims=True)
        if final_norm:
            o_ref[...] = o_ref[...] * lax.rsqrt(sq / d + NORM_EPS) * gf_ref[...]

    if o_tail_ref is None:
        finish(o_ref)
    else:
        pl.when(t < n_head_tiles)(lambda: finish(o_ref))
        pl.when(t >= n_head_tiles)(lambda: finish(o_tail_ref))


def _combine(dest_km, h1, gate, gf, y, rows, final_norm, split_at=None):
    m, d = h1.shape
    n_head = m // rows if split_at is None else split_at // rows
    out_specs = [pl.BlockSpec((rows, d), lambda t, *_: (jnp.minimum(t, n_head - 1), 0))]
    out_shape = [jax.ShapeDtypeStruct((n_head * rows, d), F32)]
    if split_at is not None:
        assert split_at % rows == 0 and (m - split_at) % rows == 0
        out_specs.append(pl.BlockSpec((rows, d), lambda t, *_: (jnp.maximum(t - n_head, 0), 0)))
        out_shape.append(jax.ShapeDtypeStruct((m - split_at, d), F32))
    grid_spec = pltpu.PrefetchScalarGridSpec(
        num_scalar_prefetch=1,
        grid=(m // rows,),
        in_specs=[
            pl.BlockSpec((rows, d), lambda t, *_: (t, 0)),
            pl.BlockSpec((rows, TOP_K), lambda t, *_: (t, 0)),
            pl.BlockSpec((1, d), lambda t, *_: (0, 0)),
            pl.BlockSpec(memory_space=pl.ANY),
        ],
        out_specs=out_specs,
        scratch_shapes=[pltpu.VMEM((2, rows * TOP_K * d // LANES, LANES), F32), pltpu.SemaphoreType.DMA((2,))],
    )
    return pl.pallas_call(
        functools.partial(_combine_kernel, rows=rows, final_norm=final_norm, n_head_tiles=n_head,
                          split=split_at is not None),
        grid_spec=grid_spec,
        out_shape=out_shape,
        compiler_params=_cparams("arbitrary"),
        name="moe_combine",
    )(dest_km, h1, gate, gf, y)


def _routing_tables(top_e, rank, counts, rows):
    n_experts = counts.shape[0]
    tiles_per_e = (counts + rows - 1) // rows
    tile_end = jnp.cumsum(tiles_per_e)
    pad_start = (tile_end - tiles_per_e) * rows
    experts = jnp.arange(n_experts, dtype=jnp.int32)
    dest = rank + jnp.sum(jnp.where(top_e[..., None] == experts, pad_start, 0), axis=-1)
    dest = dest.astype(jnp.int32).reshape(-1)
    return dest, tiles_per_e.astype(jnp.int32), pad_start.astype(jnp.int32)


def _largest_divisor(n, cap, mult):
    best = mult
    for c in range(mult, cap + 1, mult):
        if n % c == 0:
            best = c
    return best


def kernel(x_prompt, x_sample, cache_kv_w128, cache_kv_w512, cache_kv_w2048, state_pool, rel_bias, norm1, w_in,
           w_branch_a, w_branch_b, w_out, w_pool, pool_scale, norm2, router_w, router_b, w_gate_up, b_gate_up,
           w_down, b_down, norm_f):
    batch, seq, d = x_prompt.shape
    db, t_new, _ = x_sample.shape
    depth = norm1.shape[0]
    n_experts = router_w.shape[-1]
    caches = (cache_kv_w128, cache_kv_w512, cache_kv_w2048)
    m_p, m_s = batch * seq, db * t_new
    m = m_p + m_s
    tm = _largest_divisor(m, 640, 8)

    h = jnp.concatenate([x_prompt.reshape(m_p, d), x_sample.reshape(m_s, d)], axis=0)
    bias_p = _prompt_bias_tiles(rel_bias)
    bias_sc, bias_sn = _sample_bias_tiles(rel_bias, t_new)

    cuts = [ATTN_WIDTH, 2 * ATTN_WIDTH, 3 * ATTN_WIDTH, 3 * ATTN_WIDTH + POOL_WIDTH, 3 * ATTN_WIDTH + POOL_WIDTH + d]
    kv_w = HEADS_PER_GROUP * HEAD_DIM
    qkv0 = 2 * d
    kv0 = qkv0 + ATTN_WIDTH
    z0 = qkv0 + 3 * ATTN_WIDTH

    kv_p = [[] for _ in range(N_GROUPS)]
    kv_s = [[] for _ in range(N_GROUPS)]
    pool_p, pool_s = [], []
    for l in range(depth):
        wq, wk, wv, wz, wga, wgb = jnp.split(w_in[l], cuts, axis=-1)
        w_kv = [w[:, g * kv_w:(g + 1) * kv_w] for g in range(N_GROUPS) for w in (wk, wv)]
        w_perm = jnp.concatenate([wga, wgb, wq] + w_kv + [wz], axis=-1).astype(BF16)
        proj = _inproj(h, norm1[l][None, :], w_perm, tm, _largest_divisor(w_perm.shape[1], 1536, 128))

        proj_s = proj[m_p:]
        z_s = proj_s[:, z0:z0 + POOL_WIDTH].reshape(db, t_new, POOL_WIDTH)

        attn_p = _prompt_attn(proj, bias_p, batch, seq, qkv0 // HEAD_DIM, kv0 // HEAD_DIM)
        qkv_s = proj_s[:, qkv0:qkv0 + 3 * ATTN_WIDTH].reshape(db, t_new, 3 * ATTN_WIDTH)
        c0 = caches[0][l].reshape(db, -1, HEAD_DIM)
        c1 = caches[1][l].reshape(db, -1, HEAD_DIM)
        dil2 = ATTN_GROUPS[2][1]
        c2v = caches[2][l].reshape(db, -1, dil2 * CACHE_ROW_TILE, HEAD_DIM)

        news = []
        for g, (window, _) in enumerate(ATTN_GROUPS):
            keep = min(window, seq)
            c = kv0 + 2 * g * kv_w
            kv_p[g].append(jnp.stack([
                proj[(b + 1) * seq - keep:(b + 1) * seq, c:c + 2 * kv_w].reshape(keep, 2, HEADS_PER_GROUP, HEAD_DIM)
                for b in range(batch)]))
            news.append(proj_s[:, c:c + 2 * kv_w].reshape(db, t_new * CACHE_ROW_TILE, HEAD_DIM))
        news[2] = news[2][:, None]
        attn_s, *rolled = _sample_attn(qkv_s, c0, c1, c2v, news, bias_sc, bias_sn)
        attn = jnp.concatenate([attn_p, attn_s.reshape(m_s, ATTN_OUT_WIDTH)], axis=0)
        for g in range(N_GROUPS):
            kv_s[g].append(rolled[g].reshape(caches[g][l].shape))

        w_pool_b = w_pool[l].astype(BF16)
        scale = pool_scale[l][None, :]
        hist_p = jnp.zeros((batch, POOL_HIST, POOL_WIDTH), F32)
        hist_s = jnp.concatenate([jnp.zeros((db, POOL_HIST - POOL_BUF, POOL_WIDTH), F32), state_pool[l]], axis=1)
        pool = jnp.concatenate([
            _pool_mix(hist_p, proj, pl.BlockSpec((seq, POOL_WIDTH), lambda i: (i, z0 // POOL_WIDTH)),
                      batch, seq, w_pool_b, scale, 0).reshape(m_p, POOL_WIDTH),
            _pool_mix(hist_s, z_s, pl.BlockSpec((1, t_new, POOL_WIDTH), lambda i: (i, 0, 0)),
                      db, t_new, w_pool_b, scale, PAST_LEN).reshape(m_s, POOL_WIDTH),
        ], axis=0)
        pool_p.append(jnp.stack([proj[(b + 1) * seq - POOL_BUF:(b + 1) * seq, z0:z0 + POOL_WIDTH]
                                 for b in range(batch)]))
        pool_s.append(jnp.concatenate([state_pool[l], z_s], axis=1)[:, t_new:])

        rw = router_w[l]
        rw_hi = rw.astype(BF16)
        rw_lo = (rw - rw_hi.astype(F32)).astype(BF16)
        h1, xn_rows, top_e, gate, rank, counts = _merge(
            h, attn, pool, proj, w_branch_a[l].astype(BF16), w_branch_b[l].astype(BF16), w_out[l].astype(BF16),
            norm2[l][None, :], rw_hi, rw_lo, router_b[l][None, :], _largest_divisor(m, 320, 8))

        rows = MOE_ROW_TILE
        groups = d // LANES
        n_tiles = (m * TOP_K + n_experts * (rows - 1)) // rows
        dest, tiles_per_e, row_start = _routing_tables(top_e, rank, counts[0], rows)
        xs = _row_scatter(dest, xn_rows, n_tiles * rows, _largest_divisor(m, 256, 8))
        hid = _moe_up(tiles_per_e, row_start, xs, w_gate_up[l], b_gate_up[l][:, None, :], rows, 1024)
        y = _moe_down(tiles_per_e, row_start, hid, w_down[l], b_down[l][:, None, :], rows)
        crow = _largest_divisor(math.gcd(m_p, m_s), 128, 8)
        dest_km = (dest * groups).reshape(m // crow, crow, TOP_K).transpose(0, 2, 1).reshape(-1)
        if l == depth - 1:
            h_p, h_s = _combine(dest_km, h1, gate, norm_f[None, :], y, crow, final_norm=True, split_at=m_p)
        else:
            h, = _combine(dest_km, h1, gate, norm_f[None, :], y, crow, final_norm=False)

    y_prompt = h_p.reshape(batch, seq, d)
    y_sample = h_s.reshape(db, t_new, d)
    return (y_prompt, y_sample,
            jnp.stack(kv_p[0]), jnp.stack(kv_p[1]), jnp.stack(kv_p[2]), jnp.stack(pool_p),
            jnp.stack(kv_s[0]), jnp.stack(kv_s[1]), jnp.stack(kv_s[2]), jnp.stack(pool_s))
```

```python
import functools
import math

import jax
import jax.numpy as jnp
from jax import lax
from jax.experimental import pallas as pl
from jax.experimental.pallas import tpu as pltpu

F32 = jnp.float32
BF16 = jnp.bfloat16

HEAD_DIM = 128
HEADS_PER_GROUP = 4
ATTN_GROUPS = ((128, 1), (512, 4), (2048, 16))
N_GROUPS = len(ATTN_GROUPS)
N_ATTN_HEADS = N_GROUPS * HEADS_PER_GROUP
ATTN_WIDTH = N_ATTN_HEADS * HEAD_DIM
ATTN_OUT_WIDTH = HEADS_PER_GROUP * HEAD_DIM
ATTN_SCALE = HEAD_DIM ** -0.5
N_BACK = 128
N_REL_BUCKETS = 32
REL_MAX_DIST = 2048
POOL_WINDOWS = (2, 4, 8, 16)
POOL_GROUPS = len(POOL_WINDOWS)
PGW = 128
POOL_WIDTH = POOL_GROUPS * PGW
POOL_BUF = max(POOL_WINDOWS) - 1
POOL_HIST = 16
TOP_K = 4
SWIGLU_LIMIT = 7.0
SWIGLU_ALPHA = 1.702
NORM_EPS = 1e-5
NEG_INF = -1e30
PAST_LEN = 8192
CACHE_ROW_TILE = 2 * HEADS_PER_GROUP

VMEM_LIMIT_BYTES = 56 * 1024 * 1024
MOE_ROW_TILE = 256
DMA_ISSUE_UNROLL = 8
DMA_SPLIT = 8
ATTN_INTERLEAVE = 8
LANES = 128
CHUNK_IN_SLOTS = 3
MOE_COL_BLOCK = 512
WEIGHT_PAIRS_PER_CHUNK = 2


def _cparams(*sem):
    return pltpu.CompilerParams(dimension_semantics=sem, vmem_limit_bytes=VMEM_LIMIT_BYTES)


def _rmsnorm_f32(x, g):
    return x * lax.rsqrt(jnp.mean(x * x, axis=-1, keepdims=True) + NORM_EPS) * g


def _inproj_kernel(x_ref, g_ref, w_ref, o_ref):
    xn = _rmsnorm_f32(x_ref[...], g_ref[...]).astype(BF16)
    o_ref[...] = jnp.dot(xn, w_ref[...], preferred_element_type=F32)


def _inproj(x, g, w, tm, tn):
    m, d = x.shape
    n = w.shape[1]
    return pl.pallas_call(
        _inproj_kernel,
        grid=(n // tn, m // tm),
        in_specs=[
            pl.BlockSpec((tm, d), lambda j, i: (i, 0)),
            pl.BlockSpec((1, d), lambda j, i: (0, 0)),
            pl.BlockSpec((d, tn), lambda j, i: (0, j)),
        ],
        out_specs=pl.BlockSpec((tm, tn), lambda j, i: (i, j)),
        out_shape=jax.ShapeDtypeStruct((m, n), F32),
        compiler_params=_cparams("arbitrary", "arbitrary"),
        name="inproj",
    )(x, g, w)


def _t5_bucket(dist):
    max_exact = N_REL_BUCKETS // 2
    df = jnp.maximum(dist, 1).astype(F32)
    large = max_exact + (jnp.log(df / max_exact) / math.log(REL_MAX_DIST / max_exact)
                         * (N_REL_BUCKETS - max_exact)).astype(jnp.int32)
    large = jnp.minimum(large, N_REL_BUCKETS - 1)
    return jnp.where(dist < max_exact, dist, large)


def _step_bias(rel_bias, g):
    _, dil = ATTN_GROUPS[g]
    buckets = _t5_bucket(jnp.arange(N_BACK + 1, dtype=jnp.int32) * dil)
    tab = rel_bias[buckets].astype(F32)
    return tab[:, g * HEADS_PER_GROUP:(g + 1) * HEADS_PER_GROUP].T


def _prompt_bias_tiles(rel_bias):
    nb = N_BACK
    period = 3 * nb - 1
    tiles = []
    for g in range(N_GROUPS):
        bj = _step_bias(rel_bias, g)
        h = bj.shape[0]
        pad = jnp.full((h, nb - 1), NEG_INF, F32)
        v = jnp.concatenate([pad, bj[:, ::-1], pad], axis=1)
        skew = jnp.tile(v, (1, nb + 1))[:, :nb * (period + 1)].reshape(h, nb, period + 1)
        tiles.append(skew[:, ::-1, :2 * nb])
    return jnp.stack(tiles, axis=1)


def _sample_bias_tiles(rel_bias, t_new):
    tq = jnp.arange(t_new)[:, None]
    outs = []
    new = []
    for g, (window, dil) in enumerate(ATTN_GROUPS):
        bj = _step_bias(rel_bias, g)
        if g < 2:
            p = jnp.arange(window)[None, :]
            dist = window + tq - p
            ok = (dist % dil == 0) & (dist // dil <= N_BACK)
            outs.append(jnp.where(ok[None], bj[:, jnp.clip(dist // dil, 0, N_BACK)], NEG_INF))
        else:
            col = jnp.arange(t_new * N_BACK)[None, :]
            blk, r = col // N_BACK, col % N_BACK
            ok = blk == tq
            outs.append(jnp.where(ok[None], bj[:, jnp.broadcast_to(N_BACK - r, (t_new, t_new * N_BACK))], NEG_INF))
        tk = jnp.arange(t_new)[None, :]
        d = tq - tk
        okn = (d >= 0) & (d % dil == 0) & (d // dil <= N_BACK)
        new.append(jnp.where(okn[None], bj[:, jnp.clip(d // dil, 0, N_BACK)], NEG_INF))
    return outs, jnp.stack(new, axis=0)


def _softmax_block(s, v):
    m = jnp.max(s, axis=-1, keepdims=True)
    p = jnp.exp(s - m)
    den = jnp.sum(p, axis=-1, keepdims=True)
    o = jnp.dot(p.astype(BF16), v, preferred_element_type=F32) / den
    return o, m + jnp.log(den)


def _prompt_attn_kernel(q0, k0, v0, q1, k1, v1, q2, k2, v2, bias_ref, o_ref, o_sc, l_sc, *, seq):
    qkv = ((q0, k0, v0), (q1, k1, v1), (q2, k2, v2))
    nb = N_BACK

    def rows(ref, start, size, dil):
        if dil == 1:
            return ref[pl.ds(start, size), :]
        return ref[pl.ds(start, size, stride=dil), :]

    def put(g, start, dil, o, lse):
        idx = pl.ds(start, nb) if dil == 1 else pl.ds(start, nb, stride=dil)
        o_sc[g, idx, :] = o
        l_sc[g, idx, :] = jnp.broadcast_to(lse, (nb, HEAD_DIM))

    dn = (((1,), (1,)), ((), ()))

    def attend(g, dil, q_starts, kv_starts, n_kv):
        q_ref, k_ref, v_ref = qkv[g]
        bias = bias_ref[0, g] if n_kv == 2 * nb else bias_ref[0, g, :, nb:]
        qs = [rows(q_ref, st, nb, dil).astype(BF16) for st in q_starts]
        ks = [rows(k_ref, st, n_kv, dil).astype(BF16) for st in kv_starts]
        ss = [lax.dot_general(q, k, dn, preferred_element_type=F32) * ATTN_SCALE + bias for q, k in zip(qs, ks)]
        ms = [jnp.max(s, axis=-1, keepdims=True) for s in ss]
        ps = [jnp.exp(s - m) for s, m in zip(ss, ms)]
        dens = [jnp.sum(p, axis=-1, keepdims=True) for p in ps]
        vs = [rows(v_ref, st, n_kv, dil).astype(BF16) for st in kv_starts]
        for st, p, v, m, den in zip(q_starts, ps, vs, ms, dens):
            o = jnp.dot(p.astype(BF16), v, preferred_element_type=F32) / den
            put(g, st, dil, o, m + jnp.log(den))

    for g, (_, dil) in enumerate(ATTN_GROUPS):
        n_blk = seq // (dil * nb)
        span = dil * nb
        for r0 in range(0, dil, ATTN_INTERLEAVE):
            starts = list(range(r0, min(r0 + ATTN_INTERLEAVE, dil)))
            attend(g, dil, starts, starts, nb)
        if n_blk > 1:
            per_iter = max(c for c in range(1, max(1, ATTN_INTERLEAVE // dil) + 1) if (n_blk - 1) % c == 0)

            def body(it, carry, g=g, dil=dil, span=span, per_iter=per_iter):
                q_starts, kv_starts = [], []
                for j in range(per_iter):
                    b = 1 + it * per_iter + j
                    for r in range(dil):
                        q_starts.append(b * span + r)
                        kv_starts.append((b - 1) * span + r)
                attend(g, dil, q_starts, kv_starts, 2 * nb)
                return carry

            lax.fori_loop(0, (n_blk - 1) // per_iter, body, 0)

    chunk = 256

    def comb(c, carry):
        sl = pl.ds(pl.multiple_of(c * chunk, chunk), chunk)
        l0, l1, l2 = l_sc[0, sl, :], l_sc[1, sl, :], l_sc[2, sl, :]
        mx = jnp.maximum(jnp.maximum(l0, l1), l2)
        e0, e1, e2 = jnp.exp(l0 - mx), jnp.exp(l1 - mx), jnp.exp(l2 - mx)
        num = e0 * o_sc[0, sl, :] + e1 * o_sc[1, sl, :] + e2 * o_sc[2, sl, :]
        o_ref[sl, :] = (num / (e0 + e1 + e2)).astype(o_ref.dtype)
        return carry

    lax.fori_loop(0, seq // chunk, comb, 0)


def _prompt_attn(p, bias, batch, seq, q_block0, kv_block0):
    def spec(kind, g):
        if kind == 0:
            base = q_block0 + g * HEADS_PER_GROUP
        else:
            base = kv_block0 + (2 * g + kind - 1) * HEADS_PER_GROUP
        return pl.BlockSpec((seq, HEAD_DIM), lambda b, i, base=base: (b, base + i))

    in_specs = []
    for g in range(N_GROUPS):
        in_specs += [spec(0, g), spec(1, g), spec(2, g)]
    in_specs.append(pl.BlockSpec((1, N_GROUPS, N_BACK, 2 * N_BACK), lambda b, i: (i, 0, 0, 0)))
    return pl.pallas_call(
        functools.partial(_prompt_attn_kernel, seq=seq),
        grid=(batch, HEADS_PER_GROUP),
        in_specs=in_specs,
        out_specs=pl.BlockSpec((seq, HEAD_DIM), lambda b, i: (b, i)),
        out_shape=jax.ShapeDtypeStruct((batch * seq, ATTN_OUT_WIDTH), BF16),
        scratch_shapes=[pltpu.VMEM((N_GROUPS, seq, HEAD_DIM), F32), pltpu.VMEM((N_GROUPS, seq, HEAD_DIM), F32)],
        compiler_params=_cparams("arbitrary", "arbitrary"),
        name="prompt_attn",
    )(*([p] * 9), bias)


def _sample_attn_kernel(qkv_ref, c0_ref, c1_ref, c2_ref, n0_ref, n1_ref, n2_ref, b0_ref, b1_ref, b2_ref, bn_ref,
                        o_ref, r0_ref, r1_ref, r2_ref, *, t_new):
    caches = (c0_ref, c1_ref, c2_ref)
    biases = (b0_ref, b1_ref, b2_ref)
    dn = (((1,), (1,)), ((), ()))

    shift = t_new * CACHE_ROW_TILE
    for c_ref, n_ref, r_ref in ((c0_ref, n0_ref, r0_ref), (c1_ref, n1_ref, r1_ref)):
        keep = c_ref.shape[1] - shift
        r_ref[0, :keep] = c_ref[0, shift:]
        r_ref[0, keep:] = n_ref[0]
    n_r, width = c2_ref.shape[1], c2_ref.shape[2]
    r2_ref[0, :, :width - shift] = c2_ref[0, :, shift:]
    r2_ref[0, :n_r - 1, width - shift:] = c2_ref[0, 1:, :shift]
    r2_ref[0, n_r - 1:, width - shift:] = n2_ref[0]

    for i in range(HEADS_PER_GROUP):
        outs, lses = [], []
        for g in range(N_GROUPS):
            h = g * HEADS_PER_GROUP + i
            q = qkv_ref[0, :, h * HEAD_DIM:(h + 1) * HEAD_DIM].astype(BF16)
            k_col = ATTN_WIDTH + (2 * g * HEADS_PER_GROUP + i) * HEAD_DIM
            v_col = k_col + HEADS_PER_GROUP * HEAD_DIM
            kn = qkv_ref[0, :, k_col:k_col + HEAD_DIM].astype(BF16)
            vn = qkv_ref[0, :, v_col:v_col + HEAD_DIM].astype(BF16)
            c_ref = caches[g]
            if g < 2:
                n_rows = c_ref.shape[1] // CACHE_ROW_TILE
                kc = c_ref[0, pl.ds(i, n_rows, stride=CACHE_ROW_TILE), :].astype(BF16)
                vc = c_ref[0, pl.ds(HEADS_PER_GROUP + i, n_rows, stride=CACHE_ROW_TILE), :].astype(BF16)
            else:
                kc = jnp.concatenate(
                    [c_ref[0, :, t * CACHE_ROW_TILE + i, :] for t in range(t_new)], axis=0).astype(BF16)
                vc = jnp.concatenate(
                    [c_ref[0, :, t * CACHE_ROW_TILE + HEADS_PER_GROUP + i, :] for t in range(t_new)],
                    axis=0).astype(BF16)
            sc = lax.dot_general(q, kc, dn, preferred_element_type=F32) * ATTN_SCALE + biases[g][i]
            sn = lax.dot_general(q, kn, dn, preferred_element_type=F32) * ATTN_SCALE + bn_ref[g, i]
            m = jnp.maximum(jnp.max(sc, axis=-1, keepdims=True), jnp.max(sn, axis=-1, keepdims=True))
            pc = jnp.exp(sc - m)
            pn = jnp.exp(sn - m)
            den = jnp.sum(pc, axis=-1, keepdims=True) + jnp.sum(pn, axis=-1, keepdims=True)
            o = (jnp.dot(pc.astype(BF16), vc, preferred_element_type=F32)
                 + jnp.dot(pn.astype(BF16), vn, preferred_element_type=F32)) / den
            outs.append(o)
            lses.append(m + jnp.log(den))
        mx = jnp.maximum(jnp.maximum(lses[0], lses[1]), lses[2])
        es = [jnp.exp(l - mx) for l in lses]
        num = es[0] * outs[0] + es[1] * outs[1] + es[2] * outs[2]
        o_ref[0, :, i * HEAD_DIM:(i + 1) * HEAD_DIM] = (num / (es[0] + es[1] + es[2])).astype(o_ref.dtype)


def _sample_attn(qkv_s, c0, c1, c2v, news, bias_c, bias_n):
    db, t_new, _ = qkv_s.shape

    def full(a):
        return pl.BlockSpec(a.shape, lambda b, nd=a.ndim: (0,) * nd)

    def per_seq(a):
        return pl.BlockSpec((1,) + a.shape[1:], lambda b, nd=a.ndim: (b,) + (0,) * (nd - 1))

    return pl.pallas_call(
        functools.partial(_sample_attn_kernel, t_new=t_new),
        grid=(db,),
        in_specs=[per_seq(qkv_s), per_seq(c0), per_seq(c1), per_seq(c2v), per_seq(news[0]), per_seq(news[1]),
                  per_seq(news[2]), full(bias_c[0]), full(bias_c[1]), full(bias_c[2]), full(bias_n)],
        out_specs=[pl.BlockSpec((1, t_new, ATTN_OUT_WIDTH), lambda b: (b, 0, 0)),
                   per_seq(c0), per_seq(c1), per_seq(c2v)],
        out_shape=[jax.ShapeDtypeStruct((db, t_new, ATTN_OUT_WIDTH), BF16)]
                  + [jax.ShapeDtypeStruct(c.shape, c.dtype) for c in (c0, c1, c2v)],
        compiler_params=_cparams("arbitrary"),
        name="sample_attn",
    )(qkv_s, c0, c1, c2v, *news, bias_c[0], bias_c[1], bias_c[2], bias_n)


def _pool_kernel(hist_ref, z_ref, w_ref, s_ref, o_ref, zbuf, *, t_len, pos0, chunk):
    zbuf[0:POOL_HIST, :] = hist_ref[0]
    zbuf[POOL_HIST:, :] = z_ref[...].reshape(t_len, POOL_WIDTH)
    for c in range(t_len // chunk):
        base = POOL_HIST + c * chunk
        for g, w in enumerate(POOL_WINDOWS):
            sl = slice(g * PGW, (g + 1) * PGW)
            cur = zbuf[base:base + chunk, sl]
            tot = cur
            for i in range(1, w):
                tot = tot + zbuf[base - i:base - i + chunk, sl]
            pos = pos0 + c * chunk + lax.broadcasted_iota(jnp.int32, (chunk, PGW), 0)
            cnt = jnp.minimum(w, pos + 1).astype(F32)
            mixed = tot / cnt - cur
            y = jnp.dot(mixed.astype(BF16), w_ref[g], preferred_element_type=F32) * s_ref[:, sl]
            o_ref[0, c * chunk:(c + 1) * chunk, sl] = y.astype(o_ref.dtype)


def _pool_mix(hist, z, z_spec, n_seq, t_len, w_pool, pool_scale, pos0):
    chunk = min(t_len, 256)
    return pl.pallas_call(
        functools.partial(_pool_kernel, t_len=t_len, pos0=pos0, chunk=chunk),
        grid=(n_seq,),
        in_specs=[
            pl.BlockSpec((1, POOL_HIST, POOL_WIDTH), lambda i: (i, 0, 0)),
            z_spec,
            pl.BlockSpec(w_pool.shape, lambda i: (0, 0, 0)),
            pl.BlockSpec((1, POOL_WIDTH), lambda i: (0, 0)),
        ],
        out_specs=pl.BlockSpec((1, t_len, POOL_WIDTH), lambda i: (i, 0, 0)),
        out_shape=jax.ShapeDtypeStruct((n_seq, t_len, POOL_WIDTH), BF16),
        scratch_shapes=[pltpu.VMEM((POOL_HIST + t_len, POOL_WIDTH), F32)],
        compiler_params=_cparams("arbitrary"),
        name="pool_mix",
    )(hist, z, w_pool, pool_scale)


def _merge_kernel(h_ref, a_ref, p_ref, ga_ref, gb_ref, wa_ref, wb_ref, wo_ref, g2_ref, rwh_ref, rwl_ref, rb_ref,
                  h1_ref, xn_ref, e_ref, gate_ref, rank_ref, cnt_ref, carry):
    ya = jnp.dot(a_ref[...], wa_ref[...], preferred_element_type=F32)
    yb = jnp.dot(p_ref[...], wb_ref[...], preferred_element_type=F32)
    u = jax.nn.sigmoid(ga_ref[...]) * ya + jax.nn.sigmoid(gb_ref[...]) * yb
    h1 = h_ref[...] + jnp.dot(u.astype(BF16), wo_ref[...], preferred_element_type=F32)
    h1_ref[...] = h1
    xn = _rmsnorm_f32(h1, g2_ref[...])
    half = xn.shape[1] // 2
    lo = _bf16_bits(xn[:, :half])
    hi = _bf16_bits(xn[:, half:])
    xn_ref[...] = (hi & jnp.uint32(0xFFFF0000)) | (lo >> 16)
    _route_tile(xn, rwh_ref, rwl_ref, rb_ref, e_ref, gate_ref, rank_ref, cnt_ref, carry)


def _bf16_bits(x):
    return lax.bitcast_convert_type(x.astype(BF16).astype(F32), jnp.uint32)


def _merge(h, attn, pool, proj, wa, wb, wo, g2, rwh, rwl, rb, tm):
    m, d = h.shape
    n_e = rwh.shape[1]

    def const(a):
        return pl.BlockSpec(a.shape, lambda i, nd=a.ndim: (0,) * nd, pipeline_mode=pl.Buffered(1))

    tok_spec = pl.BlockSpec((tm, TOP_K), lambda i: (i, 0))
    return pl.pallas_call(
        _merge_kernel,
        grid=(m // tm,),
        in_specs=[
            pl.BlockSpec((tm, d), lambda i: (i, 0)),
            pl.BlockSpec((tm, attn.shape[1]), lambda i: (i, 0)),
            pl.BlockSpec((tm, pool.shape[1]), lambda i: (i, 0)),
            pl.BlockSpec((tm, d), lambda i: (i, 0)),
            pl.BlockSpec((tm, d), lambda i: (i, 1)),
            const(wa), const(wb), const(wo), const(g2), const(rwh), const(rwl), const(rb),
        ],
        out_specs=[pl.BlockSpec((tm, d), lambda i: (i, 0)),
                   pl.BlockSpec((tm, d // 2), lambda i: (i, 0)),
                   tok_spec, tok_spec, tok_spec, pl.BlockSpec((1, n_e), lambda i: (0, 0))],
        out_shape=[jax.ShapeDtypeStruct((m, d), F32), jax.ShapeDtypeStruct((m, d // 2), jnp.uint32),
                   jax.ShapeDtypeStruct((m, TOP_K), jnp.int32), jax.ShapeDtypeStruct((m, TOP_K), F32),
                   jax.ShapeDtypeStruct((m, TOP_K), jnp.int32), jax.ShapeDtypeStruct((1, n_e), jnp.int32)],
        scratch_shapes=[pltpu.VMEM((1, n_e), F32)],
        compiler_params=_cparams("arbitrary"),
        name="merge_route",
    )(h, attn, pool, proj, proj, wa, wb, wo, g2, rwh, rwl, rb)


def _route_tile(x, wh_ref, wl_ref, b_ref, e_ref, gate_ref, rank_ref, cnt_ref, carry):
    @pl.when(pl.program_id(0) == 0)
    def _():
        carry[...] = jnp.zeros_like(carry)

    xh = x.astype(BF16)
    xl = (x - xh.astype(F32)).astype(BF16)
    logits = (jnp.dot(xh, wh_ref[...], preferred_element_type=F32)
              + jnp.dot(xh, wl_ref[...], preferred_element_type=F32)
              + jnp.dot(xl, wh_ref[...], preferred_element_type=F32)) + b_ref[...]
    n_e = logits.shape[-1]
    lane = lax.broadcasted_iota(jnp.int32, logits.shape, 1)
    vals = logits
    tops, idxs = [], []
    for _ in range(TOP_K):
        m = jnp.max(vals, axis=-1, keepdims=True)
        idx = jnp.min(jnp.where(vals == m, lane, n_e), axis=-1, keepdims=True)
        tops.append(m)
        idxs.append(idx)
        vals = jnp.where(lane == idx, -jnp.inf, vals)
    ex = [jnp.exp(t - tops[0]) for t in tops]
    den = ex[0] + ex[1] + ex[2] + ex[3]
    for k in range(TOP_K):
        e_ref[:, k:k + 1] = idxs[k]
        gate_ref[:, k:k + 1] = ex[k] / den

    onehots = [(lane == idx).astype(F32) for idx in idxs]
    cnt = onehots[0] + onehots[1] + onehots[2] + onehots[3]
    tm = cnt.shape[0]
    lower = (lax.broadcasted_iota(jnp.int32, (tm, tm), 0) > lax.broadcasted_iota(jnp.int32, (tm, tm), 1))
    prefix = jnp.dot(lower.astype(BF16), cnt.astype(BF16), preferred_element_type=F32)
    base = carry[...] + prefix
    for k in range(TOP_K):
        rank_ref[:, k:k + 1] = jnp.sum(onehots[k] * base, axis=-1, keepdims=True).astype(jnp.int32)
    carry[...] = carry[...] + jnp.sum(cnt, axis=0, keepdims=True)
    cnt_ref[...] = carry[...].astype(jnp.int32)


def _row_scatter_kernel(dest_ref, tpe_ref, start_ref, x_ref, o_hbm, zbuf, sem, zsem, *, rows, tile):
    t = pl.program_id(0)
    n_copies = rows * TOP_K
    tokens_per_iter = DMA_ISSUE_UNROLL // TOP_K
    n_experts = tpe_ref.shape[0]
    n_tiles = o_hbm.shape[0] // tile

    @pl.when(t == 0)
    def _():
        zbuf[...] = jnp.zeros_like(zbuf)

        def zero_tile(tile_idx):
            return pltpu.make_async_copy(zbuf, o_hbm.at[pl.ds(pl.multiple_of(tile_idx * tile, tile), tile)], zsem.at[0])

        def last_tile(e):
            return start_ref[e] // tile + tpe_ref[e] - 1

        def each_expert(fn):
            def body(e, carry):
                pl.when(tpe_ref[e] > 0)(lambda: fn(zero_tile(last_tile(e))))
                return carry
            lax.fori_loop(0, n_experts, body, 0)

        def each_tail_tile(fn):
            first = start_ref[n_experts - 1] // tile + tpe_ref[n_experts - 1]

            def body(i, carry):
                fn(zero_tile(i))
                return carry
            lax.fori_loop(first, n_tiles, body, 0)

        each_expert(lambda c: c.start())
        each_tail_tile(lambda c: c.start())
        each_expert(lambda c: c.wait())
        each_tail_tile(lambda c: c.wait())

    def body(j, carry):
        for u in range(DMA_ISSUE_UNROLL):
            src = pl.ds(j * tokens_per_iter + u // TOP_K, 1)
            dst = pl.ds(dest_ref[t * n_copies + j * DMA_ISSUE_UNROLL + u], 1)
            pltpu.make_async_copy(x_ref.at[src], o_hbm.at[dst], sem.at[0]).start(priority=u % 2)
        return carry

    lax.fori_loop(0, n_copies // DMA_ISSUE_UNROLL, body, 0)
    for _ in range(TOP_K):
        pltpu.make_async_copy(x_ref, o_hbm.at[pl.ds(0, rows)], sem.at[0]).wait()


def _row_scatter(dest, tiles_per_e, row_start, src, n_rows, rows, tile):
    m, width = src.shape
    assert DMA_ISSUE_UNROLL % TOP_K == 0 and m % rows == 0 and n_rows % tile == 0
    grid_spec = pltpu.PrefetchScalarGridSpec(
        num_scalar_prefetch=3,
        grid=(m // rows,),
        in_specs=[pl.BlockSpec((rows, width), lambda t, *_: (t, 0))],
        out_specs=pl.BlockSpec(memory_space=pl.ANY),
        scratch_shapes=[pltpu.VMEM((tile, width), src.dtype), pltpu.SemaphoreType.DMA((1,)),
                        pltpu.SemaphoreType.DMA((1,))],
    )
    return pl.pallas_call(
        functools.partial(_row_scatter_kernel, rows=rows, tile=tile),
        grid_spec=grid_spec,
        out_shape=jax.ShapeDtypeStruct((n_rows, width), src.dtype),
        compiler_params=_cparams("arbitrary"),
        name="moe_row_scatter",
    )(dest, tiles_per_e, row_start, src)


def _start_all(copies):
    for c in copies:
        c.start()


def _wait_all(copies):
    for c in copies:
        c.wait()


def _start_weight_pairs(piece, step, slot, q_lo, q_hi):
    def body(q, carry):
        piece(step, slot, 2 * q).start(priority=0)
        piece(step, slot, 2 * q + 1).start(priority=1)
        return carry
    lax.fori_loop(q_lo, q_hi, body, 0)


def _wait_step_weights(step, piece, n_pieces, whole):
    slot = step % 2

    @pl.when(step == 0)
    def _():
        _start_weight_pairs(piece, step, slot, 0, n_pieces // 2)

    _wait_all(whole(step, slot))
    return slot


def _expert_chunk_loop(step, n_steps, n_chunks, next_n_chunks, in_copies, next_in_copies, out_copies, compute,
                       weight_piece, n_weight_pieces):
    def start_first_two(count, copies):
        @pl.when(count > 0)
        def _():
            _start_all(copies(0, 0))

        @pl.when(count > 1)
        def _():
            _start_all(copies(1, 1))

    n_pairs = n_weight_pieces // 2

    def next_weights(q_lo, q_hi):
        @pl.when(step + 1 < n_steps)
        def _():
            _start_weight_pairs(weight_piece, step + 1, (step + 1) % 2, q_lo, q_hi)

    @pl.when(step == 0)
    def _():
        start_first_two(n_chunks, in_copies)

    @pl.when(n_chunks > 0)
    def _():
        def chunk(r, first):
            in_slot = r % CHUNK_IN_SLOTS
            out_slot = r % 2

            @pl.when(r + 2 < n_chunks)
            def _():
                _start_all(in_copies(r + 2, (r + 2) % CHUNK_IN_SLOTS))

            next_weights(jnp.minimum(r * WEIGHT_PAIRS_PER_CHUNK, n_pairs),
                         jnp.minimum((r + 1) * WEIGHT_PAIRS_PER_CHUNK, n_pairs))
            _wait_all(in_copies(r, in_slot))

            if not first:
                @pl.when(r >= 2)
                def _():
                    _wait_all(out_copies(r - 2, out_slot))

            compute(in_slot, out_slot, first)
            _start_all(out_copies(r, out_slot))

        chunk(0, True)

        def body(r, carry):
            chunk(r, False)
            return carry

        lax.fori_loop(1, n_chunks, body, 0)

    @pl.when(step + 1 < n_steps)
    def _():
        start_first_two(next_n_chunks, next_in_copies)

    next_weights(jnp.minimum(n_chunks * WEIGHT_PAIRS_PER_CHUNK, n_pairs), n_pairs)

    @pl.when(n_chunks >= 2)
    def _():
        _wait_all(out_copies(n_chunks - 2, n_chunks % 2))

    @pl.when(n_chunks >= 1)
    def _():
        _wait_all(out_copies(n_chunks - 1, (n_chunks - 1) % 2))


def _zero_fill_tiles(first, last, zbuf, dst, sem):
    zbuf[...] = jnp.zeros_like(zbuf)

    def copy(t):
        return pltpu.make_async_copy(zbuf, dst(t), sem)

    def start(t, carry):
        copy(t).start()
        return carry

    def wait(t, carry):
        copy(t).wait()
        return carry

    lax.fori_loop(first, last, start, 0)
    lax.fori_loop(first, last, wait, 0)


def _moe_up_kernel(tpe_ref, start_ref, xs_hbm, w_hbm, bg_ref, bu_ref, hid_hbm,
                   wbuf, wg_sc, wu_sc, xbuf, obuf, w_sem, in_sem, out_sem, *, rows):
    e = pl.program_id(0)
    n = pl.program_id(1)
    n_chunks = pl.num_programs(1)
    row0 = start_ref[e]
    d, nc = wg_sc.shape

    piece_rows = d // DMA_SPLIT
    n_pieces = 2 * DMA_SPLIT

    def w_piece(step, slot, p):
        e_, n_ = step // n_chunks, step % n_chunks
        mat, blk = p // DMA_SPLIT, p % DMA_SPLIT
        cols = pl.ds(pl.multiple_of((mat * n_chunks + n_) * nc, nc), nc)
        rws = pl.ds(pl.multiple_of(blk * piece_rows, piece_rows), piece_rows)
        return pltpu.make_async_copy(w_hbm.at[e_, rws, cols], wbuf.at[slot, mat, rws], w_sem.at[slot])

    def w_whole(step, slot):
        e_, n_ = step // n_chunks, step % n_chunks
        return [pltpu.make_async_copy(
            w_hbm.at[e_, :, pl.ds(pl.multiple_of((mat * n_chunks + n_) * nc, nc), nc)], wbuf.at[slot, mat],
            w_sem.at[slot]) for mat in range(2)]

    step, n_steps = e * n_chunks + n, pl.num_programs(0) * n_chunks
    w_slot = _wait_step_weights(step, w_piece, n_pieces, w_whole)
    next_e = jnp.minimum((step + 1) // n_chunks, pl.num_programs(0) - 1)

    def chunk_rows(first_row, r):
        return pl.ds(pl.multiple_of(first_row + r * rows, rows), rows)

    def in_copies_from(first_row):
        def in_copies(r, slot):
            return [pltpu.make_async_copy(xs_hbm.at[chunk_rows(first_row, r)], xbuf.at[slot], in_sem.at[slot])]
        return in_copies

    def out_copies(r, slot):
        return [pltpu.make_async_copy(obuf.at[slot], hid_hbm.at[n, chunk_rows(row0, r)], out_sem.at[slot])]

    half = d // 2

    def compute(in_slot, out_slot, first):
        words = xbuf[in_slot]
        x_lo = lax.bitcast_convert_type(words << 16, F32).astype(BF16)
        x_hi = lax.bitcast_convert_type(words & jnp.uint32(0xFFFF0000), F32).astype(BF16)

        def x_dot(w_ref, cs):
            return (jnp.dot(x_lo, w_ref[:half, cs], preferred_element_type=F32)
                    + jnp.dot(x_hi, w_ref[half:, cs], preferred_element_type=F32))

        for c0 in range(0, nc, MOE_COL_BLOCK):
            cs = slice(c0, c0 + MOE_COL_BLOCK)
            if first:
                wg_sc[:, cs] = wbuf[w_slot, 0, :, cs].astype(BF16)
                wu_sc[:, cs] = wbuf[w_slot, 1, :, cs].astype(BF16)
            g = x_dot(wg_sc, cs) + bg_ref[0, :, cs]
            u = x_dot(wu_sc, cs) + bu_ref[0, :, cs]
            g = jnp.minimum(g, SWIGLU_LIMIT)
            u = jnp.clip(u, -SWIGLU_LIMIT, SWIGLU_LIMIT)
            obuf[out_slot, :, cs] = ((u + 1.0) * (g * jax.nn.sigmoid(SWIGLU_ALPHA * g))).astype(obuf.dtype)

    _expert_chunk_loop(step, n_steps, tpe_ref[e], tpe_ref[next_e], in_copies_from(row0),
                       in_copies_from(start_ref[next_e]), out_copies, compute, w_piece, n_pieces)

    @pl.when(e == pl.num_programs(0) - 1)
    def _():
        _zero_fill_tiles(row0 // rows + tpe_ref[e], hid_hbm.shape[1] // rows, obuf.at[0],
                         lambda t: hid_hbm.at[n, pl.ds(pl.multiple_of(t * rows, rows), rows)], out_sem.at[0])


def _moe_up(tiles_per_e, row_start, xs, w_gu, b_gu, rows, nc):
    r_pad = xs.shape[0]
    d = w_gu.shape[1]
    n_e = w_gu.shape[0]
    d_ff = w_gu.shape[2] // 2
    n_chunks = d_ff // nc
    any_spec = pl.BlockSpec(memory_space=pl.ANY)
    grid_spec = pltpu.PrefetchScalarGridSpec(
        num_scalar_prefetch=2,
        grid=(n_e, n_chunks),
        in_specs=[
            any_spec,
            any_spec,
            pl.BlockSpec((1, 1, nc), lambda e, n, *_: (e, 0, n)),
            pl.BlockSpec((1, 1, nc), lambda e, n, *_: (e, 0, n_chunks + n)),
        ],
        out_specs=any_spec,
        scratch_shapes=[pltpu.VMEM((2, 2, d, nc), w_gu.dtype), pltpu.VMEM((d, nc), BF16), pltpu.VMEM((d, nc), BF16),
                        pltpu.VMEM((CHUNK_IN_SLOTS, rows, d // 2), xs.dtype), pltpu.VMEM((2, rows, nc), BF16),
                        pltpu.SemaphoreType.DMA((2,)), pltpu.SemaphoreType.DMA((CHUNK_IN_SLOTS,)),
                        pltpu.SemaphoreType.DMA((2,))],
    )
    return pl.pallas_call(
        functools.partial(_moe_up_kernel, rows=rows),
        grid_spec=grid_spec,
        out_shape=jax.ShapeDtypeStruct((n_chunks, r_pad, nc), BF16),
        compiler_params=_cparams("arbitrary", "arbitrary"),
        name="moe_up",
    )(tiles_per_e, row_start, xs, w_gu, b_gu, b_gu)


def _moe_down_kernel(tpe_ref, start_ref, hid_hbm, w_hbm, b_ref, y_hbm,
                     wbuf, w_sc, xbuf, obuf, w_sem, in_sem, out_sem, *, rows):
    e = pl.program_id(0)
    n = pl.program_id(1)
    n_chunks = pl.num_programs(1)
    row0 = start_ref[e]
    k_chunks, _, kc = xbuf.shape[1:]
    d_ff, nc = w_sc.shape

    piece_rows = d_ff // DMA_SPLIT

    def w_piece(step, slot, p):
        e_, n_ = step // n_chunks, step % n_chunks
        cols = pl.ds(pl.multiple_of(n_ * nc, nc), nc)
        rws = pl.ds(pl.multiple_of(p * piece_rows, piece_rows), piece_rows)
        return pltpu.make_async_copy(w_hbm.at[e_, rws, cols], wbuf.at[slot, rws], w_sem.at[slot])

    def w_whole(step, slot):
        e_, n_ = step // n_chunks, step % n_chunks
        return [pltpu.make_async_copy(w_hbm.at[e_, :, pl.ds(pl.multiple_of(n_ * nc, nc), nc)], wbuf.at[slot],
                                      w_sem.at[slot])]

    step, n_steps = e * n_chunks + n, pl.num_programs(0) * n_chunks
    w_slot = _wait_step_weights(step, w_piece, DMA_SPLIT, w_whole)
    next_e = jnp.minimum((step + 1) // n_chunks, pl.num_programs(0) - 1)

    def in_copies_from(first_row):
        def in_copies(r, slot):
            chunk = pl.ds(pl.multiple_of(first_row + r * rows, rows), rows)
            return [pltpu.make_async_copy(hid_hbm.at[:, chunk], xbuf.at[slot], in_sem.at[slot])]
        return in_copies

    lanes_per_row = nc // LANES
    tile_rows = rows * lanes_per_row

    def out_copies(r, slot):
        dst = pl.ds(pl.multiple_of((row0 + r * rows) * lanes_per_row, tile_rows), tile_rows)
        return [pltpu.make_async_copy(obuf.at[slot], y_hbm.at[dst], out_sem.at[slot])]

    def compute(in_slot, out_slot, first):
        for c0 in range(0, nc, MOE_COL_BLOCK):
            cs = slice(c0, c0 + MOE_COL_BLOCK)
            if first:
                w_sc[:, cs] = wbuf[w_slot, :, cs].astype(BF16)
            acc = b_ref[0, :, cs] + jnp.dot(xbuf[in_slot, 0], w_sc[0:kc, cs], preferred_element_type=F32)
            for c in range(1, k_chunks):
                acc = acc + jnp.dot(xbuf[in_slot, c], w_sc[c * kc:(c + 1) * kc, cs], preferred_element_type=F32)
            for j in range(MOE_COL_BLOCK // LANES):
                lane_group = c0 // LANES + j
                obuf[out_slot, pl.ds(lane_group, rows, stride=lanes_per_row), :] = acc[:, j * LANES:(j + 1) * LANES]

    _expert_chunk_loop(step, n_steps, tpe_ref[e], tpe_ref[next_e], in_copies_from(row0),
                       in_copies_from(start_ref[next_e]), out_copies, compute, w_piece, DMA_SPLIT)

    @pl.when(e == pl.num_programs(0) - 1)
    def _():
        _zero_fill_tiles(row0 // rows + tpe_ref[e], y_hbm.shape[0] // tile_rows, obuf.at[0],
                         lambda t: y_hbm.at[pl.ds(pl.multiple_of(t * tile_rows, tile_rows), tile_rows)], out_sem.at[0])


def _moe_down(tiles_per_e, row_start, hid, w_dn, b_dn, rows):
    k_chunks, r_pad, kc = hid.shape
    n_e, d_ff, d = w_dn.shape
    nc = d
    n_chunks = 1
    any_spec = pl.BlockSpec(memory_space=pl.ANY)
    grid_spec = pltpu.PrefetchScalarGridSpec(
        num_scalar_prefetch=2,
        grid=(n_e, n_chunks),
        in_specs=[
            any_spec,
            any_spec,
            pl.BlockSpec((1, 1, nc), lambda e, n, *_: (e, 0, n)),
        ],
        out_specs=any_spec,
        scratch_shapes=[pltpu.VMEM((2, d_ff, nc), w_dn.dtype), pltpu.VMEM((d_ff, nc), BF16),
                        pltpu.VMEM((CHUNK_IN_SLOTS, k_chunks, rows, kc), hid.dtype),
                        pltpu.VMEM((2, rows * nc // LANES, LANES), F32),
                        pltpu.SemaphoreType.DMA((2,)), pltpu.SemaphoreType.DMA((CHUNK_IN_SLOTS,)),
                        pltpu.SemaphoreType.DMA((2,))],
    )
    return pl.pallas_call(
        functools.partial(_moe_down_kernel, rows=rows),
        grid_spec=grid_spec,
        out_shape=jax.ShapeDtypeStruct((r_pad * d // LANES, LANES), F32),
        compiler_params=_cparams("arbitrary", "arbitrary"),
        name="moe_down",
    )(tiles_per_e, row_start, hid, w_dn, b_dn)


def _combine_kernel(dest_ref, h_ref, gate_ref, gf_ref, y_hbm, *rest, rows, final_norm, n_head_tiles, split):
    if split:
        o_ref, o_tail_ref, buf, sem = rest
    else:
        (o_ref, buf, sem), o_tail_ref = rest, None
    t = pl.program_id(0)
    n_t = pl.num_programs(0)
    n_dma = rows * TOP_K
    d = o_ref.shape[1]
    groups = d // LANES

    def issue(tile, slot):
        def body(j, carry):
            for u in range(DMA_ISSUE_UNROLL):
                i = j * DMA_ISSUE_UNROLL + u
                src = pl.ds(pl.multiple_of(dest_ref[tile * n_dma + i], groups), groups)
                dst = pl.ds(pl.multiple_of(i * groups, groups), groups)
                pltpu.make_async_copy(y_hbm.at[src], buf.at[slot, dst], sem.at[slot]).start(priority=u % 2)
            return carry
        lax.fori_loop(0, n_dma // DMA_ISSUE_UNROLL, body, 0)

    @pl.when(t == 0)
    def _():
        issue(0, 0)

    @pl.when(t + 1 < n_t)
    def _():
        issue(t + 1, (t + 1) % 2)

    slot = t % 2
    pltpu.make_async_copy(y_hbm.at[pl.ds(0, n_dma * groups)], buf.at[slot], sem.at[slot]).wait()

    def finish(o_ref):
        gates = [gate_ref[:, k:k + 1] for k in range(TOP_K)]
        sq = jnp.zeros((rows, 1), F32)
        for j in range(groups):
            cs = slice(j * LANES, (j + 1) * LANES)
            acc = h_ref[:, cs]
            for k in range(TOP_K):
                acc = acc + gates[k] * buf[slot, pl.ds(k * rows * groups + j, rows, stride=groups), :]
            o_ref[:, cs] = acc
            sq = sq + jnp.sum(acc * acc, axis=-1, keepdims=True)
        if final_norm:
            o_ref[...] = o_ref[...] * lax.rsqrt(sq / d + NORM_EPS) * gf_ref[...]

    if o_tail_ref is None:
        finish(o_ref)
    else:
        pl.when(t < n_head_tiles)(lambda: finish(o_ref))
        pl.when(t >= n_head_tiles)(lambda: finish(o_tail_ref))


def _combine(dest_km, h1, gate, gf, y, rows, final_norm, split_at=None):
    m, d = h1.shape
    n_head = m // rows if split_at is None else split_at // rows
    out_specs = [pl.BlockSpec((rows, d), lambda t, *_: (jnp.minimum(t, n_head - 1), 0))]
    out_shape = [jax.ShapeDtypeStruct((n_head * rows, d), F32)]
    if split_at is not None:
        assert split_at % rows == 0 and (m - split_at) % rows == 0
        out_specs.append(pl.BlockSpec((rows, d), lambda t, *_: (jnp.maximum(t - n_head, 0), 0)))
        out_shape.append(jax.ShapeDtypeStruct((m - split_at, d), F32))
    grid_spec = pltpu.PrefetchScalarGridSpec(
        num_scalar_prefetch=1,
        grid=(m // rows,),
        in_specs=[
            pl.BlockSpec((rows, d), lambda t, *_: (t, 0)),
            pl.BlockSpec((rows, TOP_K), lambda t, *_: (t, 0)),
            pl.BlockSpec((1, d), lambda t, *_: (0, 0)),
            pl.BlockSpec(memory_space=pl.ANY),
        ],
        out_specs=out_specs,
        scratch_shapes=[pltpu.VMEM((2, rows * TOP_K * d // LANES, LANES), F32), pltpu.SemaphoreType.DMA((2,))],
    )
    return pl.pallas_call(
        functools.partial(_combine_kernel, rows=rows, final_norm=final_norm, n_head_tiles=n_head,
                          split=split_at is not None),
        grid_spec=grid_spec,
        out_shape=out_shape,
        compiler_params=_cparams("arbitrary"),
        name="moe_combine",
    )(dest_km, h1, gate, gf, y)


def _routing_tables(top_e, rank, counts, rows):
    n_experts = counts.shape[0]
    tiles_per_e = (counts + rows - 1) // rows
    tile_end = jnp.cumsum(tiles_per_e)
    pad_start = (tile_end - tiles_per_e) * rows
    experts = jnp.arange(n_experts, dtype=jnp.int32)
    dest = rank + jnp.sum(jnp.where(top_e[..., None] == experts, pad_start, 0), axis=-1)
    dest = dest.astype(jnp.int32).reshape(-1)
    return dest, tiles_per_e.astype(jnp.int32), pad_start.astype(jnp.int32)


def _largest_divisor(n, cap, mult):
    best = mult
    for c in range(mult, cap + 1, mult):
        if n % c == 0:
            best = c
    return best


def kernel(x_prompt, x_sample, cache_kv_w128, cache_kv_w512, cache_kv_w2048, state_pool, rel_bias, norm1, w_in,
           w_branch_a, w_branch_b, w_out, w_pool, pool_scale, norm2, router_w, router_b, w_gate_up, b_gate_up,
           w_down, b_down, norm_f):
    batch, seq, d = x_prompt.shape
    db, t_new, _ = x_sample.shape
    depth = norm1.shape[0]
    n_experts = router_w.shape[-1]
    caches = (cache_kv_w128, cache_kv_w512, cache_kv_w2048)
    m_p, m_s = batch * seq, db * t_new
    m = m_p + m_s
    tm = _largest_divisor(m, 640, 8)

    h = jnp.concatenate([x_prompt.reshape(m_p, d), x_sample.reshape(m_s, d)], axis=0)
    bias_p = _prompt_bias_tiles(rel_bias)
    bias_sc, bias_sn = _sample_bias_tiles(rel_bias, t_new)

    cuts = [ATTN_WIDTH, 2 * ATTN_WIDTH, 3 * ATTN_WIDTH, 3 * ATTN_WIDTH + POOL_WIDTH, 3 * ATTN_WIDTH + POOL_WIDTH + d]
    kv_w = HEADS_PER_GROUP * HEAD_DIM
    qkv0 = 2 * d
    kv0 = qkv0 + ATTN_WIDTH
    z0 = qkv0 + 3 * ATTN_WIDTH

    kv_p = [[] for _ in range(N_GROUPS)]
    kv_s = [[] for _ in range(N_GROUPS)]
    pool_p, pool_s = [], []
    for l in range(depth):
        wq, wk, wv, wz, wga, wgb = jnp.split(w_in[l], cuts, axis=-1)
        w_kv = [w[:, g * kv_w:(g + 1) * kv_w] for g in range(N_GROUPS) for w in (wk, wv)]
        w_perm = jnp.concatenate([wga, wgb, wq] + w_kv + [wz], axis=-1).astype(BF16)
        proj = _inproj(h, norm1[l][None, :], w_perm, tm, _largest_divisor(w_perm.shape[1], 1536, 128))

        proj_s = proj[m_p:]
        z_s = proj_s[:, z0:z0 + POOL_WIDTH].reshape(db, t_new, POOL_WIDTH)

        attn_p = _prompt_attn(proj, bias_p, batch, seq, qkv0 // HEAD_DIM, kv0 // HEAD_DIM)
        qkv_s = proj_s[:, qkv0:qkv0 + 3 * ATTN_WIDTH].reshape(db, t_new, 3 * ATTN_WIDTH)
        c0 = caches[0][l].reshape(db, -1, HEAD_DIM)
        c1 = caches[1][l].reshape(db, -1, HEAD_DIM)
        dil2 = ATTN_GROUPS[2][1]
        c2v = caches[2][l].reshape(db, -1, dil2 * CACHE_ROW_TILE, HEAD_DIM)

        news = []
        for g, (window, _) in enumerate(ATTN_GROUPS):
            keep = min(window, seq)
            c = kv0 + 2 * g * kv_w
            kv_p[g].append(jnp.stack([
                proj[(b + 1) * seq - keep:(b + 1) * seq, c:c + 2 * kv_w].reshape(keep, 2, HEADS_PER_GROUP, HEAD_DIM)
                for b in range(batch)]))
            news.append(proj_s[:, c:c + 2 * kv_w].reshape(db, t_new * CACHE_ROW_TILE, HEAD_DIM))
        news[2] = news[2][:, None]
        attn_s, *rolled = _sample_attn(qkv_s, c0, c1, c2v, news, bias_sc, bias_sn)
        attn = jnp.concatenate([attn_p, attn_s.reshape(m_s, ATTN_OUT_WIDTH)], axis=0)
        for g in range(N_GROUPS):
            kv_s[g].append(rolled[g].reshape(caches[g][l].shape))

        w_pool_b = w_pool[l].astype(BF16)
        scale = pool_scale[l][None, :]
        hist_p = jnp.zeros((batch, POOL_HIST, POOL_WIDTH), F32)
        hist_s = jnp.concatenate([jnp.zeros((db, POOL_HIST - POOL_BUF, POOL_WIDTH), F32), state_pool[l]], axis=1)
        pool = jnp.concatenate([
            _pool_mix(hist_p, proj, pl.BlockSpec((seq, POOL_WIDTH), lambda i: (i, z0 // POOL_WIDTH)),
                      batch, seq, w_pool_b, scale, 0).reshape(m_p, POOL_WIDTH),
            _pool_mix(hist_s, z_s, pl.BlockSpec((1, t_new, POOL_WIDTH), lambda i: (i, 0, 0)),
                      db, t_new, w_pool_b, scale, PAST_LEN).reshape(m_s, POOL_WIDTH),
        ], axis=0)
        pool_p.append(jnp.stack([proj[(b + 1) * seq - POOL_BUF:(b + 1) * seq, z0:z0 + POOL_WIDTH]
                                 for b in range(batch)]))
        pool_s.append(jnp.concatenate([state_pool[l], z_s], axis=1)[:, t_new:])

        rw = router_w[l]
        rw_hi = rw.astype(BF16)
        rw_lo = (rw - rw_hi.astype(F32)).astype(BF16)
        h1, xn_rows, top_e, gate, rank, counts = _merge(
            h, attn, pool, proj, w_branch_a[l].astype(BF16), w_branch_b[l].astype(BF16), w_out[l].astype(BF16),
            norm2[l][None, :], rw_hi, rw_lo, router_b[l][None, :], _largest_divisor(m, 320, 8))

        rows = MOE_ROW_TILE
        groups = d // LANES
        n_tiles = (m * TOP_K + n_experts * (rows - 1)) // rows
        dest, tiles_per_e, row_start = _routing_tables(top_e, rank, counts[0], rows)
        xs = _row_scatter(dest, tiles_per_e, row_start, xn_rows, n_tiles * rows, _largest_divisor(m, 256, 8), rows)
        hid = _moe_up(tiles_per_e, row_start, xs, w_gate_up[l], b_gate_up[l][:, None, :], rows, 1024)
        y = _moe_down(tiles_per_e, row_start, hid, w_down[l], b_down[l][:, None, :], rows)
        crow = _largest_divisor(math.gcd(m_p, m_s), 128, 8)
        dest_km = (dest * groups).reshape(m // crow, crow, TOP_K).transpose(0, 2, 1).reshape(-1)
        if l == depth - 1:
            h_p, h_s = _combine(dest_km, h1, gate, norm_f[None, :], y, crow, final_norm=True, split_at=m_p)
        else:
            h, = _combine(dest_km, h1, gate, norm_f[None, :], y, crow, final_norm=False)

    y_prompt = h_p.reshape(batch, seq, d)
    y_sample = h_s.reshape(db, t_new, d)
    return (y_prompt, y_sample,
            jnp.stack(kv_p[0]), jnp.stack(kv_p[1]), jnp.stack(kv_p[2]), jnp.stack(pool_p),
            jnp.stack(kv_s[0]), jnp.stack(kv_s[1]), jnp.stack(kv_s[2]), jnp.stack(pool_s))
```

```python
import functools
import math

import jax
import jax.numpy as jnp
from jax import lax
from jax.experimental import pallas as pl
from jax.experimental.pallas import tpu as pltpu

F32 = jnp.float32
BF16 = jnp.bfloat16

HEAD_DIM = 128
HEADS_PER_GROUP = 4
ATTN_GROUPS = ((128, 1), (512, 4), (2048, 16))
N_GROUPS = len(ATTN_GROUPS)
N_ATTN_HEADS = N_GROUPS * HEADS_PER_GROUP
ATTN_WIDTH = N_ATTN_HEADS * HEAD_DIM
ATTN_OUT_WIDTH = HEADS_PER_GROUP * HEAD_DIM
ATTN_SCALE = HEAD_DIM ** -0.5
N_BACK = 128
N_REL_BUCKETS = 32
REL_MAX_DIST = 2048
POOL_WINDOWS = (2, 4, 8, 16)
POOL_GROUPS = len(POOL_WINDOWS)
PGW = 128
POOL_WIDTH = POOL_GROUPS * PGW
POOL_BUF = max(POOL_WINDOWS) - 1
POOL_HIST = 16
TOP_K = 4
SWIGLU_LIMIT = 7.0
SWIGLU_ALPHA = 1.702
NORM_EPS = 1e-5
NEG_INF = -1e30
PAST_LEN = 8192
CACHE_ROW_TILE = 2 * HEADS_PER_GROUP

VMEM_LIMIT_BYTES = 56 * 1024 * 1024
MOE_ROW_TILE = 256
INPROJ_COL_TILE = 2304
DMA_ISSUE_UNROLL = 8
DMA_SPLIT = 8
ATTN_INTERLEAVE = 8
LANES = 128
CHUNK_IN_SLOTS = 3
MOE_COL_BLOCK = 512
WEIGHT_PAIRS_PER_CHUNK = 2


def _cparams(*sem):
    return pltpu.CompilerParams(dimension_semantics=sem, vmem_limit_bytes=VMEM_LIMIT_BYTES)


def _rmsnorm_f32(x, g):
    return x * lax.rsqrt(jnp.mean(x * x, axis=-1, keepdims=True) + NORM_EPS) * g


def _inproj_kernel(x_ref, g_ref, w_ref, o_ref):
    xn = _rmsnorm_f32(x_ref[...], g_ref[...]).astype(BF16)
    o_ref[...] = jnp.dot(xn, w_ref[...], preferred_element_type=F32)


def _inproj(x, g, w, tm, tn):
    m, d = x.shape
    n = w.shape[1]
    return pl.pallas_call(
        _inproj_kernel,
        grid=(n // tn, m // tm),
        in_specs=[
            pl.BlockSpec((tm, d), lambda j, i: (i, 0)),
            pl.BlockSpec((1, d), lambda j, i: (0, 0)),
            pl.BlockSpec((d, tn), lambda j, i: (0, j)),
        ],
        out_specs=pl.BlockSpec((tm, tn), lambda j, i: (i, j)),
        out_shape=jax.ShapeDtypeStruct((m, n), F32),
        compiler_params=_cparams("arbitrary", "arbitrary"),
        name="inproj",
    )(x, g, w)


def _t5_bucket(dist):
    max_exact = N_REL_BUCKETS // 2
    df = jnp.maximum(dist, 1).astype(F32)
    large = max_exact + (jnp.log(df / max_exact) / math.log(REL_MAX_DIST / max_exact)
                         * (N_REL_BUCKETS - max_exact)).astype(jnp.int32)
    large = jnp.minimum(large, N_REL_BUCKETS - 1)
    return jnp.where(dist < max_exact, dist, large)


def _step_bias(rel_bias, g):
    _, dil = ATTN_GROUPS[g]
    buckets = _t5_bucket(jnp.arange(N_BACK + 1, dtype=jnp.int32) * dil)
    tab = rel_bias[buckets].astype(F32)
    return tab[:, g * HEADS_PER_GROUP:(g + 1) * HEADS_PER_GROUP].T


def _prompt_bias_tiles(rel_bias):
    nb = N_BACK
    period = 3 * nb - 1
    tiles = []
    for g in range(N_GROUPS):
        bj = _step_bias(rel_bias, g)
        h = bj.shape[0]
        pad = jnp.full((h, nb - 1), NEG_INF, F32)
        v = jnp.concatenate([pad, bj[:, ::-1], pad], axis=1)
        skew = jnp.tile(v, (1, nb + 1))[:, :nb * (period + 1)].reshape(h, nb, period + 1)
        tiles.append(skew[:, ::-1, :2 * nb])
    return jnp.stack(tiles, axis=1)


def _sample_bias_tiles(rel_bias, t_new):
    tq = jnp.arange(t_new)[:, None]
    outs = []
    new = []
    for g, (window, dil) in enumerate(ATTN_GROUPS):
        bj = _step_bias(rel_bias, g)
        if g < 2:
            p = jnp.arange(window)[None, :]
            dist = window + tq - p
            ok = (dist % dil == 0) & (dist // dil <= N_BACK)
            outs.append(jnp.where(ok[None], bj[:, jnp.clip(dist // dil, 0, N_BACK)], NEG_INF))
        else:
            col = jnp.arange(t_new * N_BACK)[None, :]
            blk, r = col // N_BACK, col % N_BACK
            ok = blk == tq
            outs.append(jnp.where(ok[None], bj[:, jnp.broadcast_to(N_BACK - r, (t_new, t_new * N_BACK))], NEG_INF))
        tk = jnp.arange(t_new)[None, :]
        d = tq - tk
        okn = (d >= 0) & (d % dil == 0) & (d // dil <= N_BACK)
        new.append(jnp.where(okn[None], bj[:, jnp.clip(d // dil, 0, N_BACK)], NEG_INF))
    return outs, jnp.stack(new, axis=0)


def _softmax_block(s, v):
    m = jnp.max(s, axis=-1, keepdims=True)
    p = jnp.exp(s - m)
    den = jnp.sum(p, axis=-1, keepdims=True)
    o = jnp.dot(p.astype(BF16), v, preferred_element_type=F32) / den
    return o, m + jnp.log(den)


def _prompt_attn_kernel(q0, k0, v0, q1, k1, v1, q2, k2, v2, bias_ref, o_ref, o_sc, l_sc, *, seq):
    qkv = ((q0, k0, v0), (q1, k1, v1), (q2, k2, v2))
    nb = N_BACK

    def rows(ref, start, size, dil):
        if dil == 1:
            return ref[pl.ds(start, size), :]
        return ref[pl.ds(start, size, stride=dil), :]

    def put(g, start, dil, o, lse):
        idx = pl.ds(start, nb) if dil == 1 else pl.ds(start, nb, stride=dil)
        o_sc[g, idx, :] = o
        l_sc[g, idx, :] = jnp.broadcast_to(lse, (nb, HEAD_DIM))

    dn = (((1,), (1,)), ((), ()))

    def attend(g, dil, q_starts, kv_starts, n_kv):
        q_ref, k_ref, v_ref = qkv[g]
        bias = bias_ref[0, g] if n_kv == 2 * nb else bias_ref[0, g, :, nb:]
        qs = [rows(q_ref, st, nb, dil).astype(BF16) for st in q_starts]
        ks = [rows(k_ref, st, n_kv, dil).astype(BF16) for st in kv_starts]
        ss = [lax.dot_general(q, k, dn, preferred_element_type=F32) * ATTN_SCALE + bias for q, k in zip(qs, ks)]
        ms = [jnp.max(s, axis=-1, keepdims=True) for s in ss]
        ps = [jnp.exp(s - m) for s, m in zip(ss, ms)]
        dens = [jnp.sum(p, axis=-1, keepdims=True) for p in ps]
        vs = [rows(v_ref, st, n_kv, dil).astype(BF16) for st in kv_starts]
        for st, p, v, m, den in zip(q_starts, ps, vs, ms, dens):
            o = jnp.dot(p.astype(BF16), v, preferred_element_type=F32) / den
            put(g, st, dil, o, m + jnp.log(den))

    for g, (_, dil) in enumerate(ATTN_GROUPS):
        n_blk = seq // (dil * nb)
        span = dil * nb
        for r0 in range(0, dil, ATTN_INTERLEAVE):
            starts = list(range(r0, min(r0 + ATTN_INTERLEAVE, dil)))
            attend(g, dil, starts, starts, nb)
        if n_blk > 1:
            per_iter = max(c for c in range(1, max(1, ATTN_INTERLEAVE // dil) + 1) if (n_blk - 1) % c == 0)

            def body(it, carry, g=g, dil=dil, span=span, per_iter=per_iter):
                q_starts, kv_starts = [], []
                for j in range(per_iter):
                    b = 1 + it * per_iter + j
                    for r in range(dil):
                        q_starts.append(b * span + r)
                        kv_starts.append((b - 1) * span + r)
                attend(g, dil, q_starts, kv_starts, 2 * nb)
                return carry

            lax.fori_loop(0, (n_blk - 1) // per_iter, body, 0)

    chunk = 256

    def comb(c, carry):
        sl = pl.ds(pl.multiple_of(c * chunk, chunk), chunk)
        l0, l1, l2 = l_sc[0, sl, :], l_sc[1, sl, :], l_sc[2, sl, :]
        mx = jnp.maximum(jnp.maximum(l0, l1), l2)
        e0, e1, e2 = jnp.exp(l0 - mx), jnp.exp(l1 - mx), jnp.exp(l2 - mx)
        num = e0 * o_sc[0, sl, :] + e1 * o_sc[1, sl, :] + e2 * o_sc[2, sl, :]
        o_ref[sl, :] = (num / (e0 + e1 + e2)).astype(o_ref.dtype)
        return carry

    lax.fori_loop(0, seq // chunk, comb, 0)


def _prompt_attn(p, bias, batch, seq, q_block0, kv_block0):
    def spec(kind, g):
        if kind == 0:
            base = q_block0 + g * HEADS_PER_GROUP
        else:
            base = kv_block0 + (2 * g + kind - 1) * HEADS_PER_GROUP
        return pl.BlockSpec((seq, HEAD_DIM), lambda b, i, base=base: (b, base + i))

    in_specs = []
    for g in range(N_GROUPS):
        in_specs += [spec(0, g), spec(1, g), spec(2, g)]
    in_specs.append(pl.BlockSpec((1, N_GROUPS, N_BACK, 2 * N_BACK), lambda b, i: (i, 0, 0, 0)))
    return pl.pallas_call(
        functools.partial(_prompt_attn_kernel, seq=seq),
        grid=(batch, HEADS_PER_GROUP),
        in_specs=in_specs,
        out_specs=pl.BlockSpec((seq, HEAD_DIM), lambda b, i: (b, i)),
        out_shape=jax.ShapeDtypeStruct((batch * seq, ATTN_OUT_WIDTH), BF16),
        scratch_shapes=[pltpu.VMEM((N_GROUPS, seq, HEAD_DIM), F32), pltpu.VMEM((N_GROUPS, seq, HEAD_DIM), F32)],
        compiler_params=_cparams("arbitrary", "arbitrary"),
        name="prompt_attn",
    )(*([p] * 9), bias)


def _sample_attn_kernel(qkv_ref, c0_ref, c1_ref, c2_ref, n0_ref, n1_ref, n2_ref, b0_ref, b1_ref, b2_ref, bn_ref,
                        o_ref, r0_ref, r1_ref, r2_ref, *, t_new):
    caches = (c0_ref, c1_ref, c2_ref)
    biases = (b0_ref, b1_ref, b2_ref)
    dn = (((1,), (1,)), ((), ()))

    shift = t_new * CACHE_ROW_TILE
    for c_ref, n_ref, r_ref in ((c0_ref, n0_ref, r0_ref), (c1_ref, n1_ref, r1_ref)):
        keep = c_ref.shape[1] - shift
        r_ref[0, :keep] = c_ref[0, shift:]
        r_ref[0, keep:] = n_ref[0]
    n_r, width = c2_ref.shape[1], c2_ref.shape[2]
    r2_ref[0, :, :width - shift] = c2_ref[0, :, shift:]
    r2_ref[0, :n_r - 1, width - shift:] = c2_ref[0, 1:, :shift]
    r2_ref[0, n_r - 1:, width - shift:] = n2_ref[0]

    for i in range(HEADS_PER_GROUP):
        outs, lses = [], []
        for g in range(N_GROUPS):
            h = g * HEADS_PER_GROUP + i
            q = qkv_ref[0, :, h * HEAD_DIM:(h + 1) * HEAD_DIM].astype(BF16)
            k_col = ATTN_WIDTH + (2 * g * HEADS_PER_GROUP + i) * HEAD_DIM
            v_col = k_col + HEADS_PER_GROUP * HEAD_DIM
            kn = qkv_ref[0, :, k_col:k_col + HEAD_DIM].astype(BF16)
            vn = qkv_ref[0, :, v_col:v_col + HEAD_DIM].astype(BF16)
            c_ref = caches[g]
            if g < 2:
                n_rows = c_ref.shape[1] // CACHE_ROW_TILE
                kc = c_ref[0, pl.ds(i, n_rows, stride=CACHE_ROW_TILE), :].astype(BF16)
                vc = c_ref[0, pl.ds(HEADS_PER_GROUP + i, n_rows, stride=CACHE_ROW_TILE), :].astype(BF16)
            else:
                kc = jnp.concatenate(
                    [c_ref[0, :, t * CACHE_ROW_TILE + i, :] for t in range(t_new)], axis=0).astype(BF16)
                vc = jnp.concatenate(
                    [c_ref[0, :, t * CACHE_ROW_TILE + HEADS_PER_GROUP + i, :] for t in range(t_new)],
                    axis=0).astype(BF16)
            sc = lax.dot_general(q, kc, dn, preferred_element_type=F32) * ATTN_SCALE + biases[g][i]
            sn = lax.dot_general(q, kn, dn, preferred_element_type=F32) * ATTN_SCALE + bn_ref[g, i]
            m = jnp.maximum(jnp.max(sc, axis=-1, keepdims=True), jnp.max(sn, axis=-1, keepdims=True))
            pc = jnp.exp(sc - m)
            pn = jnp.exp(sn - m)
            den = jnp.sum(pc, axis=-1, keepdims=True) + jnp.sum(pn, axis=-1, keepdims=True)
            o = (jnp.dot(pc.astype(BF16), vc, preferred_element_type=F32)
                 + jnp.dot(pn.astype(BF16), vn, preferred_element_type=F32)) / den
            outs.append(o)
            lses.append(m + jnp.log(den))
        mx = jnp.maximum(jnp.maximum(lses[0], lses[1]), lses[2])
        es = [jnp.exp(l - mx) for l in lses]
        num = es[0] * outs[0] + es[1] * outs[1] + es[2] * outs[2]
        o_ref[0, :, i * HEAD_DIM:(i + 1) * HEAD_DIM] = (num / (es[0] + es[1] + es[2])).astype(o_ref.dtype)


def _sample_attn(qkv_s, c0, c1, c2v, news, bias_c, bias_n):
    db, t_new, _ = qkv_s.shape

    def full(a):
        return pl.BlockSpec(a.shape, lambda b, nd=a.ndim: (0,) * nd)

    def per_seq(a):
        return pl.BlockSpec((1,) + a.shape[1:], lambda b, nd=a.ndim: (b,) + (0,) * (nd - 1))

    return pl.pallas_call(
        functools.partial(_sample_attn_kernel, t_new=t_new),
        grid=(db,),
        in_specs=[per_seq(qkv_s), per_seq(c0), per_seq(c1), per_seq(c2v), per_seq(news[0]), per_seq(news[1]),
                  per_seq(news[2]), full(bias_c[0]), full(bias_c[1]), full(bias_c[2]), full(bias_n)],
        out_specs=[pl.BlockSpec((1, t_new, ATTN_OUT_WIDTH), lambda b: (b, 0, 0)),
                   per_seq(c0), per_seq(c1), per_seq(c2v)],
        out_shape=[jax.ShapeDtypeStruct((db, t_new, ATTN_OUT_WIDTH), BF16)]
                  + [jax.ShapeDtypeStruct(c.shape, c.dtype) for c in (c0, c1, c2v)],
        compiler_params=_cparams("arbitrary"),
        name="sample_attn",
    )(qkv_s, c0, c1, c2v, *news, bias_c[0], bias_c[1], bias_c[2], bias_n)


def _pool_kernel(hist_ref, z_ref, w_ref, s_ref, o_ref, zbuf, *, t_len, pos0, chunk):
    zbuf[0:POOL_HIST, :] = hist_ref[0]
    zbuf[POOL_HIST:, :] = z_ref[...].reshape(t_len, POOL_WIDTH)
    for c in range(t_len // chunk):
        base = POOL_HIST + c * chunk
        for g, w in enumerate(POOL_WINDOWS):
            sl = slice(g * PGW, (g + 1) * PGW)
            cur = zbuf[base:base + chunk, sl]
            tot = cur
            for i in range(1, w):
                tot = tot + zbuf[base - i:base - i + chunk, sl]
            pos = pos0 + c * chunk + lax.broadcasted_iota(jnp.int32, (chunk, PGW), 0)
            cnt = jnp.minimum(w, pos + 1).astype(F32)
            mixed = tot / cnt - cur
            y = jnp.dot(mixed.astype(BF16), w_ref[g], preferred_element_type=F32) * s_ref[:, sl]
            o_ref[0, c * chunk:(c + 1) * chunk, sl] = y.astype(o_ref.dtype)


def _pool_mix(hist, z, z_spec, n_seq, t_len, w_pool, pool_scale, pos0):
    chunk = min(t_len, 256)
    return pl.pallas_call(
        functools.partial(_pool_kernel, t_len=t_len, pos0=pos0, chunk=chunk),
        grid=(n_seq,),
        in_specs=[
            pl.BlockSpec((1, POOL_HIST, POOL_WIDTH), lambda i: (i, 0, 0)),
            z_spec,
            pl.BlockSpec(w_pool.shape, lambda i: (0, 0, 0)),
            pl.BlockSpec((1, POOL_WIDTH), lambda i: (0, 0)),
        ],
        out_specs=pl.BlockSpec((1, t_len, POOL_WIDTH), lambda i: (i, 0, 0)),
        out_shape=jax.ShapeDtypeStruct((n_seq, t_len, POOL_WIDTH), BF16),
        scratch_shapes=[pltpu.VMEM((POOL_HIST + t_len, POOL_WIDTH), F32)],
        compiler_params=_cparams("arbitrary"),
        name="pool_mix",
    )(hist, z, w_pool, pool_scale)


def _merge_kernel(h_ref, a_ref, p_ref, ga_ref, gb_ref, wa_ref, wb_ref, wo_ref, g2_ref, rwh_ref, rwl_ref, rb_ref,
                  h1_ref, xn_ref, e_ref, gate_ref, rank_ref, cnt_ref, carry):
    ya = jnp.dot(a_ref[...], wa_ref[...], preferred_element_type=F32)
    yb = jnp.dot(p_ref[...], wb_ref[...], preferred_element_type=F32)
    u = jax.nn.sigmoid(ga_ref[...]) * ya + jax.nn.sigmoid(gb_ref[...]) * yb
    h1 = h_ref[...] + jnp.dot(u.astype(BF16), wo_ref[...], preferred_element_type=F32)
    h1_ref[...] = h1
    xn = _rmsnorm_f32(h1, g2_ref[...])
    half = xn.shape[1] // 2
    lo = _bf16_bits(xn[:, :half])
    hi = _bf16_bits(xn[:, half:])
    xn_ref[...] = (hi & jnp.uint32(0xFFFF0000)) | (lo >> 16)
    _route_tile(xn, rwh_ref, rwl_ref, rb_ref, e_ref, gate_ref, rank_ref, cnt_ref, carry)


def _bf16_bits(x):
    return lax.bitcast_convert_type(x.astype(BF16).astype(F32), jnp.uint32)


def _merge(h, attn, pool, proj, wa, wb, wo, g2, rwh, rwl, rb, tm):
    m, d = h.shape
    n_e = rwh.shape[1]

    def const(a):
        return pl.BlockSpec(a.shape, lambda i, nd=a.ndim: (0,) * nd, pipeline_mode=pl.Buffered(1))

    tok_spec = pl.BlockSpec((tm, TOP_K), lambda i: (i, 0))
    return pl.pallas_call(
        _merge_kernel,
        grid=(m // tm,),
        in_specs=[
            pl.BlockSpec((tm, d), lambda i: (i, 0)),
            pl.BlockSpec((tm, attn.shape[1]), lambda i: (i, 0)),
            pl.BlockSpec((tm, pool.shape[1]), lambda i: (i, 0)),
            pl.BlockSpec((tm, d), lambda i: (i, 0)),
            pl.BlockSpec((tm, d), lambda i: (i, 1)),
            const(wa), const(wb), const(wo), const(g2), const(rwh), const(rwl), const(rb),
        ],
        out_specs=[pl.BlockSpec((tm, d), lambda i: (i, 0)),
                   pl.BlockSpec((tm, d // 2), lambda i: (i, 0)),
                   tok_spec, tok_spec, tok_spec, pl.BlockSpec((1, n_e), lambda i: (0, 0))],
        out_shape=[jax.ShapeDtypeStruct((m, d), F32), jax.ShapeDtypeStruct((m, d // 2), jnp.uint32),
                   jax.ShapeDtypeStruct((m, TOP_K), jnp.int32), jax.ShapeDtypeStruct((m, TOP_K), F32),
                   jax.ShapeDtypeStruct((m, TOP_K), jnp.int32), jax.ShapeDtypeStruct((1, n_e), jnp.int32)],
        scratch_shapes=[pltpu.VMEM((1, n_e), F32)],
        compiler_params=_cparams("arbitrary"),
        name="merge_route",
    )(h, attn, pool, proj, proj, wa, wb, wo, g2, rwh, rwl, rb)


def _route_tile(x, wh_ref, wl_ref, b_ref, e_ref, gate_ref, rank_ref, cnt_ref, carry):
    @pl.when(pl.program_id(0) == 0)
    def _():
        carry[...] = jnp.zeros_like(carry)

    xh = x.astype(BF16)
    xl = (x - xh.astype(F32)).astype(BF16)
    logits = (jnp.dot(xh, wh_ref[...], preferred_element_type=F32)
              + jnp.dot(xh, wl_ref[...], preferred_element_type=F32)
              + jnp.dot(xl, wh_ref[...], preferred_element_type=F32)) + b_ref[...]
    n_e = logits.shape[-1]
    lane = lax.broadcasted_iota(jnp.int32, logits.shape, 1)
    vals = logits
    tops, idxs = [], []
    for _ in range(TOP_K):
        m = jnp.max(vals, axis=-1, keepdims=True)
        idx = jnp.min(jnp.where(vals == m, lane, n_e), axis=-1, keepdims=True)
        tops.append(m)
        idxs.append(idx)
        vals = jnp.where(lane == idx, -jnp.inf, vals)
    ex = [jnp.exp(t - tops[0]) for t in tops]
    den = ex[0] + ex[1] + ex[2] + ex[3]
    for k in range(TOP_K):
        e_ref[:, k:k + 1] = idxs[k]
        gate_ref[:, k:k + 1] = ex[k] / den

    onehots = [(lane == idx).astype(F32) for idx in idxs]
    cnt = onehots[0] + onehots[1] + onehots[2] + onehots[3]
    tm = cnt.shape[0]
    lower = (lax.broadcasted_iota(jnp.int32, (tm, tm), 0) > lax.broadcasted_iota(jnp.int32, (tm, tm), 1))
    prefix = jnp.dot(lower.astype(BF16), cnt.astype(BF16), preferred_element_type=F32)
    base = carry[...] + prefix
    for k in range(TOP_K):
        rank_ref[:, k:k + 1] = jnp.sum(onehots[k] * base, axis=-1, keepdims=True).astype(jnp.int32)
    carry[...] = carry[...] + jnp.sum(cnt, axis=0, keepdims=True)
    cnt_ref[...] = carry[...].astype(jnp.int32)


def _row_scatter_kernel(dest_ref, tpe_ref, start_ref, x_ref, o_hbm, zbuf, sem, zsem, *, rows, tile):
    t = pl.program_id(0)
    n_copies = rows * TOP_K
    tokens_per_iter = DMA_ISSUE_UNROLL // TOP_K
    n_experts = tpe_ref.shape[0]
    n_tiles = o_hbm.shape[0] // tile

    @pl.when(t == 0)
    def _():
        zbuf[...] = jnp.zeros_like(zbuf)

        def zero_tile(tile_idx):
            return pltpu.make_async_copy(zbuf, o_hbm.at[pl.ds(pl.multiple_of(tile_idx * tile, tile), tile)], zsem.at[0])

        def last_tile(e):
            return start_ref[e] // tile + tpe_ref[e] - 1

        def each_expert(fn):
            def body(e, carry):
                pl.when(tpe_ref[e] > 0)(lambda: fn(zero_tile(last_tile(e))))
                return carry
            lax.fori_loop(0, n_experts, body, 0)

        def each_tail_tile(fn):
            first = start_ref[n_experts - 1] // tile + tpe_ref[n_experts - 1]

            def body(i, carry):
                fn(zero_tile(i))
                return carry
            lax.fori_loop(first, n_tiles, body, 0)

        each_expert(lambda c: c.start())
        each_tail_tile(lambda c: c.start())
        each_expert(lambda c: c.wait())
        each_tail_tile(lambda c: c.wait())

    def body(j, carry):
        for u in range(DMA_ISSUE_UNROLL):
            src = pl.ds(j * tokens_per_iter + u // TOP_K, 1)
            dst = pl.ds(dest_ref[t * n_copies + j * DMA_ISSUE_UNROLL + u], 1)
            pltpu.make_async_copy(x_ref.at[src], o_hbm.at[dst], sem.at[0]).start(priority=u % 2)
        return carry

    lax.fori_loop(0, n_copies // DMA_ISSUE_UNROLL, body, 0)
    for _ in range(TOP_K):
        pltpu.make_async_copy(x_ref, o_hbm.at[pl.ds(0, rows)], sem.at[0]).wait()


def _row_scatter(dest, tiles_per_e, row_start, src, n_rows, rows, tile):
    m, width = src.shape
    assert DMA_ISSUE_UNROLL % TOP_K == 0 and m % rows == 0 and n_rows % tile == 0
    grid_spec = pltpu.PrefetchScalarGridSpec(
        num_scalar_prefetch=3,
        grid=(m // rows,),
        in_specs=[pl.BlockSpec((rows, width), lambda t, *_: (t, 0))],
        out_specs=pl.BlockSpec(memory_space=pl.ANY),
        scratch_shapes=[pltpu.VMEM((tile, width), src.dtype), pltpu.SemaphoreType.DMA((1,)),
                        pltpu.SemaphoreType.DMA((1,))],
    )
    return pl.pallas_call(
        functools.partial(_row_scatter_kernel, rows=rows, tile=tile),
        grid_spec=grid_spec,
        out_shape=jax.ShapeDtypeStruct((n_rows, width), src.dtype),
        compiler_params=_cparams("arbitrary"),
        name="moe_row_scatter",
    )(dest, tiles_per_e, row_start, src)


def _start_all(copies):
    for c in copies:
        c.start()


def _wait_all(copies):
    for c in copies:
        c.wait()


def _start_weight_pairs(piece, step, slot, q_lo, q_hi):
    def body(q, carry):
        piece(step, slot, 2 * q).start(priority=0)
        piece(step, slot, 2 * q + 1).start(priority=1)
        return carry
    lax.fori_loop(q_lo, q_hi, body, 0)


def _wait_step_weights(step, piece, n_pieces, whole):
    slot = step % 2

    @pl.when(step == 0)
    def _():
        _start_weight_pairs(piece, step, slot, 0, n_pieces // 2)

    _wait_all(whole(step, slot))
    return slot


def _expert_chunk_loop(step, n_steps, n_chunks, next_n_chunks, in_copies, next_in_copies, out_copies, compute,
                       weight_piece, n_weight_pieces):
    def start_first_two(count, copies):
        @pl.when(count > 0)
        def _():
            _start_all(copies(0, 0))

        @pl.when(count > 1)
        def _():
            _start_all(copies(1, 1))

    n_pairs = n_weight_pieces // 2

    def next_weights(q_lo, q_hi):
        @pl.when(step + 1 < n_steps)
        def _():
            _start_weight_pairs(weight_piece, step + 1, (step + 1) % 2, q_lo, q_hi)

    @pl.when(step == 0)
    def _():
        start_first_two(n_chunks, in_copies)

    @pl.when(n_chunks > 0)
    def _():
        def chunk(r, first):
            in_slot = r % CHUNK_IN_SLOTS
            out_slot = r % 2

            @pl.when(r + 2 < n_chunks)
            def _():
                _start_all(in_copies(r + 2, (r + 2) % CHUNK_IN_SLOTS))

            next_weights(jnp.minimum(r * WEIGHT_PAIRS_PER_CHUNK, n_pairs),
                         jnp.minimum((r + 1) * WEIGHT_PAIRS_PER_CHUNK, n_pairs))
            _wait_all(in_copies(r, in_slot))

            if not first:
                @pl.when(r >= 2)
                def _():
                    _wait_all(out_copies(r - 2, out_slot))

            compute(in_slot, out_slot, first)
            _start_all(out_copies(r, out_slot))

        chunk(0, True)

        def body(r, carry):
            chunk(r, False)
            return carry

        lax.fori_loop(1, n_chunks, body, 0)

    @pl.when(step + 1 < n_steps)
    def _():
        start_first_two(next_n_chunks, next_in_copies)

    next_weights(jnp.minimum(n_chunks * WEIGHT_PAIRS_PER_CHUNK, n_pairs), n_pairs)

    @pl.when(n_chunks >= 2)
    def _():
        _wait_all(out_copies(n_chunks - 2, n_chunks % 2))

    @pl.when(n_chunks >= 1)
    def _():
        _wait_all(out_copies(n_chunks - 1, (n_chunks - 1) % 2))


def _zero_fill_tiles(first, last, zbuf, dst, sem):
    zbuf[...] = jnp.zeros_like(zbuf)

    def copy(t):
        return pltpu.make_async_copy(zbuf, dst(t), sem)

    def start(t, carry):
        copy(t).start()
        return carry

    def wait(t, carry):
        copy(t).wait()
        return carry

    lax.fori_loop(first, last, start, 0)
    lax.fori_loop(first, last, wait, 0)


def _moe_up_kernel(tpe_ref, start_ref, xs_hbm, w_hbm, bg_ref, bu_ref, hid_hbm,
                   wbuf, wg_sc, wu_sc, xbuf, obuf, w_sem, in_sem, out_sem, *, rows):
    e = pl.program_id(0)
    n = pl.program_id(1)
    n_chunks = pl.num_programs(1)
    row0 = start_ref[e]
    d, nc = wg_sc.shape

    piece_rows = d // DMA_SPLIT
    n_pieces = 2 * DMA_SPLIT

    def w_piece(step, slot, p):
        e_, n_ = step // n_chunks, step % n_chunks
        mat, blk = p // DMA_SPLIT, p % DMA_SPLIT
        cols = pl.ds(pl.multiple_of((mat * n_chunks + n_) * nc, nc), nc)
        rws = pl.ds(pl.multiple_of(blk * piece_rows, piece_rows), piece_rows)
        return pltpu.make_async_copy(w_hbm.at[e_, rws, cols], wbuf.at[slot, mat, rws], w_sem.at[slot])

    def w_whole(step, slot):
        e_, n_ = step // n_chunks, step % n_chunks
        return [pltpu.make_async_copy(
            w_hbm.at[e_, :, pl.ds(pl.multiple_of((mat * n_chunks + n_) * nc, nc), nc)], wbuf.at[slot, mat],
            w_sem.at[slot]) for mat in range(2)]

    step, n_steps = e * n_chunks + n, pl.num_programs(0) * n_chunks
    w_slot = _wait_step_weights(step, w_piece, n_pieces, w_whole)
    next_e = jnp.minimum((step + 1) // n_chunks, pl.num_programs(0) - 1)

    def chunk_rows(first_row, r):
        return pl.ds(pl.multiple_of(first_row + r * rows, rows), rows)

    def in_copies_from(first_row):
        def in_copies(r, slot):
            return [pltpu.make_async_copy(xs_hbm.at[chunk_rows(first_row, r)], xbuf.at[slot], in_sem.at[slot])]
        return in_copies

    def out_copies(r, slot):
        return [pltpu.make_async_copy(obuf.at[slot], hid_hbm.at[n, chunk_rows(row0, r)], out_sem.at[slot])]

    half = d // 2

    def compute(in_slot, out_slot, first):
        words = xbuf[in_slot]
        x_lo = lax.bitcast_convert_type(words << 16, F32).astype(BF16)
        x_hi = lax.bitcast_convert_type(words & jnp.uint32(0xFFFF0000), F32).astype(BF16)

        def x_dot(w_ref, cs):
            return (jnp.dot(x_lo, w_ref[:half, cs], preferred_element_type=F32)
                    + jnp.dot(x_hi, w_ref[half:, cs], preferred_element_type=F32))

        for c0 in range(0, nc, MOE_COL_BLOCK):
            cs = slice(c0, c0 + MOE_COL_BLOCK)
            if first:
                wg_sc[:, cs] = wbuf[w_slot, 0, :, cs].astype(BF16)
                wu_sc[:, cs] = wbuf[w_slot, 1, :, cs].astype(BF16)
            g = x_dot(wg_sc, cs) + bg_ref[0, :, cs]
            u = x_dot(wu_sc, cs) + bu_ref[0, :, cs]
            g = jnp.minimum(g, SWIGLU_LIMIT)
            u = jnp.clip(u, -SWIGLU_LIMIT, SWIGLU_LIMIT)
            obuf[out_slot, :, cs] = ((u + 1.0) * (g * jax.nn.sigmoid(SWIGLU_ALPHA * g))).astype(obuf.dtype)

    _expert_chunk_loop(step, n_steps, tpe_ref[e], tpe_ref[next_e], in_copies_from(row0),
                       in_copies_from(start_ref[next_e]), out_copies, compute, w_piece, n_pieces)

    @pl.when(e == pl.num_programs(0) - 1)
    def _():
        _zero_fill_tiles(row0 // rows + tpe_ref[e], hid_hbm.shape[1] // rows, obuf.at[0],
                         lambda t: hid_hbm.at[n, pl.ds(pl.multiple_of(t * rows, rows), rows)], out_sem.at[0])


def _moe_up(tiles_per_e, row_start, xs, w_gu, b_gu, rows, nc):
    r_pad = xs.shape[0]
    d = w_gu.shape[1]
    n_e = w_gu.shape[0]
    d_ff = w_gu.shape[2] // 2
    n_chunks = d_ff // nc
    any_spec = pl.BlockSpec(memory_space=pl.ANY)
    grid_spec = pltpu.PrefetchScalarGridSpec(
        num_scalar_prefetch=2,
        grid=(n_e, n_chunks),
        in_specs=[
            any_spec,
            any_spec,
            pl.BlockSpec((1, 1, nc), lambda e, n, *_: (e, 0, n)),
            pl.BlockSpec((1, 1, nc), lambda e, n, *_: (e, 0, n_chunks + n)),
        ],
        out_specs=any_spec,
        scratch_shapes=[pltpu.VMEM((2, 2, d, nc), w_gu.dtype), pltpu.VMEM((d, nc), BF16), pltpu.VMEM((d, nc), BF16),
                        pltpu.VMEM((CHUNK_IN_SLOTS, rows, d // 2), xs.dtype), pltpu.VMEM((2, rows, nc), BF16),
                        pltpu.SemaphoreType.DMA((2,)), pltpu.SemaphoreType.DMA((CHUNK_IN_SLOTS,)),
                        pltpu.SemaphoreType.DMA((2,))],
    )
    return pl.pallas_call(
        functools.partial(_moe_up_kernel, rows=rows),
        grid_spec=grid_spec,
        out_shape=jax.ShapeDtypeStruct((n_chunks, r_pad, nc), BF16),
        compiler_params=_cparams("arbitrary", "arbitrary"),
        name="moe_up",
    )(tiles_per_e, row_start, xs, w_gu, b_gu, b_gu)


def _moe_down_kernel(tpe_ref, start_ref, hid_hbm, w_hbm, b_ref, y_hbm,
                     wbuf, w_sc, xbuf, obuf, w_sem, in_sem, out_sem, *, rows):
    e = pl.program_id(0)
    n = pl.program_id(1)
    n_chunks = pl.num_programs(1)
    row0 = start_ref[e]
    k_chunks, _, kc = xbuf.shape[1:]
    d_ff, nc = w_sc.shape

    piece_rows = d_ff // DMA_SPLIT

    def w_piece(step, slot, p):
        e_, n_ = step // n_chunks, step % n_chunks
        cols = pl.ds(pl.multiple_of(n_ * nc, nc), nc)
        rws = pl.ds(pl.multiple_of(p * piece_rows, piece_rows), piece_rows)
        return pltpu.make_async_copy(w_hbm.at[e_, rws, cols], wbuf.at[slot, rws], w_sem.at[slot])

    def w_whole(step, slot):
        e_, n_ = step // n_chunks, step % n_chunks
        return [pltpu.make_async_copy(w_hbm.at[e_, :, pl.ds(pl.multiple_of(n_ * nc, nc), nc)], wbuf.at[slot],
                                      w_sem.at[slot])]

    step, n_steps = e * n_chunks + n, pl.num_programs(0) * n_chunks
    w_slot = _wait_step_weights(step, w_piece, DMA_SPLIT, w_whole)
    next_e = jnp.minimum((step + 1) // n_chunks, pl.num_programs(0) - 1)

    def in_copies_from(first_row):
        def in_copies(r, slot):
            chunk = pl.ds(pl.multiple_of(first_row + r * rows, rows), rows)
            return [pltpu.make_async_copy(hid_hbm.at[:, chunk], xbuf.at[slot], in_sem.at[slot])]
        return in_copies

    lanes_per_row = nc // LANES
    tile_rows = rows * lanes_per_row

    def out_copies(r, slot):
        dst = pl.ds(pl.multiple_of((row0 + r * rows) * lanes_per_row, tile_rows), tile_rows)
        return [pltpu.make_async_copy(obuf.at[slot], y_hbm.at[dst], out_sem.at[slot])]

    def compute(in_slot, out_slot, first):
        for c0 in range(0, nc, MOE_COL_BLOCK):
            cs = slice(c0, c0 + MOE_COL_BLOCK)
            if first:
                w_sc[:, cs] = wbuf[w_slot, :, cs].astype(BF16)
            acc = b_ref[0, :, cs] + jnp.dot(xbuf[in_slot, 0], w_sc[0:kc, cs], preferred_element_type=F32)
            for c in range(1, k_chunks):
                acc = acc + jnp.dot(xbuf[in_slot, c], w_sc[c * kc:(c + 1) * kc, cs], preferred_element_type=F32)
            for j in range(MOE_COL_BLOCK // LANES):
                lane_group = c0 // LANES + j
                obuf[out_slot, pl.ds(lane_group, rows, stride=lanes_per_row), :] = acc[:, j * LANES:(j + 1) * LANES]

    _expert_chunk_loop(step, n_steps, tpe_ref[e], tpe_ref[next_e], in_copies_from(row0),
                       in_copies_from(start_ref[next_e]), out_copies, compute, w_piece, DMA_SPLIT)

    @pl.when(e == pl.num_programs(0) - 1)
    def _():
        _zero_fill_tiles(row0 // rows + tpe_ref[e], y_hbm.shape[0] // tile_rows, obuf.at[0],
                         lambda t: y_hbm.at[pl.ds(pl.multiple_of(t * tile_rows, tile_rows), tile_rows)], out_sem.at[0])


def _moe_down(tiles_per_e, row_start, hid, w_dn, b_dn, rows):
    k_chunks, r_pad, kc = hid.shape
    n_e, d_ff, d = w_dn.shape
    nc = d
    n_chunks = 1
    any_spec = pl.BlockSpec(memory_space=pl.ANY)
    grid_spec = pltpu.PrefetchScalarGridSpec(
        num_scalar_prefetch=2,
        grid=(n_e, n_chunks),
        in_specs=[
            any_spec,
            any_spec,
            pl.BlockSpec((1, 1, nc), lambda e, n, *_: (e, 0, n)),
        ],
        out_specs=any_spec,
        scratch_shapes=[pltpu.VMEM((2, d_ff, nc), w_dn.dtype), pltpu.VMEM((d_ff, nc), BF16),
                        pltpu.VMEM((CHUNK_IN_SLOTS, k_chunks, rows, kc), hid.dtype),
                        pltpu.VMEM((2, rows * nc // LANES, LANES), F32),
                        pltpu.SemaphoreType.DMA((2,)), pltpu.SemaphoreType.DMA((CHUNK_IN_SLOTS,)),
                        pltpu.SemaphoreType.DMA((2,))],
    )
    return pl.pallas_call(
        functools.partial(_moe_down_kernel, rows=rows),
        grid_spec=grid_spec,
        out_shape=jax.ShapeDtypeStruct((r_pad * d // LANES, LANES), F32),
        compiler_params=_cparams("arbitrary", "arbitrary"),
        name="moe_down",
    )(tiles_per_e, row_start, hid, w_dn, b_dn)


def _combine_kernel(dest_ref, h_ref, gate_ref, gf_ref, y_hbm, *rest, rows, final_norm, n_head_tiles, split):
    if split:
        o_ref, o_tail_ref, buf, sem = rest
    else:
        (o_ref, buf, sem), o_tail_ref = rest, None
    t = pl.program_id(0)
    n_t = pl.num_programs(0)
    n_dma = rows * TOP_K
    d = o_ref.shape[1]
    groups = d // LANES

    def issue(tile, slot):
        def body(j, carry):
            for u in range(DMA_ISSUE_UNROLL):
                i = j * DMA_ISSUE_UNROLL + u
                src = pl.ds(pl.multiple_of(dest_ref[tile * n_dma + i], groups), groups)
                dst = pl.ds(pl.multiple_of(i * groups, groups), groups)
                pltpu.make_async_copy(y_hbm.at[src], buf.at[slot, dst], sem.at[slot]).start(priority=u % 2)
            return carry
        lax.fori_loop(0, n_dma // DMA_ISSUE_UNROLL, body, 0)

    @pl.when(t == 0)
    def _():
        issue(0, 0)

    @pl.when(t + 1 < n_t)
    def _():
        issue(t + 1, (t + 1) % 2)

    slot = t % 2
    pltpu.make_async_copy(y_hbm.at[pl.ds(0, n_dma * groups)], buf.at[slot], sem.at[slot]).wait()

    def finish(o_ref):
        gates = [gate_ref[:, k:k + 1] for k in range(TOP_K)]
        sq = jnp.zeros((rows, 1), F32)
        for j in range(groups):
            cs = slice(j * LANES, (j + 1) * LANES)
            acc = h_ref[:, cs]
            for k in range(TOP_K):
                acc = acc + gates[k] * buf[slot, pl.ds(k * rows * groups + j, rows, stride=groups), :]
            o_ref[:, cs] = acc
            sq = sq + jnp.sum(acc * acc, axis=-1, keepdims=True)
        if final_norm:
            o_ref[...] = o_ref[...] * lax.rsqrt(sq / d + NORM_EPS) * gf_ref[...]

    if o_tail_ref is None:
        finish(o_ref)
    else:
        pl.when(t < n_head_tiles)(lambda: finish(o_ref))
        pl.when(t >= n_head_tiles)(lambda: finish(o_tail_ref))


def _combine(dest_km, h1, gate, gf, y, rows, final_norm, split_at=None):
    m, d = h1.shape
    n_head = m // rows if split_at is None else split_at // rows
    out_specs = [pl.BlockSpec((rows, d), lambda t, *_: (jnp.minimum(t, n_head - 1), 0))]
    out_shape = [jax.ShapeDtypeStruct((n_head * rows, d), F32)]
    if split_at is not None:
        assert split_at % rows == 0 and (m - split_at) % rows == 0
        out_specs.append(pl.BlockSpec((rows, d), lambda t, *_: (jnp.maximum(t - n_head, 0), 0)))
        out_shape.append(jax.ShapeDtypeStruct((m - split_at, d), F32))
    grid_spec = pltpu.PrefetchScalarGridSpec(
        num_scalar_prefetch=1,
        grid=(m // rows,),
        in_specs=[
            pl.BlockSpec((rows, d), lambda t, *_: (t, 0)),
            pl.BlockSpec((rows, TOP_K), lambda t, *_: (t, 0)),
            pl.BlockSpec((1, d), lambda t, *_: (0, 0)),
            pl.BlockSpec(memory_space=pl.ANY),
        ],
        out_specs=out_specs,
        scratch_shapes=[pltpu.VMEM((2, rows * TOP_K * d // LANES, LANES), F32), pltpu.SemaphoreType.DMA((2,))],
    )
    return pl.pallas_call(
        functools.partial(_combine_kernel, rows=rows, final_norm=final_norm, n_head_tiles=n_head,
                          split=split_at is not None),
        grid_spec=grid_spec,
        out_shape=out_shape,
        compiler_params=_cparams("arbitrary"),
        name="moe_combine",
    )(dest_km, h1, gate, gf, y)


def _routing_tables(top_e, rank, counts, rows):
    n_experts = counts.shape[0]
    tiles_per_e = (counts + rows - 1) // rows
    tile_end = jnp.cumsum(tiles_per_e)
    pad_start = (tile_end - tiles_per_e) * rows
    experts = jnp.arange(n_experts, dtype=jnp.int32)
    dest = rank + jnp.sum(jnp.where(top_e[..., None] == experts, pad_start, 0), axis=-1)
    dest = dest.astype(jnp.int32).reshape(-1)
    return dest, tiles_per_e.astype(jnp.int32), pad_start.astype(jnp.int32)


def _largest_divisor(n, cap, mult):
    best = mult
    for c in range(mult, cap + 1, mult):
        if n % c == 0:
            best = c
    return best


def kernel(x_prompt, x_sample, cache_kv_w128, cache_kv_w512, cache_kv_w2048, state_pool, rel_bias, norm1, w_in,
           w_branch_a, w_branch_b, w_out, w_pool, pool_scale, norm2, router_w, router_b, w_gate_up, b_gate_up,
           w_down, b_down, norm_f):
    batch, seq, d = x_prompt.shape
    db, t_new, _ = x_sample.shape
    depth = norm1.shape[0]
    n_experts = router_w.shape[-1]
    caches = (cache_kv_w128, cache_kv_w512, cache_kv_w2048)
    m_p, m_s = batch * seq, db * t_new
    m = m_p + m_s
    tm = _largest_divisor(m, 640, 8)

    h = jnp.concatenate([x_prompt.reshape(m_p, d), x_sample.reshape(m_s, d)], axis=0)
    bias_p = _prompt_bias_tiles(rel_bias)
    bias_sc, bias_sn = _sample_bias_tiles(rel_bias, t_new)

    cuts = [ATTN_WIDTH, 2 * ATTN_WIDTH, 3 * ATTN_WIDTH, 3 * ATTN_WIDTH + POOL_WIDTH, 3 * ATTN_WIDTH + POOL_WIDTH + d]
    kv_w = HEADS_PER_GROUP * HEAD_DIM
    qkv0 = 2 * d
    kv0 = qkv0 + ATTN_WIDTH
    z0 = qkv0 + 3 * ATTN_WIDTH

    kv_p = [[] for _ in range(N_GROUPS)]
    kv_s = [[] for _ in range(N_GROUPS)]
    pool_p, pool_s = [], []
    for l in range(depth):
        wq, wk, wv, wz, wga, wgb = jnp.split(w_in[l], cuts, axis=-1)
        w_kv = [w[:, g * kv_w:(g + 1) * kv_w] for g in range(N_GROUPS) for w in (wk, wv)]
        w_perm = jnp.concatenate([wga, wgb, wq] + w_kv + [wz], axis=-1).astype(BF16)
        proj = _inproj(h, norm1[l][None, :], w_perm, tm, _largest_divisor(w_perm.shape[1], INPROJ_COL_TILE, 128))

        proj_s = proj[m_p:]
        z_s = proj_s[:, z0:z0 + POOL_WIDTH].reshape(db, t_new, POOL_WIDTH)

        attn_p = _prompt_attn(proj, bias_p, batch, seq, qkv0 // HEAD_DIM, kv0 // HEAD_DIM)
        qkv_s = proj_s[:, qkv0:qkv0 + 3 * ATTN_WIDTH].reshape(db, t_new, 3 * ATTN_WIDTH)
        c0 = caches[0][l].reshape(db, -1, HEAD_DIM)
        c1 = caches[1][l].reshape(db, -1, HEAD_DIM)
        dil2 = ATTN_GROUPS[2][1]
        c2v = caches[2][l].reshape(db, -1, dil2 * CACHE_ROW_TILE, HEAD_DIM)

        news = []
        for g, (window, _) in enumerate(ATTN_GROUPS):
            keep = min(window, seq)
            c = kv0 + 2 * g * kv_w
            kv_p[g].append(jnp.stack([
                proj[(b + 1) * seq - keep:(b + 1) * seq, c:c + 2 * kv_w].reshape(keep, 2, HEADS_PER_GROUP, HEAD_DIM)
                for b in range(batch)]))
            news.append(proj_s[:, c:c + 2 * kv_w].reshape(db, t_new * CACHE_ROW_TILE, HEAD_DIM))
        news[2] = news[2][:, None]
        attn_s, *rolled = _sample_attn(qkv_s, c0, c1, c2v, news, bias_sc, bias_sn)
        attn = jnp.concatenate([attn_p, attn_s.reshape(m_s, ATTN_OUT_WIDTH)], axis=0)
        for g in range(N_GROUPS):
            kv_s[g].append(rolled[g].reshape(caches[g][l].shape))

        w_pool_b = w_pool[l].astype(BF16)
        scale = pool_scale[l][None, :]
        hist_p = jnp.zeros((batch, POOL_HIST, POOL_WIDTH), F32)
        hist_s = jnp.concatenate([jnp.zeros((db, POOL_HIST - POOL_BUF, POOL_WIDTH), F32), state_pool[l]], axis=1)
        pool = jnp.concatenate([
            _pool_mix(hist_p, proj, pl.BlockSpec((seq, POOL_WIDTH), lambda i: (i, z0 // POOL_WIDTH)),
                      batch, seq, w_pool_b, scale, 0).reshape(m_p, POOL_WIDTH),
            _pool_mix(hist_s, z_s, pl.BlockSpec((1, t_new, POOL_WIDTH), lambda i: (i, 0, 0)),
                      db, t_new, w_pool_b, scale, PAST_LEN).reshape(m_s, POOL_WIDTH),
        ], axis=0)
        pool_p.append(jnp.stack([proj[(b + 1) * seq - POOL_BUF:(b + 1) * seq, z0:z0 + POOL_WIDTH]
                                 for b in range(batch)]))
        pool_s.append(jnp.concatenate([state_pool[l], z_s], axis=1)[:, t_new:])

        rw = router_w[l]
        rw_hi = rw.astype(BF16)
        rw_lo = (rw - rw_hi.astype(F32)).astype(BF16)
        h1, xn_rows, top_e, gate, rank, counts = _merge(
            h, attn, pool, proj, w_branch_a[l].astype(BF16), w_branch_b[l].astype(BF16), w_out[l].astype(BF16),
            norm2[l][None, :], rw_hi, rw_lo, router_b[l][None, :], _largest_divisor(m, 320, 8))

        rows = MOE_ROW_TILE
        groups = d // LANES
        n_tiles = (m * TOP_K + n_experts * (rows - 1)) // rows
        dest, tiles_per_e, row_start = _routing_tables(top_e, rank, counts[0], rows)
        xs = _row_scatter(dest, tiles_per_e, row_start, xn_rows, n_tiles * rows, _largest_divisor(m, 256, 8), rows)
        hid = _moe_up(tiles_per_e, row_start, xs, w_gate_up[l], b_gate_up[l][:, None, :], rows, 1024)
        y = _moe_down(tiles_per_e, row_start, hid, w_down[l], b_down[l][:, None, :], rows)
        crow = _largest_divisor(math.gcd(m_p, m_s), 128, 8)
        dest_km = (dest * groups).reshape(m // crow, crow, TOP_K).transpose(0, 2, 1).reshape(-1)
        if l == depth - 1:
            h_p, h_s = _combine(dest_km, h1, gate, norm_f[None, :], y, crow, final_norm=True, split_at=m_p)
        else:
            h, = _combine(dest_km, h1, gate, norm_f[None, :], y, crow, final_norm=False)

    y_prompt = h_p.reshape(batch, seq, d)
    y_sample = h_s.reshape(db, t_new, d)
    return (y_prompt, y_sample,
            jnp.stack(kv_p[0]), jnp.stack(kv_p[1]), jnp.stack(kv_p[2]), jnp.stack(pool_p),
            jnp.stack(kv_s[0]), jnp.stack(kv_s[1]), jnp.stack(kv_s[2]), jnp.stack(pool_s))
```
